```python
import jax, jax.numpy as jnp
from jax import lax
import numpy as np

D_MODEL = 1024
BATCH = 4
SEQ = 4096
DEPTH = 1
DEC_BATCH = 16
DEC_SEQ = 64
PAST_LEN = 2048

CHUNK = 64
LEFT_CHUNKS = 8
WINDOW = LEFT_CHUNKS * CHUNK
ATT_HEADS = 8
ATT_DIM = 64
ATT_WIDTH = ATT_HEADS * ATT_DIM
MAX_REL = 256
HG_HEADS = 4
HG_DK = 128
HG_DV = 128
HG_WIDTH = HG_HEADS * HG_DK
HG_BLOCK = 16
N_EXPERTS = 32
TOP_K = 4
D_FF = D_MODEL
SWIGLU_LIMIT = 7.0
SWIGLU_ALPHA = 1.702
RMS_EPS = 1e-5
IN_COLS = 3 * ATT_WIDTH + 4 * HG_WIDTH + 2 * D_MODEL

kernel_name = "hybrid_chunk_attn_hgrn2_moe_stream_step"

F32 = jnp.float32


def rms_norm(x, g):
    xf = x.astype(F32)
    y = xf * lax.rsqrt(jnp.mean(xf * xf, axis=-1, keepdims=True) + RMS_EPS)
    return (y * g.astype(F32)).astype(x.dtype)


def project(h, w):
    z = h @ w
    lead = h.shape[:-1]
    o = np.cumsum([0, ATT_WIDTH, ATT_WIDTH, ATT_WIDTH, HG_WIDTH, HG_WIDTH, HG_WIDTH, HG_WIDTH, D_MODEL, D_MODEL])
    heads = lambda a: a.reshape(*lead, ATT_HEADS, ATT_DIM)
    qa, ka, va = heads(z[..., o[0]:o[1]]), heads(z[..., o[1]:o[2]]), heads(z[..., o[2]:o[3]])
    qb, fb, ib, ogb = z[..., o[3]:o[4]], z[..., o[4]:o[5]], z[..., o[5]:o[6]], z[..., o[6]:o[7]]
    ga, gb = z[..., o[7]:o[8]], z[..., o[8]:o[9]]
    return qa, ka, va, qb, fb, ib, ogb, ga, gb


def rel_bias(table, d):
    return jnp.take(table, jnp.clip(d, -MAX_REL, MAX_REL) + MAX_REL, axis=1).astype(F32)


def chunk_attn_prompt(q, k, v, table):
    B, T = q.shape[:2]
    nc = T // CHUNK
    band = LEFT_CHUNKS + 1
    pad = ((0, 0), (LEFT_CHUNKS * CHUNK, 0), (0, 0), (0, 0))
    kp = jnp.pad(k, pad).reshape(B, nc + LEFT_CHUNKS, CHUNK, ATT_HEADS, ATT_DIM)
    vp = jnp.pad(v, pad).reshape(B, nc + LEFT_CHUNKS, CHUNK, ATT_HEADS, ATT_DIM)
    kb = jnp.concatenate([kp[:, j:j + nc] for j in range(band)], axis=2)
    vb = jnp.concatenate([vp[:, j:j + nc] for j in range(band)], axis=2)
    qc = q.reshape(B, nc, CHUNK, ATT_HEADS, ATT_DIM)
    slot = jnp.arange(band * CHUNK)
    j, ki = slot // CHUNK, slot % CHUNK
    qi = jnp.arange(CHUNK)
    d = (LEFT_CHUNKS - j)[None, :] * CHUNK + qi[:, None] - ki[None, :]
    bias = rel_bias(table, d)
    valid = (jnp.arange(nc)[:, None] - LEFT_CHUNKS + j[None, :]) >= 0
    s = jnp.einsum('bcqhd,bckhd->bchqk', qc, kb, preferred_element_type=F32) * (ATT_DIM ** -0.5)
    s = jnp.where(valid[None, :, None, None, :], s + bias[None, None], -jnp.inf)
    p = jax.nn.softmax(s, axis=-1).astype(v.dtype)
    o = jnp.einsum('bchqk,bckhd->bcqhd', p, vb)
    return o.reshape(B, T, ATT_WIDTH)


def chunk_attn_sample(q, k, v, ck, cv, table):
    B, S = q.shape[:2]
    cw = ck.shape[1]
    kk = jnp.concatenate([ck.astype(k.dtype), k], axis=1)
    vv = jnp.concatenate([cv.astype(v.dtype), v], axis=1)
    kpos = jnp.concatenate([jnp.arange(cw) - cw, jnp.arange(S)])
    d = jnp.arange(S)[:, None] - kpos[None, :]
    s = jnp.einsum('bqhd,bkhd->bhqk', q, kk, preferred_element_type=F32) * (ATT_DIM ** -0.5)
    p = jax.nn.softmax(s + rel_bias(table, d)[None], axis=-1).astype(v.dtype)
    return jnp.einsum('bhqk,bkhd->bqhd', p, vv).reshape(B, S, ATT_WIDTH)


def hgrn2_recurrence(q, log_f, k, v, s0):
    B, T = q.shape[:2]
    nb = -(-T // HG_BLOCK)
    tp = nb * HG_BLOCK
    pad = ((0, 0), (0, tp - T), (0, 0), (0, 0))

    def blocks(a):
        a = jnp.pad(a.astype(F32), pad)
        return a.reshape(B, nb, HG_BLOCK, a.shape[2], a.shape[3]).transpose(1, 0, 2, 3, 4)

    causal = (jnp.arange(HG_BLOCK)[:, None] >= jnp.arange(HG_BLOCK)[None, :])[None, :, :, None, None]

    def step(S, blk):
        qb, lfb, kb, vb = blk
        b = jnp.cumsum(lfb, axis=1)
        inter = jnp.einsum('bthd,bhde->bthe', qb * jnp.exp(b), S)
        diff = b[:, :, None] - b[:, None, :]
        decay = jnp.where(causal, jnp.exp(jnp.where(causal, diff, 0.0)), 0.0)
        att = jnp.einsum('bthd,bshd,btshd->bhts', qb, kb, decay)
        intra = jnp.einsum('bhts,bshe->bthe', att, vb)
        b_last = b[:, -1]
        s_new = jnp.exp(b_last)[..., None] * S + jnp.einsum(
            'bshd,bshe->bhde', kb * jnp.exp(b_last[:, None] - b), vb)
        return s_new, inter + intra

    s_fin, o = lax.scan(step, s0.astype(F32), (blocks(q), blocks(log_f), blocks(k), blocks(v)))
    o = o.transpose(1, 0, 2, 3, 4).reshape(B, tp, HG_HEADS, HG_DV)[:, :T]
    return o, s_fin


def hgrn2_branch(qb, fb, ib, ogb, lower, g_on, s0):
    B, T = qb.shape[:2]
    f = lower + (1.0 - lower) * jax.nn.sigmoid(fb.astype(F32))
    heads = lambda a: a.reshape(B, T, HG_HEADS, -1)
    o, s_fin = hgrn2_recurrence(heads(qb), heads(jnp.log(f)), heads(1.0 - f),
                                heads(jax.nn.silu(ib.astype(F32))), s0)
    o = rms_norm(o, g_on) * jax.nn.silu(heads(ogb).astype(F32))
    return o.reshape(B, T, HG_HEADS * HG_DV).astype(qb.dtype), s_fin


def merge_branches(a, b, ga, gb, w_pa, w_pb, w_out):
    y = jax.nn.sigmoid(ga) * (a @ w_pa) + jax.nn.sigmoid(gb) * (b @ w_pb)
    return y @ w_out


def moe(h, w_router, b_router, w_gu, b_gu, w_down, b_down):
    logits = (h @ w_router).astype(F32) + b_router.astype(F32)
    top_v, top_i = lax.top_k(logits, TOP_K)
    probs = jax.nn.softmax(top_v, axis=-1)
    gates = jnp.sum(jax.nn.one_hot(top_i, N_EXPERTS, dtype=F32) * probs[..., None], axis=1)
    out = jnp.zeros((h.shape[0], D_MODEL), F32)
    for e in range(N_EXPERTS):
        gu = h @ w_gu[e] + b_gu[e]
        gate = jnp.minimum(gu[:, :D_FF], SWIGLU_LIMIT)
        up = jnp.clip(gu[:, D_FF:], -SWIGLU_LIMIT, SWIGLU_LIMIT)
        act = (up + 1.0) * gate * jax.nn.sigmoid(SWIGLU_ALPHA * gate)
        out = out + gates[:, e:e + 1] * (act @ w_down[e] + b_down[e]).astype(F32)
    return out.astype(h.dtype)


def setup_inputs(seed: int = 0) -> dict:
    key = jax.random.key(seed)
    ks = jax.random.split(key, 21)
    nrm = lambda k, shape, scale: jax.random.normal(k, shape, F32) * scale
    cw = min(WINDOW, PAST_LEN)
    return {
        "x_prompt": nrm(ks[0], (BATCH, SEQ, D_MODEL), 1.0),
        "x_sample": nrm(ks[1], (DEC_BATCH, DEC_SEQ, D_MODEL), 1.0),
        "cache_k": nrm(ks[2], (DEPTH, DEC_BATCH, cw, ATT_HEADS, ATT_DIM), 1.0),
        "cache_v": nrm(ks[3], (DEPTH, DEC_BATCH, cw, ATT_HEADS, ATT_DIM), 1.0),
        "state_s": nrm(ks[4], (DEPTH, DEC_BATCH, HG_HEADS, HG_DK, HG_DV), 0.5),
        "g_mix": 1.0 + nrm(ks[5], (DEPTH, D_MODEL), 0.05),
        "w_in": nrm(ks[6], (DEPTH, D_MODEL, IN_COLS), D_MODEL ** -0.5),
        "rel_bias": nrm(ks[7], (DEPTH, ATT_HEADS, 2 * MAX_REL + 1), 0.2),
        "lb_logits": nrm(ks[8], (DEPTH + 1, HG_WIDTH), 1.0),
        "g_out_norm": 1.0 + nrm(ks[9], (DEPTH, HG_DV), 0.05),
        "w_pa": nrm(ks[10], (DEPTH, ATT_WIDTH, D_MODEL), ATT_WIDTH ** -0.5),
        "w_pb": nrm(ks[11], (DEPTH, HG_HEADS * HG_DV, D_MODEL), (HG_HEADS * HG_DV) ** -0.5),
        "w_out": nrm(ks[12], (DEPTH, D_MODEL, D_MODEL), D_MODEL ** -0.5),
        "g_ffn": 1.0 + nrm(ks[13], (DEPTH, D_MODEL), 0.05),
        "w_router": nrm(ks[14], (DEPTH, D_MODEL, N_EXPERTS), D_MODEL ** -0.5),
        "b_router": nrm(ks[15], (DEPTH, N_EXPERTS), 0.01),
        "w_gu": nrm(ks[16], (DEPTH, N_EXPERTS, D_MODEL, 2 * D_FF), D_MODEL ** -0.5),
        "b_gu": nrm(ks[17], (DEPTH, N_EXPERTS, 2 * D_FF), 0.02),
        "w_down": nrm(ks[18], (DEPTH, N_EXPERTS, D_FF, D_MODEL), D_FF ** -0.5),
        "b_down": nrm(ks[19], (DEPTH, N_EXPERTS, D_MODEL), 0.02),
        "g_final": 1.0 + nrm(ks[20], (D_MODEL,), 0.05),
    }


def reference(x_prompt, x_sample, cache_k, cache_v, state_s, g_mix, w_in, rel_bias, lb_logits,
              g_out_norm, w_pa, w_pb, w_out, g_ffn, w_router, b_router, w_gu, b_gu, w_down,
              b_down, g_final):
    B, T = x_prompt.shape[:2]
    DB, S = x_sample.shape[:2]
    lower = jnp.cumsum(jax.nn.softmax(lb_logits.astype(F32), axis=0), axis=0)
    cwp = min(WINDOW, T)
    xp, xs = x_prompt, x_sample
    nkp, nvp, nsp, nks, nvs, nss = [], [], [], [], [], []
    for l in range(DEPTH):
        hp, hs = rms_norm(xp, g_mix[l]), rms_norm(xs, g_mix[l])
        qa, ka, va, qb, fb, ib, ogb, ga, gb = project(hp, w_in[l])
        att_p = chunk_attn_prompt(qa, ka, va, rel_bias[l])
        hg_p, s_p = hgrn2_branch(qb, fb, ib, ogb, lower[l], g_out_norm[l],
                                 jnp.zeros((B, HG_HEADS, HG_DK, HG_DV), F32))
        xp = xp + merge_branches(att_p, hg_p, ga, gb, w_pa[l], w_pb[l], w_out[l])
        nkp.append(ka[:, T - cwp:])
        nvp.append(va[:, T - cwp:])
        nsp.append(s_p.astype(x_prompt.dtype))

        qa, ka, va, qb, fb, ib, ogb, ga, gb = project(hs, w_in[l])
        att_s = chunk_attn_sample(qa, ka, va, cache_k[l], cache_v[l], rel_bias[l])
        hg_s, s_s = hgrn2_branch(qb, fb, ib, ogb, lower[l], g_out_norm[l], state_s[l])
        xs = xs + merge_branches(att_s, hg_s, ga, gb, w_pa[l], w_pb[l], w_out[l])
        nks.append(ka)
        nvs.append(va)
        nss.append(s_s.astype(state_s.dtype))

        h2 = jnp.concatenate([rms_norm(xp, g_ffn[l]).reshape(B * T, D_MODEL),
                              rms_norm(xs, g_ffn[l]).reshape(DB * S, D_MODEL)], axis=0)
        f2 = moe(h2, w_router[l], b_router[l], w_gu[l], b_gu[l], w_down[l], b_down[l])
        xp = xp + f2[:B * T].reshape(B, T, D_MODEL)
        xs = xs + f2[B * T:].reshape(DB, S, D_MODEL)
    y_prompt = rms_norm(xp, g_final)
    y_sample = rms_norm(xs, g_final)
    return (y_prompt, y_sample, jnp.stack(nkp), jnp.stack(nvp), jnp.stack(nsp),
            jnp.stack(nks), jnp.stack(nvs), jnp.stack(nss))
```

```python
import functools

import numpy as np
import jax
import jax.numpy as jnp
from jax import lax
from jax.experimental import pallas as pl
from jax.experimental.pallas import tpu as pltpu

F32 = jnp.float32
BF16 = jnp.bfloat16

D_MODEL = 1024
CHUNK = 64
LEFT_CHUNKS = 8
WINDOW = LEFT_CHUNKS * CHUNK
ATT_HEADS = 8
ATT_DIM = 64
ATT_WIDTH = ATT_HEADS * ATT_DIM
MAX_REL = 256
HG_HEADS = 4
HG_DK = 128
HG_WIDTH = HG_HEADS * HG_DK
N_EXPERTS = 32
TOP_K = 4
D_FF = D_MODEL
SWIGLU_LIMIT = 7.0
SWIGLU_ALPHA = 1.702
RMS_EPS = 1e-5

LANES = 128
NEG = -1e30
ATT_QBLK = 4 * CHUNK
ATT_KBLKS = LEFT_CHUNKS * CHUNK // ATT_QBLK + 1
HG_C = 64
VMEM_LIMIT = 56 * 1024 * 1024

NT = (((1,), (1,)), ((), ()))
TN = (((0,), (0,)), ((), ()))


def _rms(x, g):
    return x * lax.rsqrt(jnp.mean(x * x, axis=-1, keepdims=True) + RMS_EPS) * g


def _sigmoid(x):
    return 1.0 / (1.0 + jnp.exp(-x))


def _params(*sem):
    return pltpu.CompilerParams(dimension_semantics=sem, vmem_limit_bytes=VMEM_LIMIT)


def _inproj_body(x_ref, g_ref, w_ref, za_ref, zb_ref, zg_ref):
    h = _rms(x_ref[...], g_ref[...]).astype(BF16)
    a, b = 3 * ATT_WIDTH, 3 * ATT_WIDTH + 4 * HG_WIDTH
    za_ref[...] = jnp.dot(h, w_ref[:, :a], preferred_element_type=F32)
    zb_ref[...] = jnp.dot(h, w_ref[:, a:b], preferred_element_type=F32)
    zg_ref[...] = jnp.dot(h, w_ref[:, b:], preferred_element_type=F32)


def _inproj(x, g, w_bf16, tm=256):
    n = x.shape[0]
    cols = w_bf16.shape[1]
    wa, wb, wg = 3 * ATT_WIDTH, 4 * HG_WIDTH, 2 * D_MODEL
    return pl.pallas_call(
        _inproj_body,
        grid=(n // tm,),
        in_specs=[
            pl.BlockSpec((tm, D_MODEL), lambda i: (i, 0)),
            pl.BlockSpec((1, D_MODEL), lambda i: (0, 0)),
            pl.BlockSpec((D_MODEL, cols), lambda i: (0, 0)),
        ],
        out_specs=[
            pl.BlockSpec((tm, wa), lambda i: (i, 0)),
            pl.BlockSpec((tm, wb), lambda i: (i, 0)),
            pl.BlockSpec((tm, wg), lambda i: (i, 0)),
        ],
        out_shape=[
            jax.ShapeDtypeStruct((n, wa), F32),
            jax.ShapeDtypeStruct((n, wb), F32),
            jax.ShapeDtypeStruct((n, wg), F32),
        ],
        compiler_params=_params("arbitrary"),
        name="inproj",
    )(x, g, w_bf16)


def _attn_heads(q_ref, k_refs, v_refs, bias_fn, pens, o_ref):
    lane = lax.broadcasted_iota(jnp.int32, (1, LANES), 1)
    first = lane < ATT_DIM
    for hp in range(ATT_HEADS // 2):
        sl = slice(hp * LANES, (hp + 1) * LANES)
        q2 = q_ref[:, sl] * (ATT_DIM ** -0.5)
        ks = [k[:, sl].astype(BF16) for k in k_refs]
        vs = [v[:, sl].astype(BF16) for v in v_refs]
        outs = []
        for half in range(2):
            head = 2 * hp + half
            qm = jnp.where(first if half == 0 else lane >= ATT_DIM, q2, 0.0).astype(BF16)
            ss = []
            for j, kj in enumerate(ks):
                s = lax.dot_general(qm, kj, NT, preferred_element_type=F32) + bias_fn(head, j)
                if pens[j] is not None:
                    s = s + pens[j]
                ss.append(s)
            m = functools.reduce(jnp.maximum, [jnp.max(s, axis=-1, keepdims=True) for s in ss])
            ps = [jnp.exp(s - m) for s in ss]
            l = functools.reduce(jnp.add, [jnp.sum(p, axis=-1, keepdims=True) for p in ps])
            o = functools.reduce(jnp.add, [jnp.dot(p.astype(BF16), vj, preferred_element_type=F32)
                                           for p, vj in zip(ps, vs)])
            outs.append(o * (1.0 / l))
        o_ref[:, sl] = jnp.where(first, outs[0], outs[1])


def _attn_prompt_body(q_ref, k0, k1, k2, v0, v1, v2, bias_ref, o_ref):
    i = pl.program_id(1)
    pens = [jnp.where(i - (ATT_KBLKS - 1) + j >= 0, 0.0, NEG) for j in range(ATT_KBLKS - 1)] + [None]
    bias_fn = lambda h, j: bias_ref[h, :, j * ATT_QBLK:(j + 1) * ATT_QBLK]
    _attn_heads(q_ref, [k0, k1, k2], [v0, v1, v2], bias_fn, pens, o_ref)


def _attn_prompt(za, bias, batch, seq):
    nq = seq // ATT_QBLK
    back = ATT_KBLKS - 1
    qspec = pl.BlockSpec((ATT_QBLK, ATT_WIDTH), lambda b, i: (b * nq + i, 0))

    def kvspec(j, col):
        return pl.BlockSpec((ATT_QBLK, ATT_WIDTH),
                            lambda b, i: (b * nq + jnp.maximum(i - back + j, 0), col))

    return pl.pallas_call(
        _attn_prompt_body,
        grid=(batch, nq),
        in_specs=[qspec] + [kvspec(j, 1) for j in range(ATT_KBLKS)] + [kvspec(j, 2) for j in range(ATT_KBLKS)]
        + [pl.BlockSpec(bias.shape, lambda b, i: (0, 0, 0))],
        out_specs=pl.BlockSpec((ATT_QBLK, ATT_WIDTH), lambda b, i: (b * nq + i, 0)),
        out_shape=jax.ShapeDtypeStruct((batch * seq, ATT_WIDTH), F32),
        compiler_params=_params("arbitrary", "arbitrary"),
        name="attn_prompt",
    )(za, za, za, za, za, za, za, bias)


def _attn_sample_body(q_ref, kn_ref, vn_ref, ck_ref, cv_ref, bias_ref, o_ref):
    cw = ck_ref.shape[0]
    bias_fn = lambda h, j: bias_ref[h, :, :cw] if j == 0 else bias_ref[h, :, cw:]
    _attn_heads(q_ref, [ck_ref, kn_ref], [cv_ref, vn_ref], bias_fn, [None, None], o_ref)


def _attn_sample(za, ck, cv, bias, batch, seq):
    cw = ck.shape[1]
    return pl.pallas_call(
        _attn_sample_body,
        grid=(batch,),
        in_specs=[
            pl.BlockSpec((seq, ATT_WIDTH), lambda b: (b, 0)),
            pl.BlockSpec((seq, ATT_WIDTH), lambda b: (b, 1)),
            pl.BlockSpec((seq, ATT_WIDTH), lambda b: (b, 2)),
            pl.BlockSpec((None, cw, ATT_WIDTH), lambda b: (b, 0, 0)),
            pl.BlockSpec((None, cw, ATT_WIDTH), lambda b: (b, 0, 0)),
            pl.BlockSpec(bias.shape, lambda b: (0, 0, 0)),
        ],
        out_specs=pl.BlockSpec((seq, ATT_WIDTH), lambda b: (b, 0)),
        out_shape=jax.ShapeDtypeStruct((batch * seq, ATT_WIDTH), F32),
        compiler_params=_params("arbitrary"),
        name="attn_sample",
    )(za, za, za, ck, cv, bias)


def _rel_bias_block(table, nq, nk):
    r = np.arange(nq)[:, None]
    s = np.arange(nk)[None, :]
    off = nk - nq
    d = np.clip(r + off - s, -MAX_REL, MAX_REL) + MAX_REL
    qc = (r + off) // CHUNK
    kc = s // CHUNK
    band = (kc <= qc) & (kc >= qc - LEFT_CHUNKS)
    b = jnp.take(table.astype(F32), jnp.asarray(d), axis=1)
    return jnp.where(jnp.asarray(band)[None], b, NEG)


def _hgrn_consts(c):
    t = np.arange(c)[:, None]
    j = np.arange(c)[None, :]
    mats = [j <= t, j > t]
    masks = []
    m = c // 2
    while m >= 1:
        ref = (t // (2 * m)) * (2 * m) + m - 1
        second = (t % (2 * m)) >= m
        mats.append((second & (j > ref) & (j <= t)) | (~second & (j > t) & (j <= ref)))
        masks.append((t // (2 * m)) == (j // (2 * m)))
        m //= 2
    return (jnp.asarray(np.concatenate(mats, 0).astype(np.float32), BF16),
            jnp.asarray(np.stack(masks).astype(np.float32)))


def _hgrn_body(zb_ref, s0_ref, lower_ref, gon_ref, p_ref, mask_ref, o_ref, sfin_ref, st_ref):
    c = zb_ref.shape[0]
    step = pl.program_id(1)

    @pl.when(step == 0)
    def _():
        for h in range(HG_HEADS):
            st_ref[h] = s0_ref[0, h].T

    pmat = p_ref[...]
    row = lax.broadcasted_iota(jnp.int32, (c, HG_DK), 0)
    n_levels = mask_ref.shape[0]
    for h in range(HG_HEADS):
        sl = lambda part: slice(part * HG_WIDTH + h * HG_DK, part * HG_WIDTH + (h + 1) * HG_DK)
        q = zb_ref[:, sl(0)]
        low = lower_ref[:, h * HG_DK:(h + 1) * HG_DK]
        f = low + (1.0 - low) * _sigmoid(zb_ref[:, sl(1)])
        lf = jnp.log(f)
        k = 1.0 - f
        ib = zb_ref[:, sl(2)]
        v = ib * _sigmoid(ib)
        og = zb_ref[:, sl(3)]

        hi = lf.astype(BF16)
        r1 = lf - hi.astype(F32)
        mid = r1.astype(BF16)
        lo = (r1 - mid.astype(F32)).astype(BF16)
        e = (jnp.dot(pmat, hi, preferred_element_type=F32) + jnp.dot(pmat, mid, preferred_element_type=F32)
             + jnp.dot(pmat, lo, preferred_element_type=F32))
        b = e[0:c]
        b_last = e[c - 1:c]
        st = st_ref[h]
        vb = v.astype(BF16)

        inter = lax.dot_general((q * jnp.exp(b)).astype(BF16), st.astype(BF16), NT, preferred_element_type=F32)
        att = None
        for lvl in range(n_levels):
            m = c >> (lvl + 1)
            x = jnp.exp(e[(2 + lvl) * c:(3 + lvl) * c])
            second = (row & m) != 0
            qm = jnp.where(second, q * x, 0.0).astype(BF16)
            km = jnp.where(second, 0.0, k * x).astype(BF16)
            a = lax.dot_general(qm, km, NT, preferred_element_type=F32)
            if lvl > 0:
                a = a * mask_ref[lvl]
            att = a if att is None else att + a
        intra = jnp.dot(att.astype(BF16), vb, preferred_element_type=F32)
        intra = intra + jnp.sum(q * k, axis=-1, keepdims=True) * v
        o = inter + intra

        kt = (k * jnp.exp(e[c:2 * c])).astype(BF16)
        st_ref[h] = st * jnp.exp(b_last) + lax.dot_general(vb, kt, TN, preferred_element_type=F32)

        o = _rms(o, gon_ref[...]) * (og * _sigmoid(og))
        o_ref[:, h * HG_DK:(h + 1) * HG_DK] = o

    @pl.when(step == pl.num_programs(1) - 1)
    def _():
        for h in range(HG_HEADS):
            sfin_ref[0, h] = st_ref[h].T


def _hgrn(zb, s0, lower, g_on, pmat, masks, batch, seq):
    c = HG_C
    nc = seq // c
    return pl.pallas_call(
        _hgrn_body,
        grid=(batch, nc),
        in_specs=[
            pl.BlockSpec((c, 4 * HG_WIDTH), lambda b, i: (b * nc + i, 0)),
            pl.BlockSpec((1, HG_HEADS, HG_DK, HG_DK), lambda b, i: (b, 0, 0, 0)),
            pl.BlockSpec((1, HG_WIDTH), lambda b, i: (0, 0)),
            pl.BlockSpec((1, HG_DK), lambda b, i: (0, 0)),
            pl.BlockSpec(pmat.shape, lambda b, i: (0, 0)),
            pl.BlockSpec(masks.shape, lambda b, i: (0, 0, 0)),
        ],
        out_specs=[
            pl.BlockSpec((c, HG_WIDTH), lambda b, i: (b * nc + i, 0)),
            pl.BlockSpec((1, HG_HEADS, HG_DK, HG_DK), lambda b, i: (b, 0, 0, 0)),
        ],
        out_shape=[
            jax.ShapeDtypeStruct((batch * seq, HG_WIDTH), F32),
            jax.ShapeDtypeStruct((batch, HG_HEADS, HG_DK, HG_DK), F32),
        ],
        scratch_shapes=[pltpu.VMEM((HG_HEADS, HG_DK, HG_DK), F32)],
        compiler_params=_params("arbitrary", "arbitrary"),
        name="hgrn2",
    )(zb, s0, lower, g_on, pmat, masks)


def _split_bf16(x):
    hi = x.astype(BF16)
    return hi, (x - hi.astype(F32)).astype(BF16)


def _merge_body(att_ref, hg_ref, zg_ref, x_ref, wpa_ref, wpb_ref, wout_ref, gffn_ref, wr_ref, br_ref,
                x1_ref, h2_ref, gates_ref):
    pa = jnp.dot(att_ref[...].astype(BF16), wpa_ref[...], preferred_element_type=F32)
    pb = jnp.dot(hg_ref[...].astype(BF16), wpb_ref[...], preferred_element_type=F32)
    y = _sigmoid(zg_ref[:, :D_MODEL]) * pa + _sigmoid(zg_ref[:, D_MODEL:]) * pb
    x1 = x_ref[...] + jnp.dot(y.astype(BF16), wout_ref[...], preferred_element_type=F32)
    x1_ref[...] = x1
    h2 = _rms(x1, gffn_ref[...])
    h2_ref[...] = h2.astype(BF16)

    h_hi, h_lo = _split_bf16(h2)
    w_hi, w_lo = _split_bf16(wr_ref[...])
    logits = (jnp.dot(h_hi, w_hi, preferred_element_type=F32) + jnp.dot(h_lo, w_hi, preferred_element_type=F32)
              + jnp.dot(h_hi, w_lo, preferred_element_type=F32)) + br_ref[...]
    lane = lax.broadcasted_iota(jnp.int32, logits.shape, 1).astype(F32)
    cur = logits
    vals, idxs = [], []
    for _ in range(TOP_K):
        m = jnp.max(cur, axis=-1, keepdims=True)
        idx = jnp.min(jnp.where(cur == m, lane, float(N_EXPERTS)), axis=-1, keepdims=True)
        vals.append(m)
        idxs.append(idx)
        cur = jnp.where(lane == idx, -jnp.inf, cur)
    es = [jnp.exp(v - vals[0]) for v in vals]
    inv = 1.0 / functools.reduce(jnp.add, es)
    gates = jnp.zeros_like(logits)
    for ex, idx in zip(es, idxs):
        gates = gates + jnp.where(lane == idx, ex * inv, 0.0)
    gates_ref[...] = gates


def _merge(att, hg, zg, x, wpa, wpb, wout, g_ffn, w_router, b_router, tm=256):
    n = x.shape[0]
    row = lambda w: pl.BlockSpec((tm, w), lambda i: (i, 0))
    full = lambda a: pl.BlockSpec(a.shape, lambda i: (0,) * a.ndim)
    return pl.pallas_call(
        _merge_body,
        grid=(n // tm,),
        in_specs=[row(ATT_WIDTH), row(HG_WIDTH), row(2 * D_MODEL), row(D_MODEL),
                  full(wpa), full(wpb), full(wout), full(g_ffn), full(w_router), full(b_router)],
        out_specs=[row(D_MODEL), row(D_MODEL), row(N_EXPERTS)],
        out_shape=[
            jax.ShapeDtypeStruct((n, D_MODEL), F32),
            jax.ShapeDtypeStruct((n, D_MODEL), BF16),
            jax.ShapeDtypeStruct((n, N_EXPERTS), F32),
        ],
        compiler_params=_params("arbitrary"),
        name="merge_router",
    )(att, hg, zg, x, wpa, wpb, wout, g_ffn, w_router, b_router)


MOE_SUB = 256


def _moe_body(h2_ref, gates_ref, x1_ref, wgu_ref, bgu_ref, wd_ref, bd_ref, gfin_ref, y_ref, acc_ref):
    e = pl.program_id(1)
    tm = h2_ref.shape[0]

    @pl.when(e == 0)
    def _():
        acc_ref[...] = jnp.zeros_like(acc_ref)

    wg = wgu_ref[...].astype(BF16)
    wd = wd_ref[...].astype(BF16)
    lane = lax.broadcasted_iota(jnp.int32, (1, N_EXPERTS), 1)
    for r in range(tm // MOE_SUB):
        rows = pl.ds(r * MOE_SUB, MOE_SUB)
        gu = jnp.dot(h2_ref[rows, :], wg, preferred_element_type=F32) + bgu_ref[...]
        gate = jnp.minimum(gu[:, :D_FF], SWIGLU_LIMIT)
        up = jnp.clip(gu[:, D_FF:], -SWIGLU_LIMIT, SWIGLU_LIMIT)
        act = (up + 1.0) * gate * _sigmoid(SWIGLU_ALPHA * gate)
        out = jnp.dot(act.astype(BF16), wd, preferred_element_type=F32) + bd_ref[...]
        g = jnp.sum(jnp.where(lane == e, gates_ref[rows, :], 0.0), axis=-1, keepdims=True)
        acc_ref[rows, :] += g * out

    @pl.when(e == pl.num_programs(1) - 1)
    def _():
        y_ref[...] = _rms(x1_ref[...] + acc_ref[...], gfin_ref[...])


def _moe(h2, gates, x1, w_gu, b_gu, w_down, b_down, g_final, tm=512):
    n = h2.shape[0]
    row = lambda w: pl.BlockSpec((tm, w), lambda i, e: (i, 0))
    return pl.pallas_call(
        _moe_body,
        grid=(n // tm, N_EXPERTS),
        in_specs=[
            row(D_MODEL), row(N_EXPERTS), row(D_MODEL),
            pl.BlockSpec((None, D_MODEL, 2 * D_FF), lambda i, e: (e, 0, 0)),
            pl.BlockSpec((None, 1, 2 * D_FF), lambda i, e: (e, 0, 0)),
            pl.BlockSpec((None, D_FF, D_MODEL), lambda i, e: (e, 0, 0)),
            pl.BlockSpec((None, 1, D_MODEL), lambda i, e: (e, 0, 0)),
            pl.BlockSpec((1, D_MODEL), lambda i, e: (0, 0)),
        ],
        out_specs=row(D_MODEL),
        out_shape=jax.ShapeDtypeStruct((n, D_MODEL), F32),
        scratch_shapes=[pltpu.VMEM((tm, D_MODEL), F32)],
        compiler_params=_params("arbitrary", "arbitrary"),
        name="moe_dense",
    )(h2, gates, x1, w_gu, b_gu, w_down, b_down, g_final)


def kernel(x_prompt, x_sample, cache_k, cache_v, state_s, g_mix, w_in, rel_bias, lb_logits, g_out_norm,
           w_pa, w_pb, w_out, g_ffn, w_router, b_router, w_gu, b_gu, w_down, b_down, g_final):
    B, T = x_prompt.shape[:2]
    DB, S = x_sample.shape[:2]
    depth = w_in.shape[0]
    assert depth == 1 and T % ATT_QBLK == 0 and T % HG_C == 0 and S == CHUNK and S % HG_C == 0
    cw = cache_k.shape[2]
    assert cw == WINDOW
    l = 0

    lower = jnp.cumsum(jax.nn.softmax(lb_logits.astype(F32), axis=0), axis=0)[l].reshape(1, HG_WIDTH)
    w_in_b = w_in[l].astype(BF16)
    wpa, wpb, wout = w_pa[l].astype(BF16), w_pb[l].astype(BF16), w_out[l].astype(BF16)
    row = lambda a: a.reshape(1, -1).astype(F32)
    bias_p = _rel_bias_block(rel_bias[l], ATT_QBLK, ATT_KBLKS * ATT_QBLK)
    bias_s = _rel_bias_block(rel_bias[l], S, cw + S)
    pmat, masks = _hgrn_consts(HG_C)
    b_gu3 = b_gu[l].reshape(N_EXPERTS, 1, 2 * D_FF)
    b_down3 = b_down[l].reshape(N_EXPERTS, 1, D_MODEL)

    def group(x, batch, seq, s0, attend):
        xf = x.reshape(batch * seq, D_MODEL)
        za, zb, zg = _inproj(xf, row(g_mix[l]), w_in_b)
        att = attend(za)
        hg, s_fin = _hgrn(zb, s0, lower, row(g_out_norm[l]), pmat, masks, batch, seq)
        x1, h2, gates = _merge(att, hg, zg, xf, wpa, wpb, wout, row(g_ffn[l]), w_router[l], row(b_router[l]))
        y = _moe(h2, gates, x1, w_gu[l], b_gu3, w_down[l], b_down3, row(g_final))
        za3 = za.reshape(batch, seq, 3 * ATT_WIDTH)
        heads = lambda a: a.reshape(1, batch, a.shape[1], ATT_HEADS, ATT_DIM)
        keep = min(WINDOW, seq)
        nk = heads(za3[:, seq - keep:, ATT_WIDTH:2 * ATT_WIDTH])
        nv = heads(za3[:, seq - keep:, 2 * ATT_WIDTH:])
        return y.reshape(batch, seq, D_MODEL), nk, nv, s_fin[None]

    ck = cache_k[l].reshape(DB, cw, ATT_WIDTH)
    cv = cache_v[l].reshape(DB, cw, ATT_WIDTH)
    yp, nkp, nvp, nsp = group(x_prompt, B, T, jnp.zeros((B, HG_HEADS, HG_DK, HG_DK), F32),
                              lambda za: _attn_prompt(za, bias_p, B, T))
    ys, nks, nvs, nss = group(x_sample, DB, S, state_s[l].astype(F32),
                              lambda za: _attn_sample(za, ck, cv, bias_s, DB, S))
    return (yp, ys, nkp, nvp, nsp, nks, nvs, nss)
```

```python
import functools

import numpy as np
import jax
import jax.numpy as jnp
from jax import lax
from jax.experimental import pallas as pl
from jax.experimental.pallas import tpu as pltpu

F32 = jnp.float32
BF16 = jnp.bfloat16

D_MODEL = 1024
CHUNK = 64
LEFT_CHUNKS = 8
WINDOW = LEFT_CHUNKS * CHUNK
ATT_HEADS = 8
ATT_DIM = 64
ATT_WIDTH = ATT_HEADS * ATT_DIM
MAX_REL = 256
HG_HEADS = 4
HG_DK = 128
HG_WIDTH = HG_HEADS * HG_DK
N_EXPERTS = 32
TOP_K = 4
D_FF = D_MODEL
SWIGLU_LIMIT = 7.0
SWIGLU_ALPHA = 1.702
RMS_EPS = 1e-5

LANES = 128
NEG = -1e30
ATT_QBLK = 4 * CHUNK
ATT_KBLKS = LEFT_CHUNKS * CHUNK // ATT_QBLK + 1
HG_C = 64
VMEM_LIMIT = 56 * 1024 * 1024
BIAS_W = 1024
SUBLANES = 8
TBK = 256
ROW_CH = SUBLANES
RB = TBK * TOP_K + N_EXPERTS * ROW_CH
EXP_TM = 256
CH_PER_TILE = EXP_TM // ROW_CH
ROUTE_IDX = 64
ROUTE_P = 72

NT = (((1,), (1,)), ((), ()))
TN = (((0,), (0,)), ((), ()))


def _rms(x, g):
    return x * lax.rsqrt(jnp.mean(x * x, axis=-1, keepdims=True) + RMS_EPS) * g


def _sigmoid(x):
    return 1.0 / (1.0 + jnp.exp(-x))


def _params(*sem):
    return pltpu.CompilerParams(dimension_semantics=sem, vmem_limit_bytes=VMEM_LIMIT)


def _inproj_body(x_ref, g_ref, w_ref, za_ref, zb_ref, zg_ref):
    h = _rms(x_ref[...], g_ref[...]).astype(BF16)
    a, b = 3 * ATT_WIDTH, 3 * ATT_WIDTH + 4 * HG_WIDTH
    za_ref[...] = jnp.dot(h, w_ref[:, :a], preferred_element_type=F32)
    zb_ref[...] = jnp.dot(h, w_ref[:, a:b], preferred_element_type=F32)
    zg_ref[...] = jnp.dot(h, w_ref[:, b:], preferred_element_type=F32)


def _inproj(x, g, w_bf16, tm=256):
    n = x.shape[0]
    cols = w_bf16.shape[1]
    wa, wb, wg = 3 * ATT_WIDTH, 4 * HG_WIDTH, 2 * D_MODEL
    return pl.pallas_call(
        _inproj_body,
        grid=(n // tm,),
        in_specs=[
            pl.BlockSpec((tm, D_MODEL), lambda i: (i, 0)),
            pl.BlockSpec((1, D_MODEL), lambda i: (0, 0)),
            pl.BlockSpec((D_MODEL, cols), lambda i: (0, 0)),
        ],
        out_specs=[
            pl.BlockSpec((tm, wa), lambda i: (i, 0)),
            pl.BlockSpec((tm, wb), lambda i: (i, 0)),
            pl.BlockSpec((tm, wg), lambda i: (i, 0)),
        ],
        out_shape=[
            jax.ShapeDtypeStruct((n, wa), F32),
            jax.ShapeDtypeStruct((n, wb), F32),
            jax.ShapeDtypeStruct((n, wg), F32),
        ],
        compiler_params=_params("arbitrary"),
        name="inproj",
    )(x, g, w_bf16)


def _attn_heads(q_ref, k_refs, v_refs, bias_fn, pens, o_ref):
    lane = lax.broadcasted_iota(jnp.int32, (1, LANES), 1)
    first = lane < ATT_DIM
    for hp in range(ATT_HEADS // 2):
        sl = slice(hp * LANES, (hp + 1) * LANES)
        q2 = q_ref[:, sl] * (ATT_DIM ** -0.5)
        ks = [k[:, sl].astype(BF16) for k in k_refs]
        vs = [v[:, sl].astype(BF16) for v in v_refs]
        outs = []
        for half in range(2):
            head = 2 * hp + half
            qm = jnp.where(first if half == 0 else lane >= ATT_DIM, q2, 0.0).astype(BF16)
            ss = []
            for j, kj in enumerate(ks):
                s = lax.dot_general(qm, kj, NT, preferred_element_type=F32) + bias_fn(head, j)
                if pens[j] is not None:
                    s = s + pens[j]
                ss.append(s)
            m = functools.reduce(jnp.maximum, [jnp.max(s, axis=-1, keepdims=True) for s in ss])
            ps = [jnp.exp(s - m) for s in ss]
            l = functools.reduce(jnp.add, [jnp.sum(p, axis=-1, keepdims=True) for p in ps])
            o = functools.reduce(jnp.add, [jnp.dot(p.astype(BF16), vj, preferred_element_type=F32)
                                           for p, vj in zip(ps, vs)])
            outs.append(o * (1.0 / l))
        o_ref[:, sl] = jnp.where(first, outs[0], outs[1])


def _fill_bias(base_ref, bias_ref, banded):
    nq, nk = bias_ref.shape[1:]
    if banded:
        r = lax.broadcasted_iota(jnp.int32, (nq, nk), 0)
        s = lax.broadcasted_iota(jnp.int32, (nq, nk), 1)
        qc = (r + WINDOW) // CHUNK
        kc = s // CHUNK
        pen = jnp.where(kc <= qc, jnp.where(kc >= qc - LEFT_CHUNKS, 0.0, NEG), NEG)
    for h in range(ATT_HEADS):
        rows = jnp.broadcast_to(base_ref[h:h + 1, :], (nq, BIAS_W))
        t = pltpu.roll(rows, 0, 1, stride=1, stride_axis=0)[:, :nk]
        bias_ref[h] = t + pen if banded else t


def _attn_prompt_body(q_ref, k0, k1, k2, v0, v1, v2, base_ref, o_ref, bias_ref):
    i = pl.program_id(1)

    @pl.when((pl.program_id(0) == 0) & (i == 0))
    def _():
        _fill_bias(base_ref, bias_ref, True)

    pens = [jnp.where(i - (ATT_KBLKS - 1) + j >= 0, 0.0, NEG) for j in range(ATT_KBLKS - 1)] + [None]
    bias_fn = lambda h, j: bias_ref[h, :, j * ATT_QBLK:(j + 1) * ATT_QBLK]
    _attn_heads(q_ref, [k0, k1, k2], [v0, v1, v2], bias_fn, pens, o_ref)


def _attn_prompt(za, base, batch, seq):
    nq = seq // ATT_QBLK
    back = ATT_KBLKS - 1
    qspec = pl.BlockSpec((ATT_QBLK, ATT_WIDTH), lambda b, i: (b * nq + i, 0))

    def kvspec(j, col):
        return pl.BlockSpec((ATT_QBLK, ATT_WIDTH),
                            lambda b, i: (b * nq + jnp.maximum(i - back + j, 0), col))

    return pl.pallas_call(
        _attn_prompt_body,
        grid=(batch, nq),
        in_specs=[qspec] + [kvspec(j, 1) for j in range(ATT_KBLKS)] + [kvspec(j, 2) for j in range(ATT_KBLKS)]
        + [pl.BlockSpec(base.shape, lambda b, i: (0, 0))],
        out_specs=pl.BlockSpec((ATT_QBLK, ATT_WIDTH), lambda b, i: (b * nq + i, 0)),
        out_shape=jax.ShapeDtypeStruct((batch * seq, ATT_WIDTH), F32),
        scratch_shapes=[pltpu.VMEM((ATT_HEADS, ATT_QBLK, ATT_KBLKS * ATT_QBLK), F32)],
        compiler_params=_params("arbitrary", "arbitrary"),
        name="attn_prompt",
    )(za, za, za, za, za, za, za, base)


def _attn_sample_body(q_ref, kn_ref, vn_ref, ck_ref, cv_ref, base_ref, o_ref, bias_ref):
    @pl.when(pl.program_id(0) == 0)
    def _():
        _fill_bias(base_ref, bias_ref, False)

    cw = ck_ref.shape[0]
    bias_fn = lambda h, j: bias_ref[h, :, :cw] if j == 0 else bias_ref[h, :, cw:]
    _attn_heads(q_ref, [ck_ref, kn_ref], [cv_ref, vn_ref], bias_fn, [None, None], o_ref)


def _attn_sample(za, ck, cv, base, batch, seq):
    cw = ck.shape[1]
    return pl.pallas_call(
        _attn_sample_body,
        grid=(batch,),
        in_specs=[
            pl.BlockSpec((seq, ATT_WIDTH), lambda b: (b, 0)),
            pl.BlockSpec((seq, ATT_WIDTH), lambda b: (b, 1)),
            pl.BlockSpec((seq, ATT_WIDTH), lambda b: (b, 2)),
            pl.BlockSpec((None, cw, ATT_WIDTH), lambda b: (b, 0, 0)),
            pl.BlockSpec((None, cw, ATT_WIDTH), lambda b: (b, 0, 0)),
            pl.BlockSpec(base.shape, lambda b: (0, 0)),
        ],
        out_specs=pl.BlockSpec((seq, ATT_WIDTH), lambda b: (b, 0)),
        out_shape=jax.ShapeDtypeStruct((batch * seq, ATT_WIDTH), F32),
        scratch_shapes=[pltpu.VMEM((ATT_HEADS, seq, cw + seq), F32)],
        compiler_params=_params("arbitrary"),
        name="attn_sample",
    )(za, za, za, ck, cv, base)


def _rel_bias_base(table):
    top = table[:, 2 * MAX_REL:].astype(F32)
    rev = table[:, ::-1][:, :2 * MAX_REL].astype(F32)
    left = WINDOW - MAX_REL
    return jnp.concatenate([jnp.broadcast_to(top, (ATT_HEADS, left)), rev,
                            jnp.broadcast_to(top, (ATT_HEADS, BIAS_W - left - 2 * MAX_REL))], axis=1)


def _hgrn_consts(c):
    t = np.arange(c)[:, None]
    j = np.arange(c)[None, :]
    mats = [j <= t, j > t]
    masks = []
    m = c // 2
    while m >= 1:
        ref = (t // (2 * m)) * (2 * m) + m - 1
        second = (t % (2 * m)) >= m
        mats.append((second & (j > ref) & (j <= t)) | (~second & (j > t) & (j <= ref)))
        masks.append((t // (2 * m)) == (j // (2 * m)))
        m //= 2
    return (jnp.asarray(np.concatenate(mats, 0).astype(np.float32), BF16),
            jnp.asarray(np.stack(masks).astype(np.float32)))


def _hgrn_body(zb_ref, s0_ref, lower_ref, gon_ref, p_ref, mask_ref, o_ref, sfin_ref, st_ref, *, single_step):
    c = zb_ref.shape[0]
    step = pl.program_id(1)

    def load_state():
        for h in range(HG_HEADS):
            st_ref[h] = s0_ref[0, h].T

    if single_step:
        load_state()
    else:
        pl.when(step == 0)(load_state)

    pmat = p_ref[...]
    row = lax.broadcasted_iota(jnp.int32, (c, HG_DK), 0)
    n_levels = mask_ref.shape[0]
    for h in range(HG_HEADS):
        sl = lambda part: slice(part * HG_WIDTH + h * HG_DK, part * HG_WIDTH + (h + 1) * HG_DK)
        q = zb_ref[:, sl(0)]
        low = lower_ref[:, h * HG_DK:(h + 1) * HG_DK]
        f = low + (1.0 - low) * _sigmoid(zb_ref[:, sl(1)])
        lf = jnp.log(f)
        k = 1.0 - f
        ib = zb_ref[:, sl(2)]
        v = ib * _sigmoid(ib)
        og = zb_ref[:, sl(3)]

        hi = lf.astype(BF16)
        r1 = lf - hi.astype(F32)
        mid = r1.astype(BF16)
        lo = (r1 - mid.astype(F32)).astype(BF16)
        e = (jnp.dot(pmat, hi, preferred_element_type=F32) + jnp.dot(pmat, mid, preferred_element_type=F32)
             + jnp.dot(pmat, lo, preferred_element_type=F32))
        b = e[0:c]
        b_last = e[c - 1:c]
        st = st_ref[h]
        vb = v.astype(BF16)

        inter = lax.dot_general((q * jnp.exp(b)).astype(BF16), st.astype(BF16), NT, preferred_element_type=F32)
        att = None
        for lvl in range(n_levels):
            m = c >> (lvl + 1)
            x = jnp.exp(e[(2 + lvl) * c:(3 + lvl) * c])
            second = (row & m) != 0
            qm = jnp.where(second, q * x, 0.0).astype(BF16)
            km = jnp.where(second, 0.0, k * x).astype(BF16)
            a = lax.dot_general(qm, km, NT, preferred_element_type=F32)
            if lvl > 0:
                a = a * mask_ref[lvl]
            att = a if att is None else att + a
        intra = jnp.dot(att.astype(BF16), vb, preferred_element_type=F32)
        intra = intra + jnp.sum(q * k, axis=-1, keepdims=True) * v
        o = inter + intra

        kt = (k * jnp.exp(e[c:2 * c])).astype(BF16)
        st_ref[h] = st * jnp.exp(b_last) + lax.dot_general(vb, kt, TN, preferred_element_type=F32)

        o = _rms(o, gon_ref[...]) * (og * _sigmoid(og))
        o_ref[:, h * HG_DK:(h + 1) * HG_DK] = o

    def write_state():
        for h in range(HG_HEADS):
            sfin_ref[0, h] = st_ref[h].T

    if single_step:
        write_state()
    else:
        pl.when(step == pl.num_programs(1) - 1)(write_state)


def _hgrn(zb, s0, lower, g_on, pmat, masks, batch, seq):
    c = HG_C
    nc = seq // c
    return pl.pallas_call(
        functools.partial(_hgrn_body, single_step=nc == 1),
        grid=(batch, nc),
        in_specs=[
            pl.BlockSpec((c, 4 * HG_WIDTH), lambda b, i: (b * nc + i, 0)),
            pl.BlockSpec((1, HG_HEADS, HG_DK, HG_DK), lambda b, i: (b, 0, 0, 0)),
            pl.BlockSpec((1, HG_WIDTH), lambda b, i: (0, 0)),
            pl.BlockSpec((1, HG_DK), lambda b, i: (0, 0)),
            pl.BlockSpec(pmat.shape, lambda b, i: (0, 0)),
            pl.BlockSpec(masks.shape, lambda b, i: (0, 0, 0)),
        ],
        out_specs=[
            pl.BlockSpec((c, HG_WIDTH), lambda b, i: (b * nc + i, 0)),
            pl.BlockSpec((1, HG_HEADS, HG_DK, HG_DK), lambda b, i: (b, 0, 0, 0)),
        ],
        out_shape=[
            jax.ShapeDtypeStruct((batch * seq, HG_WIDTH), F32),
            jax.ShapeDtypeStruct((batch, HG_HEADS, HG_DK, HG_DK), F32),
        ],
        scratch_shapes=[pltpu.VMEM((HG_HEADS, HG_DK, HG_DK), F32)],
        compiler_params=_params("arbitrary", "arbitrary"),
        name="hgrn2",
    )(zb, s0, lower, g_on, pmat, masks)


def _split_bf16(x):
    hi = x.astype(BF16)
    return hi, (x - hi.astype(F32)).astype(BF16)


def _merge_body(att_ref, hg_ref, zg_ref, x_ref, wpa_ref, wpb_ref, wout_ref, gffn_ref, wr_ref, br_ref,
                x1_ref, h2_ref, route_ref, cnt_ref):
    pa = jnp.dot(att_ref[...].astype(BF16), wpa_ref[...], preferred_element_type=F32)
    pb = jnp.dot(hg_ref[...].astype(BF16), wpb_ref[...], preferred_element_type=F32)
    y = _sigmoid(zg_ref[:, :D_MODEL]) * pa + _sigmoid(zg_ref[:, D_MODEL:]) * pb
    x1 = x_ref[...] + jnp.dot(y.astype(BF16), wout_ref[...], preferred_element_type=F32)
    x1_ref[...] = x1
    h2 = _rms(x1, gffn_ref[...])
    h2_ref[...] = h2.astype(BF16)

    h_hi, h_lo = _split_bf16(h2)
    w_hi, w_lo = _split_bf16(wr_ref[...])
    logits = (jnp.dot(h_hi, w_hi, preferred_element_type=F32) + jnp.dot(h_lo, w_hi, preferred_element_type=F32)
              + jnp.dot(h_hi, w_lo, preferred_element_type=F32)) + br_ref[...]
    lane = lax.broadcasted_iota(jnp.int32, logits.shape, 1).astype(F32)
    cur = logits
    vals, idxs = [], []
    for _ in range(TOP_K):
        m = jnp.max(cur, axis=-1, keepdims=True)
        idx = jnp.min(jnp.where(cur == m, lane, float(LANES)), axis=-1, keepdims=True)
        vals.append(m)
        idxs.append(idx)
        cur = jnp.where(lane == idx, -jnp.inf, cur)
    es = [jnp.exp(v - vals[0]) for v in vals]
    inv = 1.0 / functools.reduce(jnp.add, es)
    route = jnp.zeros_like(logits)
    for k, (ex, idx) in enumerate(zip(es, idxs)):
        route = (route + jnp.where(lane == idx, 1.0, 0.0) + jnp.where(lane == float(ROUTE_IDX + k), idx, 0.0)
                 + jnp.where(lane == float(ROUTE_P + k), ex * inv, 0.0))
    route_ref[...] = route
    cnt_ref[...] = jnp.sum(jnp.where(lane < float(N_EXPERTS), route, 0.0), axis=0, keepdims=True)


def _merge(att, hg, zg, x, wpa, wpb, wout, g_ffn, w_router, b_router):
    n = x.shape[0]
    tm = TBK
    row = lambda w: pl.BlockSpec((tm, w), lambda i: (i, 0))
    full = lambda a: pl.BlockSpec(a.shape, lambda i: (0,) * a.ndim)
    return pl.pallas_call(
        _merge_body,
        grid=(n // tm,),
        in_specs=[row(ATT_WIDTH), row(HG_WIDTH), row(2 * D_MODEL), row(D_MODEL),
                  full(wpa), full(wpb), full(wout), full(g_ffn), full(w_router), full(b_router)],
        out_specs=[row(D_MODEL), row(D_MODEL), row(LANES), pl.BlockSpec((None, 1, LANES), lambda i: (i, 0, 0))],
        out_shape=[
            jax.ShapeDtypeStruct((n, D_MODEL), F32),
            jax.ShapeDtypeStruct((n, D_MODEL), BF16),
            jax.ShapeDtypeStruct((n, LANES), F32),
            jax.ShapeDtypeStruct((n // tm, 1, LANES), F32),
        ],
        compiler_params=_params("arbitrary"),
        name="merge_router",
    )(att, hg, zg, x, wpa, wpb, wout, g_ffn, w_router, b_router)


def _route_plan(cnt, nt_max):
    pc = (cnt + ROW_CH - 1) // ROW_CH
    loff = jnp.cumsum(pc, axis=1) - pc
    tot = jnp.sum(pc, axis=0)
    reg = (tot + CH_PER_TILE - 1) // CH_PER_TILE * CH_PER_TILE
    gstart = jnp.cumsum(reg) - reg
    goff = gstart[None, :] + jnp.cumsum(pc, axis=0) - pc
    ntiles = jnp.sum(reg) // CH_PER_TILE
    tile_end = jnp.cumsum(reg // CH_PER_TILE)
    t = jnp.minimum(jnp.arange(nt_max, dtype=jnp.int32), ntiles - 1)
    tile_expert = jnp.sum((tile_end[None, :] <= t[:, None]).astype(jnp.int32), axis=1)
    i32 = lambda a: a.astype(jnp.int32)
    return dict(pc=i32(pc), loff=i32(loff), goff=i32(goff), totc=i32(jnp.sum(pc, axis=1)),
                pad_start=i32(gstart + tot), pad_cnt=i32(reg - tot), ntiles=i32(ntiles).reshape(1),
                tile_expert=tile_expert)


def _chunk_rows(ref, chunk):
    return ref.at[pl.ds(pl.multiple_of(chunk * ROW_CH, ROW_CH), ROW_CH)]


def _dispatch_body(loff_s, pc_s, goff_s, totc_s, pads_s, padn_s, nt_s, h2_ref, route_ref, tri_ref, loffv_ref,
                   xs_hbm, buf, zbuf, sem):
    b = pl.program_id(0)
    nb = pl.num_programs(0)
    slot = lax.rem(b, 2)
    fill_sem, tile_sem = 2, 3
    nt_max = xs_hbm.shape[0] // EXP_TM

    def copy_out(src, gchunk, sem_i):
        return pltpu.make_async_copy(src, _chunk_rows(xs_hbm, gchunk), sem.at[sem_i])

    def zero_tile(t):
        return pltpu.make_async_copy(zbuf, xs_hbm.at[pl.ds(pl.multiple_of(t * EXP_TM, EXP_TM), EXP_TM)],
                                     sem.at[tile_sem])

    def wait_block(bb, slot_):
        def one(c, carry):
            copy_out(_chunk_rows(buf.at[slot_], 0), 0, slot_).wait()
            return carry
        lax.fori_loop(0, totc_s[bb], one, 0)

    @pl.when(b == 0)
    def _():
        zbuf[...] = jnp.zeros_like(zbuf)

    @pl.when(b >= 2)
    def _():
        wait_block(b - 2, slot)

    rt = route_ref[...].T
    rank_t = jnp.dot(rt[0:N_EXPERTS].astype(BF16), tri_ref[...], preferred_element_type=F32)
    lpos_t = loffv_ref[...] * float(ROW_CH) + rank_t
    erow = lax.broadcasted_iota(jnp.int32, (N_EXPERTS, TBK), 0).astype(F32)
    lposk = [jnp.sum(jnp.where(rt[ROUTE_IDX + k:ROUTE_IDX + k + 1] == erow, lpos_t, 0.0), axis=0, keepdims=True)
             for k in range(TOP_K)]
    h2 = h2_ref[...]
    for r0 in range(0, RB, EXP_TM):
        rrow = lax.broadcasted_iota(jnp.int32, (EXP_TM, TBK), 0).astype(F32) + float(r0)
        perm = functools.reduce(jnp.add, [jnp.where(lposk[k] == rrow, 1.0, 0.0) for k in range(TOP_K)])
        buf[slot, r0:r0 + EXP_TM, :] = jnp.dot(perm.astype(BF16), h2, preferred_element_type=F32)

    def send_expert(e, carry):
        base = b * N_EXPERTS + e
        lo, go = loff_s[base], goff_s[base]

        def one(c, carry2):
            copy_out(_chunk_rows(buf.at[slot], lo + c), go + c, slot).start()
            return carry2
        lax.fori_loop(0, pc_s[base], one, 0)
        return carry
    lax.fori_loop(0, N_EXPERTS, send_expert, 0)

    @pl.when(b == nb - 1)
    def _():
        zchunk = _chunk_rows(zbuf, 0)

        def fill_expert(e, carry):
            def one(c, carry2):
                copy_out(zchunk, pads_s[e] + c, fill_sem).start()
                return carry2
            lax.fori_loop(0, padn_s[e], one, 0)
            return carry
        lax.fori_loop(0, N_EXPERTS, fill_expert, 0)

        def fill_tile(t, carry):
            zero_tile(t).start()
            return carry
        lax.fori_loop(nt_s[0], nt_max, fill_tile, 0)

        @pl.when(b >= 1)
        def _():
            wait_block(b - 1, 1 - slot)
        wait_block(b, slot)

        def drain_expert(e, carry):
            def one(c, carry2):
                copy_out(zchunk, 0, fill_sem).wait()
                return carry2
            lax.fori_loop(0, padn_s[e], one, 0)
            return carry
        lax.fori_loop(0, N_EXPERTS, drain_expert, 0)

        def drain_tile(t, carry):
            zero_tile(0).wait()
            return carry
        lax.fori_loop(nt_s[0], nt_max, drain_tile, 0)


def _dispatch(plan, h2, route, loffv, n_rows):
    nb = h2.shape[0] // TBK
    tri = jnp.asarray(np.triu(np.ones((TBK, TBK), np.float32), 1), BF16)
    grid_spec = pltpu.PrefetchScalarGridSpec(
        num_scalar_prefetch=7,
        grid=(nb,),
        in_specs=[
            pl.BlockSpec((TBK, D_MODEL), lambda b, *_: (b, 0)),
            pl.BlockSpec((TBK, LANES), lambda b, *_: (b, 0)),
            pl.BlockSpec((TBK, TBK), lambda b, *_: (0, 0)),
            pl.BlockSpec((None, N_EXPERTS, 1), lambda b, *_: (b, 0, 0)),
        ],
        out_specs=pl.BlockSpec(memory_space=pl.ANY),
        scratch_shapes=[pltpu.VMEM((2, RB, D_MODEL), F32), pltpu.VMEM((EXP_TM, D_MODEL), F32),
                        pltpu.SemaphoreType.DMA((4,))],
    )
    return pl.pallas_call(
        _dispatch_body,
        grid_spec=grid_spec,
        out_shape=jax.ShapeDtypeStruct((n_rows, D_MODEL), F32),
        compiler_params=_params("arbitrary"),
        name="moe_dispatch",
    )(plan["loff"].reshape(-1), plan["pc"].reshape(-1), plan["goff"].reshape(-1), plan["totc"],
      plan["pad_start"], plan["pad_cnt"], plan["ntiles"], h2, route, tri, loffv)


def _experts_body(te_s, nt_s, x_ref, wgu_ref, bgu_ref, wd_ref, bd_ref, y_ref, wg_b, wd_b):
    t = pl.program_id(0)

    @pl.when((t == 0) | (te_s[t] != te_s[jnp.maximum(t - 1, 0)]))
    def _():
        wg_b[...] = wgu_ref[...].astype(BF16)
        wd_b[...] = wd_ref[...].astype(BF16)

    @pl.when(t < nt_s[0])
    def _():
        gu = jnp.dot(x_ref[...].astype(BF16), wg_b[...], preferred_element_type=F32) + bgu_ref[...]
        gate = jnp.minimum(gu[:, :D_FF], SWIGLU_LIMIT)
        up = jnp.clip(gu[:, D_FF:], -SWIGLU_LIMIT, SWIGLU_LIMIT)
        act = (up + 1.0) * gate * _sigmoid(SWIGLU_ALPHA * gate)
        y_ref[...] = jnp.dot(act.astype(BF16), wd_b[...], preferred_element_type=F32) + bd_ref[...]

    @pl.when(t >= nt_s[0])
    def _():
        y_ref[...] = jnp.zeros_like(y_ref)


def _experts(plan, xs, w_gu, b_gu, w_down, b_down):
    n_rows = xs.shape[0]
    tile = lambda t, te, nt: (jnp.minimum(t, nt[0] - 1), 0)
    out_tile = lambda t, te, nt: (t, 0)
    of_expert = lambda t, te, nt: (te[t], 0, 0)
    grid_spec = pltpu.PrefetchScalarGridSpec(
        num_scalar_prefetch=2,
        grid=(n_rows // EXP_TM,),
        in_specs=[
            pl.BlockSpec((EXP_TM, D_MODEL), tile),
            pl.BlockSpec((None, D_MODEL, 2 * D_FF), of_expert),
            pl.BlockSpec((None, 1, 2 * D_FF), of_expert),
            pl.BlockSpec((None, D_FF, D_MODEL), of_expert),
            pl.BlockSpec((None, 1, D_MODEL), of_expert),
        ],
        out_specs=pl.BlockSpec((EXP_TM, D_MODEL), out_tile),
        scratch_shapes=[pltpu.VMEM((D_MODEL, 2 * D_FF), BF16), pltpu.VMEM((D_FF, D_MODEL), BF16)],
    )
    return pl.pallas_call(
        _experts_body,
        grid_spec=grid_spec,
        out_shape=jax.ShapeDtypeStruct((n_rows, D_MODEL), F32),
        compiler_params=_params("arbitrary"),
        name="moe_experts",
    )(plan["tile_expert"], plan["ntiles"], xs, w_gu, b_gu, w_down, b_down)


def _combine_body(loff_s, pc_s, goff_s, totc_s, route_ref, tril_ref, loffrow_ref, x1_ref, gfin_ref, ys_hbm,
                  y_ref, buf, sem):
    b = pl.program_id(0)
    nb = pl.num_programs(0)
    slot = lax.rem(b, 2)

    def copy_in(slot_, lchunk, gchunk):
        return pltpu.make_async_copy(_chunk_rows(ys_hbm, gchunk), _chunk_rows(buf.at[slot_], lchunk), sem.at[slot_])

    def fetch_block(bb, slot_):
        def fetch_expert(e, carry):
            base = bb * N_EXPERTS + e
            lo, go = loff_s[base], goff_s[base]

            def one(c, carry2):
                copy_in(slot_, lo + c, go + c).start()
                return carry2
            lax.fori_loop(0, pc_s[base], one, 0)
            return carry
        lax.fori_loop(0, N_EXPERTS, fetch_expert, 0)

    def wait_block(bb, slot_):
        def one(c, carry):
            copy_in(slot_, 0, 0).wait()
            return carry
        lax.fori_loop(0, totc_s[bb], one, 0)

    @pl.when(b == 0)
    def _():
        buf[...] = jnp.zeros_like(buf)
        fetch_block(0, 0)

    @pl.when(b + 1 < nb)
    def _():
        fetch_block(b + 1, 1 - slot)

    wait_block(b, slot)

    route = route_ref[...]
    lane = lax.broadcasted_iota(jnp.int32, (1, LANES), 1).astype(F32)
    sel = jnp.where(lane < float(N_EXPERTS), route, 0.0).astype(BF16)
    rank = jnp.dot(tril_ref[...], sel, preferred_element_type=F32)
    lpos = loffrow_ref[...] * float(ROW_CH) + rank
    lposk, pk = [], []
    for k in range(TOP_K):
        idx = route[:, ROUTE_IDX + k:ROUTE_IDX + k + 1]
        lposk.append(jnp.sum(jnp.where(lane == idx, lpos, 0.0), axis=-1, keepdims=True))
        pk.append(route[:, ROUTE_P + k:ROUTE_P + k + 1])
    acc = x1_ref[...]
    for r0 in range(0, RB, EXP_TM):
        col = lax.broadcasted_iota(jnp.int32, (TBK, EXP_TM), 1).astype(F32) + float(r0)
        w = functools.reduce(jnp.add, [jnp.where(lposk[k] == col, pk[k], 0.0) for k in range(TOP_K)])
        acc = acc + jnp.dot(w.astype(BF16), buf[slot, r0:r0 + EXP_TM, :].astype(BF16), preferred_element_type=F32)
    y_ref[...] = _rms(acc, gfin_ref[...])


def _combine(plan, blocks, route, loffrow, x1, g_final, ys):
    b0, b1 = blocks
    nb = b1 - b0
    tril = jnp.asarray(np.tril(np.ones((TBK, TBK), np.float32), -1), BF16)
    flat = lambda a: a[b0:b1].reshape(-1)
    grid_spec = pltpu.PrefetchScalarGridSpec(
        num_scalar_prefetch=4,
        grid=(nb,),
        in_specs=[
            pl.BlockSpec((TBK, LANES), lambda b, *_: (b, 0)),
            pl.BlockSpec((TBK, TBK), lambda b, *_: (0, 0)),
            pl.BlockSpec((None, 1, LANES), lambda b, *_: (b, 0, 0)),
            pl.BlockSpec((TBK, D_MODEL), lambda b, *_: (b, 0)),
            pl.BlockSpec((1, D_MODEL), lambda b, *_: (0, 0)),
            pl.BlockSpec(memory_space=pl.ANY),
        ],
        out_specs=pl.BlockSpec((TBK, D_MODEL), lambda b, *_: (b, 0)),
        scratch_shapes=[pltpu.VMEM((2, RB, D_MODEL), F32), pltpu.SemaphoreType.DMA((2,))],
    )
    return pl.pallas_call(
        _combine_body,
        grid_spec=grid_spec,
        out_shape=jax.ShapeDtypeStruct((nb * TBK, D_MODEL), F32),
        compiler_params=_params("arbitrary"),
        name="moe_combine",
    )(flat(plan["loff"]), flat(plan["pc"]), flat(plan["goff"]), plan["totc"][b0:b1],
      route, tril, loffrow[b0:b1], x1, g_final, ys)


def kernel(x_prompt, x_sample, cache_k, cache_v, state_s, g_mix, w_in, rel_bias, lb_logits, g_out_norm,
           w_pa, w_pb, w_out, g_ffn, w_router, b_router, w_gu, b_gu, w_down, b_down, g_final):
    B, T = x_prompt.shape[:2]
    DB, S = x_sample.shape[:2]
    depth = w_in.shape[0]
    assert depth == 1 and T % ATT_QBLK == 0 and T % HG_C == 0 and S == CHUNK and S % HG_C == 0
    cw = cache_k.shape[2]
    assert cw == WINDOW
    l = 0

    lower = jnp.cumsum(jax.nn.softmax(lb_logits.astype(F32), axis=0), axis=0)[l].reshape(1, HG_WIDTH)
    w_in_b = w_in[l].astype(BF16)
    wpa, wpb, wout = w_pa[l].astype(BF16), w_pb[l].astype(BF16), w_out[l].astype(BF16)
    row = lambda a: a.reshape(1, -1).astype(F32)
    base = _rel_bias_base(rel_bias[l])
    pmat, masks = _hgrn_consts(HG_C)
    b_gu3 = b_gu[l].reshape(N_EXPERTS, 1, 2 * D_FF)
    b_down3 = b_down[l].reshape(N_EXPERTS, 1, D_MODEL)
    pad_e = LANES - N_EXPERTS
    wr = jnp.pad(w_router[l].astype(F32), ((0, 0), (0, pad_e)))
    br = jnp.concatenate([b_router[l].astype(F32), jnp.full((pad_e,), NEG, F32)]).reshape(1, LANES)

    def front(x, batch, seq, s0, attend):
        xf = x.reshape(batch * seq, D_MODEL)
        za, zb, zg = _inproj(xf, row(g_mix[l]), w_in_b)
        att = attend(za)
        hg, s_fin = _hgrn(zb, s0, lower, row(g_out_norm[l]), pmat, masks, batch, seq)
        x1, h2, route, cnt = _merge(att, hg, zg, xf, wpa, wpb, wout, row(g_ffn[l]), wr, br)
        za3 = za.reshape(batch, seq, 3 * ATT_WIDTH)
        heads = lambda a: a.reshape(1, batch, a.shape[1], ATT_HEADS, ATT_DIM)
        keep = min(WINDOW, seq)
        nk = heads(za3[:, seq - keep:, ATT_WIDTH:2 * ATT_WIDTH])
        nv = heads(za3[:, seq - keep:, 2 * ATT_WIDTH:])
        return dict(x1=x1, h2=h2, route=route, cnt=cnt, nk=nk, nv=nv, s=s_fin[None])

    ck = cache_k[l].reshape(DB, cw, ATT_WIDTH)
    cv = cache_v[l].reshape(DB, cw, ATT_WIDTH)
    fp = front(x_prompt, B, T, jnp.zeros((B, HG_HEADS, HG_DK, HG_DK), F32), lambda za: _attn_prompt(za, base, B, T))
    fs = front(x_sample, DB, S, state_s[l].astype(F32), lambda za: _attn_sample(za, ck, cv, base, DB, S))

    h2 = jnp.concatenate([fp["h2"], fs["h2"]], axis=0)
    route = jnp.concatenate([fp["route"], fs["route"]], axis=0)
    cnt = jnp.concatenate([fp["cnt"], fs["cnt"]], axis=0)[:, 0, :N_EXPERTS].astype(jnp.int32)
    n_tok = h2.shape[0]
    nb, nbp = n_tok // TBK, (B * T) // TBK
    max_rows = n_tok * TOP_K + nb * N_EXPERTS * (ROW_CH - 1) + N_EXPERTS * (EXP_TM - 1)
    nt_max = -(-max_rows // EXP_TM)
    plan = _route_plan(cnt, nt_max)
    loff_f = plan["loff"].astype(F32)
    xs = _dispatch(plan, h2, route, loff_f[:, :, None], nt_max * EXP_TM)
    ysort = _experts(plan, xs, w_gu[l], b_gu3, w_down[l], b_down3)
    loffrow = jnp.pad(loff_f, ((0, 0), (0, pad_e)))[:, None, :]
    yp = _combine(plan, (0, nbp), fp["route"], loffrow, fp["x1"], row(g_final), ysort)
    ys = _combine(plan, (nbp, nb), fs["route"], loffrow, fs["x1"], row(g_final), ysort)
    return (yp.reshape(B, T, D_MODEL), ys.reshape(DB, S, D_MODEL), fp["nk"], fp["nv"], fp["s"],
            fs["nk"], fs["nv"], fs["s"])
```

```python
import functools

import numpy as np
import jax
import jax.numpy as jnp
from jax import lax
from jax.experimental import pallas as pl
from jax.experimental.pallas import tpu as pltpu

F32 = jnp.float32
BF16 = jnp.bfloat16

D_MODEL = 1024
CHUNK = 64
LEFT_CHUNKS = 8
WINDOW = LEFT_CHUNKS * CHUNK
ATT_HEADS = 8
ATT_DIM = 64
ATT_WIDTH = ATT_HEADS * ATT_DIM
MAX_REL = 256
HG_HEADS = 4
HG_DK = 128
HG_WIDTH = HG_HEADS * HG_DK
N_EXPERTS = 32
TOP_K = 4
D_FF = D_MODEL
SWIGLU_LIMIT = 7.0
SWIGLU_ALPHA = 1.702
RMS_EPS = 1e-5

LANES = 128
NEG = -1e30
ATT_QBLK = 4 * CHUNK
ATT_KBLKS = LEFT_CHUNKS * CHUNK // ATT_QBLK + 1
HG_C = 128
VMEM_LIMIT = 56 * 1024 * 1024
BIAS_W = 1024
SUBLANES = 8
TBK = 256
ROW_CH = SUBLANES
RB = TBK * TOP_K + N_EXPERTS * ROW_CH
EXP_TM = 256
CH_PER_TILE = EXP_TM // ROW_CH
ROUTE_IDX = 64
ROUTE_P = 72

NT = (((1,), (1,)), ((), ()))
TN = (((0,), (0,)), ((), ()))


def _rms(x, g):
    return x * lax.rsqrt(jnp.mean(x * x, axis=-1, keepdims=True) + RMS_EPS) * g


def _sigmoid(x):
    return 1.0 / (1.0 + jnp.exp(-x))


def _params(*sem):
    return pltpu.CompilerParams(dimension_semantics=sem, vmem_limit_bytes=VMEM_LIMIT)


def _inproj_body(x_ref, g_ref, w_ref, za_ref, zb_ref, zg_ref):
    h = _rms(x_ref[...], g_ref[...]).astype(BF16)
    a, b = 3 * ATT_WIDTH, 3 * ATT_WIDTH + 4 * HG_WIDTH
    za_ref[...] = jnp.dot(h, w_ref[:, :a], preferred_element_type=F32)
    zb_ref[...] = jnp.dot(h, w_ref[:, a:b], preferred_element_type=F32)
    zg_ref[...] = jnp.dot(h, w_ref[:, b:], preferred_element_type=F32)


def _inproj(x, g, w_bf16, tm=256):
    n = x.shape[0]
    cols = w_bf16.shape[1]
    wa, wb, wg = 3 * ATT_WIDTH, 4 * HG_WIDTH, 2 * D_MODEL
    return pl.pallas_call(
        _inproj_body,
        grid=(n // tm,),
        in_specs=[
            pl.BlockSpec((tm, D_MODEL), lambda i: (i, 0)),
            pl.BlockSpec((1, D_MODEL), lambda i: (0, 0)),
            pl.BlockSpec((D_MODEL, cols), lambda i: (0, 0)),
        ],
        out_specs=[
            pl.BlockSpec((tm, wa), lambda i: (i, 0)),
            pl.BlockSpec((tm, wb), lambda i: (i, 0)),
            pl.BlockSpec((tm, wg), lambda i: (i, 0)),
        ],
        out_shape=[
            jax.ShapeDtypeStruct((n, wa), F32),
            jax.ShapeDtypeStruct((n, wb), F32),
            jax.ShapeDtypeStruct((n, wg), F32),
        ],
        compiler_params=_params("arbitrary"),
        name="inproj",
    )(x, g, w_bf16)


def _attn_heads(q_ref, k_refs, v_refs, bias_fn, pens, o_ref):
    lane = lax.broadcasted_iota(jnp.int32, (1, LANES), 1)
    first = lane < ATT_DIM
    for hp in range(ATT_HEADS // 2):
        sl = slice(hp * LANES, (hp + 1) * LANES)
        q2 = q_ref[:, sl] * (ATT_DIM ** -0.5)
        ks = [k[:, sl].astype(BF16) for k in k_refs]
        vs = [v[:, sl].astype(BF16) for v in v_refs]
        outs = []
        for half in range(2):
            head = 2 * hp + half
            qm = jnp.where(first if half == 0 else lane >= ATT_DIM, q2, 0.0).astype(BF16)
            ss = []
            for j, kj in enumerate(ks):
                s = lax.dot_general(qm, kj, NT, preferred_element_type=F32) + bias_fn(head, j)
                if pens[j] is not None:
                    s = s + pens[j]
                ss.append(s)
            m = functools.reduce(jnp.maximum, [jnp.max(s, axis=-1, keepdims=True) for s in ss])
            ps = [jnp.exp(s - m) for s in ss]
            l = functools.reduce(jnp.add, [jnp.sum(p, axis=-1, keepdims=True) for p in ps])
            o = functools.reduce(jnp.add, [jnp.dot(p.astype(BF16), vj, preferred_element_type=F32)
                                           for p, vj in zip(ps, vs)])
            outs.append(o * (1.0 / l))
        o_ref[:, sl] = jnp.where(first, outs[0], outs[1])


def _fill_bias(base_ref, bias_ref, banded):
    nq, nk = bias_ref.shape[1:]
    if banded:
        r = lax.broadcasted_iota(jnp.int32, (nq, nk), 0)
        s = lax.broadcasted_iota(jnp.int32, (nq, nk), 1)
        qc = (r + WINDOW) // CHUNK
        kc = s // CHUNK
        pen = jnp.where(kc <= qc, jnp.where(kc >= qc - LEFT_CHUNKS, 0.0, NEG), NEG)
    for h in range(ATT_HEADS):
        rows = jnp.broadcast_to(base_ref[h:h + 1, :], (nq, BIAS_W))
        t = pltpu.roll(rows, 0, 1, stride=1, stride_axis=0)[:, :nk]
        bias_ref[h] = t + pen if banded else t


def _attn_prompt_body(q_ref, k0, k1, k2, v0, v1, v2, base_ref, o_ref, bias_ref):
    i = pl.program_id(1)

    @pl.when((pl.program_id(0) == 0) & (i == 0))
    def _():
        _fill_bias(base_ref, bias_ref, True)

    pens = [jnp.where(i - (ATT_KBLKS - 1) + j >= 0, 0.0, NEG) for j in range(ATT_KBLKS - 1)] + [None]
    bias_fn = lambda h, j: bias_ref[h, :, j * ATT_QBLK:(j + 1) * ATT_QBLK]
    _attn_heads(q_ref, [k0, k1, k2], [v0, v1, v2], bias_fn, pens, o_ref)


def _attn_prompt(za, base, batch, seq):
    nq = seq // ATT_QBLK
    back = ATT_KBLKS - 1
    qspec = pl.BlockSpec((ATT_QBLK, ATT_WIDTH), lambda b, i: (b * nq + i, 0))

    def kvspec(j, col):
        return pl.BlockSpec((ATT_QBLK, ATT_WIDTH),
                            lambda b, i: (b * nq + jnp.maximum(i - back + j, 0), col))

    return pl.pallas_call(
        _attn_prompt_body,
        grid=(batch, nq),
        in_specs=[qspec] + [kvspec(j, 1) for j in range(ATT_KBLKS)] + [kvspec(j, 2) for j in range(ATT_KBLKS)]
        + [pl.BlockSpec(base.shape, lambda b, i: (0, 0))],
        out_specs=pl.BlockSpec((ATT_QBLK, ATT_WIDTH), lambda b, i: (b * nq + i, 0)),
        out_shape=jax.ShapeDtypeStruct((batch * seq, ATT_WIDTH), F32),
        scratch_shapes=[pltpu.VMEM((ATT_HEADS, ATT_QBLK, ATT_KBLKS * ATT_QBLK), F32)],
        compiler_params=_params("arbitrary", "arbitrary"),
        name="attn_prompt",
    )(za, za, za, za, za, za, za, base)


def _attn_sample_body(q_ref, kn_ref, vn_ref, ck_ref, cv_ref, base_ref, o_ref, bias_ref):
    @pl.when(pl.program_id(0) == 0)
    def _():
        _fill_bias(base_ref, bias_ref, False)

    cw = ck_ref.shape[0]
    bias_fn = lambda h, j: bias_ref[h, :, :cw] if j == 0 else bias_ref[h, :, cw:]
    _attn_heads(q_ref, [ck_ref, kn_ref], [cv_ref, vn_ref], bias_fn, [None, None], o_ref)


def _attn_sample(za, ck, cv, base, batch, seq):
    cw = ck.shape[1]
    return pl.pallas_call(
        _attn_sample_body,
        grid=(batch,),
        in_specs=[
            pl.BlockSpec((seq, ATT_WIDTH), lambda b: (b, 0)),
            pl.BlockSpec((seq, ATT_WIDTH), lambda b: (b, 1)),
            pl.BlockSpec((seq, ATT_WIDTH), lambda b: (b, 2)),
            pl.BlockSpec((None, cw, ATT_WIDTH), lambda b: (b, 0, 0)),
            pl.BlockSpec((None, cw, ATT_WIDTH), lambda b: (b, 0, 0)),
            pl.BlockSpec(base.shape, lambda b: (0, 0)),
        ],
        out_specs=pl.BlockSpec((seq, ATT_WIDTH), lambda b: (b, 0)),
        out_shape=jax.ShapeDtypeStruct((batch * seq, ATT_WIDTH), F32),
        scratch_shapes=[pltpu.VMEM((ATT_HEADS, seq, cw + seq), F32)],
        compiler_params=_params("arbitrary"),
        name="attn_sample",
    )(za, za, za, ck, cv, base)


def _rel_bias_base(table):
    top = table[:, 2 * MAX_REL:].astype(F32)
    rev = table[:, ::-1][:, :2 * MAX_REL].astype(F32)
    left = WINDOW - MAX_REL
    return jnp.concatenate([jnp.broadcast_to(top, (ATT_HEADS, left)), rev,
                            jnp.broadcast_to(top, (ATT_HEADS, BIAS_W - left - 2 * MAX_REL))], axis=1)


def _hgrn_consts(c):
    t = np.arange(c)[:, None]
    j = np.arange(c)[None, :]
    mats = [j <= t, j > t]
    masks = []
    m = c // 2
    while m >= 1:
        ref = (t // (2 * m)) * (2 * m) + m - 1
        second = (t % (2 * m)) >= m
        mats.append((second & (j > ref) & (j <= t)) | (~second & (j > t) & (j <= ref)))
        masks.append((t // (2 * m)) == (j // (2 * m)))
        m //= 2
    return (jnp.asarray(np.concatenate(mats, 0).astype(np.float32), BF16),
            jnp.asarray(np.stack(masks).astype(np.float32)))


def _hgrn_body(zb_ref, s0_ref, lower_ref, gon_ref, p_ref, mask_ref, o_ref, sfin_ref, st_ref, *, single_step):
    c = zb_ref.shape[0]
    step = pl.program_id(1)

    def load_state():
        for h in range(HG_HEADS):
            st_ref[h] = s0_ref[0, h].T

    if single_step:
        load_state()
    else:
        pl.when(step == 0)(load_state)

    pmat = p_ref[...]
    n_levels = mask_ref.shape[0]
    part = lambda i: zb_ref[:, i * HG_WIDTH:(i + 1) * HG_WIDTH]
    head = lambda a, h: a[:, h * HG_DK:(h + 1) * HG_DK]
    q = part(0)
    low = lower_ref[...]
    f = low + (1.0 - low) * _sigmoid(part(1))
    lf = jnp.log(f)
    k = 1.0 - f
    ib = part(2)
    v = ib * _sigmoid(ib)
    og = part(3)

    hi = lf.astype(BF16)
    r1 = lf - hi.astype(F32)
    mid = r1.astype(BF16)
    lo = (r1 - mid.astype(F32)).astype(BF16)
    e = (jnp.dot(pmat, hi, preferred_element_type=F32) + jnp.dot(pmat, mid, preferred_element_type=F32)
         + jnp.dot(pmat, lo, preferred_element_type=F32))
    b = e[0:c]
    decay = jnp.exp(e[c - 1:c])
    qe = (q * jnp.exp(b)).astype(BF16)
    kt = (k * jnp.exp(e[c:2 * c])).astype(BF16)
    vb = v.astype(BF16)
    qk = q * k
    gate = og * _sigmoid(og)

    row = lax.broadcasted_iota(jnp.int32, (c, HG_WIDTH), 0)
    att = [None] * HG_HEADS
    for lvl in range(n_levels):
        m = c >> (lvl + 1)
        x = jnp.exp(e[(2 + lvl) * c:(3 + lvl) * c])
        second = (row & m) != 0
        qm = jnp.where(second, q * x, 0.0).astype(BF16)
        km = jnp.where(second, 0.0, k * x).astype(BF16)
        for h in range(HG_HEADS):
            a = lax.dot_general(head(qm, h), head(km, h), NT, preferred_element_type=F32)
            if lvl > 0:
                a = a * mask_ref[lvl]
            att[h] = a if att[h] is None else att[h] + a

    for h in range(HG_HEADS):
        st = st_ref[h]
        inter = lax.dot_general(head(qe, h), st.astype(BF16), NT, preferred_element_type=F32)
        intra = jnp.dot(att[h].astype(BF16), head(vb, h), preferred_element_type=F32)
        intra = intra + jnp.sum(head(qk, h), axis=-1, keepdims=True) * head(v, h)
        st_ref[h] = st * head(decay, h) + lax.dot_general(head(vb, h), head(kt, h), TN, preferred_element_type=F32)
        o_ref[:, h * HG_DK:(h + 1) * HG_DK] = _rms(inter + intra, gon_ref[...]) * head(gate, h)

    def write_state():
        for h in range(HG_HEADS):
            sfin_ref[0, h] = st_ref[h].T

    if single_step:
        write_state()
    else:
        pl.when(step == pl.num_programs(1) - 1)(write_state)


def _hgrn(zb, s0, lower, g_on, batch, seq):
    c = min(HG_C, seq)
    assert seq % c == 0
    pmat, masks = _hgrn_consts(c)
    nc = seq // c
    return pl.pallas_call(
        functools.partial(_hgrn_body, single_step=nc == 1),
        grid=(batch, nc),
        in_specs=[
            pl.BlockSpec((c, 4 * HG_WIDTH), lambda b, i: (b * nc + i, 0)),
            pl.BlockSpec((1, HG_HEADS, HG_DK, HG_DK), lambda b, i: (b, 0, 0, 0)),
            pl.BlockSpec((1, HG_WIDTH), lambda b, i: (0, 0)),
            pl.BlockSpec((1, HG_DK), lambda b, i: (0, 0)),
            pl.BlockSpec(pmat.shape, lambda b, i: (0, 0)),
            pl.BlockSpec(masks.shape, lambda b, i: (0, 0, 0)),
        ],
        out_specs=[
            pl.BlockSpec((c, HG_WIDTH), lambda b, i: (b * nc + i, 0)),
            pl.BlockSpec((1, HG_HEADS, HG_DK, HG_DK), lambda b, i: (b, 0, 0, 0)),
        ],
        out_shape=[
            jax.ShapeDtypeStruct((batch * seq, HG_WIDTH), F32),
            jax.ShapeDtypeStruct((batch, HG_HEADS, HG_DK, HG_DK), F32),
        ],
        scratch_shapes=[pltpu.VMEM((HG_HEADS, HG_DK, HG_DK), F32)],
        compiler_params=_params("arbitrary", "arbitrary"),
        name="hgrn2",
    )(zb, s0, lower, g_on, pmat, masks)


def _split_bf16(x):
    hi = x.astype(BF16)
    return hi, (x - hi.astype(F32)).astype(BF16)


def _merge_body(att_ref, hg_ref, zg_ref, x_ref, wpa_ref, wpb_ref, wout_ref, gffn_ref, wr_ref, br_ref,
                x1_ref, h2_ref, route_ref, cnt_ref):
    pa = jnp.dot(att_ref[...].astype(BF16), wpa_ref[...], preferred_element_type=F32)
    pb = jnp.dot(hg_ref[...].astype(BF16), wpb_ref[...], preferred_element_type=F32)
    y = _sigmoid(zg_ref[:, :D_MODEL]) * pa + _sigmoid(zg_ref[:, D_MODEL:]) * pb
    x1 = x_ref[...] + jnp.dot(y.astype(BF16), wout_ref[...], preferred_element_type=F32)
    x1_ref[...] = x1
    h2 = _rms(x1, gffn_ref[...])
    h2_ref[...] = h2.astype(BF16)

    h_hi, h_lo = _split_bf16(h2)
    w_hi, w_lo = _split_bf16(wr_ref[...])
    logits = (jnp.dot(h_hi, w_hi, preferred_element_type=F32) + jnp.dot(h_lo, w_hi, preferred_element_type=F32)
              + jnp.dot(h_hi, w_lo, preferred_element_type=F32)) + br_ref[...]
    lane = lax.broadcasted_iota(jnp.int32, logits.shape, 1).astype(F32)
    cur = logits
    vals, idxs = [], []
    for _ in range(TOP_K):
        m = jnp.max(cur, axis=-1, keepdims=True)
        idx = jnp.min(jnp.where(cur == m, lane, float(LANES)), axis=-1, keepdims=True)
        vals.append(m)
        idxs.append(idx)
        cur = jnp.where(lane == idx, -jnp.inf, cur)
    es = [jnp.exp(v - vals[0]) for v in vals]
    inv = 1.0 / functools.reduce(jnp.add, es)
    route = jnp.zeros_like(logits)
    for k, (ex, idx) in enumerate(zip(es, idxs)):
        route = (route + jnp.where(lane == idx, 1.0, 0.0) + jnp.where(lane == float(ROUTE_IDX + k), idx, 0.0)
                 + jnp.where(lane == float(ROUTE_P + k), ex * inv, 0.0))
    route_ref[...] = route
    cnt_ref[...] = jnp.sum(jnp.where(lane < float(N_EXPERTS), route, 0.0), axis=0, keepdims=True)


def _merge(att, hg, zg, x, wpa, wpb, wout, g_ffn, w_router, b_router):
    n = x.shape[0]
    tm = TBK
    row = lambda w: pl.BlockSpec((tm, w), lambda i: (i, 0))
    full = lambda a: pl.BlockSpec(a.shape, lambda i: (0,) * a.ndim)
    return pl.pallas_call(
        _merge_body,
        grid=(n // tm,),
        in_specs=[row(ATT_WIDTH), row(HG_WIDTH), row(2 * D_MODEL), row(D_MODEL),
                  full(wpa), full(wpb), full(wout), full(g_ffn), full(w_router), full(b_router)],
        out_specs=[row(D_MODEL), row(D_MODEL), row(LANES), pl.BlockSpec((None, 1, LANES), lambda i: (i, 0, 0))],
        out_shape=[
            jax.ShapeDtypeStruct((n, D_MODEL), F32),
            jax.ShapeDtypeStruct((n, D_MODEL), BF16),
            jax.ShapeDtypeStruct((n, LANES), F32),
            jax.ShapeDtypeStruct((n // tm, 1, LANES), F32),
        ],
        compiler_params=_params("arbitrary"),
        name="merge_router",
    )(att, hg, zg, x, wpa, wpb, wout, g_ffn, w_router, b_router)


def _route_plan(cnt, nt_max):
    pc = (cnt + ROW_CH - 1) // ROW_CH
    loff = jnp.cumsum(pc, axis=1) - pc
    tot = jnp.sum(pc, axis=0)
    reg = (tot + CH_PER_TILE - 1) // CH_PER_TILE * CH_PER_TILE
    gstart = jnp.cumsum(reg) - reg
    goff = gstart[None, :] + jnp.cumsum(pc, axis=0) - pc
    ntiles = jnp.sum(reg) // CH_PER_TILE
    tile_end = jnp.cumsum(reg // CH_PER_TILE)
    t = jnp.minimum(jnp.arange(nt_max, dtype=jnp.int32), ntiles - 1)
    tile_expert = jnp.sum((tile_end[None, :] <= t[:, None]).astype(jnp.int32), axis=1)
    present = reg > 0
    ids = jnp.arange(N_EXPERTS, dtype=jnp.int32)
    later = (ids[None, :] > ids[:, None]) & present[None, :]
    nxt = jnp.min(jnp.where(later, ids[None, :], N_EXPERTS), axis=1)
    nxt = jnp.where(nxt == N_EXPERTS, -1, nxt)
    slot = (jnp.cumsum(present.astype(jnp.int32)) - 1) % 2
    i32 = lambda a: a.astype(jnp.int32)
    return dict(pc=i32(pc), loff=i32(loff), goff=i32(goff), totc=i32(jnp.sum(pc, axis=1)),
                pad_start=i32(gstart + tot), pad_cnt=i32(reg - tot), ntiles=i32(ntiles).reshape(1),
                tile_expert=i32(tile_expert), tile_next=i32(jnp.take(nxt, tile_expert)),
                tile_slot=i32(jnp.take(slot, tile_expert)))


def _chunk_rows(ref, chunk):
    return ref.at[pl.ds(pl.multiple_of(chunk * ROW_CH, ROW_CH), ROW_CH)]


def _wait_chunks(n, copy_of_rows):
    for bit in range((RB // ROW_CH).bit_length()):
        @pl.when(((n >> bit) & 1) == 1)
        def _(bit=bit):
            copy_of_rows((1 << bit) * ROW_CH).wait()


def _dispatch_body(loff_s, pc_s, goff_s, totc_s, pads_s, padn_s, nt_s, h2_ref, route_ref, tri_ref, loffv_ref,
                   xs_hbm, buf, zbuf, sem):
    b = pl.program_id(0)
    nb = pl.num_programs(0)
    slot = lax.rem(b, 2)
    fill_sem, tile_sem = 2, 3
    nt_max = xs_hbm.shape[0] // EXP_TM

    def copy_out(src, gchunk, sem_i):
        return pltpu.make_async_copy(src, _chunk_rows(xs_hbm, gchunk), sem.at[sem_i])

    def zero_tile(t):
        return pltpu.make_async_copy(zbuf, xs_hbm.at[pl.ds(pl.multiple_of(t * EXP_TM, EXP_TM), EXP_TM)],
                                     sem.at[tile_sem])

    def wait_block(bb, slot_):
        _wait_chunks(totc_s[bb], lambda rows: pltpu.make_async_copy(
            buf.at[slot_, pl.ds(0, rows)], xs_hbm.at[pl.ds(0, rows)], sem.at[slot_]))

    @pl.when(b == 0)
    def _():
        zbuf[...] = jnp.zeros_like(zbuf)

    @pl.when(b >= 2)
    def _():
        wait_block(b - 2, slot)

    rt = route_ref[...].T
    rank_t = jnp.dot(rt[0:N_EXPERTS].astype(BF16), tri_ref[...], preferred_element_type=F32)
    lpos_t = loffv_ref[...] * float(ROW_CH) + rank_t
    erow = lax.broadcasted_iota(jnp.int32, (N_EXPERTS, TBK), 0).astype(F32)
    lposk = [jnp.sum(jnp.where(rt[ROUTE_IDX + k:ROUTE_IDX + k + 1] == erow, lpos_t, 0.0), axis=0, keepdims=True)
             for k in range(TOP_K)]
    h2 = h2_ref[...]
    for r0 in range(0, RB, EXP_TM):
        rrow = lax.broadcasted_iota(jnp.int32, (EXP_TM, TBK), 0).astype(F32) + float(r0)
        perm = functools.reduce(jnp.add, [jnp.where(lposk[k] == rrow, 1.0, 0.0) for k in range(TOP_K)])
        buf[slot, r0:r0 + EXP_TM, :] = jnp.dot(perm.astype(BF16), h2, preferred_element_type=F32)

    def send_expert(e, carry):
        base = b * N_EXPERTS + e
        lo, go = loff_s[base], goff_s[base]

        def one(c, carry2):
            copy_out(_chunk_rows(buf.at[slot], lo + c), go + c, slot).start()
            return carry2
        lax.fori_loop(0, pc_s[base], one, 0)
        return carry
    lax.fori_loop(0, N_EXPERTS, send_expert, 0)

    @pl.when(b == nb - 1)
    def _():
        zchunk = _chunk_rows(zbuf, 0)

        def fill_expert(e, carry):
            def one(c, carry2):
                copy_out(zchunk, pads_s[e] + c, fill_sem).start()
                return carry2
            lax.fori_loop(0, padn_s[e], one, 0)
            return carry
        lax.fori_loop(0, N_EXPERTS, fill_expert, 0)

        def fill_tile(t, carry):
            zero_tile(t).start()
            return carry
        lax.fori_loop(nt_s[0], nt_max, fill_tile, 0)

        @pl.when(b >= 1)
        def _():
            wait_block(b - 1, 1 - slot)
        wait_block(b, slot)

        def drain_expert(e, carry):
            def one(c, carry2):
                copy_out(zchunk, 0, fill_sem).wait()
                return carry2
            lax.fori_loop(0, padn_s[e], one, 0)
            return carry
        lax.fori_loop(0, N_EXPERTS, drain_expert, 0)

        def drain_tile(t, carry):
            zero_tile(0).wait()
            return carry
        lax.fori_loop(nt_s[0], nt_max, drain_tile, 0)


def _dispatch(plan, h2, route, loffv, n_rows):
    nb = h2.shape[0] // TBK
    tri = jnp.asarray(np.triu(np.ones((TBK, TBK), np.float32), 1), BF16)
    grid_spec = pltpu.PrefetchScalarGridSpec(
        num_scalar_prefetch=7,
        grid=(nb,),
        in_specs=[
            pl.BlockSpec((TBK, D_MODEL), lambda b, *_: (b, 0)),
            pl.BlockSpec((TBK, LANES), lambda b, *_: (b, 0)),
            pl.BlockSpec((TBK, TBK), lambda b, *_: (0, 0)),
            pl.BlockSpec((None, N_EXPERTS, 1), lambda b, *_: (b, 0, 0)),
        ],
        out_specs=pl.BlockSpec(memory_space=pl.ANY),
        scratch_shapes=[pltpu.VMEM((2, RB, D_MODEL), F32), pltpu.VMEM((EXP_TM, D_MODEL), F32),
                        pltpu.SemaphoreType.DMA((4,))],
    )
    return pl.pallas_call(
        _dispatch_body,
        grid_spec=grid_spec,
        out_shape=jax.ShapeDtypeStruct((n_rows, D_MODEL), F32),
        compiler_params=_params("arbitrary"),
        name="moe_dispatch",
    )(plan["loff"].reshape(-1), plan["pc"].reshape(-1), plan["goff"].reshape(-1), plan["totc"],
      plan["pad_start"], plan["pad_cnt"], plan["ntiles"], h2, route, tri, loffv)


def _experts_body(te_s, nxt_s, par_s, nt_s, x_ref, wgu_hbm, bgu_ref, wd_hbm, bd_ref, y_ref,
                  wg_f, wd_f, wg_b, wd_b, sem):
    t = pl.program_id(0)

    def weight_copies(e, s):
        return (pltpu.make_async_copy(wgu_hbm.at[e], wg_f.at[s], sem.at[s, 0]),
                pltpu.make_async_copy(wd_hbm.at[e], wd_f.at[s], sem.at[s, 1]))

    @pl.when(t == 0)
    def _():
        for cp in weight_copies(te_s[0], par_s[0]):
            cp.start()

    @pl.when((t == 0) | (te_s[t] != te_s[jnp.maximum(t - 1, 0)]))
    def _():
        s = par_s[t]
        for cp in weight_copies(te_s[t], s):
            cp.wait()
        wg_b[...] = wg_f[s].astype(BF16)
        wd_b[...] = wd_f[s].astype(BF16)

        @pl.when(nxt_s[t] >= 0)
        def _():
            for cp in weight_copies(nxt_s[t], 1 - s):
                cp.start()

    @pl.when(t < nt_s[0])
    def _():
        gu = jnp.dot(x_ref[...].astype(BF16), wg_b[...], preferred_element_type=F32) + bgu_ref[...]
        gate = jnp.minimum(gu[:, :D_FF], SWIGLU_LIMIT)
        up = jnp.clip(gu[:, D_FF:], -SWIGLU_LIMIT, SWIGLU_LIMIT)
        act = (up + 1.0) * gate * _sigmoid(SWIGLU_ALPHA * gate)
        y_ref[...] = jnp.dot(act.astype(BF16), wd_b[...], preferred_element_type=F32) + bd_ref[...]

    @pl.when(t >= nt_s[0])
    def _():
        y_ref[...] = jnp.zeros_like(y_ref)


def _experts(plan, xs, w_gu, b_gu, w_down, b_down):
    n_rows = xs.shape[0]
    tile = lambda t, te, nxt, par, nt: (jnp.minimum(t, nt[0] - 1), 0)
    out_tile = lambda t, te, nxt, par, nt: (t, 0)
    of_expert = lambda t, te, nxt, par, nt: (te[t], 0, 0)
    grid_spec = pltpu.PrefetchScalarGridSpec(
        num_scalar_prefetch=4,
        grid=(n_rows // EXP_TM,),
        in_specs=[
            pl.BlockSpec((EXP_TM, D_MODEL), tile),
            pl.BlockSpec(memory_space=pl.ANY),
            pl.BlockSpec((None, 1, 2 * D_FF), of_expert),
            pl.BlockSpec(memory_space=pl.ANY),
            pl.BlockSpec((None, 1, D_MODEL), of_expert),
        ],
        out_specs=pl.BlockSpec((EXP_TM, D_MODEL), out_tile),
        scratch_shapes=[pltpu.VMEM((2, D_MODEL, 2 * D_FF), F32), pltpu.VMEM((2, D_FF, D_MODEL), F32),
                        pltpu.VMEM((D_MODEL, 2 * D_FF), BF16), pltpu.VMEM((D_FF, D_MODEL), BF16),
                        pltpu.SemaphoreType.DMA((2, 2))],
    )
    return pl.pallas_call(
        _experts_body,
        grid_spec=grid_spec,
        out_shape=jax.ShapeDtypeStruct((n_rows, D_MODEL), F32),
        compiler_params=_params("arbitrary"),
        name="moe_experts",
    )(plan["tile_expert"], plan["tile_next"], plan["tile_slot"], plan["ntiles"], xs, w_gu, b_gu, w_down, b_down)


def _combine_body(loff_s, pc_s, goff_s, totc_s, route_ref, tril_ref, loffrow_ref, x1_ref, gfin_ref, ys_hbm,
                  y_ref, buf, sem):
    b = pl.program_id(0)
    nb = pl.num_programs(0)
    slot = lax.rem(b, 2)

    def copy_in(slot_, lchunk, gchunk):
        return pltpu.make_async_copy(_chunk_rows(ys_hbm, gchunk), _chunk_rows(buf.at[slot_], lchunk), sem.at[slot_])

    def fetch_block(bb, slot_):
        def fetch_expert(e, carry):
            base = bb * N_EXPERTS + e
            lo, go = loff_s[base], goff_s[base]

            def one(c, carry2):
                copy_in(slot_, lo + c, go + c).start()
                return carry2
            lax.fori_loop(0, pc_s[base], one, 0)
            return carry
        lax.fori_loop(0, N_EXPERTS, fetch_expert, 0)

    def wait_block(bb, slot_):
        _wait_chunks(totc_s[bb], lambda rows: pltpu.make_async_copy(
            ys_hbm.at[pl.ds(0, rows)], buf.at[slot_, pl.ds(0, rows)], sem.at[slot_]))

    @pl.when(b == 0)
    def _():
        buf[...] = jnp.zeros_like(buf)
        fetch_block(0, 0)

    @pl.when(b + 1 < nb)
    def _():
        fetch_block(b + 1, 1 - slot)

    wait_block(b, slot)

    route = route_ref[...]
    lane = lax.broadcasted_iota(jnp.int32, (1, LANES), 1).astype(F32)
    sel = jnp.where(lane < float(N_EXPERTS), route, 0.0).astype(BF16)
    rank = jnp.dot(tril_ref[...], sel, preferred_element_type=F32)
    lpos = loffrow_ref[...] * float(ROW_CH) + rank
    lposk, pk = [], []
    for k in range(TOP_K):
        idx = route[:, ROUTE_IDX + k:ROUTE_IDX + k + 1]
        lposk.append(jnp.sum(jnp.where(lane == idx, lpos, 0.0), axis=-1, keepdims=True))
        pk.append(route[:, ROUTE_P + k:ROUTE_P + k + 1])
    acc = x1_ref[...]
    for r0 in range(0, RB, EXP_TM):
        col = lax.broadcasted_iota(jnp.int32, (TBK, EXP_TM), 1).astype(F32) + float(r0)
        w = functools.reduce(jnp.add, [jnp.where(lposk[k] == col, pk[k], 0.0) for k in range(TOP_K)])
        acc = acc + jnp.dot(w.astype(BF16), buf[slot, r0:r0 + EXP_TM, :].astype(BF16), preferred_element_type=F32)
    y_ref[...] = _rms(acc, gfin_ref[...])


def _combine(plan, blocks, route, loffrow, x1, g_final, ys):
    b0, b1 = blocks
    nb = b1 - b0
    tril = jnp.asarray(np.tril(np.ones((TBK, TBK), np.float32), -1), BF16)
    flat = lambda a: a[b0:b1].reshape(-1)
    grid_spec = pltpu.PrefetchScalarGridSpec(
        num_scalar_prefetch=4,
        grid=(nb,),
        in_specs=[
            pl.BlockSpec((TBK, LANES), lambda b, *_: (b, 0)),
            pl.BlockSpec((TBK, TBK), lambda b, *_: (0, 0)),
            pl.BlockSpec((None, 1, LANES), lambda b, *_: (b, 0, 0)),
            pl.BlockSpec((TBK, D_MODEL), lambda b, *_: (b, 0)),
            pl.BlockSpec((1, D_MODEL), lambda b, *_: (0, 0)),
            pl.BlockSpec(memory_space=pl.ANY),
        ],
        out_specs=pl.BlockSpec((TBK, D_MODEL), lambda b, *_: (b, 0)),
        scratch_shapes=[pltpu.VMEM((2, RB, D_MODEL), F32), pltpu.SemaphoreType.DMA((2,))],
    )
    return pl.pallas_call(
        _combine_body,
        grid_spec=grid_spec,
        out_shape=jax.ShapeDtypeStruct((nb * TBK, D_MODEL), F32),
        compiler_params=_params("arbitrary"),
        name="moe_combine",
    )(flat(plan["loff"]), flat(plan["pc"]), flat(plan["goff"]), plan["totc"][b0:b1],
      route, tril, loffrow[b0:b1], x1, g_final, ys)


def kernel(x_prompt, x_sample, cache_k, cache_v, state_s, g_mix, w_in, rel_bias, lb_logits, g_out_norm,
           w_pa, w_pb, w_out, g_ffn, w_router, b_router, w_gu, b_gu, w_down, b_down, g_final):
    B, T = x_prompt.shape[:2]
    DB, S = x_sample.shape[:2]
    depth = w_in.shape[0]
    assert depth == 1 and T % ATT_QBLK == 0 and S == CHUNK
    cw = cache_k.shape[2]
    assert cw == WINDOW
    l = 0

    lower = jnp.cumsum(jax.nn.softmax(lb_logits.astype(F32), axis=0), axis=0)[l].reshape(1, HG_WIDTH)
    w_in_b = w_in[l].astype(BF16)
    wpa, wpb, wout = w_pa[l].astype(BF16), w_pb[l].astype(BF16), w_out[l].astype(BF16)
    row = lambda a: a.reshape(1, -1).astype(F32)
    base = _rel_bias_base(rel_bias[l])
    b_gu3 = b_gu[l].reshape(N_EXPERTS, 1, 2 * D_FF)
    b_down3 = b_down[l].reshape(N_EXPERTS, 1, D_MODEL)
    pad_e = LANES - N_EXPERTS
    wr = jnp.pad(w_router[l].astype(F32), ((0, 0), (0, pad_e)))
    br = jnp.concatenate([b_router[l].astype(F32), jnp.full((pad_e,), NEG, F32)]).reshape(1, LANES)

    def front(x, batch, seq, s0, attend):
        xf = x.reshape(batch * seq, D_MODEL)
        za, zb, zg = _inproj(xf, row(g_mix[l]), w_in_b)
        att = attend(za)
        hg, s_fin = _hgrn(zb, s0, lower, row(g_out_norm[l]), batch, seq)
        x1, h2, route, cnt = _merge(att, hg, zg, xf, wpa, wpb, wout, row(g_ffn[l]), wr, br)
        za3 = za.reshape(batch, seq, 3 * ATT_WIDTH)
        heads = lambda a: a.reshape(1, batch, a.shape[1], ATT_HEADS, ATT_DIM)
        keep = min(WINDOW, seq)
        nk = heads(za3[:, seq - keep:, ATT_WIDTH:2 * ATT_WIDTH])
        nv = heads(za3[:, seq - keep:, 2 * ATT_WIDTH:])
        return dict(x1=x1, h2=h2, route=route, cnt=cnt, nk=nk, nv=nv, s=s_fin[None])

    ck = cache_k[l].reshape(DB, cw, ATT_WIDTH)
    cv = cache_v[l].reshape(DB, cw, ATT_WIDTH)
    fp = front(x_prompt, B, T, jnp.zeros((B, HG_HEADS, HG_DK, HG_DK), F32), lambda za: _attn_prompt(za, base, B, T))
    fs = front(x_sample, DB, S, state_s[l].astype(F32), lambda za: _attn_sample(za, ck, cv, base, DB, S))

    h2 = jnp.concatenate([fp["h2"], fs["h2"]], axis=0)
    route = jnp.concatenate([fp["route"], fs["route"]], axis=0)
    cnt = jnp.concatenate([fp["cnt"], fs["cnt"]], axis=0)[:, 0, :N_EXPERTS].astype(jnp.int32)
    n_tok = h2.shape[0]
    nb, nbp = n_tok // TBK, (B * T) // TBK
    max_rows = n_tok * TOP_K + nb * N_EXPERTS * (ROW_CH - 1) + N_EXPERTS * (EXP_TM - 1)
    nt_max = -(-max_rows // EXP_TM)
    plan = _route_plan(cnt, nt_max)
    loff_f = plan["loff"].astype(F32)
    xs = _dispatch(plan, h2, route, loff_f[:, :, None], nt_max * EXP_TM)
    ysort = _experts(plan, xs, w_gu[l], b_gu3, w_down[l], b_down3)
    loffrow = jnp.pad(loff_f, ((0, 0), (0, pad_e)))[:, None, :]
    yp = _combine(plan, (0, nbp), fp["route"], loffrow, fp["x1"], row(g_final), ysort)
    ys = _combine(plan, (nbp, nb), fs["route"], loffrow, fs["x1"], row(g_final), ysort)
    return (yp.reshape(B, T, D_MODEL), ys.reshape(DB, S, D_MODEL), fp["nk"], fp["nv"], fp["s"],
            fs["nk"], fs["nv"], fs["s"])
```

```python
import functools

import numpy as np
import jax
import jax.numpy as jnp
from jax import lax
from jax.experimental import pallas as pl
from jax.experimental.pallas import tpu as pltpu

F32 = jnp.float32
BF16 = jnp.bfloat16

D_MODEL = 1024
CHUNK = 64
LEFT_CHUNKS = 8
WINDOW = LEFT_CHUNKS * CHUNK
ATT_HEADS = 8
ATT_DIM = 64
ATT_WIDTH = ATT_HEADS * ATT_DIM
MAX_REL = 256
HG_HEADS = 4
HG_DK = 128
HG_WIDTH = HG_HEADS * HG_DK
N_EXPERTS = 32
TOP_K = 4
D_FF = D_MODEL
SWIGLU_LIMIT = 7.0
SWIGLU_ALPHA = 1.702
RMS_EPS = 1e-5

LANES = 128
NEG = -1e30
ATT_QBLK = 4 * CHUNK
ATT_KBLKS = LEFT_CHUNKS * CHUNK // ATT_QBLK + 1
HG_C = 128
VMEM_LIMIT = 56 * 1024 * 1024
BIAS_W = 1024
SUBLANES = 8
TBK = 256
ROW_CH = SUBLANES
RB = TBK * TOP_K + N_EXPERTS * ROW_CH
EXP_TM = 256
CH_PER_TILE = EXP_TM // ROW_CH
ROUTE_IDX = 64
ROUTE_P = 72

NT = (((1,), (1,)), ((), ()))
TN = (((0,), (0,)), ((), ()))


def _rms(x, g):
    return x * lax.rsqrt(jnp.mean(x * x, axis=-1, keepdims=True) + RMS_EPS) * g


def _sigmoid(x):
    return 1.0 / (1.0 + jnp.exp(-x))


def _params(*sem):
    return pltpu.CompilerParams(dimension_semantics=sem, vmem_limit_bytes=VMEM_LIMIT)


def _inproj_body(x_ref, g_ref, w_ref, za_ref, zb_ref, zg_ref):
    h = _rms(x_ref[...], g_ref[...]).astype(BF16)
    a, b = 3 * ATT_WIDTH, 3 * ATT_WIDTH + 4 * HG_WIDTH
    za_ref[...] = jnp.dot(h, w_ref[:, :a], preferred_element_type=F32)
    zb_ref[...] = jnp.dot(h, w_ref[:, a:b], preferred_element_type=F32)
    zg_ref[...] = jnp.dot(h, w_ref[:, b:], preferred_element_type=F32)


def _inproj(x, g, w_bf16, tm=256):
    n = x.shape[0]
    cols = w_bf16.shape[1]
    wa, wb, wg = 3 * ATT_WIDTH, 4 * HG_WIDTH, 2 * D_MODEL
    return pl.pallas_call(
        _inproj_body,
        grid=(n // tm,),
        in_specs=[
            pl.BlockSpec((tm, D_MODEL), lambda i: (i, 0)),
            pl.BlockSpec((1, D_MODEL), lambda i: (0, 0)),
            pl.BlockSpec((D_MODEL, cols), lambda i: (0, 0)),
        ],
        out_specs=[
            pl.BlockSpec((tm, wa), lambda i: (i, 0)),
            pl.BlockSpec((tm, wb), lambda i: (i, 0)),
            pl.BlockSpec((tm, wg), lambda i: (i, 0)),
        ],
        out_shape=[
            jax.ShapeDtypeStruct((n, wa), F32),
            jax.ShapeDtypeStruct((n, wb), F32),
            jax.ShapeDtypeStruct((n, wg), F32),
        ],
        compiler_params=_params("arbitrary"),
        name="inproj",
    )(x, g, w_bf16)


def _attn_heads(q_ref, k_refs, v_refs, bias_fn, pens, o_ref):
    lane = lax.broadcasted_iota(jnp.int32, (1, LANES), 1)
    first = lane < ATT_DIM
    for hp in range(ATT_HEADS // 2):
        sl = slice(hp * LANES, (hp + 1) * LANES)
        q2 = q_ref[:, sl] * (ATT_DIM ** -0.5)
        ks = [k[:, sl].astype(BF16) for k in k_refs]
        outs = []
        for half in range(2):
            head = 2 * hp + half
            mine = first if half == 0 else lane >= ATT_DIM
            qm = jnp.where(mine, q2, 0.0).astype(BF16)
            vs = [jnp.where(mine, v[:, sl], 1.0).astype(BF16) for v in v_refs]
            ss = []
            for j, kj in enumerate(ks):
                s = lax.dot_general(qm, kj, NT, preferred_element_type=F32) + bias_fn(head, j)
                if pens[j] is not None:
                    s = s + pens[j]
                ss.append(s)
            if all(s.shape == ss[0].shape for s in ss):
                m = jnp.max(functools.reduce(jnp.maximum, ss), axis=-1, keepdims=True)
            else:
                m = functools.reduce(jnp.maximum, [jnp.max(s, axis=-1, keepdims=True) for s in ss])
            outs.append(functools.reduce(jnp.add, [jnp.dot(jnp.exp(s - m).astype(BF16), vj, preferred_element_type=F32)
                                                   for s, vj in zip(ss, vs)]))
        num = jnp.where(first, outs[0], outs[1])
        den = pltpu.roll(jnp.where(first, outs[1], outs[0]), ATT_DIM, 1)
        o_ref[:, sl] = num * (1.0 / den)


def _fill_bias(base_ref, bias_ref, banded):
    nq, nk = bias_ref.shape[1:]
    if banded:
        r = lax.broadcasted_iota(jnp.int32, (nq, nk), 0)
        s = lax.broadcasted_iota(jnp.int32, (nq, nk), 1)
        qc = (r + WINDOW) // CHUNK
        kc = s // CHUNK
        pen = jnp.where(kc <= qc, jnp.where(kc >= qc - LEFT_CHUNKS, 0.0, NEG), NEG)
    for h in range(ATT_HEADS):
        rows = jnp.broadcast_to(base_ref[h:h + 1, :], (nq, BIAS_W))
        t = pltpu.roll(rows, 0, 1, stride=1, stride_axis=0)[:, :nk]
        bias_ref[h] = t + pen if banded else t


def _attn_prompt_body(q_ref, k0, k1, k2, v0, v1, v2, base_ref, o_ref, bias_ref):
    i = pl.program_id(1)

    @pl.when((pl.program_id(0) == 0) & (i == 0))
    def _():
        _fill_bias(base_ref, bias_ref, True)

    pens = [jnp.where(i - (ATT_KBLKS - 1) + j >= 0, 0.0, NEG) for j in range(ATT_KBLKS - 1)] + [None]
    bias_fn = lambda h, j: bias_ref[h, :, j * ATT_QBLK:(j + 1) * ATT_QBLK]
    _attn_heads(q_ref, [k0, k1, k2], [v0, v1, v2], bias_fn, pens, o_ref)


def _attn_prompt(za, base, batch, seq):
    nq = seq // ATT_QBLK
    back = ATT_KBLKS - 1
    qspec = pl.BlockSpec((ATT_QBLK, ATT_WIDTH), lambda b, i: (b * nq + i, 0))

    def kvspec(j, col):
        return pl.BlockSpec((ATT_QBLK, ATT_WIDTH),
                            lambda b, i: (b * nq + jnp.maximum(i - back + j, 0), col))

    return pl.pallas_call(
        _attn_prompt_body,
        grid=(batch, nq),
        in_specs=[qspec] + [kvspec(j, 1) for j in range(ATT_KBLKS)] + [kvspec(j, 2) for j in range(ATT_KBLKS)]
        + [pl.BlockSpec(base.shape, lambda b, i: (0, 0))],
        out_specs=pl.BlockSpec((ATT_QBLK, ATT_WIDTH), lambda b, i: (b * nq + i, 0)),
        out_shape=jax.ShapeDtypeStruct((batch * seq, ATT_WIDTH), F32),
        scratch_shapes=[pltpu.VMEM((ATT_HEADS, ATT_QBLK, ATT_KBLKS * ATT_QBLK), F32)],
        compiler_params=_params("arbitrary", "arbitrary"),
        name="attn_prompt",
    )(za, za, za, za, za, za, za, base)


def _attn_sample_body(q_ref, kn_ref, vn_ref, ck_ref, cv_ref, base_ref, o_ref, bias_ref):
    @pl.when(pl.program_id(0) == 0)
    def _():
        _fill_bias(base_ref, bias_ref, False)

    cw = ck_ref.shape[0]
    bias_fn = lambda h, j: bias_ref[h, :, :cw] if j == 0 else bias_ref[h, :, cw:]
    _attn_heads(q_ref, [ck_ref, kn_ref], [cv_ref, vn_ref], bias_fn, [None, None], o_ref)


def _attn_sample(za, ck, cv, base, batch, seq):
    cw = ck.shape[1]
    return pl.pallas_call(
        _attn_sample_body,
        grid=(batch,),
        in_specs=[
            pl.BlockSpec((seq, ATT_WIDTH), lambda b: (b, 0)),
            pl.BlockSpec((seq, ATT_WIDTH), lambda b: (b, 1)),
            pl.BlockSpec((seq, ATT_WIDTH), lambda b: (b, 2)),
            pl.BlockSpec((None, cw, ATT_WIDTH), lambda b: (b, 0, 0)),
            pl.BlockSpec((None, cw, ATT_WIDTH), lambda b: (b, 0, 0)),
            pl.BlockSpec(base.shape, lambda b: (0, 0)),
        ],
        out_specs=pl.BlockSpec((seq, ATT_WIDTH), lambda b: (b, 0)),
        out_shape=jax.ShapeDtypeStruct((batch * seq, ATT_WIDTH), F32),
        scratch_shapes=[pltpu.VMEM((ATT_HEADS, seq, cw + seq), F32)],
        compiler_params=_params("arbitrary"),
        name="attn_sample",
    )(za, za, za, ck, cv, base)


def _rel_bias_base(table):
    top = table[:, 2 * MAX_REL:].astype(F32)
    rev = table[:, ::-1][:, :2 * MAX_REL].astype(F32)
    left = WINDOW - MAX_REL
    return jnp.concatenate([jnp.broadcast_to(top, (ATT_HEADS, left)), rev,
                            jnp.broadcast_to(top, (ATT_HEADS, BIAS_W - left - 2 * MAX_REL))], axis=1)


def _hgrn_consts(c):
    t = np.arange(c)[:, None]
    j = np.arange(c)[None, :]
    mats = [j <= t, j > t]
    masks = []
    m = c // 2
    while m >= 1:
        ref = (t // (2 * m)) * (2 * m) + m - 1
        second = (t % (2 * m)) >= m
        mats.append((second & (j > ref) & (j <= t)) | (~second & (j > t) & (j <= ref)))
        masks.append((t // (2 * m)) == (j // (2 * m)))
        m //= 2
    return (jnp.asarray(np.concatenate(mats, 0).astype(np.float32), BF16),
            jnp.asarray(np.stack(masks).astype(np.float32)))


def _hgrn_body(zb_ref, s0_ref, lower_ref, gon_ref, p_ref, mask_ref, o_ref, sfin_ref, st_ref, *, single_step):
    c = zb_ref.shape[0]
    step = pl.program_id(1)

    def load_state():
        for h in range(HG_HEADS):
            st_ref[h] = s0_ref[0, h].T

    if single_step:
        load_state()
    else:
        pl.when(step == 0)(load_state)

    pmat = p_ref[...]
    n_levels = mask_ref.shape[0]
    part = lambda i: zb_ref[:, i * HG_WIDTH:(i + 1) * HG_WIDTH]
    head = lambda a, h: a[:, h * HG_DK:(h + 1) * HG_DK]
    q = part(0)
    low = lower_ref[...]
    f = low + (1.0 - low) * _sigmoid(part(1))
    lf = jnp.log(f)
    k = 1.0 - f
    ib = part(2)
    v = ib * _sigmoid(ib)
    og = part(3)

    hi = lf.astype(BF16)
    r1 = lf - hi.astype(F32)
    mid = r1.astype(BF16)
    lo = (r1 - mid.astype(F32)).astype(BF16)
    e = (jnp.dot(pmat, hi, preferred_element_type=F32) + jnp.dot(pmat, mid, preferred_element_type=F32)
         + jnp.dot(pmat, lo, preferred_element_type=F32))
    b = e[0:c]
    decay = jnp.exp(e[c - 1:c])
    qe = (q * jnp.exp(b)).astype(BF16)
    kt = (k * jnp.exp(e[c:2 * c])).astype(BF16)
    vb = v.astype(BF16)
    qk = q * k
    gate = og * _sigmoid(og)

    row = lax.broadcasted_iota(jnp.int32, (c, HG_WIDTH), 0)
    att = [None] * HG_HEADS
    for lvl in range(n_levels):
        m = c >> (lvl + 1)
        x = jnp.exp(e[(2 + lvl) * c:(3 + lvl) * c])
        second = (row & m) != 0
        qm = jnp.where(second, q * x, 0.0).astype(BF16)
        km = jnp.where(second, 0.0, k * x).astype(BF16)
        for h in range(HG_HEADS):
            a = lax.dot_general(head(qm, h), head(km, h), NT, preferred_element_type=F32)
            if lvl > 0:
                a = a * mask_ref[lvl]
            att[h] = a if att[h] is None else att[h] + a

    for h in range(HG_HEADS):
        st = st_ref[h]
        inter = lax.dot_general(head(qe, h), st.astype(BF16), NT, preferred_element_type=F32)
        intra = jnp.dot(att[h].astype(BF16), head(vb, h), preferred_element_type=F32)
        intra = intra + jnp.sum(head(qk, h), axis=-1, keepdims=True) * head(v, h)
        st_ref[h] = st * head(decay, h) + lax.dot_general(head(vb, h), head(kt, h), TN, preferred_element_type=F32)
        o_ref[:, h * HG_DK:(h + 1) * HG_DK] = _rms(inter + intra, gon_ref[...]) * head(gate, h)

    def write_state():
        for h in range(HG_HEADS):
            sfin_ref[0, h] = st_ref[h].T

    if single_step:
        write_state()
    else:
        pl.when(step == pl.num_programs(1) - 1)(write_state)


def _hgrn(zb, s0, lower, g_on, batch, seq):
    c = min(HG_C, seq)
    assert seq % c == 0
    pmat, masks = _hgrn_consts(c)
    nc = seq // c
    return pl.pallas_call(
        functools.partial(_hgrn_body, single_step=nc == 1),
        grid=(batch, nc),
        in_specs=[
            pl.BlockSpec((c, 4 * HG_WIDTH), lambda b, i: (b * nc + i, 0)),
            pl.BlockSpec((1, HG_HEADS, HG_DK, HG_DK), lambda b, i: (b, 0, 0, 0)),
            pl.BlockSpec((1, HG_WIDTH), lambda b, i: (0, 0)),
            pl.BlockSpec((1, HG_DK), lambda b, i: (0, 0)),
            pl.BlockSpec(pmat.shape, lambda b, i: (0, 0)),
            pl.BlockSpec(masks.shape, lambda b, i: (0, 0, 0)),
        ],
        out_specs=[
            pl.BlockSpec((c, HG_WIDTH), lambda b, i: (b * nc + i, 0)),
            pl.BlockSpec((1, HG_HEADS, HG_DK, HG_DK), lambda b, i: (b, 0, 0, 0)),
        ],
        out_shape=[
            jax.ShapeDtypeStruct((batch * seq, HG_WIDTH), F32),
            jax.ShapeDtypeStruct((batch, HG_HEADS, HG_DK, HG_DK), F32),
        ],
        scratch_shapes=[pltpu.VMEM((HG_HEADS, HG_DK, HG_DK), F32)],
        compiler_params=_params("arbitrary", "arbitrary"),
        name="hgrn2",
    )(zb, s0, lower, g_on, pmat, masks)


def _split_bf16(x):
    hi = x.astype(BF16)
    return hi, (x - hi.astype(F32)).astype(BF16)


def _merge_body(att_p, hg_p, zg_p, x_p, att_s, hg_s, zg_s, x_s, wpa_ref, wpb_ref, wout_ref, gffn_ref, wr_ref, br_ref,
                x1_ref, h2_ref, route_ref, cnt_ref, *, n_first):
    weights = (wpa_ref, wpb_ref, wout_ref, gffn_ref, wr_ref, br_ref)
    outs = (x1_ref, h2_ref, route_ref, cnt_ref)
    i = pl.program_id(0)
    pl.when(i < n_first)(functools.partial(_merge_block, att_p, hg_p, zg_p, x_p, *weights, *outs))
    pl.when(i >= n_first)(functools.partial(_merge_block, att_s, hg_s, zg_s, x_s, *weights, *outs))


def _merge_block(att_ref, hg_ref, zg_ref, x_ref, wpa_ref, wpb_ref, wout_ref, gffn_ref, wr_ref, br_ref,
                 x1_ref, h2_ref, route_ref, cnt_ref):
    pa = jnp.dot(att_ref[...].astype(BF16), wpa_ref[...], preferred_element_type=F32)
    pb = jnp.dot(hg_ref[...].astype(BF16), wpb_ref[...], preferred_element_type=F32)
    y = _sigmoid(zg_ref[:, :D_MODEL]) * pa + _sigmoid(zg_ref[:, D_MODEL:]) * pb
    x1 = x_ref[...] + jnp.dot(y.astype(BF16), wout_ref[...], preferred_element_type=F32)
    x1_ref[...] = x1
    h2 = _rms(x1, gffn_ref[...])
    h2_ref[...] = h2.astype(BF16)

    h_hi, h_lo = _split_bf16(h2)
    w_hi, w_lo = _split_bf16(wr_ref[...])
    logits = (jnp.dot(h_hi, w_hi, preferred_element_type=F32) + jnp.dot(h_lo, w_hi, preferred_element_type=F32)
              + jnp.dot(h_hi, w_lo, preferred_element_type=F32)) + br_ref[...]
    lane = lax.broadcasted_iota(jnp.int32, logits.shape, 1).astype(F32)
    cur = logits
    vals, idxs = [], []
    for _ in range(TOP_K):
        m = jnp.max(cur, axis=-1, keepdims=True)
        idx = jnp.min(jnp.where(cur == m, lane, float(LANES)), axis=-1, keepdims=True)
        vals.append(m)
        idxs.append(idx)
        cur = jnp.where(lane == idx, -jnp.inf, cur)
    es = [jnp.exp(v - vals[0]) for v in vals]
    inv = 1.0 / functools.reduce(jnp.add, es)
    route = jnp.zeros_like(logits)
    for k, (ex, idx) in enumerate(zip(es, idxs)):
        route = (route + jnp.where(lane == idx, 1.0, 0.0) + jnp.where(lane == float(ROUTE_IDX + k), idx, 0.0)
                 + jnp.where(lane == float(ROUTE_P + k), ex * inv, 0.0))
    route_ref[...] = route
    cnt_ref[...] = jnp.sum(jnp.where(lane < float(N_EXPERTS), route, 0.0), axis=0, keepdims=True)


def _merge(first, second, wpa, wpb, wout, g_ffn, w_router, b_router):
    tm = TBK
    n1, n2 = first[3].shape[0] // tm, second[3].shape[0] // tm
    widths = (ATT_WIDTH, HG_WIDTH, 2 * D_MODEL, D_MODEL)
    spec1 = [pl.BlockSpec((tm, w), lambda i: (jnp.minimum(i, n1 - 1), 0)) for w in widths]
    spec2 = [pl.BlockSpec((tm, w), lambda i: (jnp.maximum(i - n1, 0), 0)) for w in widths]
    row = lambda w: pl.BlockSpec((tm, w), lambda i: (i, 0))
    full = lambda a: pl.BlockSpec(a.shape, lambda i: (0,) * a.ndim)
    n = (n1 + n2) * tm
    return pl.pallas_call(
        functools.partial(_merge_body, n_first=n1),
        grid=(n1 + n2,),
        in_specs=spec1 + spec2 + [full(wpa), full(wpb), full(wout), full(g_ffn), full(w_router), full(b_router)],
        out_specs=[row(D_MODEL), row(D_MODEL), row(LANES), pl.BlockSpec((None, 1, LANES), lambda i: (i, 0, 0))],
        out_shape=[
            jax.ShapeDtypeStruct((n, D_MODEL), F32),
            jax.ShapeDtypeStruct((n, D_MODEL), BF16),
            jax.ShapeDtypeStruct((n, LANES), F32),
            jax.ShapeDtypeStruct((n1 + n2, 1, LANES), F32),
        ],
        compiler_params=_params("arbitrary"),
        name="merge_router",
    )(*first, *second, wpa, wpb, wout, g_ffn, w_router, b_router)


def _route_plan(cnt, nt_max):
    pc = (cnt + ROW_CH - 1) // ROW_CH
    loff = jnp.cumsum(pc, axis=1) - pc
    tot = jnp.sum(pc, axis=0)
    reg = (tot + CH_PER_TILE - 1) // CH_PER_TILE * CH_PER_TILE
    gstart = jnp.cumsum(reg) - reg
    goff = gstart[None, :] + jnp.cumsum(pc, axis=0) - pc
    ntiles = jnp.sum(reg) // CH_PER_TILE
    tile_end = jnp.cumsum(reg // CH_PER_TILE)
    t = jnp.minimum(jnp.arange(nt_max, dtype=jnp.int32), ntiles - 1)
    tile_expert = jnp.sum((tile_end[None, :] <= t[:, None]).astype(jnp.int32), axis=1)
    present = reg > 0
    ids = jnp.arange(N_EXPERTS, dtype=jnp.int32)
    later = (ids[None, :] > ids[:, None]) & present[None, :]
    nxt = jnp.min(jnp.where(later, ids[None, :], N_EXPERTS), axis=1)
    nxt = jnp.where(nxt == N_EXPERTS, -1, nxt)
    slot = (jnp.cumsum(present.astype(jnp.int32)) - 1) % 2
    i32 = lambda a: a.astype(jnp.int32)
    of_tile = lambda a: jnp.sum(jnp.where(tile_expert[:, None] == ids[None, :], a[None, :], 0), axis=1)
    return dict(pc=i32(pc), loff=i32(loff), goff=i32(goff), totc=i32(jnp.sum(pc, axis=1)),
                pad_start=i32(gstart + tot), pad_cnt=i32(reg - tot), ntiles=i32(ntiles).reshape(1),
                tile_expert=i32(tile_expert), tile_next=i32(of_tile(nxt)), tile_slot=i32(of_tile(slot)))


def _chunk_rows(ref, chunk):
    return ref.at[pl.ds(pl.multiple_of(chunk * ROW_CH, ROW_CH), ROW_CH)]


def _wait_chunks(n, copy_of_rows):
    for bit in range((RB // ROW_CH).bit_length()):
        @pl.when(((n >> bit) & 1) == 1)
        def _(bit=bit):
            copy_of_rows((1 << bit) * ROW_CH).wait()


def _dispatch_body(loff_s, pc_s, goff_s, totc_s, pads_s, padn_s, nt_s, h2_ref, route_ref, tri_ref, loffv_ref,
                   xs_hbm, buf, zbuf, sem):
    b = pl.program_id(0)
    nb = pl.num_programs(0)
    slot = lax.rem(b, 2)
    fill_sem, tile_sem = 2, 3
    nt_max = xs_hbm.shape[0] // EXP_TM

    def copy_out(src, gchunk, sem_i):
        return pltpu.make_async_copy(src, _chunk_rows(xs_hbm, gchunk), sem.at[sem_i])

    def zero_tile(t):
        return pltpu.make_async_copy(zbuf, xs_hbm.at[pl.ds(pl.multiple_of(t * EXP_TM, EXP_TM), EXP_TM)],
                                     sem.at[tile_sem])

    def wait_block(bb, slot_):
        _wait_chunks(totc_s[bb], lambda rows: pltpu.make_async_copy(
            buf.at[slot_, pl.ds(0, rows)], xs_hbm.at[pl.ds(0, rows)], sem.at[slot_]))

    @pl.when(b == 0)
    def _():
        zbuf[...] = jnp.zeros_like(zbuf)

    @pl.when(b >= 2)
    def _():
        wait_block(b - 2, slot)

    rt = route_ref[...].T
    rank_t = jnp.dot(rt[0:N_EXPERTS].astype(BF16), tri_ref[...], preferred_element_type=F32)
    lpos_t = loffv_ref[...] * float(ROW_CH) + rank_t
    erow = lax.broadcasted_iota(jnp.int32, (N_EXPERTS, TBK), 0).astype(F32)
    lposk = [jnp.sum(jnp.where(rt[ROUTE_IDX + k:ROUTE_IDX + k + 1] == erow, lpos_t, 0.0), axis=0, keepdims=True)
             for k in range(TOP_K)]
    h2 = h2_ref[...]
    for r0 in range(0, RB, EXP_TM):
        rrow = lax.broadcasted_iota(jnp.int32, (EXP_TM, TBK), 0).astype(F32) + float(r0)
        perm = functools.reduce(jnp.add, [jnp.where(lposk[k] == rrow, 1.0, 0.0) for k in range(TOP_K)])
        buf[slot, r0:r0 + EXP_TM, :] = jnp.dot(perm.astype(BF16), h2, preferred_element_type=F32)

    def send_expert(e, carry):
        base = b * N_EXPERTS + e
        lo, go = loff_s[base], goff_s[base]

        def one(c, carry2):
            copy_out(_chunk_rows(buf.at[slot], lo + c), go + c, slot).start()
            return carry2
        lax.fori_loop(0, pc_s[base], one, 0)
        return carry
    lax.fori_loop(0, N_EXPERTS, send_expert, 0)

    @pl.when(b == nb - 1)
    def _():
        zchunk = _chunk_rows(zbuf, 0)

        def fill_expert(e, carry):
            def one(c, carry2):
                copy_out(zchunk, pads_s[e] + c, fill_sem).start()
                return carry2
            lax.fori_loop(0, padn_s[e], one, 0)
            return carry
        lax.fori_loop(0, N_EXPERTS, fill_expert, 0)

        def fill_tile(t, carry):
            zero_tile(t).start()
            return carry
        lax.fori_loop(nt_s[0], nt_max, fill_tile, 0)

        @pl.when(b >= 1)
        def _():
            wait_block(b - 1, 1 - slot)
        wait_block(b, slot)

        def drain_expert(e, carry):
            def one(c, carry2):
                copy_out(zchunk, 0, fill_sem).wait()
                return carry2
            lax.fori_loop(0, padn_s[e], one, 0)
            return carry
        lax.fori_loop(0, N_EXPERTS, drain_expert, 0)

        def drain_tile(t, carry):
            zero_tile(0).wait()
            return carry
        lax.fori_loop(nt_s[0], nt_max, drain_tile, 0)


def _dispatch(plan, h2, route, loffv, n_rows):
    nb = h2.shape[0] // TBK
    tri = jnp.asarray(np.triu(np.ones((TBK, TBK), np.float32), 1), BF16)
    grid_spec = pltpu.PrefetchScalarGridSpec(
        num_scalar_prefetch=7,
        grid=(nb,),
        in_specs=[
            pl.BlockSpec((TBK, D_MODEL), lambda b, *_: (b, 0)),
            pl.BlockSpec((TBK, LANES), lambda b, *_: (b, 0)),
            pl.BlockSpec((TBK, TBK), lambda b, *_: (0, 0)),
            pl.BlockSpec((None, N_EXPERTS, 1), lambda b, *_: (b, 0, 0)),
        ],
        out_specs=pl.BlockSpec(memory_space=pl.ANY),
        scratch_shapes=[pltpu.VMEM((2, RB, D_MODEL), F32), pltpu.VMEM((EXP_TM, D_MODEL), F32),
                        pltpu.SemaphoreType.DMA((4,))],
    )
    return pl.pallas_call(
        _dispatch_body,
        grid_spec=grid_spec,
        out_shape=jax.ShapeDtypeStruct((n_rows, D_MODEL), F32),
        compiler_params=_params("arbitrary"),
        name="moe_dispatch",
    )(plan["loff"].reshape(-1), plan["pc"].reshape(-1), plan["goff"].reshape(-1), plan["totc"],
      plan["pad_start"], plan["pad_cnt"], plan["ntiles"], h2, route, tri, loffv)


def _experts_body(te_s, nxt_s, par_s, nt_s, x_ref, wgu_hbm, bgu_ref, wd_hbm, bd_ref, y_ref,
                  wg_f, wd_f, wg_b, wd_b, sem):
    t = pl.program_id(0)

    def weight_copies(e, s):
        return (pltpu.make_async_copy(wgu_hbm.at[e], wg_f.at[s], sem.at[s, 0]),
                pltpu.make_async_copy(wd_hbm.at[e], wd_f.at[s], sem.at[s, 1]))

    @pl.when(t == 0)
    def _():
        for cp in weight_copies(te_s[0], par_s[0]):
            cp.start()

    @pl.when((t == 0) | (te_s[t] != te_s[jnp.maximum(t - 1, 0)]))
    def _():
        s = par_s[t]
        for cp in weight_copies(te_s[t], s):
            cp.wait()
        wg_b[...] = wg_f[s].astype(BF16)
        wd_b[...] = wd_f[s].astype(BF16)

        @pl.when(nxt_s[t] >= 0)
        def _():
            for cp in weight_copies(nxt_s[t], 1 - s):
                cp.start()

    @pl.when(t < nt_s[0])
    def _():
        gu = jnp.dot(x_ref[...].astype(BF16), wg_b[...], preferred_element_type=F32) + bgu_ref[...]
        gate = jnp.minimum(gu[:, :D_FF], SWIGLU_LIMIT)
        up = jnp.clip(gu[:, D_FF:], -SWIGLU_LIMIT, SWIGLU_LIMIT)
        act = (up + 1.0) * gate * _sigmoid(SWIGLU_ALPHA * gate)
        y_ref[...] = jnp.dot(act.astype(BF16), wd_b[...], preferred_element_type=F32) + bd_ref[...]

    @pl.when(t >= nt_s[0])
    def _():
        y_ref[...] = jnp.zeros_like(y_ref)


def _experts(plan, xs, w_gu, b_gu, w_down, b_down):
    n_rows = xs.shape[0]
    tile = lambda t, te, nxt, par, nt: (jnp.minimum(t, nt[0] - 1), 0)
    out_tile = lambda t, te, nxt, par, nt: (t, 0)
    of_expert = lambda t, te, nxt, par, nt: (te[t], 0, 0)
    grid_spec = pltpu.PrefetchScalarGridSpec(
        num_scalar_prefetch=4,
        grid=(n_rows // EXP_TM,),
        in_specs=[
            pl.BlockSpec((EXP_TM, D_MODEL), tile),
            pl.BlockSpec(memory_space=pl.ANY),
            pl.BlockSpec((None, 1, 2 * D_FF), of_expert),
            pl.BlockSpec(memory_space=pl.ANY),
            pl.BlockSpec((None, 1, D_MODEL), of_expert),
        ],
        out_specs=pl.BlockSpec((EXP_TM, D_MODEL), out_tile),
        scratch_shapes=[pltpu.VMEM((2, D_MODEL, 2 * D_FF), F32), pltpu.VMEM((2, D_FF, D_MODEL), F32),
                        pltpu.VMEM((D_MODEL, 2 * D_FF), BF16), pltpu.VMEM((D_FF, D_MODEL), BF16),
                        pltpu.SemaphoreType.DMA((2, 2))],
    )
    return pl.pallas_call(
        _experts_body,
        grid_spec=grid_spec,
        out_shape=jax.ShapeDtypeStruct((n_rows, D_MODEL), F32),
        compiler_params=_params("arbitrary"),
        name="moe_experts",
    )(plan["tile_expert"], plan["tile_next"], plan["tile_slot"], plan["ntiles"], xs, w_gu, b_gu, w_down, b_down)


def _combine_body(loff_s, pc_s, goff_s, totc_s, route_ref, tril_ref, loffrow_ref, x1_ref, gfin_ref, ys_hbm,
                  y_ref, buf, sem):
    b = pl.program_id(0)
    nb = pl.num_programs(0)
    slot = lax.rem(b, 2)

    def copy_in(slot_, lchunk, gchunk):
        return pltpu.make_async_copy(_chunk_rows(ys_hbm, gchunk), _chunk_rows(buf.at[slot_], lchunk), sem.at[slot_])

    def fetch_block(bb, slot_):
        def fetch_expert(e, carry):
            base = bb * N_EXPERTS + e
            lo, go = loff_s[base], goff_s[base]

            def one(c, carry2):
                copy_in(slot_, lo + c, go + c).start()
                return carry2
            lax.fori_loop(0, pc_s[base], one, 0)
            return carry
        lax.fori_loop(0, N_EXPERTS, fetch_expert, 0)

    def wait_block(bb, slot_):
        _wait_chunks(totc_s[bb], lambda rows: pltpu.make_async_copy(
            ys_hbm.at[pl.ds(0, rows)], buf.at[slot_, pl.ds(0, rows)], sem.at[slot_]))

    @pl.when(b == 0)
    def _():
        buf[...] = jnp.zeros_like(buf)
        fetch_block(0, 0)

    @pl.when(b + 1 < nb)
    def _():
        fetch_block(b + 1, 1 - slot)

    wait_block(b, slot)

    route = route_ref[...]
    lane = lax.broadcasted_iota(jnp.int32, (1, LANES), 1).astype(F32)
    sel = jnp.where(lane < float(N_EXPERTS), route, 0.0).astype(BF16)
    rank = jnp.dot(tril_ref[...], sel, preferred_element_type=F32)
    lpos = loffrow_ref[...] * float(ROW_CH) + rank
    lposk, pk = [], []
    for k in range(TOP_K):
        idx = route[:, ROUTE_IDX + k:ROUTE_IDX + k + 1]
        lposk.append(jnp.sum(jnp.where(lane == idx, lpos, 0.0), axis=-1, keepdims=True))
        pk.append(route[:, ROUTE_P + k:ROUTE_P + k + 1])
    acc = x1_ref[...]
    for r0 in range(0, RB, EXP_TM):
        col = lax.broadcasted_iota(jnp.int32, (TBK, EXP_TM), 1).astype(F32) + float(r0)
        w = functools.reduce(jnp.add, [jnp.where(lposk[k] == col, pk[k], 0.0) for k in range(TOP_K)])
        acc = acc + jnp.dot(w.astype(BF16), buf[slot, r0:r0 + EXP_TM, :].astype(BF16), preferred_element_type=F32)
    y_ref[...] = _rms(acc, gfin_ref[...])


def _combine(plan, blocks, route, loffrow, x1, g_final, ys):
    b0, b1 = blocks
    nb = b1 - b0
    tril = jnp.asarray(np.tril(np.ones((TBK, TBK), np.float32), -1), BF16)
    flat = lambda a: a[b0:b1].reshape(-1)
    grid_spec = pltpu.PrefetchScalarGridSpec(
        num_scalar_prefetch=4,
        grid=(nb,),
        in_specs=[
            pl.BlockSpec((TBK, LANES), lambda b, *_: (b + b0, 0)),
            pl.BlockSpec((TBK, TBK), lambda b, *_: (0, 0)),
            pl.BlockSpec((None, 1, LANES), lambda b, *_: (b + b0, 0, 0)),
            pl.BlockSpec((TBK, D_MODEL), lambda b, *_: (b + b0, 0)),
            pl.BlockSpec((1, D_MODEL), lambda b, *_: (0, 0)),
            pl.BlockSpec(memory_space=pl.ANY),
        ],
        out_specs=pl.BlockSpec((TBK, D_MODEL), lambda b, *_: (b, 0)),
        scratch_shapes=[pltpu.VMEM((2, RB, D_MODEL), F32), pltpu.SemaphoreType.DMA((2,))],
    )
    return pl.pallas_call(
        _combine_body,
        grid_spec=grid_spec,
        out_shape=jax.ShapeDtypeStruct((nb * TBK, D_MODEL), F32),
        compiler_params=_params("arbitrary"),
        name="moe_combine",
    )(flat(plan["loff"]), flat(plan["pc"]), flat(plan["goff"]), plan["totc"][b0:b1],
      route, tril, loffrow, x1, g_final, ys)


def kernel(x_prompt, x_sample, cache_k, cache_v, state_s, g_mix, w_in, rel_bias, lb_logits, g_out_norm,
           w_pa, w_pb, w_out, g_ffn, w_router, b_router, w_gu, b_gu, w_down, b_down, g_final):
    B, T = x_prompt.shape[:2]
    DB, S = x_sample.shape[:2]
    depth = w_in.shape[0]
    assert depth == 1 and T % ATT_QBLK == 0 and S == CHUNK
    cw = cache_k.shape[2]
    assert cw == WINDOW
    l = 0

    lower = jnp.cumsum(jax.nn.softmax(lb_logits.astype(F32), axis=0), axis=0)[l].reshape(1, HG_WIDTH)
    w_in_b = w_in[l].astype(BF16)
    wpa, wpb, wout = w_pa[l].astype(BF16), w_pb[l].astype(BF16), w_out[l].astype(BF16)
    row = lambda a: a.reshape(1, -1).astype(F32)
    base = _rel_bias_base(rel_bias[l])
    b_gu3 = b_gu[l].reshape(N_EXPERTS, 1, 2 * D_FF)
    b_down3 = b_down[l].reshape(N_EXPERTS, 1, D_MODEL)
    pad_e = LANES - N_EXPERTS
    wr = jnp.pad(w_router[l].astype(F32), ((0, 0), (0, pad_e)))
    br = jnp.concatenate([b_router[l].astype(F32), jnp.full((pad_e,), NEG, F32)]).reshape(1, LANES)

    n_tok = B * T + DB * S
    nb, nbp = n_tok // TBK, (B * T) // TBK

    def front(x, batch, seq, s0, attend):
        xf = x.reshape(batch * seq, D_MODEL)
        za, zb, zg = _inproj(xf, row(g_mix[l]), w_in_b)
        att = attend(za)
        hg, s_fin = _hgrn(zb, s0, lower, row(g_out_norm[l]), batch, seq)
        za3 = za.reshape(batch, seq, 3 * ATT_WIDTH)
        heads = lambda a: a.reshape(1, batch, a.shape[1], ATT_HEADS, ATT_DIM)
        keep = min(WINDOW, seq)
        nk = heads(za3[:, seq - keep:, ATT_WIDTH:2 * ATT_WIDTH])
        nv = heads(za3[:, seq - keep:, 2 * ATT_WIDTH:])
        return dict(mix=(att, hg, zg, xf), nk=nk, nv=nv, s=s_fin[None])

    ck = cache_k[l].reshape(DB, cw, ATT_WIDTH)
    cv = cache_v[l].reshape(DB, cw, ATT_WIDTH)
    fp = front(x_prompt, B, T, jnp.zeros((B, HG_HEADS, HG_DK, HG_DK), F32), lambda za: _attn_prompt(za, base, B, T))
    fs = front(x_sample, DB, S, state_s[l].astype(F32), lambda za: _attn_sample(za, ck, cv, base, DB, S))

    x1, h2, route, cnt = _merge(fp["mix"], fs["mix"], wpa, wpb, wout, row(g_ffn[l]), wr, br)
    cnt = cnt[:, 0, :N_EXPERTS].astype(jnp.int32)
    max_rows = n_tok * TOP_K + nb * N_EXPERTS * (ROW_CH - 1) + N_EXPERTS * (EXP_TM - 1)
    nt_max = -(-max_rows // EXP_TM)
    plan = _route_plan(cnt, nt_max)
    loff_f = plan["loff"].astype(F32)
    xs = _dispatch(plan, h2, route, loff_f[:, :, None], nt_max * EXP_TM)
    ysort = _experts(plan, xs, w_gu[l], b_gu3, w_down[l], b_down3)
    loffrow = jnp.pad(loff_f, ((0, 0), (0, pad_e)))[:, None, :]
    yp = _combine(plan, (0, nbp), route, loffrow, x1, row(g_final), ysort)
    ys = _combine(plan, (nbp, nb), route, loffrow, x1, row(g_final), ysort)
    return (yp.reshape(B, T, D_MODEL), ys.reshape(DB, S, D_MODEL), fp["nk"], fp["nv"], fp["s"],
            fs["nk"], fs["nv"], fs["s"])
```

```python
import functools

import numpy as np
import jax
import jax.numpy as jnp
from jax import lax
from jax.experimental import pallas as pl
from jax.experimental.pallas import tpu as pltpu

F32 = jnp.float32
BF16 = jnp.bfloat16

D_MODEL = 1024
CHUNK = 64
LEFT_CHUNKS = 8
WINDOW = LEFT_CHUNKS * CHUNK
ATT_HEADS = 8
ATT_DIM = 64
ATT_WIDTH = ATT_HEADS * ATT_DIM
MAX_REL = 256
HG_HEADS = 4
HG_DK = 128
HG_WIDTH = HG_HEADS * HG_DK
N_EXPERTS = 32
TOP_K = 4
D_FF = D_MODEL
SWIGLU_LIMIT = 7.0
SWIGLU_ALPHA = 1.702
RMS_EPS = 1e-5

LANES = 128
NEG = -1e30
ATT_QBLK = 4 * CHUNK
ATT_KBLKS = LEFT_CHUNKS * CHUNK // ATT_QBLK + 1
HG_C = 128
VMEM_LIMIT = 56 * 1024 * 1024
BIAS_W = 1024
SUBLANES = 8
TBK = 256
ROW_CH = SUBLANES
RB = TBK * TOP_K + N_EXPERTS * ROW_CH
EXP_TM = 256
CH_PER_TILE = EXP_TM // ROW_CH
BLK_CH = RB // ROW_CH
ROUTE_IDX = 64
ROUTE_P = 72

NT = (((1,), (1,)), ((), ()))
TN = (((0,), (0,)), ((), ()))


def _rms(x, g):
    return x * lax.rsqrt(jnp.mean(x * x, axis=-1, keepdims=True) + RMS_EPS) * g


def _sigmoid(x):
    return 1.0 / (1.0 + jnp.exp(-x))


def _params(*sem):
    return pltpu.CompilerParams(dimension_semantics=sem, vmem_limit_bytes=VMEM_LIMIT)


def _inproj_body(x_ref, g_ref, w_ref, za_ref, zb_ref, zg_ref):
    h = _rms(x_ref[...], g_ref[...]).astype(BF16)
    a, b = 3 * ATT_WIDTH, 3 * ATT_WIDTH + 4 * HG_WIDTH
    za_ref[...] = jnp.dot(h, w_ref[:, :a], preferred_element_type=F32)
    zb_ref[...] = jnp.dot(h, w_ref[:, a:b], preferred_element_type=F32)
    zg_ref[...] = jnp.dot(h, w_ref[:, b:], preferred_element_type=F32)


def _inproj(x, g, w_bf16, tm=256):
    n = x.shape[0]
    cols = w_bf16.shape[1]
    wa, wb, wg = 3 * ATT_WIDTH, 4 * HG_WIDTH, 2 * D_MODEL
    return pl.pallas_call(
        _inproj_body,
        grid=(n // tm,),
        in_specs=[
            pl.BlockSpec((tm, D_MODEL), lambda i: (i, 0)),
            pl.BlockSpec((1, D_MODEL), lambda i: (0, 0)),
            pl.BlockSpec((D_MODEL, cols), lambda i: (0, 0)),
        ],
        out_specs=[
            pl.BlockSpec((tm, wa), lambda i: (i, 0)),
            pl.BlockSpec((tm, wb), lambda i: (i, 0)),
            pl.BlockSpec((tm, wg), lambda i: (i, 0)),
        ],
        out_shape=[
            jax.ShapeDtypeStruct((n, wa), F32),
            jax.ShapeDtypeStruct((n, wb), F32),
            jax.ShapeDtypeStruct((n, wg), F32),
        ],
        compiler_params=_params("arbitrary"),
        name="inproj",
    )(x, g, w_bf16)


def _attn_heads(q_ref, k_refs, v_refs, bias_fn, pens, o_ref):
    lane = lax.broadcasted_iota(jnp.int32, (1, LANES), 1)
    first = lane < ATT_DIM
    for hp in range(ATT_HEADS // 2):
        sl = slice(hp * LANES, (hp + 1) * LANES)
        q2 = q_ref[:, sl] * (ATT_DIM ** -0.5)
        ks = [k[:, sl].astype(BF16) for k in k_refs]
        outs = []
        for half in range(2):
            head = 2 * hp + half
            mine = first if half == 0 else lane >= ATT_DIM
            qm = jnp.where(mine, q2, 0.0).astype(BF16)
            vs = [jnp.where(mine, v[:, sl], 1.0).astype(BF16) for v in v_refs]
            ss = []
            for j, kj in enumerate(ks):
                s = lax.dot_general(qm, kj, NT, preferred_element_type=F32) + bias_fn(head, j)
                if pens[j] is not None:
                    s = s + pens[j]
                ss.append(s)
            if all(s.shape == ss[0].shape for s in ss):
                m = jnp.max(functools.reduce(jnp.maximum, ss), axis=-1, keepdims=True)
            else:
                m = functools.reduce(jnp.maximum, [jnp.max(s, axis=-1, keepdims=True) for s in ss])
            outs.append(functools.reduce(jnp.add, [jnp.dot(jnp.exp(s - m).astype(BF16), vj, preferred_element_type=F32)
                                                   for s, vj in zip(ss, vs)]))
        num = jnp.where(first, outs[0], outs[1])
        den = pltpu.roll(jnp.where(first, outs[1], outs[0]), ATT_DIM, 1)
        o_ref[:, sl] = num * (1.0 / den)


def _fill_bias(base_ref, bias_ref, banded):
    nq, nk = bias_ref.shape[1:]
    if banded:
        r = lax.broadcasted_iota(jnp.int32, (nq, nk), 0)
        s = lax.broadcasted_iota(jnp.int32, (nq, nk), 1)
        qc = (r + WINDOW) // CHUNK
        kc = s // CHUNK
        pen = jnp.where(kc <= qc, jnp.where(kc >= qc - LEFT_CHUNKS, 0.0, NEG), NEG)
    for h in range(ATT_HEADS):
        rows = jnp.broadcast_to(base_ref[h:h + 1, :], (nq, BIAS_W))
        t = pltpu.roll(rows, 0, 1, stride=1, stride_axis=0)[:, :nk]
        bias_ref[h] = t + pen if banded else t


def _attn_prompt_body(q_ref, k0, k1, k2, v0, v1, v2, base_ref, o_ref, bias_ref):
    i = pl.program_id(1)

    @pl.when((pl.program_id(0) == 0) & (i == 0))
    def _():
        _fill_bias(base_ref, bias_ref, True)

    pens = [jnp.where(i - (ATT_KBLKS - 1) + j >= 0, 0.0, NEG) for j in range(ATT_KBLKS - 1)] + [None]
    bias_fn = lambda h, j: bias_ref[h, :, j * ATT_QBLK:(j + 1) * ATT_QBLK]
    _attn_heads(q_ref, [k0, k1, k2], [v0, v1, v2], bias_fn, pens, o_ref)


def _attn_prompt(za, base, batch, seq):
    nq = seq // ATT_QBLK
    back = ATT_KBLKS - 1
    qspec = pl.BlockSpec((ATT_QBLK, ATT_WIDTH), lambda b, i: (b * nq + i, 0))

    def kvspec(j, col):
        return pl.BlockSpec((ATT_QBLK, ATT_WIDTH),
                            lambda b, i: (b * nq + jnp.maximum(i - back + j, 0), col))

    return pl.pallas_call(
        _attn_prompt_body,
        grid=(batch, nq),
        in_specs=[qspec] + [kvspec(j, 1) for j in range(ATT_KBLKS)] + [kvspec(j, 2) for j in range(ATT_KBLKS)]
        + [pl.BlockSpec(base.shape, lambda b, i: (0, 0))],
        out_specs=pl.BlockSpec((ATT_QBLK, ATT_WIDTH), lambda b, i: (b * nq + i, 0)),
        out_shape=jax.ShapeDtypeStruct((batch * seq, ATT_WIDTH), F32),
        scratch_shapes=[pltpu.VMEM((ATT_HEADS, ATT_QBLK, ATT_KBLKS * ATT_QBLK), F32)],
        compiler_params=_params("arbitrary", "arbitrary"),
        name="attn_prompt",
    )(za, za, za, za, za, za, za, base)


def _attn_sample_body(q_ref, kn_ref, vn_ref, ck_ref, cv_ref, base_ref, o_ref, bias_ref):
    @pl.when(pl.program_id(0) == 0)
    def _():
        _fill_bias(base_ref, bias_ref, False)

    cw = ck_ref.shape[0]
    bias_fn = lambda h, j: bias_ref[h, :, :cw] if j == 0 else bias_ref[h, :, cw:]
    _attn_heads(q_ref, [ck_ref, kn_ref], [cv_ref, vn_ref], bias_fn, [None, None], o_ref)


def _attn_sample(za, ck, cv, base, batch, seq):
    cw = ck.shape[1]
    return pl.pallas_call(
        _attn_sample_body,
        grid=(batch,),
        in_specs=[
            pl.BlockSpec((seq, ATT_WIDTH), lambda b: (b, 0)),
            pl.BlockSpec((seq, ATT_WIDTH), lambda b: (b, 1)),
            pl.BlockSpec((seq, ATT_WIDTH), lambda b: (b, 2)),
            pl.BlockSpec((None, cw, ATT_WIDTH), lambda b: (b, 0, 0)),
            pl.BlockSpec((None, cw, ATT_WIDTH), lambda b: (b, 0, 0)),
            pl.BlockSpec(base.shape, lambda b: (0, 0)),
        ],
        out_specs=pl.BlockSpec((seq, ATT_WIDTH), lambda b: (b, 0)),
        out_shape=jax.ShapeDtypeStruct((batch * seq, ATT_WIDTH), F32),
        scratch_shapes=[pltpu.VMEM((ATT_HEADS, seq, cw + seq), F32)],
        compiler_params=_params("arbitrary"),
        name="attn_sample",
    )(za, za, za, ck, cv, base)


def _rel_bias_base(table):
    top = table[:, 2 * MAX_REL:].astype(F32)
    rev = table[:, ::-1][:, :2 * MAX_REL].astype(F32)
    left = WINDOW - MAX_REL
    return jnp.concatenate([jnp.broadcast_to(top, (ATT_HEADS, left)), rev,
                            jnp.broadcast_to(top, (ATT_HEADS, BIAS_W - left - 2 * MAX_REL))], axis=1)


def _hgrn_consts(c):
    t = np.arange(c)[:, None]
    j = np.arange(c)[None, :]
    mats = [j <= t, j > t]
    masks = []
    m = c // 2
    while m >= 1:
        ref = (t // (2 * m)) * (2 * m) + m - 1
        second = (t % (2 * m)) >= m
        mats.append((second & (j > ref) & (j <= t)) | (~second & (j > t) & (j <= ref)))
        masks.append((t // (2 * m)) == (j // (2 * m)))
        m //= 2
    return (jnp.asarray(np.concatenate(mats, 0).astype(np.float32), BF16),
            jnp.asarray(np.stack(masks).astype(np.float32)))


def _hgrn_body(zb_ref, s0_ref, lower_ref, gon_ref, p_ref, mask_ref, o_ref, sfin_ref, st_ref, *, single_step):
    c = zb_ref.shape[0]
    step = pl.program_id(1)

    def load_state():
        for h in range(HG_HEADS):
            st_ref[h] = s0_ref[0, h].T

    if single_step:
        load_state()
    else:
        pl.when(step == 0)(load_state)

    pmat = p_ref[...]
    n_levels = mask_ref.shape[0]
    part = lambda i: zb_ref[:, i * HG_WIDTH:(i + 1) * HG_WIDTH]
    head = lambda a, h: a[:, h * HG_DK:(h + 1) * HG_DK]
    q = part(0)
    low = lower_ref[...]
    f = low + (1.0 - low) * _sigmoid(part(1))
    lf = jnp.log(f)
    k = 1.0 - f
    ib = part(2)
    v = ib * _sigmoid(ib)
    og = part(3)

    hi = lf.astype(BF16)
    r1 = lf - hi.astype(F32)
    mid = r1.astype(BF16)
    lo = (r1 - mid.astype(F32)).astype(BF16)
    e = (jnp.dot(pmat, hi, preferred_element_type=F32) + jnp.dot(pmat, mid, preferred_element_type=F32)
         + jnp.dot(pmat, lo, preferred_element_type=F32))
    b = e[0:c]
    decay = jnp.exp(e[c - 1:c])
    qe = (q * jnp.exp(b)).astype(BF16)
    kt = (k * jnp.exp(e[c:2 * c])).astype(BF16)
    vb = v.astype(BF16)
    qk = q * k
    gate = og * _sigmoid(og)

    row = lax.broadcasted_iota(jnp.int32, (c, HG_WIDTH), 0)
    att = [None] * HG_HEADS
    for lvl in range(n_levels):
        m = c >> (lvl + 1)
        x = jnp.exp(e[(2 + lvl) * c:(3 + lvl) * c])
        second = (row & m) != 0
        qm = jnp.where(second, q * x, 0.0).astype(BF16)
        km = jnp.where(second, 0.0, k * x).astype(BF16)
        for h in range(HG_HEADS):
            a = lax.dot_general(head(qm, h), head(km, h), NT, preferred_element_type=F32)
            if lvl > 0:
                a = a * mask_ref[lvl]
            att[h] = a if att[h] is None else att[h] + a

    for h in range(HG_HEADS):
        st = st_ref[h]
        inter = lax.dot_general(head(qe, h), st.astype(BF16), NT, preferred_element_type=F32)
        intra = jnp.dot(att[h].astype(BF16), head(vb, h), preferred_element_type=F32)
        intra = intra + jnp.sum(head(qk, h), axis=-1, keepdims=True) * head(v, h)
        st_ref[h] = st * head(decay, h) + lax.dot_general(head(vb, h), head(kt, h), TN, preferred_element_type=F32)
        o_ref[:, h * HG_DK:(h + 1) * HG_DK] = _rms(inter + intra, gon_ref[...]) * head(gate, h)

    def write_state():
        for h in range(HG_HEADS):
            sfin_ref[0, h] = st_ref[h].T

    if single_step:
        write_state()
    else:
        pl.when(step == pl.num_programs(1) - 1)(write_state)


def _hgrn(zb, s0, lower, g_on, batch, seq):
    c = min(HG_C, seq)
    assert seq % c == 0
    pmat, masks = _hgrn_consts(c)
    nc = seq // c
    return pl.pallas_call(
        functools.partial(_hgrn_body, single_step=nc == 1),
        grid=(batch, nc),
        in_specs=[
            pl.BlockSpec((c, 4 * HG_WIDTH), lambda b, i: (b * nc + i, 0)),
            pl.BlockSpec((1, HG_HEADS, HG_DK, HG_DK), lambda b, i: (b, 0, 0, 0)),
            pl.BlockSpec((1, HG_WIDTH), lambda b, i: (0, 0)),
            pl.BlockSpec((1, HG_DK), lambda b, i: (0, 0)),
            pl.BlockSpec(pmat.shape, lambda b, i: (0, 0)),
            pl.BlockSpec(masks.shape, lambda b, i: (0, 0, 0)),
        ],
        out_specs=[
            pl.BlockSpec((c, HG_WIDTH), lambda b, i: (b * nc + i, 0)),
            pl.BlockSpec((1, HG_HEADS, HG_DK, HG_DK), lambda b, i: (b, 0, 0, 0)),
        ],
        out_shape=[
            jax.ShapeDtypeStruct((batch * seq, HG_WIDTH), F32),
            jax.ShapeDtypeStruct((batch, HG_HEADS, HG_DK, HG_DK), F32),
        ],
        scratch_shapes=[pltpu.VMEM((HG_HEADS, HG_DK, HG_DK), F32)],
        compiler_params=_params("arbitrary", "arbitrary"),
        name="hgrn2",
    )(zb, s0, lower, g_on, pmat, masks)


def _split_bf16(x):
    hi = x.astype(BF16)
    return hi, (x - hi.astype(F32)).astype(BF16)


def _merge_body(att_p, hg_p, zg_p, x_p, att_s, hg_s, zg_s, x_s, wpa_ref, wpb_ref, wout_ref, gffn_ref, wr_ref, br_ref,
                x1_ref, h2_ref, route_ref, cnt_ref, *, n_first):
    weights = (wpa_ref, wpb_ref, wout_ref, gffn_ref, wr_ref, br_ref)
    outs = (x1_ref, h2_ref, route_ref, cnt_ref)
    i = pl.program_id(0)
    pl.when(i < n_first)(functools.partial(_merge_block, att_p, hg_p, zg_p, x_p, *weights, *outs))
    pl.when(i >= n_first)(functools.partial(_merge_block, att_s, hg_s, zg_s, x_s, *weights, *outs))


def _merge_block(att_ref, hg_ref, zg_ref, x_ref, wpa_ref, wpb_ref, wout_ref, gffn_ref, wr_ref, br_ref,
                 x1_ref, h2_ref, route_ref, cnt_ref):
    pa = jnp.dot(att_ref[...].astype(BF16), wpa_ref[...], preferred_element_type=F32)
    pb = jnp.dot(hg_ref[...].astype(BF16), wpb_ref[...], preferred_element_type=F32)
    y = _sigmoid(zg_ref[:, :D_MODEL]) * pa + _sigmoid(zg_ref[:, D_MODEL:]) * pb
    x1 = x_ref[...] + jnp.dot(y.astype(BF16), wout_ref[...], preferred_element_type=F32)
    x1_ref[...] = x1
    h2 = _rms(x1, gffn_ref[...])
    h2_ref[...] = h2.astype(BF16)

    h_hi, h_lo = _split_bf16(h2)
    w_hi, w_lo = _split_bf16(wr_ref[...])
    logits = (jnp.dot(h_hi, w_hi, preferred_element_type=F32) + jnp.dot(h_lo, w_hi, preferred_element_type=F32)
              + jnp.dot(h_hi, w_lo, preferred_element_type=F32)) + br_ref[...]
    lane = lax.broadcasted_iota(jnp.int32, logits.shape, 1).astype(F32)
    cur = logits
    vals, idxs = [], []
    for _ in range(TOP_K):
        m = jnp.max(cur, axis=-1, keepdims=True)
        idx = jnp.min(jnp.where(cur == m, lane, float(LANES)), axis=-1, keepdims=True)
        vals.append(m)
        idxs.append(idx)
        cur = jnp.where(lane == idx, -jnp.inf, cur)
    es = [jnp.exp(v - vals[0]) for v in vals]
    inv = 1.0 / functools.reduce(jnp.add, es)
    route = jnp.zeros_like(logits)
    for k, (ex, idx) in enumerate(zip(es, idxs)):
        route = (route + jnp.where(lane == idx, 1.0, 0.0) + jnp.where(lane == float(ROUTE_IDX + k), idx, 0.0)
                 + jnp.where(lane == float(ROUTE_P + k), ex * inv, 0.0))
    route_ref[...] = route
    cnt_ref[...] = jnp.sum(jnp.where(lane < float(N_EXPERTS), route, 0.0), axis=0, keepdims=True)


def _merge(first, second, wpa, wpb, wout, g_ffn, w_router, b_router):
    tm = TBK
    n1, n2 = first[3].shape[0] // tm, second[3].shape[0] // tm
    widths = (ATT_WIDTH, HG_WIDTH, 2 * D_MODEL, D_MODEL)
    spec1 = [pl.BlockSpec((tm, w), lambda i: (jnp.minimum(i, n1 - 1), 0)) for w in widths]
    spec2 = [pl.BlockSpec((tm, w), lambda i: (jnp.maximum(i - n1, 0), 0)) for w in widths]
    row = lambda w: pl.BlockSpec((tm, w), lambda i: (i, 0))
    full = lambda a: pl.BlockSpec(a.shape, lambda i: (0,) * a.ndim)
    n = (n1 + n2) * tm
    return pl.pallas_call(
        functools.partial(_merge_body, n_first=n1),
        grid=(n1 + n2,),
        in_specs=spec1 + spec2 + [full(wpa), full(wpb), full(wout), full(g_ffn), full(w_router), full(b_router)],
        out_specs=[row(D_MODEL), row(D_MODEL), row(LANES), pl.BlockSpec((None, 1, LANES), lambda i: (i, 0, 0))],
        out_shape=[
            jax.ShapeDtypeStruct((n, D_MODEL), F32),
            jax.ShapeDtypeStruct((n, D_MODEL), BF16),
            jax.ShapeDtypeStruct((n, LANES), F32),
            jax.ShapeDtypeStruct((n1 + n2, 1, LANES), F32),
        ],
        compiler_params=_params("arbitrary"),
        name="merge_router",
    )(*first, *second, wpa, wpb, wout, g_ffn, w_router, b_router)


def _route_plan(cnt, nt_max):
    pc = (cnt + ROW_CH - 1) // ROW_CH
    loff = jnp.cumsum(pc, axis=1) - pc
    tot = jnp.sum(pc, axis=0)
    reg = (tot + CH_PER_TILE - 1) // CH_PER_TILE * CH_PER_TILE
    gstart = jnp.cumsum(reg) - reg
    goff = gstart[None, :] + jnp.cumsum(pc, axis=0) - pc
    ntiles = jnp.sum(reg) // CH_PER_TILE
    tile_end = jnp.cumsum(reg // CH_PER_TILE)
    t = jnp.minimum(jnp.arange(nt_max, dtype=jnp.int32), ntiles - 1)
    tile_expert = jnp.sum((tile_end[None, :] <= t[:, None]).astype(jnp.int32), axis=1)
    present = reg > 0
    ids = jnp.arange(N_EXPERTS, dtype=jnp.int32)
    later = (ids[None, :] > ids[:, None]) & present[None, :]
    nxt = jnp.min(jnp.where(later, ids[None, :], N_EXPERTS), axis=1)
    nxt = jnp.where(nxt == N_EXPERTS, -1, nxt)
    slot = (jnp.cumsum(present.astype(jnp.int32)) - 1) % 2
    j = jnp.arange(BLK_CH, dtype=jnp.int32)
    run = jnp.sum(((loff + pc)[:, None, :] <= j[None, :, None]).astype(jnp.int32), axis=2)
    shift = jnp.sum(jnp.where(run[:, :, None] == ids[None, None, :], (goff - loff)[:, None, :], 0), axis=2)
    gmap = shift + j[None, :]
    i32 = lambda a: a.astype(jnp.int32)
    of_tile =lambda a: jnp.sum(jnp.where(tile_expert[:, None] == ids[None, :], a[None, :], 0), axis=1)
    return dict(loff=i32(loff), gmap=i32(gmap), totc=i32(jnp.sum(pc, axis=1)),
                pad_start=i32(gstart + tot), pad_cnt=i32(reg - tot), ntiles=i32(ntiles).reshape(1),
                tile_expert=i32(tile_expert), tile_next=i32(of_tile(nxt)), tile_slot=i32(of_tile(slot)))


def _chunk_rows(ref, chunk):
    return ref.at[pl.ds(pl.multiple_of(chunk * ROW_CH, ROW_CH), ROW_CH)]


def _for_chunks(n, do):
    log_unroll = 2
    groups = lax.shift_right_logical(n, log_unroll)

    def group(i, carry):
        for u in range(1 << log_unroll):
            do(lax.shift_left(i, log_unroll) + u)
        return carry
    lax.fori_loop(0, groups, group, 0)

    def single(c, carry):
        do(c)
        return carry
    lax.fori_loop(lax.shift_left(groups, log_unroll), n, single, 0)


def _wait_chunks(n, copy_of_rows):
    for bit in range((RB // ROW_CH).bit_length()):
        @pl.when(((n >> bit) & 1) == 1)
        def _(bit=bit):
            copy_of_rows((1 << bit) * ROW_CH).wait()


def _dispatch_body(gmap_s, totc_s, pads_s, padn_s, nt_s, h2_ref, route_ref, tri_ref, loffv_ref,
                   xs_hbm, buf, zbuf, sem):
    b = pl.program_id(0)
    nb = pl.num_programs(0)
    slot = lax.rem(b, 2)
    fill_sem, tile_sem = 2, 3
    nt_max = xs_hbm.shape[0] // EXP_TM

    def copy_out(src, gchunk, sem_i):
        return pltpu.make_async_copy(src, _chunk_rows(xs_hbm, gchunk), sem.at[sem_i])

    def zero_tile(t):
        return pltpu.make_async_copy(zbuf, xs_hbm.at[pl.ds(pl.multiple_of(t * EXP_TM, EXP_TM), EXP_TM)],
                                     sem.at[tile_sem])

    def wait_block(bb, slot_):
        _wait_chunks(totc_s[bb], lambda rows: pltpu.make_async_copy(
            buf.at[slot_, pl.ds(0, rows)], xs_hbm.at[pl.ds(0, rows)], sem.at[slot_]))

    @pl.when(b == 0)
    def _():
        zbuf[...] = jnp.zeros_like(zbuf)

    @pl.when(b >= 2)
    def _():
        wait_block(b - 2, slot)

    rt = route_ref[...].T
    rank_t = jnp.dot(rt[0:N_EXPERTS].astype(BF16), tri_ref[...], preferred_element_type=F32)
    lpos_t = loffv_ref[...] * float(ROW_CH) + rank_t
    erow = lax.broadcasted_iota(jnp.int32, (N_EXPERTS, TBK), 0).astype(F32)
    lposk = [jnp.sum(jnp.where(rt[ROUTE_IDX + k:ROUTE_IDX + k + 1] == erow, lpos_t, 0.0), axis=0, keepdims=True)
             for k in range(TOP_K)]
    h2 = h2_ref[...]
    for r0 in range(0, RB, EXP_TM):
        rrow = lax.broadcasted_iota(jnp.int32, (EXP_TM, TBK), 0).astype(F32) + float(r0)
        perm = functools.reduce(jnp.add, [jnp.where(lposk[k] == rrow, 1.0, 0.0) for k in range(TOP_K)])
        buf[slot, r0:r0 + EXP_TM, :] = jnp.dot(perm.astype(BF16), h2, preferred_element_type=F32)

    _for_chunks(totc_s[b], lambda c: copy_out(_chunk_rows(buf.at[slot], c), gmap_s[b * BLK_CH + c], slot).start())

    @pl.when(b == nb - 1)
    def _():
        zchunk = _chunk_rows(zbuf, 0)

        def fill_expert(e, carry):
            def one(c, carry2):
                copy_out(zchunk, pads_s[e] + c, fill_sem).start()
                return carry2
            lax.fori_loop(0, padn_s[e], one, 0)
            return carry
        lax.fori_loop(0, N_EXPERTS, fill_expert, 0)

        def fill_tile(t, carry):
            zero_tile(t).start()
            return carry
        lax.fori_loop(nt_s[0], nt_max, fill_tile, 0)

        @pl.when(b >= 1)
        def _():
            wait_block(b - 1, 1 - slot)
        wait_block(b, slot)

        def drain_expert(e, carry):
            def one(c, carry2):
                copy_out(zchunk, 0, fill_sem).wait()
                return carry2
            lax.fori_loop(0, padn_s[e], one, 0)
            return carry
        lax.fori_loop(0, N_EXPERTS, drain_expert, 0)

        def drain_tile(t, carry):
            zero_tile(0).wait()
            return carry
        lax.fori_loop(nt_s[0], nt_max, drain_tile, 0)


def _dispatch(plan, h2, route, loffv, n_rows):
    nb = h2.shape[0] // TBK
    tri = jnp.asarray(np.triu(np.ones((TBK, TBK), np.float32), 1), BF16)
    grid_spec = pltpu.PrefetchScalarGridSpec(
        num_scalar_prefetch=5,
        grid=(nb,),
        in_specs=[
            pl.BlockSpec((TBK, D_MODEL), lambda b, *_: (b, 0)),
            pl.BlockSpec((TBK, LANES), lambda b, *_: (b, 0)),
            pl.BlockSpec((TBK, TBK), lambda b, *_: (0, 0)),
            pl.BlockSpec((None, N_EXPERTS, 1), lambda b, *_: (b, 0, 0)),
        ],
        out_specs=pl.BlockSpec(memory_space=pl.ANY),
        scratch_shapes=[pltpu.VMEM((2, RB, D_MODEL), F32), pltpu.VMEM((EXP_TM, D_MODEL), F32),
                        pltpu.SemaphoreType.DMA((4,))],
    )
    return pl.pallas_call(
        _dispatch_body,
        grid_spec=grid_spec,
        out_shape=jax.ShapeDtypeStruct((n_rows, D_MODEL), F32),
        compiler_params=_params("arbitrary"),
        name="moe_dispatch",
    )(plan["gmap"].reshape(-1), plan["totc"], plan["pad_start"], plan["pad_cnt"], plan["ntiles"],
      h2, route, tri, loffv)


def _experts_body(te_s, nxt_s, par_s, nt_s, x_ref, wgu_hbm, bgu_ref, wd_hbm, bd_ref, y_ref,
                  wg_f, wd_f, wg_b, wd_b, sem):
    t = pl.program_id(0)

    def weight_copies(e, s):
        return (pltpu.make_async_copy(wgu_hbm.at[e], wg_f.at[s], sem.at[s, 0]),
                pltpu.make_async_copy(wd_hbm.at[e], wd_f.at[s], sem.at[s, 1]))

    @pl.when(t == 0)
    def _():
        for cp in weight_copies(te_s[0], par_s[0]):
            cp.start()

    @pl.when((t == 0) | (te_s[t] != te_s[jnp.maximum(t - 1, 0)]))
    def _():
        s = par_s[t]
        for cp in weight_copies(te_s[t], s):
            cp.wait()
        wg_b[...] = wg_f[s].astype(BF16)
        wd_b[...] = wd_f[s].astype(BF16)

        @pl.when(nxt_s[t] >= 0)
        def _():
            for cp in weight_copies(nxt_s[t], 1 - s):
                cp.start()

    @pl.when(t < nt_s[0])
    def _():
        gu = jnp.dot(x_ref[...].astype(BF16), wg_b[...], preferred_element_type=F32) + bgu_ref[...]
        gate = jnp.minimum(gu[:, :D_FF], SWIGLU_LIMIT)
        up = jnp.clip(gu[:, D_FF:], -SWIGLU_LIMIT, SWIGLU_LIMIT)
        act = (up + 1.0) * gate * _sigmoid(SWIGLU_ALPHA * gate)
        y_ref[...] = jnp.dot(act.astype(BF16), wd_b[...], preferred_element_type=F32) + bd_ref[...]

    @pl.when(t >= nt_s[0])
    def _():
        y_ref[...] = jnp.zeros_like(y_ref)


def _experts(plan, xs, w_gu, b_gu, w_down, b_down):
    n_rows = xs.shape[0]
    tile = lambda t, te, nxt, par, nt: (jnp.minimum(t, nt[0] - 1), 0)
    out_tile = lambda t, te, nxt, par, nt: (t, 0)
    of_expert = lambda t, te, nxt, par, nt: (te[t], 0, 0)
    grid_spec = pltpu.PrefetchScalarGridSpec(
        num_scalar_prefetch=4,
        grid=(n_rows // EXP_TM,),
        in_specs=[
            pl.BlockSpec((EXP_TM, D_MODEL), tile),
            pl.BlockSpec(memory_space=pl.ANY),
            pl.BlockSpec((None, 1, 2 * D_FF), of_expert),
            pl.BlockSpec(memory_space=pl.ANY),
            pl.BlockSpec((None, 1, D_MODEL), of_expert),
        ],
        out_specs=pl.BlockSpec((EXP_TM, D_MODEL), out_tile),
        scratch_shapes=[pltpu.VMEM((2, D_MODEL, 2 * D_FF), F32), pltpu.VMEM((2, D_FF, D_MODEL), F32),
                        pltpu.VMEM((D_MODEL, 2 * D_FF), BF16), pltpu.VMEM((D_FF, D_MODEL), BF16),
                        pltpu.SemaphoreType.DMA((2, 2))],
    )
    return pl.pallas_call(
        _experts_body,
        grid_spec=grid_spec,
        out_shape=jax.ShapeDtypeStruct((n_rows, D_MODEL), F32),
        compiler_params=_params("arbitrary"),
        name="moe_experts",
    )(plan["tile_expert"], plan["tile_next"], plan["tile_slot"], plan["ntiles"], xs, w_gu, b_gu, w_down, b_down)


def _combine_body(gmap_s, totc_s, route_ref, tril_ref, loffrow_ref, x1_ref, gfin_ref, ys_hbm, y_ref, buf, sem):
    b = pl.program_id(0)
    nb = pl.num_programs(0)
    slot = lax.rem(b, 2)

    def copy_in(slot_, lchunk, gchunk):
        return pltpu.make_async_copy(_chunk_rows(ys_hbm, gchunk), _chunk_rows(buf.at[slot_], lchunk), sem.at[slot_])

    def fetch_block(bb, slot_):
        _for_chunks(totc_s[bb], lambda c: copy_in(slot_, c, gmap_s[bb * BLK_CH + c]).start())

    def wait_block(bb, slot_):
        _wait_chunks(totc_s[bb], lambda rows: pltpu.make_async_copy(
            ys_hbm.at[pl.ds(0, rows)], buf.at[slot_, pl.ds(0, rows)], sem.at[slot_]))

    @pl.when(b == 0)
    def _():
        buf[...] = jnp.zeros_like(buf)
        fetch_block(0, 0)

    @pl.when(b + 1 < nb)
    def _():
        fetch_block(b + 1, 1 - slot)

    wait_block(b, slot)

    route = route_ref[...]
    lane = lax.broadcasted_iota(jnp.int32, (1, LANES), 1).astype(F32)
    sel = jnp.where(lane < float(N_EXPERTS), route, 0.0).astype(BF16)
    rank = jnp.dot(tril_ref[...], sel, preferred_element_type=F32)
    lpos = loffrow_ref[...] * float(ROW_CH) + rank
    lposk, pk = [], []
    for k in range(TOP_K):
        idx = route[:, ROUTE_IDX + k:ROUTE_IDX + k + 1]
        lposk.append(jnp.sum(jnp.where(lane == idx, lpos, 0.0), axis=-1, keepdims=True))
        pk.append(route[:, ROUTE_P + k:ROUTE_P + k + 1])
    acc = x1_ref[...]
    for r0 in range(0, RB, EXP_TM):
        col = lax.broadcasted_iota(jnp.int32, (TBK, EXP_TM), 1).astype(F32) + float(r0)
        w = functools.reduce(jnp.add, [jnp.where(lposk[k] == col, pk[k], 0.0) for k in range(TOP_K)])
        acc = acc + jnp.dot(w.astype(BF16), buf[slot, r0:r0 + EXP_TM, :].astype(BF16), preferred_element_type=F32)
    y_ref[...] = _rms(acc, gfin_ref[...])


def _combine(plan, blocks, route, loffrow, x1, g_final, ys):
    b0, b1 = blocks
    nb = b1 - b0
    tril = jnp.asarray(np.tril(np.ones((TBK, TBK), np.float32), -1), BF16)
    grid_spec = pltpu.PrefetchScalarGridSpec(
        num_scalar_prefetch=2,
        grid=(nb,),
        in_specs=[
            pl.BlockSpec((TBK, LANES), lambda b, *_: (b + b0, 0)),
            pl.BlockSpec((TBK, TBK), lambda b, *_: (0, 0)),
            pl.BlockSpec((None, 1, LANES), lambda b, *_: (b + b0, 0, 0)),
            pl.BlockSpec((TBK, D_MODEL), lambda b, *_: (b + b0, 0)),
            pl.BlockSpec((1, D_MODEL), lambda b, *_: (0, 0)),
            pl.BlockSpec(memory_space=pl.ANY),
        ],
        out_specs=pl.BlockSpec((TBK, D_MODEL), lambda b, *_: (b, 0)),
        scratch_shapes=[pltpu.VMEM((2, RB, D_MODEL), F32), pltpu.SemaphoreType.DMA((2,))],
    )
    return pl.pallas_call(
        _combine_body,
        grid_spec=grid_spec,
        out_shape=jax.ShapeDtypeStruct((nb * TBK, D_MODEL), F32),
        compiler_params=_params("arbitrary"),
        name="moe_combine",
    )(plan["gmap"][b0:b1].reshape(-1), plan["totc"][b0:b1], route, tril, loffrow, x1, g_final, ys)


def kernel(x_prompt, x_sample, cache_k, cache_v, state_s, g_mix, w_in, rel_bias, lb_logits, g_out_norm,
           w_pa, w_pb, w_out, g_ffn, w_router, b_router, w_gu, b_gu, w_down, b_down, g_final):
    B, T = x_prompt.shape[:2]
    DB, S = x_sample.shape[:2]
    depth = w_in.shape[0]
    assert depth == 1 and T % ATT_QBLK == 0 and S == CHUNK
    cw = cache_k.shape[2]
    assert cw == WINDOW
    l = 0

    lower = jnp.cumsum(jax.nn.softmax(lb_logits.astype(F32), axis=0), axis=0)[l].reshape(1, HG_WIDTH)
    w_in_b = w_in[l].astype(BF16)
    wpa, wpb, wout = w_pa[l].astype(BF16), w_pb[l].astype(BF16), w_out[l].astype(BF16)
    row = lambda a: a.reshape(1, -1).astype(F32)
    base = _rel_bias_base(rel_bias[l])
    b_gu3 = b_gu[l].reshape(N_EXPERTS, 1, 2 * D_FF)
    b_down3 = b_down[l].reshape(N_EXPERTS, 1, D_MODEL)
    pad_e = LANES - N_EXPERTS
    wr = jnp.pad(w_router[l].astype(F32), ((0, 0), (0, pad_e)))
    br = jnp.concatenate([b_router[l].astype(F32), jnp.full((pad_e,), NEG, F32)]).reshape(1, LANES)

    n_tok = B * T + DB * S
    nb, nbp = n_tok // TBK, (B * T) // TBK

    def front(x, batch, seq, s0, attend):
        xf = x.reshape(batch * seq, D_MODEL)
        za, zb, zg = _inproj(xf, row(g_mix[l]), w_in_b)
        att = attend(za)
        hg, s_fin = _hgrn(zb, s0, lower, row(g_out_norm[l]), batch, seq)
        za3 = za.reshape(batch, seq, 3 * ATT_WIDTH)
        heads = lambda a: a.reshape(1, batch, a.shape[1], ATT_HEADS, ATT_DIM)
        keep = min(WINDOW, seq)
        nk = heads(za3[:, seq - keep:, ATT_WIDTH:2 * ATT_WIDTH])
        nv = heads(za3[:, seq - keep:, 2 * ATT_WIDTH:])
        return dict(mix=(att, hg, zg, xf), nk=nk, nv=nv, s=s_fin[None])

    ck = cache_k[l].reshape(DB, cw, ATT_WIDTH)
    cv = cache_v[l].reshape(DB, cw, ATT_WIDTH)
    fp = front(x_prompt, B, T, jnp.zeros((B, HG_HEADS, HG_DK, HG_DK), F32), lambda za: _attn_prompt(za, base, B, T))
    fs = front(x_sample, DB, S, state_s[l].astype(F32), lambda za: _attn_sample(za, ck, cv, base, DB, S))

    x1, h2, route, cnt = _merge(fp["mix"], fs["mix"], wpa, wpb, wout, row(g_ffn[l]), wr, br)
    cnt = cnt[:, 0, :N_EXPERTS].astype(jnp.int32)
    max_rows = n_tok * TOP_K + nb * N_EXPERTS * (ROW_CH - 1) + N_EXPERTS * (EXP_TM - 1)
    nt_max = -(-max_rows // EXP_TM)
    plan = _route_plan(cnt, nt_max)
    loff_f = plan["loff"].astype(F32)
    xs = _dispatch(plan, h2, route, loff_f[:, :, None], nt_max * EXP_TM)
    ysort = _experts(plan, xs, w_gu[l], b_gu3, w_down[l], b_down3)
    loffrow = jnp.pad(loff_f, ((0, 0), (0, pad_e)))[:, None, :]
    yp = _combine(plan, (0, nbp), route, loffrow, x1, row(g_final), ysort)
    ys = _combine(plan, (nbp, nb), route, loffrow, x1, row(g_final), ysort)
    return (yp.reshape(B, T, D_MODEL), ys.reshape(DB, S, D_MODEL), fp["nk"], fp["nv"], fp["s"],
            fs["nk"], fs["nv"], fs["s"])
```

```python
import functools

import numpy as np
import jax
import jax.numpy as jnp
from jax import lax
from jax.experimental import pallas as pl
from jax.experimental.pallas import tpu as pltpu

F32 = jnp.float32
BF16 = jnp.bfloat16

D_MODEL = 1024
CHUNK = 64
LEFT_CHUNKS = 8
WINDOW = LEFT_CHUNKS * CHUNK
ATT_HEADS = 8
ATT_DIM = 64
ATT_WIDTH = ATT_HEADS * ATT_DIM
MAX_REL = 256
HG_HEADS = 4
HG_DK = 128
HG_WIDTH = HG_HEADS * HG_DK
N_EXPERTS = 32
TOP_K = 4
D_FF = D_MODEL
SWIGLU_LIMIT = 7.0
SWIGLU_ALPHA = 1.702
RMS_EPS = 1e-5

LANES = 128
NEG = -1e30
ATT_QBLK = 4 * CHUNK
ATT_KBLKS = LEFT_CHUNKS * CHUNK // ATT_QBLK + 1
HG_C = 128
VMEM_LIMIT = 56 * 1024 * 1024
BIAS_W = 1024
SUBLANES = 8
TBK = 256
ROW_CH = SUBLANES
RB = TBK * TOP_K + N_EXPERTS * ROW_CH
EXP_TM = 256
CH_PER_TILE = EXP_TM // ROW_CH
BLK_CH = RB // ROW_CH
ROUTE_IDX = 64
ROUTE_P = 72

NT = (((1,), (1,)), ((), ()))
TN = (((0,), (0,)), ((), ()))


def _rms(x, g):
    return x * lax.rsqrt(jnp.mean(x * x, axis=-1, keepdims=True) + RMS_EPS) * g


def _sigmoid(x):
    return 1.0 / (1.0 + jnp.exp(-x))


def _params(*sem):
    return pltpu.CompilerParams(dimension_semantics=sem, vmem_limit_bytes=VMEM_LIMIT)


def _inproj_body(x_ref, g_ref, w_ref, za_ref, zb_ref, zg_ref):
    h = _rms(x_ref[...], g_ref[...]).astype(BF16)
    a, b = 3 * ATT_WIDTH, 3 * ATT_WIDTH + 4 * HG_WIDTH
    za_ref[...] = jnp.dot(h, w_ref[:, :a], preferred_element_type=F32)
    zb_ref[...] = jnp.dot(h, w_ref[:, a:b], preferred_element_type=F32)
    zg_ref[...] = jnp.dot(h, w_ref[:, b:], preferred_element_type=F32)


def _inproj(x, g, w_bf16, tm=256):
    n = x.shape[0]
    cols = w_bf16.shape[1]
    wa, wb, wg = 3 * ATT_WIDTH, 4 * HG_WIDTH, 2 * D_MODEL
    return pl.pallas_call(
        _inproj_body,
        grid=(n // tm,),
        in_specs=[
            pl.BlockSpec((tm, D_MODEL), lambda i: (i, 0)),
            pl.BlockSpec((1, D_MODEL), lambda i: (0, 0)),
            pl.BlockSpec((D_MODEL, cols), lambda i: (0, 0)),
        ],
        out_specs=[
            pl.BlockSpec((tm, wa), lambda i: (i, 0)),
            pl.BlockSpec((tm, wb), lambda i: (i, 0)),
            pl.BlockSpec((tm, wg), lambda i: (i, 0)),
        ],
        out_shape=[
            jax.ShapeDtypeStruct((n, wa), F32),
            jax.ShapeDtypeStruct((n, wb), F32),
            jax.ShapeDtypeStruct((n, wg), F32),
        ],
        compiler_params=_params("arbitrary"),
        name="inproj",
    )(x, g, w_bf16)


def _attn_heads(q_ref, k_refs, v_refs, bias_fn, pens, o_ref):
    lane = lax.broadcasted_iota(jnp.int32, (1, LANES), 1)
    first = lane < ATT_DIM
    for hp in range(ATT_HEADS // 2):
        sl = slice(hp * LANES, (hp + 1) * LANES)
        q2 = q_ref[:, sl] * (ATT_DIM ** -0.5)
        ks = [k[:, sl].astype(BF16) for k in k_refs]
        outs = []
        for half in range(2):
            head = 2 * hp + half
            mine = first if half == 0 else lane >= ATT_DIM
            qm = jnp.where(mine, q2, 0.0).astype(BF16)
            vs = [jnp.where(mine, v[:, sl], 1.0).astype(BF16) for v in v_refs]
            ss = []
            for j, kj in enumerate(ks):
                s = lax.dot_general(qm, kj, NT, preferred_element_type=F32) + bias_fn(head, j)
                if pens[j] is not None:
                    s = s + pens[j]
                ss.append(s)
            if all(s.shape == ss[0].shape for s in ss):
                m = jnp.max(functools.reduce(jnp.maximum, ss), axis=-1, keepdims=True)
            else:
                m = functools.reduce(jnp.maximum, [jnp.max(s, axis=-1, keepdims=True) for s in ss])
            outs.append(functools.reduce(jnp.add, [jnp.dot(jnp.exp(s - m).astype(BF16), vj, preferred_element_type=F32)
                                                   for s, vj in zip(ss, vs)]))
        num = jnp.where(first, outs[0], outs[1])
        den = pltpu.roll(jnp.where(first, outs[1], outs[0]), ATT_DIM, 1)
        o_ref[:, sl] = num * (1.0 / den)


def _fill_bias(base_ref, bias_ref, banded):
    nq, nk = bias_ref.shape[1:]
    if banded:
        r = lax.broadcasted_iota(jnp.int32, (nq, nk), 0)
        s = lax.broadcasted_iota(jnp.int32, (nq, nk), 1)
        qc = (r + WINDOW) // CHUNK
        kc = s // CHUNK
        pen = jnp.where(kc <= qc, jnp.where(kc >= qc - LEFT_CHUNKS, 0.0, NEG), NEG)
    for h in range(ATT_HEADS):
        rows = jnp.broadcast_to(base_ref[h:h + 1, :], (nq, BIAS_W))
        t = pltpu.roll(rows, 0, 1, stride=1, stride_axis=0)[:, :nk]
        bias_ref[h] = t + pen if banded else t


def _attn_prompt_body(q_ref, k0, k1, k2, v0, v1, v2, base_ref, o_ref, bias_ref):
    i = pl.program_id(1)

    @pl.when((pl.program_id(0) == 0) & (i == 0))
    def _():
        _fill_bias(base_ref, bias_ref, True)

    pens = [jnp.where(i - (ATT_KBLKS - 1) + j >= 0, 0.0, NEG) for j in range(ATT_KBLKS - 1)] + [None]
    bias_fn = lambda h, j: bias_ref[h, :, j * ATT_QBLK:(j + 1) * ATT_QBLK]
    _attn_heads(q_ref, [k0, k1, k2], [v0, v1, v2], bias_fn, pens, o_ref)


def _attn_prompt(za, base, batch, seq):
    nq = seq // ATT_QBLK
    back = ATT_KBLKS - 1
    qspec = pl.BlockSpec((ATT_QBLK, ATT_WIDTH), lambda b, i: (b * nq + i, 0))

    def kvspec(j, col):
        return pl.BlockSpec((ATT_QBLK, ATT_WIDTH),
                            lambda b, i: (b * nq + jnp.maximum(i - back + j, 0), col))

    return pl.pallas_call(
        _attn_prompt_body,
        grid=(batch, nq),
        in_specs=[qspec] + [kvspec(j, 1) for j in range(ATT_KBLKS)] + [kvspec(j, 2) for j in range(ATT_KBLKS)]
        + [pl.BlockSpec(base.shape, lambda b, i: (0, 0))],
        out_specs=pl.BlockSpec((ATT_QBLK, ATT_WIDTH), lambda b, i: (b * nq + i, 0)),
        out_shape=jax.ShapeDtypeStruct((batch * seq, ATT_WIDTH), F32),
        scratch_shapes=[pltpu.VMEM((ATT_HEADS, ATT_QBLK, ATT_KBLKS * ATT_QBLK), F32)],
        compiler_params=_params("arbitrary", "arbitrary"),
        name="attn_prompt",
    )(za, za, za, za, za, za, za, base)


def _attn_sample_body(q_ref, kn_ref, vn_ref, ck_ref, cv_ref, base_ref, o_ref, bias_ref):
    @pl.when(pl.program_id(0) == 0)
    def _():
        _fill_bias(base_ref, bias_ref, False)

    cw = ck_ref.shape[0]
    bias_fn = lambda h, j: bias_ref[h, :, :cw] if j == 0 else bias_ref[h, :, cw:]
    _attn_heads(q_ref, [ck_ref, kn_ref], [cv_ref, vn_ref], bias_fn, [None, None], o_ref)


def _attn_sample(za, ck, cv, base, batch, seq):
    cw = ck.shape[1]
    return pl.pallas_call(
        _attn_sample_body,
        grid=(batch,),
        in_specs=[
            pl.BlockSpec((seq, ATT_WIDTH), lambda b: (b, 0)),
            pl.BlockSpec((seq, ATT_WIDTH), lambda b: (b, 1)),
            pl.BlockSpec((seq, ATT_WIDTH), lambda b: (b, 2)),
            pl.BlockSpec((None, cw, ATT_WIDTH), lambda b: (b, 0, 0)),
            pl.BlockSpec((None, cw, ATT_WIDTH), lambda b: (b, 0, 0)),
            pl.BlockSpec(base.shape, lambda b: (0, 0)),
        ],
        out_specs=pl.BlockSpec((seq, ATT_WIDTH), lambda b: (b, 0)),
        out_shape=jax.ShapeDtypeStruct((batch * seq, ATT_WIDTH), F32),
        scratch_shapes=[pltpu.VMEM((ATT_HEADS, seq, cw + seq), F32)],
        compiler_params=_params("arbitrary"),
        name="attn_sample",
    )(za, za, za, ck, cv, base)


def _rel_bias_base(table):
    top = table[:, 2 * MAX_REL:].astype(F32)
    rev = table[:, ::-1][:, :2 * MAX_REL].astype(F32)
    left = WINDOW - MAX_REL
    return jnp.concatenate([jnp.broadcast_to(top, (ATT_HEADS, left)), rev,
                            jnp.broadcast_to(top, (ATT_HEADS, BIAS_W - left - 2 * MAX_REL))], axis=1)


def _hgrn_consts(c):
    t = np.arange(c)[:, None]
    j = np.arange(c)[None, :]
    mats = [j <= t, j > t]
    masks = []
    m = c // 2
    while m >= 1:
        ref = (t // (2 * m)) * (2 * m) + m - 1
        second = (t % (2 * m)) >= m
        mats.append((second & (j > ref) & (j <= t)) | (~second & (j > t) & (j <= ref)))
        masks.append((t // (2 * m)) == (j // (2 * m)))
        m //= 2
    return (jnp.asarray(np.concatenate(mats, 0).astype(np.float32), BF16),
            jnp.asarray(np.stack(masks).astype(np.float32)))


def _hgrn_body(zb_ref, s0_ref, lower_ref, gon_ref, p_ref, mask_ref, o_ref, sfin_ref, st_ref, *, single_step):
    c = zb_ref.shape[0]
    step = pl.program_id(1)

    def load_state():
        for h in range(HG_HEADS):
            st_ref[h] = s0_ref[0, h].T

    if single_step:
        load_state()
    else:
        pl.when(step == 0)(load_state)

    pmat = p_ref[...]
    n_levels = mask_ref.shape[0]
    part = lambda i: zb_ref[:, i * HG_WIDTH:(i + 1) * HG_WIDTH]
    head = lambda a, h: a[:, h * HG_DK:(h + 1) * HG_DK]
    q = part(0)
    low = lower_ref[...]
    f = low + (1.0 - low) * _sigmoid(part(1))
    lf = jnp.log(f)
    k = 1.0 - f
    ib = part(2)
    v = ib * _sigmoid(ib)
    og = part(3)

    hi = lf.astype(BF16)
    r1 = lf - hi.astype(F32)
    mid = r1.astype(BF16)
    lo = (r1 - mid.astype(F32)).astype(BF16)
    e = (jnp.dot(pmat, hi, preferred_element_type=F32) + jnp.dot(pmat, mid, preferred_element_type=F32)
         + jnp.dot(pmat, lo, preferred_element_type=F32))
    b = e[0:c]
    decay = jnp.exp(e[c - 1:c])
    qe = (q * jnp.exp(b)).astype(BF16)
    kt = (k * jnp.exp(e[c:2 * c])).astype(BF16)
    vb = v.astype(BF16)
    qk = q * k
    gate = og * _sigmoid(og)

    row = lax.broadcasted_iota(jnp.int32, (c, HG_WIDTH), 0)
    att = [None] * HG_HEADS
    for lvl in range(n_levels):
        m = c >> (lvl + 1)
        x = jnp.exp(e[(2 + lvl) * c:(3 + lvl) * c])
        second = (row & m) != 0
        qm = jnp.where(second, q * x, 0.0).astype(BF16)
        km = jnp.where(second, 0.0, k * x).astype(BF16)
        for h in range(HG_HEADS):
            a = lax.dot_general(head(qm, h), head(km, h), NT, preferred_element_type=F32)
            if lvl > 0:
                a = a * mask_ref[lvl]
            att[h] = a if att[h] is None else att[h] + a

    for h in range(HG_HEADS):
        st = st_ref[h]
        inter = lax.dot_general(head(qe, h), st.astype(BF16), NT, preferred_element_type=F32)
        intra = jnp.dot(att[h].astype(BF16), head(vb, h), preferred_element_type=F32)
        intra = intra + jnp.sum(head(qk, h), axis=-1, keepdims=True) * head(v, h)
        st_ref[h] = st * head(decay, h) + lax.dot_general(head(vb, h), head(kt, h), TN, preferred_element_type=F32)
        o_ref[:, h * HG_DK:(h + 1) * HG_DK] = _rms(inter + intra, gon_ref[...]) * head(gate, h)

    def write_state():
        for h in range(HG_HEADS):
            sfin_ref[0, h] = st_ref[h].T

    if single_step:
        write_state()
    else:
        pl.when(step == pl.num_programs(1) - 1)(write_state)


def _hgrn(zb, s0, lower, g_on, batch, seq):
    c = min(HG_C, seq)
    assert seq % c == 0
    pmat, masks = _hgrn_consts(c)
    nc = seq // c
    return pl.pallas_call(
        functools.partial(_hgrn_body, single_step=nc == 1),
        grid=(batch, nc),
        in_specs=[
            pl.BlockSpec((c, 4 * HG_WIDTH), lambda b, i: (b * nc + i, 0)),
            pl.BlockSpec((1, HG_HEADS, HG_DK, HG_DK), lambda b, i: (b, 0, 0, 0)),
            pl.BlockSpec((1, HG_WIDTH), lambda b, i: (0, 0)),
            pl.BlockSpec((1, HG_DK), lambda b, i: (0, 0)),
            pl.BlockSpec(pmat.shape, lambda b, i: (0, 0)),
            pl.BlockSpec(masks.shape, lambda b, i: (0, 0, 0)),
        ],
        out_specs=[
            pl.BlockSpec((c, HG_WIDTH), lambda b, i: (b * nc + i, 0)),
            pl.BlockSpec((1, HG_HEADS, HG_DK, HG_DK), lambda b, i: (b, 0, 0, 0)),
        ],
        out_shape=[
            jax.ShapeDtypeStruct((batch * seq, HG_WIDTH), F32),
            jax.ShapeDtypeStruct((batch, HG_HEADS, HG_DK, HG_DK), F32),
        ],
        scratch_shapes=[pltpu.VMEM((HG_HEADS, HG_DK, HG_DK), F32)],
        compiler_params=_params("arbitrary", "arbitrary"),
        name="hgrn2",
    )(zb, s0, lower, g_on, pmat, masks)


def _split_bf16(x):
    hi = x.astype(BF16)
    return hi, (x - hi.astype(F32)).astype(BF16)


def _merge_body(att_p, hg_p, zg_p, x_p, att_s, hg_s, zg_s, x_s, wpa_ref, wpb_ref, wout_ref, gffn_ref, wr_ref, br_ref,
                x1_ref, h2_ref, route_ref, cnt_ref, *, n_first):
    weights = (wpa_ref, wpb_ref, wout_ref, gffn_ref, wr_ref, br_ref)
    outs = (x1_ref, h2_ref, route_ref, cnt_ref)
    i = pl.program_id(0)
    pl.when(i < n_first)(functools.partial(_merge_block, att_p, hg_p, zg_p, x_p, *weights, *outs))
    pl.when(i >= n_first)(functools.partial(_merge_block, att_s, hg_s, zg_s, x_s, *weights, *outs))


def _merge_block(att_ref, hg_ref, zg_ref, x_ref, wpa_ref, wpb_ref, wout_ref, gffn_ref, wr_ref, br_ref,
                 x1_ref, h2_ref, route_ref, cnt_ref):
    pa = jnp.dot(att_ref[...].astype(BF16), wpa_ref[...], preferred_element_type=F32)
    pb = jnp.dot(hg_ref[...].astype(BF16), wpb_ref[...], preferred_element_type=F32)
    y = _sigmoid(zg_ref[:, :D_MODEL]) * pa + _sigmoid(zg_ref[:, D_MODEL:]) * pb
    x1 = x_ref[...] + jnp.dot(y.astype(BF16), wout_ref[...], preferred_element_type=F32)
    x1_ref[...] = x1
    h2 = _rms(x1, gffn_ref[...])
    h2_ref[...] = h2.astype(BF16)

    h_hi, h_lo = _split_bf16(h2)
    w_hi, w_lo = _split_bf16(wr_ref[...])
    logits = (jnp.dot(h_hi, w_hi, preferred_element_type=F32) + jnp.dot(h_lo, w_hi, preferred_element_type=F32)
              + jnp.dot(h_hi, w_lo, preferred_element_type=F32)) + br_ref[...]
    lane = lax.broadcasted_iota(jnp.int32, logits.shape, 1).astype(F32)
    cur = logits
    vals, idxs = [], []
    for _ in range(TOP_K):
        m = jnp.max(cur, axis=-1, keepdims=True)
        idx = jnp.min(jnp.where(cur == m, lane, float(LANES)), axis=-1, keepdims=True)
        vals.append(m)
        idxs.append(idx)
        cur = jnp.where(lane == idx, -jnp.inf, cur)
    es = [jnp.exp(v - vals[0]) for v in vals]
    inv = 1.0 / functools.reduce(jnp.add, es)
    route = jnp.zeros_like(logits)
    for k, (ex, idx) in enumerate(zip(es, idxs)):
        route = (route + jnp.where(lane == idx, 1.0, 0.0) + jnp.where(lane == float(ROUTE_IDX + k), idx, 0.0)
                 + jnp.where(lane == float(ROUTE_P + k), ex * inv, 0.0))
    route_ref[...] = route
    cnt_ref[...] = jnp.sum(jnp.where(lane < float(N_EXPERTS), route, 0.0), axis=0, keepdims=True)


def _merge(first, second, wpa, wpb, wout, g_ffn, w_router, b_router):
    tm = TBK
    n1, n2 = first[3].shape[0] // tm, second[3].shape[0] // tm
    widths = (ATT_WIDTH, HG_WIDTH, 2 * D_MODEL, D_MODEL)
    spec1 = [pl.BlockSpec((tm, w), lambda i: (jnp.minimum(i, n1 - 1), 0)) for w in widths]
    spec2 = [pl.BlockSpec((tm, w), lambda i: (jnp.maximum(i - n1, 0), 0)) for w in widths]
    row = lambda w: pl.BlockSpec((tm, w), lambda i: (i, 0))
    full = lambda a: pl.BlockSpec(a.shape, lambda i: (0,) * a.ndim)
    n = (n1 + n2) * tm
    return pl.pallas_call(
        functools.partial(_merge_body, n_first=n1),
        grid=(n1 + n2,),
        in_specs=spec1 + spec2 + [full(wpa), full(wpb), full(wout), full(g_ffn), full(w_router), full(b_router)],
        out_specs=[row(D_MODEL), row(D_MODEL), row(LANES), pl.BlockSpec((None, 1, LANES), lambda i: (i, 0, 0))],
        out_shape=[
            jax.ShapeDtypeStruct((n, D_MODEL), F32),
            jax.ShapeDtypeStruct((n, D_MODEL), BF16),
            jax.ShapeDtypeStruct((n, LANES), F32),
            jax.ShapeDtypeStruct((n1 + n2, 1, LANES), F32),
        ],
        compiler_params=_params("arbitrary"),
        name="merge_router",
    )(*first, *second, wpa, wpb, wout, g_ffn, w_router, b_router)


def _route_plan(cnt):
    pc = (cnt + ROW_CH - 1) // ROW_CH
    loff = jnp.cumsum(pc, axis=1) - pc
    tot = jnp.sum(pc, axis=0)
    reg = (tot + CH_PER_TILE - 1) // CH_PER_TILE * CH_PER_TILE
    gstart = jnp.cumsum(reg) - reg
    goff = gstart[None, :] + jnp.cumsum(pc, axis=0) - pc
    ntiles = jnp.sum(reg) // CH_PER_TILE
    present = reg > 0
    ids = jnp.arange(N_EXPERTS, dtype=jnp.int32)
    later = (ids[None, :] > ids[:, None]) & present[None, :]
    nxt = jnp.min(jnp.where(later, ids[None, :], N_EXPERTS), axis=1)
    nxt = jnp.where(nxt == N_EXPERTS, -1, nxt)
    slot = (jnp.cumsum(present.astype(jnp.int32)) - 1) % 2
    first = jnp.min(jnp.where(present, ids, N_EXPERTS)).reshape(1)
    j = jnp.arange(BLK_CH, dtype=jnp.int32)
    run = jnp.sum(((loff + pc)[:, None, :] <= j[None, :, None]).astype(jnp.int32), axis=2)
    shift = jnp.sum(jnp.where(run[:, :, None] == ids[None, None, :], (goff - loff)[:, None, :], 0), axis=2)
    gmap = shift + j[None, :]
    i32 = lambda a: a.astype(jnp.int32)
    return dict(loff=i32(loff), gmap=i32(gmap), totc=i32(jnp.sum(pc, axis=1)),
                pad_start=i32(gstart + tot), pad_cnt=i32(reg - tot), ntiles=i32(ntiles).reshape(1),
                tile_start=i32(gstart // CH_PER_TILE), tile_count=i32(reg // CH_PER_TILE),
                next_expert=i32(nxt), weight_slot=i32(slot), first_expert=i32(first))


def _chunk_rows(ref, chunk):
    return ref.at[pl.ds(pl.multiple_of(chunk * ROW_CH, ROW_CH), ROW_CH)]


def _for_chunks(n, do):
    log_unroll = 2
    groups = lax.shift_right_logical(n, log_unroll)

    def group(i, carry):
        for u in range(1 << log_unroll):
            do(lax.shift_left(i, log_unroll) + u)
        return carry
    lax.fori_loop(0, groups, group, 0)

    def single(c, carry):
        do(c)
        return carry
    lax.fori_loop(lax.shift_left(groups, log_unroll), n, single, 0)


def _wait_chunks(n, copy_of_rows):
    for bit in range((RB // ROW_CH).bit_length()):
        @pl.when(((n >> bit) & 1) == 1)
        def _(bit=bit):
            copy_of_rows((1 << bit) * ROW_CH).wait()


def _dispatch_body(gmap_s, totc_s, pads_s, padn_s, nt_s, h2_ref, route_ref, tri_ref, loffv_ref,
                   xs_hbm, buf, zbuf, sem):
    b = pl.program_id(0)
    nb = pl.num_programs(0)
    slot = lax.rem(b, 2)
    fill_sem, tile_sem = 2, 3
    nt_max = xs_hbm.shape[0] // EXP_TM

    def copy_out(src, gchunk, sem_i):
        return pltpu.make_async_copy(src, _chunk_rows(xs_hbm, gchunk), sem.at[sem_i])

    def zero_tile(t):
        return pltpu.make_async_copy(zbuf, xs_hbm.at[pl.ds(pl.multiple_of(t * EXP_TM, EXP_TM), EXP_TM)],
                                     sem.at[tile_sem])

    def wait_block(bb, slot_):
        _wait_chunks(totc_s[bb], lambda rows: pltpu.make_async_copy(
            buf.at[slot_, pl.ds(0, rows)], xs_hbm.at[pl.ds(0, rows)], sem.at[slot_]))

    @pl.when(b == 0)
    def _():
        zbuf[...] = jnp.zeros_like(zbuf)

    @pl.when(b >= 2)
    def _():
        wait_block(b - 2, slot)

    rt = route_ref[...].T
    rank_t = jnp.dot(rt[0:N_EXPERTS].astype(BF16), tri_ref[...], preferred_element_type=F32)
    lpos_t = loffv_ref[...] * float(ROW_CH) + rank_t
    erow = lax.broadcasted_iota(jnp.int32, (N_EXPERTS, TBK), 0).astype(F32)
    lposk = [jnp.sum(jnp.where(rt[ROUTE_IDX + k:ROUTE_IDX + k + 1] == erow, lpos_t, 0.0), axis=0, keepdims=True)
             for k in range(TOP_K)]
    h2 = h2_ref[...]
    for r0 in range(0, RB, EXP_TM):
        rrow = lax.broadcasted_iota(jnp.int32, (EXP_TM, TBK), 0).astype(F32) + float(r0)
        perm = functools.reduce(jnp.add, [jnp.where(lposk[k] == rrow, 1.0, 0.0) for k in range(TOP_K)])
        buf[slot, r0:r0 + EXP_TM, :] = jnp.dot(perm.astype(BF16), h2, preferred_element_type=F32)

    _for_chunks(totc_s[b], lambda c: copy_out(_chunk_rows(buf.at[slot], c), gmap_s[b * BLK_CH + c], slot).start())

    @pl.when(b == nb - 1)
    def _():
        zchunk = _chunk_rows(zbuf, 0)

        def fill_expert(e, carry):
            def one(c, carry2):
                copy_out(zchunk, pads_s[e] + c, fill_sem).start()
                return carry2
            lax.fori_loop(0, padn_s[e], one, 0)
            return carry
        lax.fori_loop(0, N_EXPERTS, fill_expert, 0)

        def fill_tile(t, carry):
            zero_tile(t).start()
            return carry
        lax.fori_loop(nt_s[0], nt_max, fill_tile, 0)

        @pl.when(b >= 1)
        def _():
            wait_block(b - 1, 1 - slot)
        wait_block(b, slot)

        def drain_expert(e, carry):
            def one(c, carry2):
                copy_out(zchunk, 0, fill_sem).wait()
                return carry2
            lax.fori_loop(0, padn_s[e], one, 0)
            return carry
        lax.fori_loop(0, N_EXPERTS, drain_expert, 0)

        def drain_tile(t, carry):
            zero_tile(0).wait()
            return carry
        lax.fori_loop(nt_s[0], nt_max, drain_tile, 0)


def _dispatch(plan, h2, route, loffv, n_rows):
    nb = h2.shape[0] // TBK
    tri = jnp.asarray(np.triu(np.ones((TBK, TBK), np.float32), 1), BF16)
    grid_spec = pltpu.PrefetchScalarGridSpec(
        num_scalar_prefetch=5,
        grid=(nb,),
        in_specs=[
            pl.BlockSpec((TBK, D_MODEL), lambda b, *_: (b, 0)),
            pl.BlockSpec((TBK, LANES), lambda b, *_: (b, 0)),
            pl.BlockSpec((TBK, TBK), lambda b, *_: (0, 0)),
            pl.BlockSpec((None, N_EXPERTS, 1), lambda b, *_: (b, 0, 0)),
        ],
        out_specs=pl.BlockSpec(memory_space=pl.ANY),
        scratch_shapes=[pltpu.VMEM((2, RB, D_MODEL), F32), pltpu.VMEM((EXP_TM, D_MODEL), F32),
                        pltpu.SemaphoreType.DMA((4,))],
    )
    return pl.pallas_call(
        _dispatch_body,
        grid_spec=grid_spec,
        out_shape=jax.ShapeDtypeStruct((n_rows, D_MODEL), F32),
        compiler_params=_params("arbitrary"),
        name="moe_dispatch",
    )(plan["gmap"].reshape(-1), plan["totc"], plan["pad_start"], plan["pad_cnt"], plan["ntiles"],
      h2, route, tri, loffv)


def _experts_body(t0_s, n_s, nxt_s, par_s, first_s, nt_s, xs_hbm, wgu_hbm, bgu_ref, wd_hbm, bd_ref, ys_hbm,
                  wg_f, wd_f, wg_b, wd_b, xbuf, ybuf, wsem, xsem, ysem):
    e = pl.program_id(0)
    nt = nt_s[0]
    nt_max = xs_hbm.shape[0] // EXP_TM
    tile_rows = lambda t: pl.ds(pl.multiple_of(t * EXP_TM, EXP_TM), EXP_TM)

    def weight_copies(ex, s):
        return (pltpu.make_async_copy(wgu_hbm.at[ex], wg_f.at[s], wsem.at[s, 0]),
                pltpu.make_async_copy(wd_hbm.at[ex], wd_f.at[s], wsem.at[s, 1]))

    def x_copy(t, s):
        return pltpu.make_async_copy(xs_hbm.at[tile_rows(t)], xbuf.at[s], xsem.at[s])

    def y_copy(t, s):
        return pltpu.make_async_copy(ybuf.at[s], ys_hbm.at[tile_rows(t)], ysem.at[s])

    @pl.when(e == 0)
    def _():
        for cp in weight_copies(first_s[0], 0):
            cp.start()
        x_copy(0, 0).start()

    @pl.when(n_s[e] > 0)
    def _():
        s_w = par_s[e]
        for cp in weight_copies(e, s_w):
            cp.wait()
        wg_b[...] = wg_f[s_w].astype(BF16)
        wd_b[...] = wd_f[s_w].astype(BF16)

        @pl.when(nxt_s[e] >= 0)
        def _():
            for cp in weight_copies(nxt_s[e], 1 - s_w):
                cp.start()

        def tile(i, carry):
            t = t0_s[e] + i
            s = t & 1
            x_copy(t, s).wait()

            @pl.when(t + 1 < nt)
            def _():
                x_copy(t + 1, 1 - s).start()

            @pl.when(t >= 2)
            def _():
                y_copy(t - 2, s).wait()

            gu = jnp.dot(xbuf[s].astype(BF16), wg_b[...], preferred_element_type=F32) + bgu_ref[...]
            gate = jnp.minimum(gu[:, :D_FF], SWIGLU_LIMIT)
            up = jnp.clip(gu[:, D_FF:], -SWIGLU_LIMIT, SWIGLU_LIMIT)
            act = (up + 1.0) * gate * _sigmoid(SWIGLU_ALPHA * gate)
            ybuf[s] = jnp.dot(act.astype(BF16), wd_b[...], preferred_element_type=F32) + bd_ref[...]
            y_copy(t, s).start()
            return carry
        lax.fori_loop(0, n_s[e], tile, 0)

    @pl.when(e == pl.num_programs(0) - 1)
    def _():
        @pl.when(nt >= 2)
        def _():
            y_copy(nt - 2, nt & 1).wait()
        y_copy(nt - 1, (nt - 1) & 1).wait()
        ybuf[0] = jnp.zeros((EXP_TM, D_MODEL), F32)

        def fill(t, carry):
            y_copy(t, 0).start()
            return carry
        lax.fori_loop(nt, nt_max, fill, 0)

        def drain(t, carry):
            y_copy(0, 0).wait()
            return carry
        lax.fori_loop(nt, nt_max, drain, 0)


def _experts(plan, xs, w_gu, b_gu, w_down, b_down):
    n_rows = xs.shape[0]
    of_expert = lambda e, *_: (e, 0, 0)
    grid_spec = pltpu.PrefetchScalarGridSpec(
        num_scalar_prefetch=6,
        grid=(N_EXPERTS,),
        in_specs=[
            pl.BlockSpec(memory_space=pl.ANY),
            pl.BlockSpec(memory_space=pl.ANY),
            pl.BlockSpec((None, 1, 2 * D_FF), of_expert),
            pl.BlockSpec(memory_space=pl.ANY),
            pl.BlockSpec((None, 1, D_MODEL), of_expert),
        ],
        out_specs=pl.BlockSpec(memory_space=pl.ANY),
        scratch_shapes=[pltpu.VMEM((2, D_MODEL, 2 * D_FF), F32), pltpu.VMEM((2, D_FF, D_MODEL), F32),
                        pltpu.VMEM((D_MODEL, 2 * D_FF), BF16), pltpu.VMEM((D_FF, D_MODEL), BF16),
                        pltpu.VMEM((2, EXP_TM, D_MODEL), F32), pltpu.VMEM((2, EXP_TM, D_MODEL), F32),
                        pltpu.SemaphoreType.DMA((2, 2)), pltpu.SemaphoreType.DMA((2,)),
                        pltpu.SemaphoreType.DMA((2,))],
    )
    return pl.pallas_call(
        _experts_body,
        grid_spec=grid_spec,
        out_shape=jax.ShapeDtypeStruct((n_rows, D_MODEL), F32),
        compiler_params=_params("arbitrary"),
        name="moe_experts",
    )(plan["tile_start"], plan["tile_count"], plan["next_expert"], plan["weight_slot"], plan["first_expert"],
      plan["ntiles"], xs, w_gu, b_gu, w_down, b_down)


def _combine_body(gmap_s, totc_s, route_ref, tril_ref, loffrow_ref, x1_ref, gfin_ref, ys_hbm, y_ref, buf, sem):
    b = pl.program_id(0)
    nb = pl.num_programs(0)
    slot = lax.rem(b, 2)

    def copy_in(slot_, lchunk, gchunk):
        return pltpu.make_async_copy(_chunk_rows(ys_hbm, gchunk), _chunk_rows(buf.at[slot_], lchunk), sem.at[slot_])

    def fetch_block(bb, slot_):
        _for_chunks(totc_s[bb], lambda c: copy_in(slot_, c, gmap_s[bb * BLK_CH + c]).start())

    def wait_block(bb, slot_):
        _wait_chunks(totc_s[bb], lambda rows: pltpu.make_async_copy(
            ys_hbm.at[pl.ds(0, rows)], buf.at[slot_, pl.ds(0, rows)], sem.at[slot_]))

    @pl.when(b == 0)
    def _():
        buf[...] = jnp.zeros_like(buf)
        fetch_block(0, 0)

    @pl.when(b + 1 < nb)
    def _():
        fetch_block(b + 1, 1 - slot)

    wait_block(b, slot)

    route = route_ref[...]
    lane = lax.broadcasted_iota(jnp.int32, (1, LANES), 1).astype(F32)
    sel = jnp.where(lane < float(N_EXPERTS), route, 0.0).astype(BF16)
    rank = jnp.dot(tril_ref[...], sel, preferred_element_type=F32)
    lpos = loffrow_ref[...] * float(ROW_CH) + rank
    lposk, pk = [], []
    for k in range(TOP_K):
        idx = route[:, ROUTE_IDX + k:ROUTE_IDX + k + 1]
        lposk.append(jnp.sum(jnp.where(lane == idx, lpos, 0.0), axis=-1, keepdims=True))
        pk.append(route[:, ROUTE_P + k:ROUTE_P + k + 1])
    acc = x1_ref[...]
    for r0 in range(0, RB, EXP_TM):
        col = lax.broadcasted_iota(jnp.int32, (TBK, EXP_TM), 1).astype(F32) + float(r0)
        w = functools.reduce(jnp.add, [jnp.where(lposk[k] == col, pk[k], 0.0) for k in range(TOP_K)])
        acc = acc + jnp.dot(w.astype(BF16), buf[slot, r0:r0 + EXP_TM, :].astype(BF16), preferred_element_type=F32)
    y_ref[...] = _rms(acc, gfin_ref[...])


def _combine(plan, blocks, route, loffrow, x1, g_final, ys):
    b0, b1 = blocks
    nb = b1 - b0
    tril = jnp.asarray(np.tril(np.ones((TBK, TBK), np.float32), -1), BF16)
    grid_spec = pltpu.PrefetchScalarGridSpec(
        num_scalar_prefetch=2,
        grid=(nb,),
        in_specs=[
            pl.BlockSpec((TBK, LANES), lambda b, *_: (b + b0, 0)),
            pl.BlockSpec((TBK, TBK), lambda b, *_: (0, 0)),
            pl.BlockSpec((None, 1, LANES), lambda b, *_: (b + b0, 0, 0)),
            pl.BlockSpec((TBK, D_MODEL), lambda b, *_: (b + b0, 0)),
            pl.BlockSpec((1, D_MODEL), lambda b, *_: (0, 0)),
            pl.BlockSpec(memory_space=pl.ANY),
        ],
        out_specs=pl.BlockSpec((TBK, D_MODEL), lambda b, *_: (b, 0)),
        scratch_shapes=[pltpu.VMEM((2, RB, D_MODEL), F32), pltpu.SemaphoreType.DMA((2,))],
    )
    return pl.pallas_call(
        _combine_body,
        grid_spec=grid_spec,
        out_shape=jax.ShapeDtypeStruct((nb * TBK, D_MODEL), F32),
        compiler_params=_params("arbitrary"),
        name="moe_combine",
    )(plan["gmap"][b0:b1].reshape(-1), plan["totc"][b0:b1], route, tril, loffrow, x1, g_final, ys)


def kernel(x_prompt, x_sample, cache_k, cache_v, state_s, g_mix, w_in, rel_bias, lb_logits, g_out_norm,
           w_pa, w_pb, w_out, g_ffn, w_router, b_router, w_gu, b_gu, w_down, b_down, g_final):
    B, T = x_prompt.shape[:2]
    DB, S = x_sample.shape[:2]
    depth = w_in.shape[0]
    assert depth == 1 and T % ATT_QBLK == 0 and S == CHUNK
    cw = cache_k.shape[2]
    assert cw == WINDOW
    l = 0

    lower = jnp.cumsum(jax.nn.softmax(lb_logits.astype(F32), axis=0), axis=0)[l].reshape(1, HG_WIDTH)
    w_in_b = w_in[l].astype(BF16)
    wpa, wpb, wout = w_pa[l].astype(BF16), w_pb[l].astype(BF16), w_out[l].astype(BF16)
    row = lambda a: a.reshape(1, -1).astype(F32)
    base = _rel_bias_base(rel_bias[l])
    b_gu3 = b_gu[l].reshape(N_EXPERTS, 1, 2 * D_FF)
    b_down3 = b_down[l].reshape(N_EXPERTS, 1, D_MODEL)
    pad_e = LANES - N_EXPERTS
    wr = jnp.pad(w_router[l].astype(F32), ((0, 0), (0, pad_e)))
    br = jnp.concatenate([b_router[l].astype(F32), jnp.full((pad_e,), NEG, F32)]).reshape(1, LANES)

    n_tok = B * T + DB * S
    nb, nbp = n_tok // TBK, (B * T) // TBK

    def front(x, batch, seq, s0, attend):
        xf = x.reshape(batch * seq, D_MODEL)
        za, zb, zg = _inproj(xf, row(g_mix[l]), w_in_b)
        att = attend(za)
        hg, s_fin = _hgrn(zb, s0, lower, row(g_out_norm[l]), batch, seq)
        za3 = za.reshape(batch, seq, 3 * ATT_WIDTH)
        heads = lambda a: a.reshape(1, batch, a.shape[1], ATT_HEADS, ATT_DIM)
        keep = min(WINDOW, seq)
        nk = heads(za3[:, seq - keep:, ATT_WIDTH:2 * ATT_WIDTH])
        nv = heads(za3[:, seq - keep:, 2 * ATT_WIDTH:])
        return dict(mix=(att, hg, zg, xf), nk=nk, nv=nv, s=s_fin[None])

    ck = cache_k[l].reshape(DB, cw, ATT_WIDTH)
    cv = cache_v[l].reshape(DB, cw, ATT_WIDTH)
    fp = front(x_prompt, B, T, jnp.zeros((B, HG_HEADS, HG_DK, HG_DK), F32), lambda za: _attn_prompt(za, base, B, T))
    fs = front(x_sample, DB, S, state_s[l].astype(F32), lambda za: _attn_sample(za, ck, cv, base, DB, S))

    x1, h2, route, cnt = _merge(fp["mix"], fs["mix"], wpa, wpb, wout, row(g_ffn[l]), wr, br)
    cnt = cnt[:, 0, :N_EXPERTS].astype(jnp.int32)
    max_rows = n_tok * TOP_K + nb * N_EXPERTS * (ROW_CH - 1) + N_EXPERTS * (EXP_TM - 1)
    nt_max = -(-max_rows // EXP_TM)
    plan = _route_plan(cnt)
    loff_f = plan["loff"].astype(F32)
    xs = _dispatch(plan, h2, route, loff_f[:, :, None], nt_max * EXP_TM)
    ysort = _experts(plan, xs, w_gu[l], b_gu3, w_down[l], b_down3)
    loffrow = jnp.pad(loff_f, ((0, 0), (0, pad_e)))[:, None, :]
    yp = _combine(plan, (0, nbp), route, loffrow, x1, row(g_final), ysort)
    ys = _combine(plan, (nbp, nb), route, loffrow, x1, row(g_final), ysort)
    return (yp.reshape(B, T, D_MODEL), ys.reshape(DB, S, D_MODEL), fp["nk"], fp["nv"], fp["s"],
            fs["nk"], fs["nv"], fs["s"])
```

```python
import functools

import numpy as np
import jax
import jax.numpy as jnp
from jax import lax
from jax.experimental import pallas as pl
from jax.experimental.pallas import tpu as pltpu

F32 = jnp.float32
BF16 = jnp.bfloat16

D_MODEL = 1024
CHUNK = 64
LEFT_CHUNKS = 8
WINDOW = LEFT_CHUNKS * CHUNK
ATT_HEADS = 8
ATT_DIM = 64
ATT_WIDTH = ATT_HEADS * ATT_DIM
MAX_REL = 256
HG_HEADS = 4
HG_DK = 128
HG_WIDTH = HG_HEADS * HG_DK
N_EXPERTS = 32
TOP_K = 4
D_FF = D_MODEL
SWIGLU_LIMIT = 7.0
SWIGLU_ALPHA = 1.702
RMS_EPS = 1e-5

LANES = 128
NEG = -1e30
ATT_QBLK = 4 * CHUNK
ATT_KBLKS = LEFT_CHUNKS * CHUNK // ATT_QBLK + 1
HG_C = 128
VMEM_LIMIT = 56 * 1024 * 1024
BIAS_W = 1024
SUBLANES = 8
TBK = 256
ROW_CH = SUBLANES
RB = TBK * TOP_K + N_EXPERTS * ROW_CH
PERM_TM = 256
EXP_TM = 512
CH_PER_TILE = EXP_TM // ROW_CH
BLK_CH = RB // ROW_CH
ROUTE_IDX = 64
ROUTE_P = 72

NT = (((1,), (1,)), ((), ()))
TN = (((0,), (0,)), ((), ()))


def _rms(x, g):
    return x * lax.rsqrt(jnp.mean(x * x, axis=-1, keepdims=True) + RMS_EPS) * g


def _sigmoid(x):
    return 1.0 / (1.0 + jnp.exp(-x))


def _params(*sem):
    return pltpu.CompilerParams(dimension_semantics=sem, vmem_limit_bytes=VMEM_LIMIT)


def _inproj_body(x_ref, g_ref, w_ref, za_ref, zb_ref, zg_ref):
    h = _rms(x_ref[...], g_ref[...]).astype(BF16)
    a, b = 3 * ATT_WIDTH, 3 * ATT_WIDTH + 4 * HG_WIDTH
    za_ref[...] = jnp.dot(h, w_ref[:, :a], preferred_element_type=F32)
    zb_ref[...] = jnp.dot(h, w_ref[:, a:b], preferred_element_type=F32)
    zg_ref[...] = jnp.dot(h, w_ref[:, b:], preferred_element_type=F32)


def _inproj(x, g, w_bf16, tm=256):
    n = x.shape[0]
    cols = w_bf16.shape[1]
    wa, wb, wg = 3 * ATT_WIDTH, 4 * HG_WIDTH, 2 * D_MODEL
    return pl.pallas_call(
        _inproj_body,
        grid=(n // tm,),
        in_specs=[
            pl.BlockSpec((tm, D_MODEL), lambda i: (i, 0)),
            pl.BlockSpec((1, D_MODEL), lambda i: (0, 0)),
            pl.BlockSpec((D_MODEL, cols), lambda i: (0, 0)),
        ],
        out_specs=[
            pl.BlockSpec((tm, wa), lambda i: (i, 0)),
            pl.BlockSpec((tm, wb), lambda i: (i, 0)),
            pl.BlockSpec((tm, wg), lambda i: (i, 0)),
        ],
        out_shape=[
            jax.ShapeDtypeStruct((n, wa), F32),
            jax.ShapeDtypeStruct((n, wb), F32),
            jax.ShapeDtypeStruct((n, wg), F32),
        ],
        compiler_params=_params("arbitrary"),
        name="inproj",
    )(x, g, w_bf16)


def _attn_heads(q_ref, k_refs, v_refs, bias_fn, pens, o_ref):
    lane = lax.broadcasted_iota(jnp.int32, (1, LANES), 1)
    first = lane < ATT_DIM
    for hp in range(ATT_HEADS // 2):
        sl = slice(hp * LANES, (hp + 1) * LANES)
        q2 = q_ref[:, sl] * (ATT_DIM ** -0.5)
        ks = [k[:, sl].astype(BF16) for k in k_refs]
        outs = []
        for half in range(2):
            head = 2 * hp + half
            mine = first if half == 0 else lane >= ATT_DIM
            qm = jnp.where(mine, q2, 0.0).astype(BF16)
            vs = [jnp.where(mine, v[:, sl], 1.0).astype(BF16) for v in v_refs]
            ss = []
            for j, kj in enumerate(ks):
                s = lax.dot_general(qm, kj, NT, preferred_element_type=F32) + bias_fn(head, j)
                if pens[j] is not None:
                    s = s + pens[j]
                ss.append(s)
            if all(s.shape == ss[0].shape for s in ss):
                m = jnp.max(functools.reduce(jnp.maximum, ss), axis=-1, keepdims=True)
            else:
                m = functools.reduce(jnp.maximum, [jnp.max(s, axis=-1, keepdims=True) for s in ss])
            outs.append(functools.reduce(jnp.add, [jnp.dot(jnp.exp(s - m).astype(BF16), vj, preferred_element_type=F32)
                                                   for s, vj in zip(ss, vs)]))
        num = jnp.where(first, outs[0], outs[1])
        den = pltpu.roll(jnp.where(first, outs[1], outs[0]), ATT_DIM, 1)
        o_ref[:, sl] = num * (1.0 / den)


def _fill_bias(base_ref, bias_ref, banded):
    nq, nk = bias_ref.shape[1:]
    if banded:
        r = lax.broadcasted_iota(jnp.int32, (nq, nk), 0)
        s = lax.broadcasted_iota(jnp.int32, (nq, nk), 1)
        qc = (r + WINDOW) // CHUNK
        kc = s // CHUNK
        pen = jnp.where(kc <= qc, jnp.where(kc >= qc - LEFT_CHUNKS, 0.0, NEG), NEG)
    for h in range(ATT_HEADS):
        rows = jnp.broadcast_to(base_ref[h:h + 1, :], (nq, BIAS_W))
        t = pltpu.roll(rows, 0, 1, stride=1, stride_axis=0)[:, :nk]
        bias_ref[h] = t + pen if banded else t


def _attn_prompt_body(q_ref, k0, k1, k2, v0, v1, v2, base_ref, o_ref, bias_ref):
    i = pl.program_id(1)

    @pl.when((pl.program_id(0) == 0) & (i == 0))
    def _():
        _fill_bias(base_ref, bias_ref, True)

    pens = [jnp.where(i - (ATT_KBLKS - 1) + j >= 0, 0.0, NEG) for j in range(ATT_KBLKS - 1)] + [None]
    bias_fn = lambda h, j: bias_ref[h, :, j * ATT_QBLK:(j + 1) * ATT_QBLK]
    _attn_heads(q_ref, [k0, k1, k2], [v0, v1, v2], bias_fn, pens, o_ref)


def _attn_prompt(za, base, batch, seq):
    nq = seq // ATT_QBLK
    back = ATT_KBLKS - 1
    qspec = pl.BlockSpec((ATT_QBLK, ATT_WIDTH), lambda b, i: (b * nq + i, 0))

    def kvspec(j, col):
        return pl.BlockSpec((ATT_QBLK, ATT_WIDTH),
                            lambda b, i: (b * nq + jnp.maximum(i - back + j, 0), col))

    return pl.pallas_call(
        _attn_prompt_body,
        grid=(batch, nq),
        in_specs=[qspec] + [kvspec(j, 1) for j in range(ATT_KBLKS)] + [kvspec(j, 2) for j in range(ATT_KBLKS)]
        + [pl.BlockSpec(base.shape, lambda b, i: (0, 0))],
        out_specs=pl.BlockSpec((ATT_QBLK, ATT_WIDTH), lambda b, i: (b * nq + i, 0)),
        out_shape=jax.ShapeDtypeStruct((batch * seq, ATT_WIDTH), F32),
        scratch_shapes=[pltpu.VMEM((ATT_HEADS, ATT_QBLK, ATT_KBLKS * ATT_QBLK), F32)],
        compiler_params=_params("arbitrary", "arbitrary"),
        name="attn_prompt",
    )(za, za, za, za, za, za, za, base)


def _attn_sample_body(q_ref, kn_ref, vn_ref, ck_ref, cv_ref, base_ref, o_ref, bias_ref):
    @pl.when(pl.program_id(0) == 0)
    def _():
        _fill_bias(base_ref, bias_ref, False)

    cw = ck_ref.shape[0]
    bias_fn = lambda h, j: bias_ref[h, :, :cw] if j == 0 else bias_ref[h, :, cw:]
    _attn_heads(q_ref, [ck_ref, kn_ref], [cv_ref, vn_ref], bias_fn, [None, None], o_ref)


def _attn_sample(za, ck, cv, base, batch, seq):
    cw = ck.shape[1]
    return pl.pallas_call(
        _attn_sample_body,
        grid=(batch,),
        in_specs=[
            pl.BlockSpec((seq, ATT_WIDTH), lambda b: (b, 0)),
            pl.BlockSpec((seq, ATT_WIDTH), lambda b: (b, 1)),
            pl.BlockSpec((seq, ATT_WIDTH), lambda b: (b, 2)),
            pl.BlockSpec((None, cw, ATT_WIDTH), lambda b: (b, 0, 0)),
            pl.BlockSpec((None, cw, ATT_WIDTH), lambda b: (b, 0, 0)),
            pl.BlockSpec(base.shape, lambda b: (0, 0)),
        ],
        out_specs=pl.BlockSpec((seq, ATT_WIDTH), lambda b: (b, 0)),
        out_shape=jax.ShapeDtypeStruct((batch * seq, ATT_WIDTH), F32),
        scratch_shapes=[pltpu.VMEM((ATT_HEADS, seq, cw + seq), F32)],
        compiler_params=_params("arbitrary"),
        name="attn_sample",
    )(za, za, za, ck, cv, base)


def _rel_bias_base(table):
    top = table[:, 2 * MAX_REL:].astype(F32)
    rev = table[:, ::-1][:, :2 * MAX_REL].astype(F32)
    left = WINDOW - MAX_REL
    return jnp.concatenate([jnp.broadcast_to(top, (ATT_HEADS, left)), rev,
                            jnp.broadcast_to(top, (ATT_HEADS, BIAS_W - left - 2 * MAX_REL))], axis=1)


def _hgrn_consts(c):
    t = np.arange(c)[:, None]
    j = np.arange(c)[None, :]
    mats = [j <= t, j > t]
    masks = []
    m = c // 2
    while m >= 1:
        ref = (t // (2 * m)) * (2 * m) + m - 1
        second = (t % (2 * m)) >= m
        mats.append((second & (j > ref) & (j <= t)) | (~second & (j > t) & (j <= ref)))
        masks.append((t // (2 * m)) == (j // (2 * m)))
        m //= 2
    return (jnp.asarray(np.concatenate(mats, 0).astype(np.float32), BF16),
            jnp.asarray(np.stack(masks).astype(np.float32)))


def _hgrn_body(zb_ref, s0_ref, lower_ref, gon_ref, p_ref, mask_ref, o_ref, sfin_ref, st_ref, *, single_step):
    c = zb_ref.shape[0]
    step = pl.program_id(1)

    def load_state():
        for h in range(HG_HEADS):
            st_ref[h] = s0_ref[0, h].T

    if single_step:
        load_state()
    else:
        pl.when(step == 0)(load_state)

    pmat = p_ref[...]
    n_levels = mask_ref.shape[0]
    part = lambda i: zb_ref[:, i * HG_WIDTH:(i + 1) * HG_WIDTH]
    head = lambda a, h: a[:, h * HG_DK:(h + 1) * HG_DK]
    q = part(0)
    low = lower_ref[...]
    f = low + (1.0 - low) * _sigmoid(part(1))
    lf = jnp.log(f)
    k = 1.0 - f
    ib = part(2)
    v = ib * _sigmoid(ib)
    og = part(3)

    hi = lf.astype(BF16)
    r1 = lf - hi.astype(F32)
    mid = r1.astype(BF16)
    lo = (r1 - mid.astype(F32)).astype(BF16)
    e = (jnp.dot(pmat, hi, preferred_element_type=F32) + jnp.dot(pmat, mid, preferred_element_type=F32)
         + jnp.dot(pmat, lo, preferred_element_type=F32))
    b = e[0:c]
    decay = jnp.exp(e[c - 1:c])
    qe = (q * jnp.exp(b)).astype(BF16)
    kt = (k * jnp.exp(e[c:2 * c])).astype(BF16)
    vb = v.astype(BF16)
    qk = q * k
    gate = og * _sigmoid(og)

    row = lax.broadcasted_iota(jnp.int32, (c, HG_WIDTH), 0)
    att = [None] * HG_HEADS
    for lvl in range(n_levels):
        m = c >> (lvl + 1)
        x = jnp.exp(e[(2 + lvl) * c:(3 + lvl) * c])
        second = (row & m) != 0
        qm = jnp.where(second, q * x, 0.0).astype(BF16)
        km = jnp.where(second, 0.0, k * x).astype(BF16)
        for h in range(HG_HEADS):
            a = lax.dot_general(head(qm, h), head(km, h), NT, preferred_element_type=F32)
            if lvl > 0:
                a = a * mask_ref[lvl]
            att[h] = a if att[h] is None else att[h] + a

    for h in range(HG_HEADS):
        st = st_ref[h]
        inter = lax.dot_general(head(qe, h), st.astype(BF16), NT, preferred_element_type=F32)
        intra = jnp.dot(att[h].astype(BF16), head(vb, h), preferred_element_type=F32)
        intra = intra + jnp.sum(head(qk, h), axis=-1, keepdims=True) * head(v, h)
        st_ref[h] = st * head(decay, h) + lax.dot_general(head(vb, h), head(kt, h), TN, preferred_element_type=F32)
        o_ref[:, h * HG_DK:(h + 1) * HG_DK] = _rms(inter + intra, gon_ref[...]) * head(gate, h)

    def write_state():
        for h in range(HG_HEADS):
            sfin_ref[0, h] = st_ref[h].T

    if single_step:
        write_state()
    else:
        pl.when(step == pl.num_programs(1) - 1)(write_state)


def _hgrn(zb, s0, lower, g_on, batch, seq):
    c = min(HG_C, seq)
    assert seq % c == 0
    pmat, masks = _hgrn_consts(c)
    nc = seq // c
    return pl.pallas_call(
        functools.partial(_hgrn_body, single_step=nc == 1),
        grid=(batch, nc),
        in_specs=[
            pl.BlockSpec((c, 4 * HG_WIDTH), lambda b, i: (b * nc + i, 0)),
            pl.BlockSpec((1, HG_HEADS, HG_DK, HG_DK), lambda b, i: (b, 0, 0, 0)),
            pl.BlockSpec((1, HG_WIDTH), lambda b, i: (0, 0)),
            pl.BlockSpec((1, HG_DK), lambda b, i: (0, 0)),
            pl.BlockSpec(pmat.shape, lambda b, i: (0, 0)),
            pl.BlockSpec(masks.shape, lambda b, i: (0, 0, 0)),
        ],
        out_specs=[
            pl.BlockSpec((c, HG_WIDTH), lambda b, i: (b * nc + i, 0)),
            pl.BlockSpec((1, HG_HEADS, HG_DK, HG_DK), lambda b, i: (b, 0, 0, 0)),
        ],
        out_shape=[
            jax.ShapeDtypeStruct((batch * seq, HG_WIDTH), F32),
            jax.ShapeDtypeStruct((batch, HG_HEADS, HG_DK, HG_DK), F32),
        ],
        scratch_shapes=[pltpu.VMEM((HG_HEADS, HG_DK, HG_DK), F32)],
        compiler_params=_params("arbitrary", "arbitrary"),
        name="hgrn2",
    )(zb, s0, lower, g_on, pmat, masks)


def _split_bf16(x):
    hi = x.astype(BF16)
    return hi, (x - hi.astype(F32)).astype(BF16)


def _merge_body(att_p, hg_p, zg_p, x_p, att_s, hg_s, zg_s, x_s, wpa_ref, wpb_ref, wout_ref, gffn_ref, wr_ref, br_ref,
                x1_ref, h2_ref, route_ref, cnt_ref, *, n_first):
    weights = (wpa_ref, wpb_ref, wout_ref, gffn_ref, wr_ref, br_ref)
    outs = (x1_ref, h2_ref, route_ref, cnt_ref)
    i = pl.program_id(0)
    pl.when(i < n_first)(functools.partial(_merge_block, att_p, hg_p, zg_p, x_p, *weights, *outs))
    pl.when(i >= n_first)(functools.partial(_merge_block, att_s, hg_s, zg_s, x_s, *weights, *outs))


def _merge_block(att_ref, hg_ref, zg_ref, x_ref, wpa_ref, wpb_ref, wout_ref, gffn_ref, wr_ref, br_ref,
                 x1_ref, h2_ref, route_ref, cnt_ref):
    pa = jnp.dot(att_ref[...].astype(BF16), wpa_ref[...], preferred_element_type=F32)
    pb = jnp.dot(hg_ref[...].astype(BF16), wpb_ref[...], preferred_element_type=F32)
    y = _sigmoid(zg_ref[:, :D_MODEL]) * pa + _sigmoid(zg_ref[:, D_MODEL:]) * pb
    x1 = x_ref[...] + jnp.dot(y.astype(BF16), wout_ref[...], preferred_element_type=F32)
    x1_ref[...] = x1
    h2 = _rms(x1, gffn_ref[...])
    h2_ref[...] = h2.astype(BF16)

    h_hi, h_lo = _split_bf16(h2)
    w_hi, w_lo = _split_bf16(wr_ref[...])
    logits = (jnp.dot(h_hi, w_hi, preferred_element_type=F32) + jnp.dot(h_lo, w_hi, preferred_element_type=F32)
              + jnp.dot(h_hi, w_lo, preferred_element_type=F32)) + br_ref[...]
    lane = lax.broadcasted_iota(jnp.int32, logits.shape, 1).astype(F32)
    cur = logits
    vals, idxs = [], []
    for _ in range(TOP_K):
        m = jnp.max(cur, axis=-1, keepdims=True)
        idx = jnp.min(jnp.where(cur == m, lane, float(LANES)), axis=-1, keepdims=True)
        vals.append(m)
        idxs.append(idx)
        cur = jnp.where(lane == idx, -jnp.inf, cur)
    es = [jnp.exp(v - vals[0]) for v in vals]
    inv = 1.0 / functools.reduce(jnp.add, es)
    route = jnp.zeros_like(logits)
    for k, (ex, idx) in enumerate(zip(es, idxs)):
        route = (route + jnp.where(lane == idx, 1.0, 0.0) + jnp.where(lane == float(ROUTE_IDX + k), idx, 0.0)
                 + jnp.where(lane == float(ROUTE_P + k), ex * inv, 0.0))
    route_ref[...] = route
    cnt_ref[...] = jnp.sum(jnp.where(lane < float(N_EXPERTS), route, 0.0), axis=0, keepdims=True)


def _merge(first, second, wpa, wpb, wout, g_ffn, w_router, b_router):
    tm = TBK
    n1, n2 = first[3].shape[0] // tm, second[3].shape[0] // tm
    widths = (ATT_WIDTH, HG_WIDTH, 2 * D_MODEL, D_MODEL)
    spec1 = [pl.BlockSpec((tm, w), lambda i: (jnp.minimum(i, n1 - 1), 0)) for w in widths]
    spec2 = [pl.BlockSpec((tm, w), lambda i: (jnp.maximum(i - n1, 0), 0)) for w in widths]
    row = lambda w: pl.BlockSpec((tm, w), lambda i: (i, 0))
    full = lambda a: pl.BlockSpec(a.shape, lambda i: (0,) * a.ndim)
    n = (n1 + n2) * tm
    return pl.pallas_call(
        functools.partial(_merge_body, n_first=n1),
        grid=(n1 + n2,),
        in_specs=spec1 + spec2 + [full(wpa), full(wpb), full(wout), full(g_ffn), full(w_router), full(b_router)],
        out_specs=[row(D_MODEL), row(D_MODEL), row(LANES), pl.BlockSpec((None, 1, LANES), lambda i: (i, 0, 0))],
        out_shape=[
            jax.ShapeDtypeStruct((n, D_MODEL), F32),
            jax.ShapeDtypeStruct((n, D_MODEL), BF16),
            jax.ShapeDtypeStruct((n, LANES), F32),
            jax.ShapeDtypeStruct((n1 + n2, 1, LANES), F32),
        ],
        compiler_params=_params("arbitrary"),
        name="merge_router",
    )(*first, *second, wpa, wpb, wout, g_ffn, w_router, b_router)


def _route_plan(cnt):
    pc = (cnt + ROW_CH - 1) // ROW_CH
    loff = jnp.cumsum(pc, axis=1) - pc
    tot = jnp.sum(pc, axis=0)
    reg = (tot + CH_PER_TILE - 1) // CH_PER_TILE * CH_PER_TILE
    gstart = jnp.cumsum(reg) - reg
    goff = gstart[None, :] + jnp.cumsum(pc, axis=0) - pc
    ntiles = jnp.sum(reg) // CH_PER_TILE
    present = reg > 0
    ids = jnp.arange(N_EXPERTS, dtype=jnp.int32)
    later = (ids[None, :] > ids[:, None]) & present[None, :]
    nxt = jnp.min(jnp.where(later, ids[None, :], N_EXPERTS), axis=1)
    nxt = jnp.where(nxt == N_EXPERTS, -1, nxt)
    slot = (jnp.cumsum(present.astype(jnp.int32)) - 1) % 2
    first = jnp.min(jnp.where(present, ids, N_EXPERTS)).reshape(1)
    j = jnp.arange(BLK_CH, dtype=jnp.int32)
    run = jnp.sum(((loff + pc)[:, None, :] <= j[None, :, None]).astype(jnp.int32), axis=2)
    shift = jnp.sum(jnp.where(run[:, :, None] == ids[None, None, :], (goff - loff)[:, None, :], 0), axis=2)
    gmap = shift + j[None, :]
    i32 = lambda a: a.astype(jnp.int32)
    return dict(loff=i32(loff), gmap=i32(gmap), totc=i32(jnp.sum(pc, axis=1)),
                pad_start=i32(gstart + tot), pad_cnt=i32(reg - tot), ntiles=i32(ntiles).reshape(1),
                tile_start=i32(gstart // CH_PER_TILE), tile_count=i32(reg // CH_PER_TILE),
                next_expert=i32(nxt), weight_slot=i32(slot), first_expert=i32(first))


def _chunk_rows(ref, chunk):
    return ref.at[pl.ds(pl.multiple_of(chunk * ROW_CH, ROW_CH), ROW_CH)]


def _for_chunks(n, do):
    log_unroll = 2
    groups = lax.shift_right_logical(n, log_unroll)

    def group(i, carry):
        for u in range(1 << log_unroll):
            do(lax.shift_left(i, log_unroll) + u)
        return carry
    lax.fori_loop(0, groups, group, 0)

    def single(c, carry):
        do(c)
        return carry
    lax.fori_loop(lax.shift_left(groups, log_unroll), n, single, 0)


def _wait_chunks(n, copy_of_rows):
    for bit in range((RB // ROW_CH).bit_length()):
        @pl.when(((n >> bit) & 1) == 1)
        def _(bit=bit):
            copy_of_rows((1 << bit) * ROW_CH).wait()


def _dispatch_body(gmap_s, totc_s, pads_s, padn_s, nt_s, h2_ref, route_ref, tri_ref, loffv_ref,
                   xs_hbm, buf, zbuf, sem):
    b = pl.program_id(0)
    nb = pl.num_programs(0)
    slot = lax.rem(b, 2)
    fill_sem, tile_sem = 2, 3
    nt_max = xs_hbm.shape[0] // EXP_TM

    def copy_out(src, gchunk, sem_i):
        return pltpu.make_async_copy(src, _chunk_rows(xs_hbm, gchunk), sem.at[sem_i])

    def zero_tile(t):
        return pltpu.make_async_copy(zbuf, xs_hbm.at[pl.ds(pl.multiple_of(t * EXP_TM, EXP_TM), EXP_TM)],
                                     sem.at[tile_sem])

    def wait_block(bb, slot_):
        _wait_chunks(totc_s[bb], lambda rows: pltpu.make_async_copy(
            buf.at[slot_, pl.ds(0, rows)], xs_hbm.at[pl.ds(0, rows)], sem.at[slot_]))

    @pl.when(b == 0)
    def _():
        zbuf[...] = jnp.zeros_like(zbuf)

    @pl.when(b >= 2)
    def _():
        wait_block(b - 2, slot)

    rt = route_ref[...].T
    rank_t = jnp.dot(rt[0:N_EXPERTS].astype(BF16), tri_ref[...], preferred_element_type=F32)
    lpos_t = loffv_ref[...] * float(ROW_CH) + rank_t
    erow = lax.broadcasted_iota(jnp.int32, (N_EXPERTS, TBK), 0).astype(F32)
    lposk = [jnp.sum(jnp.where(rt[ROUTE_IDX + k:ROUTE_IDX + k + 1] == erow, lpos_t, 0.0), axis=0, keepdims=True)
             for k in range(TOP_K)]
    h2 = h2_ref[...]
    for r0 in range(0, RB, PERM_TM):
        rrow = lax.broadcasted_iota(jnp.int32, (PERM_TM, TBK), 0).astype(F32) + float(r0)
        perm = functools.reduce(jnp.add, [jnp.where(lposk[k] == rrow, 1.0, 0.0) for k in range(TOP_K)])
        buf[slot, r0:r0 + PERM_TM, :] = jnp.dot(perm.astype(BF16), h2, preferred_element_type=F32)

    _for_chunks(totc_s[b], lambda c: copy_out(_chunk_rows(buf.at[slot], c), gmap_s[b * BLK_CH + c], slot).start())

    @pl.when(b == nb - 1)
    def _():
        zchunk = _chunk_rows(zbuf, 0)

        def fill_expert(e, carry):
            def one(c, carry2):
                copy_out(zchunk, pads_s[e] + c, fill_sem).start()
                return carry2
            lax.fori_loop(0, padn_s[e], one, 0)
            return carry
        lax.fori_loop(0, N_EXPERTS, fill_expert, 0)

        def fill_tile(t, carry):
            zero_tile(t).start()
            return carry
        lax.fori_loop(nt_s[0], nt_max, fill_tile, 0)

        @pl.when(b >= 1)
        def _():
            wait_block(b - 1, 1 - slot)
        wait_block(b, slot)

        def drain_expert(e, carry):
            def one(c, carry2):
                copy_out(zchunk, 0, fill_sem).wait()
                return carry2
            lax.fori_loop(0, padn_s[e], one, 0)
            return carry
        lax.fori_loop(0, N_EXPERTS, drain_expert, 0)

        def drain_tile(t, carry):
            zero_tile(0).wait()
            return carry
        lax.fori_loop(nt_s[0], nt_max, drain_tile, 0)


def _dispatch(plan, h2, route, loffv, n_rows):
    nb = h2.shape[0] // TBK
    tri = jnp.asarray(np.triu(np.ones((TBK, TBK), np.float32), 1), BF16)
    grid_spec = pltpu.PrefetchScalarGridSpec(
        num_scalar_prefetch=5,
        grid=(nb,),
        in_specs=[
            pl.BlockSpec((TBK, D_MODEL), lambda b, *_: (b, 0)),
            pl.BlockSpec((TBK, LANES), lambda b, *_: (b, 0)),
            pl.BlockSpec((TBK, TBK), lambda b, *_: (0, 0)),
            pl.BlockSpec((None, N_EXPERTS, 1), lambda b, *_: (b, 0, 0)),
        ],
        out_specs=pl.BlockSpec(memory_space=pl.ANY),
        scratch_shapes=[pltpu.VMEM((2, RB, D_MODEL), F32), pltpu.VMEM((EXP_TM, D_MODEL), F32),
                        pltpu.SemaphoreType.DMA((4,))],
    )
    return pl.pallas_call(
        _dispatch_body,
        grid_spec=grid_spec,
        out_shape=jax.ShapeDtypeStruct((n_rows, D_MODEL), F32),
        compiler_params=_params("arbitrary"),
        name="moe_dispatch",
    )(plan["gmap"].reshape(-1), plan["totc"], plan["pad_start"], plan["pad_cnt"], plan["ntiles"],
      h2, route, tri, loffv)


def _experts_body(t0_s, n_s, nxt_s, par_s, first_s, nt_s, xs_hbm, wgu_hbm, bgu_ref, wd_hbm, bd_ref, ys_hbm,
                  wg_f, wd_f, wg_b, wd_b, xbuf, ybuf, wsem, xsem, ysem):
    e = pl.program_id(0)
    nt = nt_s[0]
    nt_max = xs_hbm.shape[0] // EXP_TM
    tile_rows = lambda t: pl.ds(pl.multiple_of(t * EXP_TM, EXP_TM), EXP_TM)

    def weight_copies(ex, s):
        return (pltpu.make_async_copy(wgu_hbm.at[ex], wg_f.at[s], wsem.at[s, 0]),
                pltpu.make_async_copy(wd_hbm.at[ex], wd_f.at[s], wsem.at[s, 1]))

    def x_copy(t, s):
        return pltpu.make_async_copy(xs_hbm.at[tile_rows(t)], xbuf.at[s], xsem.at[s])

    def y_copy(t, s):
        return pltpu.make_async_copy(ybuf.at[s], ys_hbm.at[tile_rows(t)], ysem.at[s])

    @pl.when(e == 0)
    def _():
        for cp in weight_copies(first_s[0], 0):
            cp.start()
        x_copy(0, 0).start()

    @pl.when(n_s[e] > 0)
    def _():
        s_w = par_s[e]
        for cp in weight_copies(e, s_w):
            cp.wait()
        wg_b[...] = wg_f[s_w].astype(BF16)
        wd_b[...] = wd_f[s_w].astype(BF16)

        @pl.when(nxt_s[e] >= 0)
        def _():
            for cp in weight_copies(nxt_s[e], 1 - s_w):
                cp.start()

        def tile(i, carry):
            t = t0_s[e] + i
            s = t & 1
            x_copy(t, s).wait()

            @pl.when(t + 1 < nt)
            def _():
                x_copy(t + 1, 1 - s).start()

            @pl.when(t >= 2)
            def _():
                y_copy(t - 2, s).wait()

            gu = jnp.dot(xbuf[s].astype(BF16), wg_b[...], preferred_element_type=F32) + bgu_ref[...]
            gate = jnp.minimum(gu[:, :D_FF], SWIGLU_LIMIT)
            up = jnp.clip(gu[:, D_FF:], -SWIGLU_LIMIT, SWIGLU_LIMIT)
            act = (up + 1.0) * gate * _sigmoid(SWIGLU_ALPHA * gate)
            ybuf[s] = jnp.dot(act.astype(BF16), wd_b[...], preferred_element_type=F32) + bd_ref[...]
            y_copy(t, s).start()
            return carry
        lax.fori_loop(0, n_s[e], tile, 0)

    @pl.when(e == pl.num_programs(0) - 1)
    def _():
        @pl.when(nt >= 2)
        def _():
            y_copy(nt - 2, nt & 1).wait()
        y_copy(nt - 1, (nt - 1) & 1).wait()
        ybuf[0] = jnp.zeros((EXP_TM, D_MODEL), F32)

        def fill(t, carry):
            y_copy(t, 0).start()
            return carry
        lax.fori_loop(nt, nt_max, fill, 0)

        def drain(t, carry):
            y_copy(0, 0).wait()
            return carry
        lax.fori_loop(nt, nt_max, drain, 0)


def _experts(plan, xs, w_gu, b_gu, w_down, b_down):
    n_rows = xs.shape[0]
    of_expert = lambda e, *_: (e, 0, 0)
    grid_spec = pltpu.PrefetchScalarGridSpec(
        num_scalar_prefetch=6,
        grid=(N_EXPERTS,),
        in_specs=[
            pl.BlockSpec(memory_space=pl.ANY),
            pl.BlockSpec(memory_space=pl.ANY),
            pl.BlockSpec((None, 1, 2 * D_FF), of_expert),
            pl.BlockSpec(memory_space=pl.ANY),
            pl.BlockSpec((None, 1, D_MODEL), of_expert),
        ],
        out_specs=pl.BlockSpec(memory_space=pl.ANY),
        scratch_shapes=[pltpu.VMEM((2, D_MODEL, 2 * D_FF), F32), pltpu.VMEM((2, D_FF, D_MODEL), F32),
                        pltpu.VMEM((D_MODEL, 2 * D_FF), BF16), pltpu.VMEM((D_FF, D_MODEL), BF16),
                        pltpu.VMEM((2, EXP_TM, D_MODEL), F32), pltpu.VMEM((2, EXP_TM, D_MODEL), F32),
                        pltpu.SemaphoreType.DMA((2, 2)), pltpu.SemaphoreType.DMA((2,)),
                        pltpu.SemaphoreType.DMA((2,))],
    )
    return pl.pallas_call(
        _experts_body,
        grid_spec=grid_spec,
        out_shape=jax.ShapeDtypeStruct((n_rows, D_MODEL), F32),
        compiler_params=_params("arbitrary"),
        name="moe_experts",
    )(plan["tile_start"], plan["tile_count"], plan["next_expert"], plan["weight_slot"], plan["first_expert"],
      plan["ntiles"], xs, w_gu, b_gu, w_down, b_down)


def _combine_body(gmap_s, totc_s, route_ref, tril_ref, loffrow_ref, x1_ref, gfin_ref, ys_hbm, y_ref, buf, sem):
    b = pl.program_id(0)
    nb = pl.num_programs(0)
    slot = lax.rem(b, 2)

    def copy_in(slot_, lchunk, gchunk):
        return pltpu.make_async_copy(_chunk_rows(ys_hbm, gchunk), _chunk_rows(buf.at[slot_], lchunk), sem.at[slot_])

    def fetch_block(bb, slot_):
        _for_chunks(totc_s[bb], lambda c: copy_in(slot_, c, gmap_s[bb * BLK_CH + c]).start())

    def wait_block(bb, slot_):
        _wait_chunks(totc_s[bb], lambda rows: pltpu.make_async_copy(
            ys_hbm.at[pl.ds(0, rows)], buf.at[slot_, pl.ds(0, rows)], sem.at[slot_]))

    @pl.when(b == 0)
    def _():
        buf[...] = jnp.zeros_like(buf)
        fetch_block(0, 0)

    @pl.when(b + 1 < nb)
    def _():
        fetch_block(b + 1, 1 - slot)

    wait_block(b, slot)

    route = route_ref[...]
    lane = lax.broadcasted_iota(jnp.int32, (1, LANES), 1).astype(F32)
    sel = jnp.where(lane < float(N_EXPERTS), route, 0.0).astype(BF16)
    rank = jnp.dot(tril_ref[...], sel, preferred_element_type=F32)
    lpos = loffrow_ref[...] * float(ROW_CH) + rank
    lposk, pk = [], []
    for k in range(TOP_K):
        idx = route[:, ROUTE_IDX + k:ROUTE_IDX + k + 1]
        lposk.append(jnp.sum(jnp.where(lane == idx, lpos, 0.0), axis=-1, keepdims=True))
        pk.append(route[:, ROUTE_P + k:ROUTE_P + k + 1])
    acc = x1_ref[...]
    for r0 in range(0, RB, PERM_TM):
        col = lax.broadcasted_iota(jnp.int32, (TBK, PERM_TM), 1).astype(F32) + float(r0)
        w = functools.reduce(jnp.add, [jnp.where(lposk[k] == col, pk[k], 0.0) for k in range(TOP_K)])
        acc = acc + jnp.dot(w.astype(BF16), buf[slot, r0:r0 + PERM_TM, :].astype(BF16), preferred_element_type=F32)
    y_ref[...] = _rms(acc, gfin_ref[...])


def _combine(plan, blocks, route, loffrow, x1, g_final, ys):
    b0, b1 = blocks
    nb = b1 - b0
    tril = jnp.asarray(np.tril(np.ones((TBK, TBK), np.float32), -1), BF16)
    grid_spec = pltpu.PrefetchScalarGridSpec(
        num_scalar_prefetch=2,
        grid=(nb,),
        in_specs=[
            pl.BlockSpec((TBK, LANES), lambda b, *_: (b + b0, 0)),
            pl.BlockSpec((TBK, TBK), lambda b, *_: (0, 0)),
            pl.BlockSpec((None, 1, LANES), lambda b, *_: (b + b0, 0, 0)),
            pl.BlockSpec((TBK, D_MODEL), lambda b, *_: (b + b0, 0)),
            pl.BlockSpec((1, D_MODEL), lambda b, *_: (0, 0)),
            pl.BlockSpec(memory_space=pl.ANY),
        ],
        out_specs=pl.BlockSpec((TBK, D_MODEL), lambda b, *_: (b, 0)),
        scratch_shapes=[pltpu.VMEM((2, RB, D_MODEL), F32), pltpu.SemaphoreType.DMA((2,))],
    )
    return pl.pallas_call(
        _combine_body,
        grid_spec=grid_spec,
        out_shape=jax.ShapeDtypeStruct((nb * TBK, D_MODEL), F32),
        compiler_params=_params("arbitrary"),
        name="moe_combine",
    )(plan["gmap"][b0:b1].reshape(-1), plan["totc"][b0:b1], route, tril, loffrow, x1, g_final, ys)


def kernel(x_prompt, x_sample, cache_k, cache_v, state_s, g_mix, w_in, rel_bias, lb_logits, g_out_norm,
           w_pa, w_pb, w_out, g_ffn, w_router, b_router, w_gu, b_gu, w_down, b_down, g_final):
    B, T = x_prompt.shape[:2]
    DB, S = x_sample.shape[:2]
    depth = w_in.shape[0]
    assert depth == 1 and T % ATT_QBLK == 0 and S == CHUNK
    cw = cache_k.shape[2]
    assert cw == WINDOW
    l = 0

    lower = jnp.cumsum(jax.nn.softmax(lb_logits.astype(F32), axis=0), axis=0)[l].reshape(1, HG_WIDTH)
    w_in_b = w_in[l].astype(BF16)
    wpa, wpb, wout = w_pa[l].astype(BF16), w_pb[l].astype(BF16), w_out[l].astype(BF16)
    row = lambda a: a.reshape(1, -1).astype(F32)
    base = _rel_bias_base(rel_bias[l])
    b_gu3 = b_gu[l].reshape(N_EXPERTS, 1, 2 * D_FF)
    b_down3 = b_down[l].reshape(N_EXPERTS, 1, D_MODEL)
    pad_e = LANES - N_EXPERTS
    wr = jnp.pad(w_router[l].astype(F32), ((0, 0), (0, pad_e)))
    br = jnp.concatenate([b_router[l].astype(F32), jnp.full((pad_e,), NEG, F32)]).reshape(1, LANES)

    n_tok = B * T + DB * S
    nb, nbp = n_tok // TBK, (B * T) // TBK

    def front(x, batch, seq, s0, attend):
        xf = x.reshape(batch * seq, D_MODEL)
        za, zb, zg = _inproj(xf, row(g_mix[l]), w_in_b)
        att = attend(za)
        hg, s_fin = _hgrn(zb, s0, lower, row(g_out_norm[l]), batch, seq)
        za3 = za.reshape(batch, seq, 3 * ATT_WIDTH)
        heads = lambda a: a.reshape(1, batch, a.shape[1], ATT_HEADS, ATT_DIM)
        keep = min(WINDOW, seq)
        nk = heads(za3[:, seq - keep:, ATT_WIDTH:2 * ATT_WIDTH])
        nv = heads(za3[:, seq - keep:, 2 * ATT_WIDTH:])
        return dict(mix=(att, hg, zg, xf), nk=nk, nv=nv, s=s_fin[None])

    ck = cache_k[l].reshape(DB, cw, ATT_WIDTH)
    cv = cache_v[l].reshape(DB, cw, ATT_WIDTH)
    fp = front(x_prompt, B, T, jnp.zeros((B, HG_HEADS, HG_DK, HG_DK), F32), lambda za: _attn_prompt(za, base, B, T))
    fs = front(x_sample, DB, S, state_s[l].astype(F32), lambda za: _attn_sample(za, ck, cv, base, DB, S))

    x1, h2, route, cnt = _merge(fp["mix"], fs["mix"], wpa, wpb, wout, row(g_ffn[l]), wr, br)
    cnt = cnt[:, 0, :N_EXPERTS].astype(jnp.int32)
    max_rows = n_tok * TOP_K + nb * N_EXPERTS * (ROW_CH - 1) + N_EXPERTS * (EXP_TM - 1)
    nt_max = -(-max_rows // EXP_TM)
    plan = _route_plan(cnt)
    loff_f = plan["loff"].astype(F32)
    xs = _dispatch(plan, h2, route, loff_f[:, :, None], nt_max * EXP_TM)
    ysort = _experts(plan, xs, w_gu[l], b_gu3, w_down[l], b_down3)
    loffrow = jnp.pad(loff_f, ((0, 0), (0, pad_e)))[:, None, :]
    yp = _combine(plan, (0, nbp), route, loffrow, x1, row(g_final), ysort)
    ys = _combine(plan, (nbp, nb), route, loffrow, x1, row(g_final), ysort)
    return (yp.reshape(B, T, D_MODEL), ys.reshape(DB, S, D_MODEL), fp["nk"], fp["nv"], fp["s"],
            fs["nk"], fs["nv"], fs["s"])
```

```python
import functools

import numpy as np
import jax
import jax.numpy as jnp
from jax import lax
from jax.experimental import pallas as pl
from jax.experimental.pallas import tpu as pltpu

F32 = jnp.float32
BF16 = jnp.bfloat16

D_MODEL = 1024
CHUNK = 64
LEFT_CHUNKS = 8
WINDOW = LEFT_CHUNKS * CHUNK
ATT_HEADS = 8
ATT_DIM = 64
ATT_WIDTH = ATT_HEADS * ATT_DIM
MAX_REL = 256
HG_HEADS = 4
HG_DK = 128
HG_WIDTH = HG_HEADS * HG_DK
N_EXPERTS = 32
TOP_K = 4
D_FF = D_MODEL
SWIGLU_LIMIT = 7.0
SWIGLU_ALPHA = 1.702
RMS_EPS = 1e-5

LANES = 128
NEG = -1e30
ATT_QBLK = 4 * CHUNK
ATT_KBLKS = LEFT_CHUNKS * CHUNK // ATT_QBLK + 1
HG_C = 128
VMEM_LIMIT = 56 * 1024 * 1024
BIAS_W = 1024
SUBLANES = 8
TBK = 256
ROW_CH = SUBLANES
RB = TBK * TOP_K + N_EXPERTS * ROW_CH
MERGE_TM = 2 * TBK
PERM_TM = 256
EXP_TM = 512
CH_PER_TILE = EXP_TM // ROW_CH
BLK_CH = RB // ROW_CH
ROUTE_IDX = 64
ROUTE_P = 72

NT = (((1,), (1,)), ((), ()))
TN = (((0,), (0,)), ((), ()))


def _rms(x, g):
    return x * lax.rsqrt(jnp.mean(x * x, axis=-1, keepdims=True) + RMS_EPS) * g


def _sigmoid(x):
    return 1.0 / (1.0 + jnp.exp(-x))


def _params(*sem):
    return pltpu.CompilerParams(dimension_semantics=sem, vmem_limit_bytes=VMEM_LIMIT)


def _inproj_body(x_ref, g_ref, w_ref, za_ref, zb_ref, zg_ref):
    h = _rms(x_ref[...], g_ref[...]).astype(BF16)
    a, b = 3 * ATT_WIDTH, 3 * ATT_WIDTH + 4 * HG_WIDTH
    za_ref[...] = jnp.dot(h, w_ref[:, :a], preferred_element_type=F32)
    zb_ref[...] = jnp.dot(h, w_ref[:, a:b], preferred_element_type=F32)
    zg_ref[...] = jnp.dot(h, w_ref[:, b:], preferred_element_type=F32)


def _inproj(x, g, w_bf16, tm=512):
    n = x.shape[0]
    cols = w_bf16.shape[1]
    wa, wb, wg = 3 * ATT_WIDTH, 4 * HG_WIDTH, 2 * D_MODEL
    return pl.pallas_call(
        _inproj_body,
        grid=(n // tm,),
        in_specs=[
            pl.BlockSpec((tm, D_MODEL), lambda i: (i, 0)),
            pl.BlockSpec((1, D_MODEL), lambda i: (0, 0)),
            pl.BlockSpec((D_MODEL, cols), lambda i: (0, 0)),
        ],
        out_specs=[
            pl.BlockSpec((tm, wa), lambda i: (i, 0)),
            pl.BlockSpec((tm, wb), lambda i: (i, 0)),
            pl.BlockSpec((tm, wg), lambda i: (i, 0)),
        ],
        out_shape=[
            jax.ShapeDtypeStruct((n, wa), F32),
            jax.ShapeDtypeStruct((n, wb), F32),
            jax.ShapeDtypeStruct((n, wg), F32),
        ],
        compiler_params=_params("arbitrary"),
        name="inproj",
    )(x, g, w_bf16)


def _attn_heads(q_ref, k_refs, v_refs, bias_fn, pens, o_ref):
    lane = lax.broadcasted_iota(jnp.int32, (1, LANES), 1)
    first = lane < ATT_DIM
    for hp in range(ATT_HEADS // 2):
        sl = slice(hp * LANES, (hp + 1) * LANES)
        q2 = q_ref[:, sl] * (ATT_DIM ** -0.5)
        ks = [k[:, sl].astype(BF16) for k in k_refs]
        outs = []
        for half in range(2):
            head = 2 * hp + half
            mine = first if half == 0 else lane >= ATT_DIM
            qm = jnp.where(mine, q2, 0.0).astype(BF16)
            vs = [jnp.where(mine, v[:, sl], 1.0).astype(BF16) for v in v_refs]
            ss = []
            for j, kj in enumerate(ks):
                s = lax.dot_general(qm, kj, NT, preferred_element_type=F32) + bias_fn(head, j)
                if pens[j] is not None:
                    s = s + pens[j]
                ss.append(s)
            if all(s.shape == ss[0].shape for s in ss):
                m = jnp.max(functools.reduce(jnp.maximum, ss), axis=-1, keepdims=True)
            else:
                m = functools.reduce(jnp.maximum, [jnp.max(s, axis=-1, keepdims=True) for s in ss])
            outs.append(functools.reduce(jnp.add, [jnp.dot(jnp.exp(s - m).astype(BF16), vj, preferred_element_type=F32)
                                                   for s, vj in zip(ss, vs)]))
        num = jnp.where(first, outs[0], outs[1])
        den = pltpu.roll(jnp.where(first, outs[1], outs[0]), ATT_DIM, 1)
        o_ref[:, sl] = num * (1.0 / den)


def _fill_bias(base_ref, bias_ref, banded):
    nq, nk = bias_ref.shape[1:]
    if banded:
        r = lax.broadcasted_iota(jnp.int32, (nq, nk), 0)
        s = lax.broadcasted_iota(jnp.int32, (nq, nk), 1)
        qc = (r + WINDOW) // CHUNK
        kc = s // CHUNK
        pen = jnp.where(kc <= qc, jnp.where(kc >= qc - LEFT_CHUNKS, 0.0, NEG), NEG)
    for h in range(ATT_HEADS):
        rows = jnp.broadcast_to(base_ref[h:h + 1, :], (nq, BIAS_W))
        t = pltpu.roll(rows, 0, 1, stride=1, stride_axis=0)[:, :nk]
        bias_ref[h] = t + pen if banded else t


def _attn_prompt_body(q_ref, k0, k1, k2, v0, v1, v2, base_ref, o_ref, bias_ref):
    i = pl.program_id(1)

    @pl.when((pl.program_id(0) == 0) & (i == 0))
    def _():
        _fill_bias(base_ref, bias_ref, True)

    pens = [jnp.where(i - (ATT_KBLKS - 1) + j >= 0, 0.0, NEG) for j in range(ATT_KBLKS - 1)] + [None]
    bias_fn = lambda h, j: bias_ref[h, :, j * ATT_QBLK:(j + 1) * ATT_QBLK]
    _attn_heads(q_ref, [k0, k1, k2], [v0, v1, v2], bias_fn, pens, o_ref)


def _attn_prompt(za, base, batch, seq):
    nq = seq // ATT_QBLK
    back = ATT_KBLKS - 1
    qspec = pl.BlockSpec((ATT_QBLK, ATT_WIDTH), lambda b, i: (b * nq + i, 0))

    def kvspec(j, col):
        return pl.BlockSpec((ATT_QBLK, ATT_WIDTH),
                            lambda b, i: (b * nq + jnp.maximum(i - back + j, 0), col))

    return pl.pallas_call(
        _attn_prompt_body,
        grid=(batch, nq),
        in_specs=[qspec] + [kvspec(j, 1) for j in range(ATT_KBLKS)] + [kvspec(j, 2) for j in range(ATT_KBLKS)]
        + [pl.BlockSpec(base.shape, lambda b, i: (0, 0))],
        out_specs=pl.BlockSpec((ATT_QBLK, ATT_WIDTH), lambda b, i: (b * nq + i, 0)),
        out_shape=jax.ShapeDtypeStruct((batch * seq, ATT_WIDTH), F32),
        scratch_shapes=[pltpu.VMEM((ATT_HEADS, ATT_QBLK, ATT_KBLKS * ATT_QBLK), F32)],
        compiler_params=_params("arbitrary", "arbitrary"),
        name="attn_prompt",
    )(za, za, za, za, za, za, za, base)


def _attn_sample_body(q_ref, kn_ref, vn_ref, ck_ref, cv_ref, base_ref, o_ref, bias_ref):
    @pl.when(pl.program_id(0) == 0)
    def _():
        _fill_bias(base_ref, bias_ref, False)

    cw = ck_ref.shape[0]
    bias_fn = lambda h, j: bias_ref[h, :, :cw] if j == 0 else bias_ref[h, :, cw:]
    _attn_heads(q_ref, [ck_ref, kn_ref], [cv_ref, vn_ref], bias_fn, [None, None], o_ref)


def _attn_sample(za, ck, cv, base, batch, seq):
    cw = ck.shape[1]
    return pl.pallas_call(
        _attn_sample_body,
        grid=(batch,),
        in_specs=[
            pl.BlockSpec((seq, ATT_WIDTH), lambda b: (b, 0)),
            pl.BlockSpec((seq, ATT_WIDTH), lambda b: (b, 1)),
            pl.BlockSpec((seq, ATT_WIDTH), lambda b: (b, 2)),
            pl.BlockSpec((None, cw, ATT_WIDTH), lambda b: (b, 0, 0)),
            pl.BlockSpec((None, cw, ATT_WIDTH), lambda b: (b, 0, 0)),
            pl.BlockSpec(base.shape, lambda b: (0, 0)),
        ],
        out_specs=pl.BlockSpec((seq, ATT_WIDTH), lambda b: (b, 0)),
        out_shape=jax.ShapeDtypeStruct((batch * seq, ATT_WIDTH), F32),
        scratch_shapes=[pltpu.VMEM((ATT_HEADS, seq, cw + seq), F32)],
        compiler_params=_params("arbitrary"),
        name="attn_sample",
    )(za, za, za, ck, cv, base)


def _rel_bias_base(table):
    top = table[:, 2 * MAX_REL:].astype(F32)
    rev = table[:, ::-1][:, :2 * MAX_REL].astype(F32)
    left = WINDOW - MAX_REL
    return jnp.concatenate([jnp.broadcast_to(top, (ATT_HEADS, left)), rev,
                            jnp.broadcast_to(top, (ATT_HEADS, BIAS_W - left - 2 * MAX_REL))], axis=1)


def _hgrn_consts(c):
    t = np.arange(c)[:, None]
    j = np.arange(c)[None, :]
    mats = [j <= t, j > t]
    masks = []
    m = c // 2
    while m >= 1:
        ref = (t // (2 * m)) * (2 * m) + m - 1
        second = (t % (2 * m)) >= m
        mats.append((second & (j > ref) & (j <= t)) | (~second & (j > t) & (j <= ref)))
        masks.append((t // (2 * m)) == (j // (2 * m)))
        m //= 2
    return (jnp.asarray(np.concatenate(mats, 0).astype(np.float32), BF16),
            jnp.asarray(np.stack(masks).astype(np.float32)))


def _hgrn_body(zb_ref, s0_ref, lower_ref, gon_ref, p_ref, mask_ref, o_ref, sfin_ref, st_ref, *, single_step):
    c = zb_ref.shape[0]
    step = pl.program_id(1)

    def load_state():
        for h in range(HG_HEADS):
            st_ref[h] = s0_ref[0, h].T

    if single_step:
        load_state()
    else:
        pl.when(step == 0)(load_state)

    pmat = p_ref[...]
    n_levels = mask_ref.shape[0]
    part = lambda i: zb_ref[:, i * HG_WIDTH:(i + 1) * HG_WIDTH]
    head = lambda a, h: a[:, h * HG_DK:(h + 1) * HG_DK]
    q = part(0)
    low = lower_ref[...]
    f = low + (1.0 - low) * _sigmoid(part(1))
    lf = jnp.log(f)
    k = 1.0 - f
    ib = part(2)
    v = ib * _sigmoid(ib)
    og = part(3)

    hi = lf.astype(BF16)
    r1 = lf - hi.astype(F32)
    mid = r1.astype(BF16)
    lo = (r1 - mid.astype(F32)).astype(BF16)
    e = (jnp.dot(pmat, hi, preferred_element_type=F32) + jnp.dot(pmat, mid, preferred_element_type=F32)
         + jnp.dot(pmat, lo, preferred_element_type=F32))
    b = e[0:c]
    decay = jnp.exp(e[c - 1:c])
    qe = (q * jnp.exp(b)).astype(BF16)
    kt = (k * jnp.exp(e[c:2 * c])).astype(BF16)
    vb = v.astype(BF16)
    qk = q * k
    gate = og * _sigmoid(og)

    row = lax.broadcasted_iota(jnp.int32, (c, HG_WIDTH), 0)
    att = [None] * HG_HEADS
    for lvl in range(n_levels):
        m = c >> (lvl + 1)
        x = jnp.exp(e[(2 + lvl) * c:(3 + lvl) * c])
        second = (row & m) != 0
        qm = jnp.where(second, q * x, 0.0).astype(BF16)
        km = jnp.where(second, 0.0, k * x).astype(BF16)
        for h in range(HG_HEADS):
            a = lax.dot_general(head(qm, h), head(km, h), NT, preferred_element_type=F32)
            if lvl > 0:
                a = a * mask_ref[lvl]
            att[h] = a if att[h] is None else att[h] + a

    for h in range(HG_HEADS):
        st = st_ref[h]
        inter = lax.dot_general(head(qe, h), st.astype(BF16), NT, preferred_element_type=F32)
        intra = jnp.dot(att[h].astype(BF16), head(vb, h), preferred_element_type=F32)
        intra = intra + jnp.sum(head(qk, h), axis=-1, keepdims=True) * head(v, h)
        st_ref[h] = st * head(decay, h) + lax.dot_general(head(vb, h), head(kt, h), TN, preferred_element_type=F32)
        o_ref[:, h * HG_DK:(h + 1) * HG_DK] = _rms(inter + intra, gon_ref[...]) * head(gate, h)

    def write_state():
        for h in range(HG_HEADS):
            sfin_ref[0, h] = st_ref[h].T

    if single_step:
        write_state()
    else:
        pl.when(step == pl.num_programs(1) - 1)(write_state)


def _hgrn(zb, s0, lower, g_on, batch, seq):
    c = min(HG_C, seq)
    assert seq % c == 0
    pmat, masks = _hgrn_consts(c)
    nc = seq // c
    return pl.pallas_call(
        functools.partial(_hgrn_body, single_step=nc == 1),
        grid=(batch, nc),
        in_specs=[
            pl.BlockSpec((c, 4 * HG_WIDTH), lambda b, i: (b * nc + i, 0)),
            pl.BlockSpec((1, HG_HEADS, HG_DK, HG_DK), lambda b, i: (b, 0, 0, 0)),
            pl.BlockSpec((1, HG_WIDTH), lambda b, i: (0, 0)),
            pl.BlockSpec((1, HG_DK), lambda b, i: (0, 0)),
            pl.BlockSpec(pmat.shape, lambda b, i: (0, 0)),
            pl.BlockSpec(masks.shape, lambda b, i: (0, 0, 0)),
        ],
        out_specs=[
            pl.BlockSpec((c, HG_WIDTH), lambda b, i: (b * nc + i, 0)),
            pl.BlockSpec((1, HG_HEADS, HG_DK, HG_DK), lambda b, i: (b, 0, 0, 0)),
        ],
        out_shape=[
            jax.ShapeDtypeStruct((batch * seq, HG_WIDTH), F32),
            jax.ShapeDtypeStruct((batch, HG_HEADS, HG_DK, HG_DK), F32),
        ],
        scratch_shapes=[pltpu.VMEM((HG_HEADS, HG_DK, HG_DK), F32)],
        compiler_params=_params("arbitrary", "arbitrary"),
        name="hgrn2",
    )(zb, s0, lower, g_on, pmat, masks)


def _split_bf16(x):
    hi = x.astype(BF16)
    return hi, (x - hi.astype(F32)).astype(BF16)


def _merge_body(att_p, hg_p, zg_p, x_p, att_s, hg_s, zg_s, x_s, wpa_ref, wpb_ref, wout_ref, gffn_ref, wr_ref, br_ref,
                x1_ref, h2_ref, route_ref, cnt_ref, *, n_first):
    weights = (wpa_ref, wpb_ref, wout_ref, gffn_ref, wr_ref, br_ref)
    outs = (x1_ref, h2_ref, route_ref, cnt_ref)
    i = pl.program_id(0)
    pl.when(i < n_first)(functools.partial(_merge_block, att_p, hg_p, zg_p, x_p, *weights, *outs))
    pl.when(i >= n_first)(functools.partial(_merge_block, att_s, hg_s, zg_s, x_s, *weights, *outs))


def _merge_block(att_ref, hg_ref, zg_ref, x_ref, wpa_ref, wpb_ref, wout_ref, gffn_ref, wr_ref, br_ref,
                 x1_ref, h2_ref, route_ref, cnt_ref):
    pa = jnp.dot(att_ref[...].astype(BF16), wpa_ref[...], preferred_element_type=F32)
    pb = jnp.dot(hg_ref[...].astype(BF16), wpb_ref[...], preferred_element_type=F32)
    y = _sigmoid(zg_ref[:, :D_MODEL]) * pa + _sigmoid(zg_ref[:, D_MODEL:]) * pb
    x1 = x_ref[...] + jnp.dot(y.astype(BF16), wout_ref[...], preferred_element_type=F32)
    x1_ref[...] = x1
    h2 = _rms(x1, gffn_ref[...])
    h2_ref[...] = h2.astype(BF16)

    h_hi, h_lo = _split_bf16(h2)
    w_hi, w_lo = _split_bf16(wr_ref[...])
    logits = (jnp.dot(h_hi, w_hi, preferred_element_type=F32) + jnp.dot(h_lo, w_hi, preferred_element_type=F32)
              + jnp.dot(h_hi, w_lo, preferred_element_type=F32)) + br_ref[...]
    lane = lax.broadcasted_iota(jnp.int32, logits.shape, 1).astype(F32)
    cur = logits
    vals, idxs = [], []
    for _ in range(TOP_K):
        m = jnp.max(cur, axis=-1, keepdims=True)
        idx = jnp.min(jnp.where(cur == m, lane, float(LANES)), axis=-1, keepdims=True)
        vals.append(m)
        idxs.append(idx)
        cur = jnp.where(lane == idx, -jnp.inf, cur)
    es = [jnp.exp(v - vals[0]) for v in vals]
    inv = 1.0 / functools.reduce(jnp.add, es)
    route = jnp.zeros_like(logits)
    for k, (ex, idx) in enumerate(zip(es, idxs)):
        route = (route + jnp.where(lane == idx, 1.0, 0.0) + jnp.where(lane == float(ROUTE_IDX + k), idx, 0.0)
                 + jnp.where(lane == float(ROUTE_P + k), ex * inv, 0.0))
    route_ref[...] = route
    sel = jnp.where(lane < float(N_EXPERTS), route, 0.0)
    for blk in range(cnt_ref.shape[0]):
        cnt_ref[blk] = jnp.sum(sel[blk * TBK:(blk + 1) * TBK], axis=0, keepdims=True)


def _merge(first, second, wpa, wpb, wout, g_ffn, w_router, b_router):
    tm = MERGE_TM
    assert first[3].shape[0] % tm == 0 and second[3].shape[0] % tm == 0
    n1, n2 = first[3].shape[0] // tm, second[3].shape[0] // tm
    widths = (ATT_WIDTH, HG_WIDTH, 2 * D_MODEL, D_MODEL)
    spec1 = [pl.BlockSpec((tm, w), lambda i: (jnp.minimum(i, n1 - 1), 0)) for w in widths]
    spec2 = [pl.BlockSpec((tm, w), lambda i: (jnp.maximum(i - n1, 0), 0)) for w in widths]
    row = lambda w: pl.BlockSpec((tm, w), lambda i: (i, 0))
    full = lambda a: pl.BlockSpec(a.shape, lambda i: (0,) * a.ndim)
    n = (n1 + n2) * tm
    return pl.pallas_call(
        functools.partial(_merge_body, n_first=n1),
        grid=(n1 + n2,),
        in_specs=spec1 + spec2 + [full(wpa), full(wpb), full(wout), full(g_ffn), full(w_router), full(b_router)],
        out_specs=[row(D_MODEL), row(D_MODEL), row(LANES),
                   pl.BlockSpec((tm // TBK, 1, LANES), lambda i: (i, 0, 0))],
        out_shape=[
            jax.ShapeDtypeStruct((n, D_MODEL), F32),
            jax.ShapeDtypeStruct((n, D_MODEL), BF16),
            jax.ShapeDtypeStruct((n, LANES), F32),
            jax.ShapeDtypeStruct((n // TBK, 1, LANES), F32),
        ],
        compiler_params=_params("arbitrary"),
        name="merge_router",
    )(*first, *second, wpa, wpb, wout, g_ffn, w_router, b_router)


def _route_plan(cnt):
    pc = (cnt + ROW_CH - 1) // ROW_CH
    loff = jnp.cumsum(pc, axis=1) - pc
    tot = jnp.sum(pc, axis=0)
    reg = (tot + CH_PER_TILE - 1) // CH_PER_TILE * CH_PER_TILE
    gstart = jnp.cumsum(reg) - reg
    goff = gstart[None, :] + jnp.cumsum(pc, axis=0) - pc
    ntiles = jnp.sum(reg) // CH_PER_TILE
    present = reg > 0
    ids = jnp.arange(N_EXPERTS, dtype=jnp.int32)
    later = (ids[None, :] > ids[:, None]) & present[None, :]
    nxt = jnp.min(jnp.where(later, ids[None, :], N_EXPERTS), axis=1)
    nxt = jnp.where(nxt == N_EXPERTS, -1, nxt)
    slot = (jnp.cumsum(present.astype(jnp.int32)) - 1) % 2
    first = jnp.min(jnp.where(present, ids, N_EXPERTS)).reshape(1)
    j = jnp.arange(BLK_CH, dtype=jnp.int32)
    run = jnp.sum(((loff + pc)[:, None, :] <= j[None, :, None]).astype(jnp.int32), axis=2)
    shift = jnp.sum(jnp.where(run[:, :, None] == ids[None, None, :], (goff - loff)[:, None, :], 0), axis=2)
    gmap = shift + j[None, :]
    i32 = lambda a: a.astype(jnp.int32)
    return dict(loff=i32(loff), gmap=i32(gmap), totc=i32(jnp.sum(pc, axis=1)),
                pad_start=i32(gstart + tot), pad_cnt=i32(reg - tot), ntiles=i32(ntiles).reshape(1),
                tile_start=i32(gstart // CH_PER_TILE), tile_count=i32(reg // CH_PER_TILE),
                next_expert=i32(nxt), weight_slot=i32(slot), first_expert=i32(first))


def _chunk_rows(ref, chunk):
    return ref.at[pl.ds(pl.multiple_of(chunk * ROW_CH, ROW_CH), ROW_CH)]


def _for_chunks(n, do):
    log_unroll = 2
    groups = lax.shift_right_logical(n, log_unroll)

    def group(i, carry):
        for u in range(1 << log_unroll):
            do(lax.shift_left(i, log_unroll) + u)
        return carry
    lax.fori_loop(0, groups, group, 0)

    def single(c, carry):
        do(c)
        return carry
    lax.fori_loop(lax.shift_left(groups, log_unroll), n, single, 0)


def _wait_chunks(n, copy_of_rows):
    for bit in range((RB // ROW_CH).bit_length()):
        @pl.when(((n >> bit) & 1) == 1)
        def _(bit=bit):
            copy_of_rows((1 << bit) * ROW_CH).wait()


def _dispatch_body(gmap_s, totc_s, pads_s, padn_s, nt_s, h2_ref, route_ref, tri_ref, loffv_ref,
                   xs_hbm, buf, zbuf, sem):
    b = pl.program_id(0)
    nb = pl.num_programs(0)
    slot = lax.rem(b, 2)
    fill_sem, tile_sem = 2, 3
    nt_max = xs_hbm.shape[0] // EXP_TM

    def copy_out(src, gchunk, sem_i):
        return pltpu.make_async_copy(src, _chunk_rows(xs_hbm, gchunk), sem.at[sem_i])

    def zero_tile(t):
        return pltpu.make_async_copy(zbuf, xs_hbm.at[pl.ds(pl.multiple_of(t * EXP_TM, EXP_TM), EXP_TM)],
                                     sem.at[tile_sem])

    def wait_block(bb, slot_):
        _wait_chunks(totc_s[bb], lambda rows: pltpu.make_async_copy(
            buf.at[slot_, pl.ds(0, rows)], xs_hbm.at[pl.ds(0, rows)], sem.at[slot_]))

    @pl.when(b == 0)
    def _():
        zbuf[...] = jnp.zeros_like(zbuf)

    @pl.when(b >= 2)
    def _():
        wait_block(b - 2, slot)

    rt = route_ref[...].T
    rank_t = jnp.dot(rt[0:N_EXPERTS].astype(BF16), tri_ref[...], preferred_element_type=F32)
    lpos_t = loffv_ref[...] * float(ROW_CH) + rank_t
    erow = lax.broadcasted_iota(jnp.int32, (N_EXPERTS, TBK), 0).astype(F32)
    lposk = [jnp.sum(jnp.where(rt[ROUTE_IDX + k:ROUTE_IDX + k + 1] == erow, lpos_t, 0.0), axis=0, keepdims=True)
             for k in range(TOP_K)]
    h2 = h2_ref[...]
    for r0 in range(0, RB, PERM_TM):
        rrow = lax.broadcasted_iota(jnp.int32, (PERM_TM, TBK), 0).astype(F32) + float(r0)
        perm = functools.reduce(jnp.add, [jnp.where(lposk[k] == rrow, 1.0, 0.0) for k in range(TOP_K)])
        buf[slot, r0:r0 + PERM_TM, :] = jnp.dot(perm.astype(BF16), h2, preferred_element_type=F32)

    _for_chunks(totc_s[b], lambda c: copy_out(_chunk_rows(buf.at[slot], c), gmap_s[b * BLK_CH + c], slot).start())

    @pl.when(b == nb - 1)
    def _():
        zchunk = _chunk_rows(zbuf, 0)

        def fill_expert(e, carry):
            def one(c, carry2):
                copy_out(zchunk, pads_s[e] + c, fill_sem).start()
                return carry2
            lax.fori_loop(0, padn_s[e], one, 0)
            return carry
        lax.fori_loop(0, N_EXPERTS, fill_expert, 0)

        def fill_tile(t, carry):
            zero_tile(t).start()
            return carry
        lax.fori_loop(nt_s[0], nt_max, fill_tile, 0)

        @pl.when(b >= 1)
        def _():
            wait_block(b - 1, 1 - slot)
        wait_block(b, slot)

        def drain_expert(e, carry):
            def one(c, carry2):
                copy_out(zchunk, 0, fill_sem).wait()
                return carry2
            lax.fori_loop(0, padn_s[e], one, 0)
            return carry
        lax.fori_loop(0, N_EXPERTS, drain_expert, 0)

        def drain_tile(t, carry):
            zero_tile(0).wait()
            return carry
        lax.fori_loop(nt_s[0], nt_max, drain_tile, 0)


def _dispatch(plan, h2, route, loffv, n_rows):
    nb = h2.shape[0] // TBK
    tri = jnp.asarray(np.triu(np.ones((TBK, TBK), np.float32), 1), BF16)
    grid_spec = pltpu.PrefetchScalarGridSpec(
        num_scalar_prefetch=5,
        grid=(nb,),
        in_specs=[
            pl.BlockSpec((TBK, D_MODEL), lambda b, *_: (b, 0)),
            pl.BlockSpec((TBK, LANES), lambda b, *_: (b, 0)),
            pl.BlockSpec((TBK, TBK), lambda b, *_: (0, 0)),
            pl.BlockSpec((None, N_EXPERTS, 1), lambda b, *_: (b, 0, 0)),
        ],
        out_specs=pl.BlockSpec(memory_space=pl.ANY),
        scratch_shapes=[pltpu.VMEM((2, RB, D_MODEL), F32), pltpu.VMEM((EXP_TM, D_MODEL), F32),
                        pltpu.SemaphoreType.DMA((4,))],
    )
    return pl.pallas_call(
        _dispatch_body,
        grid_spec=grid_spec,
        out_shape=jax.ShapeDtypeStruct((n_rows, D_MODEL), F32),
        compiler_params=_params("arbitrary"),
        name="moe_dispatch",
    )(plan["gmap"].reshape(-1), plan["totc"], plan["pad_start"], plan["pad_cnt"], plan["ntiles"],
      h2, route, tri, loffv)


def _experts_body(t0_s, n_s, nxt_s, par_s, first_s, nt_s, xs_hbm, wgu_hbm, bgu_ref, wd_hbm, bd_ref, ys_hbm,
                  wg_f, wd_f, wg_b, wd_b, xbuf, ybuf, wsem, xsem, ysem):
    e = pl.program_id(0)
    nt = nt_s[0]
    nt_max = xs_hbm.shape[0] // EXP_TM
    tile_rows = lambda t: pl.ds(pl.multiple_of(t * EXP_TM, EXP_TM), EXP_TM)

    def weight_copies(ex, s):
        return (pltpu.make_async_copy(wgu_hbm.at[ex], wg_f.at[s], wsem.at[s, 0]),
                pltpu.make_async_copy(wd_hbm.at[ex], wd_f.at[s], wsem.at[s, 1]))

    def x_copy(t, s):
        return pltpu.make_async_copy(xs_hbm.at[tile_rows(t)], xbuf.at[s], xsem.at[s])

    def y_copy(t, s):
        return pltpu.make_async_copy(ybuf.at[s], ys_hbm.at[tile_rows(t)], ysem.at[s])

    @pl.when(e == 0)
    def _():
        for cp in weight_copies(first_s[0], 0):
            cp.start()
        x_copy(0, 0).start()

    @pl.when(n_s[e] > 0)
    def _():
        s_w = par_s[e]
        for cp in weight_copies(e, s_w):
            cp.wait()
        wg_b[...] = wg_f[s_w].astype(BF16)
        wd_b[...] = wd_f[s_w].astype(BF16)

        @pl.when(nxt_s[e] >= 0)
        def _():
            for cp in weight_copies(nxt_s[e], 1 - s_w):
                cp.start()

        def tile(i, carry):
            t = t0_s[e] + i
            s = t & 1
            x_copy(t, s).wait()

            @pl.when(t + 1 < nt)
            def _():
                x_copy(t + 1, 1 - s).start()

            @pl.when(t >= 2)
            def _():
                y_copy(t - 2, s).wait()

            gu = jnp.dot(xbuf[s].astype(BF16), wg_b[...], preferred_element_type=F32) + bgu_ref[...]
            gate = jnp.minimum(gu[:, :D_FF], SWIGLU_LIMIT)
            up = jnp.clip(gu[:, D_FF:], -SWIGLU_LIMIT, SWIGLU_LIMIT)
            act = (up + 1.0) * gate * _sigmoid(SWIGLU_ALPHA * gate)
            ybuf[s] = jnp.dot(act.astype(BF16), wd_b[...], preferred_element_type=F32) + bd_ref[...]
            y_copy(t, s).start()
            return carry
        lax.fori_loop(0, n_s[e], tile, 0)

    @pl.when(e == pl.num_programs(0) - 1)
    def _():
        @pl.when(nt >= 2)
        def _():
            y_copy(nt - 2, nt & 1).wait()
        y_copy(nt - 1, (nt - 1) & 1).wait()
        ybuf[0] = jnp.zeros((EXP_TM, D_MODEL), F32)

        def fill(t, carry):
            y_copy(t, 0).start()
            return carry
        lax.fori_loop(nt, nt_max, fill, 0)

        def drain(t, carry):
            y_copy(0, 0).wait()
            return carry
        lax.fori_loop(nt, nt_max, drain, 0)


def _experts(plan, xs, w_gu, b_gu, w_down, b_down):
    n_rows = xs.shape[0]
    of_expert = lambda e, *_: (e, 0, 0)
    grid_spec = pltpu.PrefetchScalarGridSpec(
        num_scalar_prefetch=6,
        grid=(N_EXPERTS,),
        in_specs=[
            pl.BlockSpec(memory_space=pl.ANY),
            pl.BlockSpec(memory_space=pl.ANY),
            pl.BlockSpec((None, 1, 2 * D_FF), of_expert),
            pl.BlockSpec(memory_space=pl.ANY),
            pl.BlockSpec((None, 1, D_MODEL), of_expert),
        ],
        out_specs=pl.BlockSpec(memory_space=pl.ANY),
        scratch_shapes=[pltpu.VMEM((2, D_MODEL, 2 * D_FF), F32), pltpu.VMEM((2, D_FF, D_MODEL), F32),
                        pltpu.VMEM((D_MODEL, 2 * D_FF), BF16), pltpu.VMEM((D_FF, D_MODEL), BF16),
                        pltpu.VMEM((2, EXP_TM, D_MODEL), F32), pltpu.VMEM((2, EXP_TM, D_MODEL), F32),
                        pltpu.SemaphoreType.DMA((2, 2)), pltpu.SemaphoreType.DMA((2,)),
                        pltpu.SemaphoreType.DMA((2,))],
    )
    return pl.pallas_call(
        _experts_body,
        grid_spec=grid_spec,
        out_shape=jax.ShapeDtypeStruct((n_rows, D_MODEL), F32),
        compiler_params=_params("arbitrary"),
        name="moe_experts",
    )(plan["tile_start"], plan["tile_count"], plan["next_expert"], plan["weight_slot"], plan["first_expert"],
      plan["ntiles"], xs, w_gu, b_gu, w_down, b_down)


def _combine_body(gmap_s, totc_s, route_ref, tril_ref, loffrow_ref, x1_ref, gfin_ref, ys_hbm, y_ref, buf, sem):
    b = pl.program_id(0)
    nb = pl.num_programs(0)
    slot = lax.rem(b, 2)

    def copy_in(slot_, lchunk, gchunk):
        return pltpu.make_async_copy(_chunk_rows(ys_hbm, gchunk), _chunk_rows(buf.at[slot_], lchunk), sem.at[slot_])

    def fetch_block(bb, slot_):
        _for_chunks(totc_s[bb], lambda c: copy_in(slot_, c, gmap_s[bb * BLK_CH + c]).start())

    def wait_block(bb, slot_):
        _wait_chunks(totc_s[bb], lambda rows: pltpu.make_async_copy(
            ys_hbm.at[pl.ds(0, rows)], buf.at[slot_, pl.ds(0, rows)], sem.at[slot_]))

    @pl.when(b == 0)
    def _():
        buf[...] = jnp.zeros_like(buf)
        fetch_block(0, 0)

    @pl.when(b + 1 < nb)
    def _():
        fetch_block(b + 1, 1 - slot)

    wait_block(b, slot)

    route = route_ref[...]
    lane = lax.broadcasted_iota(jnp.int32, (1, LANES), 1).astype(F32)
    sel = jnp.where(lane < float(N_EXPERTS), route, 0.0).astype(BF16)
    rank = jnp.dot(tril_ref[...], sel, preferred_element_type=F32)
    lpos = loffrow_ref[...] * float(ROW_CH) + rank
    lposk, pk = [], []
    for k in range(TOP_K):
        idx = route[:, ROUTE_IDX + k:ROUTE_IDX + k + 1]
        lposk.append(jnp.sum(jnp.where(lane == idx, lpos, 0.0), axis=-1, keepdims=True))
        pk.append(route[:, ROUTE_P + k:ROUTE_P + k + 1])
    acc = x1_ref[...]
    for r0 in range(0, RB, PERM_TM):
        col = lax.broadcasted_iota(jnp.int32, (TBK, PERM_TM), 1).astype(F32) + float(r0)
        w = functools.reduce(jnp.add, [jnp.where(lposk[k] == col, pk[k], 0.0) for k in range(TOP_K)])
        acc = acc + jnp.dot(w.astype(BF16), buf[slot, r0:r0 + PERM_TM, :].astype(BF16), preferred_element_type=F32)
    y_ref[...] = _rms(acc, gfin_ref[...])


def _combine(plan, blocks, route, loffrow, x1, g_final, ys):
    b0, b1 = blocks
    nb = b1 - b0
    tril = jnp.asarray(np.tril(np.ones((TBK, TBK), np.float32), -1), BF16)
    grid_spec = pltpu.PrefetchScalarGridSpec(
        num_scalar_prefetch=2,
        grid=(nb,),
        in_specs=[
            pl.BlockSpec((TBK, LANES), lambda b, *_: (b + b0, 0)),
            pl.BlockSpec((TBK, TBK), lambda b, *_: (0, 0)),
            pl.BlockSpec((None, 1, LANES), lambda b, *_: (b + b0, 0, 0)),
            pl.BlockSpec((TBK, D_MODEL), lambda b, *_: (b + b0, 0)),
            pl.BlockSpec((1, D_MODEL), lambda b, *_: (0, 0)),
            pl.BlockSpec(memory_space=pl.ANY),
        ],
        out_specs=pl.BlockSpec((TBK, D_MODEL), lambda b, *_: (b, 0)),
        scratch_shapes=[pltpu.VMEM((2, RB, D_MODEL), F32), pltpu.SemaphoreType.DMA((2,))],
    )
    return pl.pallas_call(
        _combine_body,
        grid_spec=grid_spec,
        out_shape=jax.ShapeDtypeStruct((nb * TBK, D_MODEL), F32),
        compiler_params=_params("arbitrary"),
        name="moe_combine",
    )(plan["gmap"][b0:b1].reshape(-1), plan["totc"][b0:b1], route, tril, loffrow, x1, g_final, ys)


def kernel(x_prompt, x_sample, cache_k, cache_v, state_s, g_mix, w_in, rel_bias, lb_logits, g_out_norm,
           w_pa, w_pb, w_out, g_ffn, w_router, b_router, w_gu, b_gu, w_down, b_down, g_final):
    B, T = x_prompt.shape[:2]
    DB, S = x_sample.shape[:2]
    depth = w_in.shape[0]
    assert depth == 1 and T % ATT_QBLK == 0 and S == CHUNK
    cw = cache_k.shape[2]
    assert cw == WINDOW
    l = 0

    lower = jnp.cumsum(jax.nn.softmax(lb_logits.astype(F32), axis=0), axis=0)[l].reshape(1, HG_WIDTH)
    w_in_b = w_in[l].astype(BF16)
    wpa, wpb, wout = w_pa[l].astype(BF16), w_pb[l].astype(BF16), w_out[l].astype(BF16)
    row = lambda a: a.reshape(1, -1).astype(F32)
    base = _rel_bias_base(rel_bias[l])
    b_gu3 = b_gu[l].reshape(N_EXPERTS, 1, 2 * D_FF)
    b_down3 = b_down[l].reshape(N_EXPERTS, 1, D_MODEL)
    pad_e = LANES - N_EXPERTS
    wr = jnp.pad(w_router[l].astype(F32), ((0, 0), (0, pad_e)))
    br = jnp.concatenate([b_router[l].astype(F32), jnp.full((pad_e,), NEG, F32)]).reshape(1, LANES)

    n_tok = B * T + DB * S
    nb, nbp = n_tok // TBK, (B * T) // TBK

    def front(x, batch, seq, s0, attend):
        xf = x.reshape(batch * seq, D_MODEL)
        za, zb, zg = _inproj(xf, row(g_mix[l]), w_in_b)
        att = attend(za)
        hg, s_fin = _hgrn(zb, s0, lower, row(g_out_norm[l]), batch, seq)
        za3 = za.reshape(batch, seq, 3 * ATT_WIDTH)
        heads = lambda a: a.reshape(1, batch, a.shape[1], ATT_HEADS, ATT_DIM)
        keep = min(WINDOW, seq)
        nk = heads(za3[:, seq - keep:, ATT_WIDTH:2 * ATT_WIDTH])
        nv = heads(za3[:, seq - keep:, 2 * ATT_WIDTH:])
        return dict(mix=(att, hg, zg, xf), nk=nk, nv=nv, s=s_fin[None])

    ck = cache_k[l].reshape(DB, cw, ATT_WIDTH)
    cv = cache_v[l].reshape(DB, cw, ATT_WIDTH)
    fp = front(x_prompt, B, T, jnp.zeros((B, HG_HEADS, HG_DK, HG_DK), F32), lambda za: _attn_prompt(za, base, B, T))
    fs = front(x_sample, DB, S, state_s[l].astype(F32), lambda za: _attn_sample(za, ck, cv, base, DB, S))

    x1, h2, route, cnt = _merge(fp["mix"], fs["mix"], wpa, wpb, wout, row(g_ffn[l]), wr, br)
    cnt = cnt[:, 0, :N_EXPERTS].astype(jnp.int32)
    max_rows = n_tok * TOP_K + nb * N_EXPERTS * (ROW_CH - 1) + N_EXPERTS * (EXP_TM - 1)
    nt_max = -(-max_rows // EXP_TM)
    plan = _route_plan(cnt)
    loff_f = plan["loff"].astype(F32)
    xs = _dispatch(plan, h2, route, loff_f[:, :, None], nt_max * EXP_TM)
    ysort = _experts(plan, xs, w_gu[l], b_gu3, w_down[l], b_down3)
    loffrow = jnp.pad(loff_f, ((0, 0), (0, pad_e)))[:, None, :]
    yp = _combine(plan, (0, nbp), route, loffrow, x1, row(g_final), ysort)
    ys = _combine(plan, (nbp, nb), route, loffrow, x1, row(g_final), ysort)
    return (yp.reshape(B, T, D_MODEL), ys.reshape(DB, S, D_MODEL), fp["nk"], fp["nv"], fp["s"],
            fs["nk"], fs["nv"], fs["s"])
```

```python
import functools

import numpy as np
import jax
import jax.numpy as jnp
from jax import lax
from jax.experimental import pallas as pl
from jax.experimental.pallas import tpu as pltpu

F32 = jnp.float32
BF16 = jnp.bfloat16

D_MODEL = 1024
CHUNK = 64
LEFT_CHUNKS = 8
WINDOW = LEFT_CHUNKS * CHUNK
ATT_HEADS = 8
ATT_DIM = 64
ATT_WIDTH = ATT_HEADS * ATT_DIM
MAX_REL = 256
HG_HEADS = 4
HG_DK = 128
HG_WIDTH = HG_HEADS * HG_DK
N_EXPERTS = 32
TOP_K = 4
D_FF = D_MODEL
SWIGLU_LIMIT = 7.0
SWIGLU_ALPHA = 1.702
RMS_EPS = 1e-5

LANES = 128
NEG = -1e30
LOG2E = 1.4426950408889634
ATT_QBLK = 4 * CHUNK
ATT_KBLKS = LEFT_CHUNKS * CHUNK // ATT_QBLK + 1
HG_C = 128
VMEM_LIMIT = 56 * 1024 * 1024
BIAS_W = 1024
SUBLANES = 8
TBK = 256
ROW_CH = SUBLANES
RB = TBK * TOP_K + N_EXPERTS * ROW_CH
MERGE_TM = 2 * TBK
PERM_TM = 256
EXP_TM = 512
CH_PER_TILE = EXP_TM // ROW_CH
BLK_CH = RB // ROW_CH
ROUTE_IDX = 64
ROUTE_P = 72

NT = (((1,), (1,)), ((), ()))
TN = (((0,), (0,)), ((), ()))


def _rms(x, g):
    return x * lax.rsqrt(jnp.mean(x * x, axis=-1, keepdims=True) + RMS_EPS) * g


def _sigmoid(x):
    return 1.0 / (1.0 + jnp.exp(-x))


def _params(*sem):
    return pltpu.CompilerParams(dimension_semantics=sem, vmem_limit_bytes=VMEM_LIMIT)


def _inproj_body(x_ref, g_ref, w_ref, za_ref, zb_ref, zg_ref):
    h = _rms(x_ref[...], g_ref[...]).astype(BF16)
    a, b = 3 * ATT_WIDTH, 3 * ATT_WIDTH + 4 * HG_WIDTH
    za_ref[...] = jnp.dot(h, w_ref[:, :a], preferred_element_type=F32)
    zb_ref[...] = jnp.dot(h, w_ref[:, a:b], preferred_element_type=F32)
    zg_ref[...] = jnp.dot(h, w_ref[:, b:], preferred_element_type=F32)


def _inproj(x, g, w_bf16, tm=512):
    n = x.shape[0]
    cols = w_bf16.shape[1]
    wa, wb, wg = 3 * ATT_WIDTH, 4 * HG_WIDTH, 2 * D_MODEL
    return pl.pallas_call(
        _inproj_body,
        grid=(n // tm,),
        in_specs=[
            pl.BlockSpec((tm, D_MODEL), lambda i: (i, 0)),
            pl.BlockSpec((1, D_MODEL), lambda i: (0, 0)),
            pl.BlockSpec((D_MODEL, cols), lambda i: (0, 0)),
        ],
        out_specs=[
            pl.BlockSpec((tm, wa), lambda i: (i, 0)),
            pl.BlockSpec((tm, wb), lambda i: (i, 0)),
            pl.BlockSpec((tm, wg), lambda i: (i, 0)),
        ],
        out_shape=[
            jax.ShapeDtypeStruct((n, wa), F32),
            jax.ShapeDtypeStruct((n, wb), F32),
            jax.ShapeDtypeStruct((n, wg), F32),
        ],
        compiler_params=_params("arbitrary"),
        name="inproj",
    )(x, g, w_bf16)


def _attn_heads(q_ref, k_refs, v_refs, bias_fn, pens, o_ref):
    lane = lax.broadcasted_iota(jnp.int32, (1, LANES), 1)
    first = lane < ATT_DIM
    for hp in range(ATT_HEADS // 2):
        sl = slice(hp * LANES, (hp + 1) * LANES)
        q2 = q_ref[:, sl] * (ATT_DIM ** -0.5 * LOG2E)
        ks = [k[:, sl].astype(BF16) for k in k_refs]
        outs = []
        for half in range(2):
            head = 2 * hp + half
            mine = first if half == 0 else lane >= ATT_DIM
            qm = jnp.where(mine, q2, 0.0).astype(BF16)
            vs = [jnp.where(mine, v[:, sl], 1.0).astype(BF16) for v in v_refs]
            ss = []
            for j, kj in enumerate(ks):
                s = lax.dot_general(qm, kj, NT, preferred_element_type=F32) + bias_fn(head, j)
                if pens[j] is not None:
                    s = s + pens[j]
                ss.append(s)
            if all(s.shape == ss[0].shape for s in ss):
                m = jnp.max(functools.reduce(jnp.maximum, ss), axis=-1, keepdims=True)
            else:
                m = functools.reduce(jnp.maximum, [jnp.max(s, axis=-1, keepdims=True) for s in ss])
            outs.append(functools.reduce(jnp.add, [jnp.dot(jnp.exp2(s - m).astype(BF16), vj, preferred_element_type=F32)
                                                   for s, vj in zip(ss, vs)]))
        num = jnp.where(first, outs[0], outs[1])
        den = pltpu.roll(jnp.where(first, outs[1], outs[0]), ATT_DIM, 1)
        o_ref[:, sl] = num * (1.0 / den)


def _fill_bias(base_ref, bias_ref, banded):
    nq, nk = bias_ref.shape[1:]
    if banded:
        r = lax.broadcasted_iota(jnp.int32, (nq, nk), 0)
        s = lax.broadcasted_iota(jnp.int32, (nq, nk), 1)
        qc = (r + WINDOW) // CHUNK
        kc = s // CHUNK
        pen = jnp.where(kc <= qc, jnp.where(kc >= qc - LEFT_CHUNKS, 0.0, NEG), NEG)
    for h in range(ATT_HEADS):
        rows = jnp.broadcast_to(base_ref[h:h + 1, :], (nq, BIAS_W))
        t = pltpu.roll(rows, 0, 1, stride=1, stride_axis=0)[:, :nk] * LOG2E
        bias_ref[h] = t + pen if banded else t


def _attn_prompt_body(q_ref, k0, k1, k2, v0, v1, v2, base_ref, o_ref, bias_ref):
    i = pl.program_id(1)

    @pl.when((pl.program_id(0) == 0) & (i == 0))
    def _():
        _fill_bias(base_ref, bias_ref, True)

    bias_fn = lambda h, j: bias_ref[h, :, j * ATT_QBLK:(j + 1) * ATT_QBLK]
    back = ATT_KBLKS - 1

    @pl.when(i < back)
    def _():
        pens = [jnp.where(i - back + j >= 0, 0.0, NEG) for j in range(back)] + [None]
        _attn_heads(q_ref, [k0, k1, k2], [v0, v1, v2], bias_fn, pens, o_ref)

    @pl.when(i >= back)
    def _():
        _attn_heads(q_ref, [k0, k1, k2], [v0, v1, v2], bias_fn, [None] * ATT_KBLKS, o_ref)


def _attn_prompt(za, base, batch, seq):
    nq = seq // ATT_QBLK
    back = ATT_KBLKS - 1
    qspec = pl.BlockSpec((ATT_QBLK, ATT_WIDTH), lambda b, i: (b * nq + i, 0))

    def kvspec(j, col):
        return pl.BlockSpec((ATT_QBLK, ATT_WIDTH),
                            lambda b, i: (b * nq + jnp.maximum(i - back + j, 0), col))

    return pl.pallas_call(
        _attn_prompt_body,
        grid=(batch, nq),
        in_specs=[qspec] + [kvspec(j, 1) for j in range(ATT_KBLKS)] + [kvspec(j, 2) for j in range(ATT_KBLKS)]
        + [pl.BlockSpec(base.shape, lambda b, i: (0, 0))],
        out_specs=pl.BlockSpec((ATT_QBLK, ATT_WIDTH), lambda b, i: (b * nq + i, 0)),
        out_shape=jax.ShapeDtypeStruct((batch * seq, ATT_WIDTH), F32),
        scratch_shapes=[pltpu.VMEM((ATT_HEADS, ATT_QBLK, ATT_KBLKS * ATT_QBLK), F32)],
        compiler_params=_params("arbitrary", "arbitrary"),
        name="attn_prompt",
    )(za, za, za, za, za, za, za, base)


def _attn_sample_body(q_ref, kn_ref, vn_ref, ck_ref, cv_ref, base_ref, o_ref, bias_ref):
    @pl.when(pl.program_id(0) == 0)
    def _():
        _fill_bias(base_ref, bias_ref, False)

    cw = ck_ref.shape[0]
    bias_fn = lambda h, j: bias_ref[h, :, :cw] if j == 0 else bias_ref[h, :, cw:]
    _attn_heads(q_ref, [ck_ref, kn_ref], [cv_ref, vn_ref], bias_fn, [None, None], o_ref)


def _attn_sample(za, ck, cv, base, batch, seq):
    cw = ck.shape[1]
    return pl.pallas_call(
        _attn_sample_body,
        grid=(batch,),
        in_specs=[
            pl.BlockSpec((seq, ATT_WIDTH), lambda b: (b, 0)),
            pl.BlockSpec((seq, ATT_WIDTH), lambda b: (b, 1)),
            pl.BlockSpec((seq, ATT_WIDTH), lambda b: (b, 2)),
            pl.BlockSpec((None, cw, ATT_WIDTH), lambda b: (b, 0, 0)),
            pl.BlockSpec((None, cw, ATT_WIDTH), lambda b: (b, 0, 0)),
            pl.BlockSpec(base.shape, lambda b: (0, 0)),
        ],
        out_specs=pl.BlockSpec((seq, ATT_WIDTH), lambda b: (b, 0)),
        out_shape=jax.ShapeDtypeStruct((batch * seq, ATT_WIDTH), F32),
        scratch_shapes=[pltpu.VMEM((ATT_HEADS, seq, cw + seq), F32)],
        compiler_params=_params("arbitrary"),
        name="attn_sample",
    )(za, za, za, ck, cv, base)


def _rel_bias_base(table):
    top = table[:, 2 * MAX_REL:].astype(F32)
    rev = table[:, ::-1][:, :2 * MAX_REL].astype(F32)
    left = WINDOW - MAX_REL
    return jnp.concatenate([jnp.broadcast_to(top, (ATT_HEADS, left)), rev,
                            jnp.broadcast_to(top, (ATT_HEADS, BIAS_W - left - 2 * MAX_REL))], axis=1)


def _hgrn_consts(c):
    t = np.arange(c)[:, None]
    j = np.arange(c)[None, :]
    mats = [j <= t, j > t]
    masks = []
    m = c // 2
    while m >= 1:
        ref = (t // (2 * m)) * (2 * m) + m - 1
        second = (t % (2 * m)) >= m
        if m < SUBLANES:
            mats.append((second & (j > ref) & (j <= t)) | (~second & (j > t) & (j <= ref)))
        masks.append((t // (2 * m)) == (j // (2 * m)))
        m //= 2
    return (jnp.asarray(np.concatenate(mats, 0).astype(np.float32), BF16),
            jnp.asarray(np.stack(masks).astype(np.float32)))


def _hgrn_body(zb_ref, s0_ref, lower_ref, gon_ref, p_ref, mask_ref, o_ref, sfin_ref, st_ref, *, single_step):
    c = zb_ref.shape[0]
    step = pl.program_id(1)

    def load_state():
        for h in range(HG_HEADS):
            st_ref[h] = s0_ref[0, h].T

    if single_step:
        load_state()
    else:
        pl.when(step == 0)(load_state)

    pmat = p_ref[...]
    n_levels = mask_ref.shape[0]
    part = lambda i: zb_ref[:, i * HG_WIDTH:(i + 1) * HG_WIDTH]
    head = lambda a, h: a[:, h * HG_DK:(h + 1) * HG_DK]
    q = part(0)
    low = lower_ref[...]
    f = low + (1.0 - low) * _sigmoid(part(1))
    lf = jnp.log(f)
    k = 1.0 - f
    ib = part(2)
    v = ib * _sigmoid(ib)
    og = part(3)

    hi = lf.astype(BF16)
    r1 = lf - hi.astype(F32)
    mid = r1.astype(BF16)
    lo = (r1 - mid.astype(F32)).astype(BF16)
    e = (jnp.dot(pmat, hi, preferred_element_type=F32) + jnp.dot(pmat, mid, preferred_element_type=F32)
         + jnp.dot(pmat, lo, preferred_element_type=F32))
    b = e[0:c]
    decay = jnp.exp(e[c - 1:c])
    qe = (q * jnp.exp(b)).astype(BF16)
    kt = (k * jnp.exp(e[c:2 * c])).astype(BF16)
    vb = v.astype(BF16)
    qk = q * k
    gate = og * _sigmoid(og)

    row = lax.broadcasted_iota(jnp.int32, (c, HG_WIDTH), 0)
    att = [None] * HG_HEADS
    n_rows_p = 2
    for lvl in range(n_levels):
        m = c >> (lvl + 1)
        if m >= SUBLANES:
            ref = [jnp.broadcast_to(b[p * 2 * m + m - 1:p * 2 * m + m], (2 * m, HG_WIDTH)) for p in range(c // (2 * m))]
            x = jnp.exp(-jnp.abs(b - (jnp.concatenate(ref, axis=0) if len(ref) > 1 else ref[0])))
        else:
            x = jnp.exp(e[n_rows_p * c:(n_rows_p + 1) * c])
            n_rows_p += 1
        second = (row & m) != 0
        qm = jnp.where(second, q * x, 0.0).astype(BF16)
        km = jnp.where(second, 0.0, k * x).astype(BF16)
        for h in range(HG_HEADS):
            a = lax.dot_general(head(qm, h), head(km, h), NT, preferred_element_type=F32)
            if lvl > 0:
                a = a * mask_ref[lvl]
            att[h] = a if att[h] is None else att[h] + a

    for h in range(HG_HEADS):
        st = st_ref[h]
        inter = lax.dot_general(head(qe, h), st.astype(BF16), NT, preferred_element_type=F32)
        intra = jnp.dot(att[h].astype(BF16), head(vb, h), preferred_element_type=F32)
        intra = intra + jnp.sum(head(qk, h), axis=-1, keepdims=True) * head(v, h)
        st_ref[h] = st * head(decay, h) + lax.dot_general(head(vb, h), head(kt, h), TN, preferred_element_type=F32)
        o_ref[:, h * HG_DK:(h + 1) * HG_DK] = _rms(inter + intra, gon_ref[...]) * head(gate, h)

    def write_state():
        for h in range(HG_HEADS):
            sfin_ref[0, h] = st_ref[h].T

    if single_step:
        write_state()
    else:
        pl.when(step == pl.num_programs(1) - 1)(write_state)


def _hgrn(zb, s0, lower, g_on, batch, seq):
    c = min(HG_C, seq)
    assert seq % c == 0
    pmat, masks = _hgrn_consts(c)
    nc = seq // c
    return pl.pallas_call(
        functools.partial(_hgrn_body, single_step=nc == 1),
        grid=(batch, nc),
        in_specs=[
            pl.BlockSpec((c, 4 * HG_WIDTH), lambda b, i: (b * nc + i, 0)),
            pl.BlockSpec((1, HG_HEADS, HG_DK, HG_DK), lambda b, i: (b, 0, 0, 0)),
            pl.BlockSpec((1, HG_WIDTH), lambda b, i: (0, 0)),
            pl.BlockSpec((1, HG_DK), lambda b, i: (0, 0)),
            pl.BlockSpec(pmat.shape, lambda b, i: (0, 0)),
            pl.BlockSpec(masks.shape, lambda b, i: (0, 0, 0)),
        ],
        out_specs=[
            pl.BlockSpec((c, HG_WIDTH), lambda b, i: (b * nc + i, 0)),
            pl.BlockSpec((1, HG_HEADS, HG_DK, HG_DK), lambda b, i: (b, 0, 0, 0)),
        ],
        out_shape=[
            jax.ShapeDtypeStruct((batch * seq, HG_WIDTH), F32),
            jax.ShapeDtypeStruct((batch, HG_HEADS, HG_DK, HG_DK), F32),
        ],
        scratch_shapes=[pltpu.VMEM((HG_HEADS, HG_DK, HG_DK), F32)],
        compiler_params=_params("arbitrary", "arbitrary"),
        name="hgrn2",
    )(zb, s0, lower, g_on, pmat, masks)


def _split_bf16(x):
    hi = x.astype(BF16)
    return hi, (x - hi.astype(F32)).astype(BF16)


def _merge_body(att_p, hg_p, zg_p, x_p, att_s, hg_s, zg_s, x_s, wpa_ref, wpb_ref, wout_ref, gffn_ref, wr_ref, br_ref,
                x1_ref, h2_ref, route_ref, cnt_ref, *, n_first):
    weights = (wpa_ref, wpb_ref, wout_ref, gffn_ref, wr_ref, br_ref)
    outs = (x1_ref, h2_ref, route_ref, cnt_ref)
    i = pl.program_id(0)
    pl.when(i < n_first)(functools.partial(_merge_block, att_p, hg_p, zg_p, x_p, *weights, *outs))
    pl.when(i >= n_first)(functools.partial(_merge_block, att_s, hg_s, zg_s, x_s, *weights, *outs))


def _merge_block(att_ref, hg_ref, zg_ref, x_ref, wpa_ref, wpb_ref, wout_ref, gffn_ref, wr_ref, br_ref,
                 x1_ref, h2_ref, route_ref, cnt_ref):
    pa = jnp.dot(att_ref[...].astype(BF16), wpa_ref[...], preferred_element_type=F32)
    pb = jnp.dot(hg_ref[...].astype(BF16), wpb_ref[...], preferred_element_type=F32)
    y = _sigmoid(zg_ref[:, :D_MODEL]) * pa + _sigmoid(zg_ref[:, D_MODEL:]) * pb
    x1 = x_ref[...] + jnp.dot(y.astype(BF16), wout_ref[...], preferred_element_type=F32)
    x1_ref[...] = x1
    h2 = _rms(x1, gffn_ref[...])
    h2_ref[...] = h2.astype(BF16)

    h_hi, h_lo = _split_bf16(h2)
    w_hi, w_lo = _split_bf16(wr_ref[...])
    logits = (jnp.dot(h_hi, w_hi, preferred_element_type=F32) + jnp.dot(h_lo, w_hi, preferred_element_type=F32)
              + jnp.dot(h_hi, w_lo, preferred_element_type=F32)) + br_ref[...]
    lane = lax.broadcasted_iota(jnp.int32, logits.shape, 1).astype(F32)
    cur = logits
    vals, idxs = [], []
    for _ in range(TOP_K):
        m = jnp.max(cur, axis=-1, keepdims=True)
        idx = jnp.min(jnp.where(cur == m, lane, float(LANES)), axis=-1, keepdims=True)
        vals.append(m)
        idxs.append(idx)
        cur = jnp.where(lane == idx, -jnp.inf, cur)
    es = [jnp.exp(v - vals[0]) for v in vals]
    inv = 1.0 / functools.reduce(jnp.add, es)
    route = jnp.zeros_like(logits)
    for k, (ex, idx) in enumerate(zip(es, idxs)):
        route = (route + jnp.where(lane == idx, 1.0, 0.0) + jnp.where(lane == float(ROUTE_IDX + k), idx, 0.0)
                 + jnp.where(lane == float(ROUTE_P + k), ex * inv, 0.0))
    route_ref[...] = route
    sel = jnp.where(lane < float(N_EXPERTS), route, 0.0)
    for blk in range(cnt_ref.shape[0]):
        cnt_ref[blk] = jnp.sum(sel[blk * TBK:(blk + 1) * TBK], axis=0, keepdims=True)


def _merge(first, second, wpa, wpb, wout, g_ffn, w_router, b_router):
    tm = MERGE_TM
    assert first[3].shape[0] % tm == 0 and second[3].shape[0] % tm == 0
    n1, n2 = first[3].shape[0] // tm, second[3].shape[0] // tm
    widths = (ATT_WIDTH, HG_WIDTH, 2 * D_MODEL, D_MODEL)
    spec1 = [pl.BlockSpec((tm, w), lambda i: (jnp.minimum(i, n1 - 1), 0)) for w in widths]
    spec2 = [pl.BlockSpec((tm, w), lambda i: (jnp.maximum(i - n1, 0), 0)) for w in widths]
    row = lambda w: pl.BlockSpec((tm, w), lambda i: (i, 0))
    full = lambda a: pl.BlockSpec(a.shape, lambda i: (0,) * a.ndim)
    n = (n1 + n2) * tm
    return pl.pallas_call(
        functools.partial(_merge_body, n_first=n1),
        grid=(n1 + n2,),
        in_specs=spec1 + spec2 + [full(wpa), full(wpb), full(wout), full(g_ffn), full(w_router), full(b_router)],
        out_specs=[row(D_MODEL), row(D_MODEL), row(LANES),
                   pl.BlockSpec((tm // TBK, 1, LANES), lambda i: (i, 0, 0))],
        out_shape=[
            jax.ShapeDtypeStruct((n, D_MODEL), F32),
            jax.ShapeDtypeStruct((n, D_MODEL), BF16),
            jax.ShapeDtypeStruct((n, LANES), F32),
            jax.ShapeDtypeStruct((n // TBK, 1, LANES), F32),
        ],
        compiler_params=_params("arbitrary"),
        name="merge_router",
    )(*first, *second, wpa, wpb, wout, g_ffn, w_router, b_router)


def _route_plan(cnt):
    pc = (cnt + ROW_CH - 1) // ROW_CH
    loff = jnp.cumsum(pc, axis=1) - pc
    tot = jnp.sum(pc, axis=0)
    reg = (tot + CH_PER_TILE - 1) // CH_PER_TILE * CH_PER_TILE
    gstart = jnp.cumsum(reg) - reg
    goff = gstart[None, :] + jnp.cumsum(pc, axis=0) - pc
    ntiles = jnp.sum(reg) // CH_PER_TILE
    present = reg > 0
    ids = jnp.arange(N_EXPERTS, dtype=jnp.int32)
    later = (ids[None, :] > ids[:, None]) & present[None, :]
    nxt = jnp.min(jnp.where(later, ids[None, :], N_EXPERTS), axis=1)
    nxt = jnp.where(nxt == N_EXPERTS, -1, nxt)
    slot = (jnp.cumsum(present.astype(jnp.int32)) - 1) % 2
    first = jnp.min(jnp.where(present, ids, N_EXPERTS)).reshape(1)
    j = jnp.arange(BLK_CH, dtype=jnp.int32)
    run = jnp.sum(((loff + pc)[:, None, :] <= j[None, :, None]).astype(jnp.int32), axis=2)
    shift = jnp.sum(jnp.where(run[:, :, None] == ids[None, None, :], (goff - loff)[:, None, :], 0), axis=2)
    gmap = shift + j[None, :]
    i32 = lambda a: a.astype(jnp.int32)
    return dict(loff=i32(loff), gmap=i32(gmap), totc=i32(jnp.sum(pc, axis=1)),
                pad_start=i32(gstart + tot), pad_cnt=i32(reg - tot), ntiles=i32(ntiles).reshape(1),
                tile_start=i32(gstart // CH_PER_TILE), tile_count=i32(reg // CH_PER_TILE),
                next_expert=i32(nxt), weight_slot=i32(slot), first_expert=i32(first))


def _chunk_rows(ref, chunk):
    return ref.at[pl.ds(pl.multiple_of(chunk * ROW_CH, ROW_CH), ROW_CH)]


def _for_chunks(n, do):
    log_unroll = 2
    groups = lax.shift_right_logical(n, log_unroll)

    def group(i, carry):
        for u in range(1 << log_unroll):
            do(lax.shift_left(i, log_unroll) + u)
        return carry
    lax.fori_loop(0, groups, group, 0)

    def single(c, carry):
        do(c)
        return carry
    lax.fori_loop(lax.shift_left(groups, log_unroll), n, single, 0)


def _wait_chunks(n, copy_of_rows):
    for bit in range((RB // ROW_CH).bit_length()):
        @pl.when(((n >> bit) & 1) == 1)
        def _(bit=bit):
            copy_of_rows((1 << bit) * ROW_CH).wait()


def _dispatch_body(gmap_s, totc_s, pads_s, padn_s, nt_s, h2_ref, route_ref, tri_ref, loffv_ref,
                   xs_hbm, buf, zbuf, sem):
    b = pl.program_id(0)
    nb = pl.num_programs(0)
    slot = lax.rem(b, 2)
    fill_sem, tile_sem = 2, 3
    nt_max = xs_hbm.shape[0] // EXP_TM

    def copy_out(src, gchunk, sem_i):
        return pltpu.make_async_copy(src, _chunk_rows(xs_hbm, gchunk), sem.at[sem_i])

    def zero_tile(t):
        return pltpu.make_async_copy(zbuf, xs_hbm.at[pl.ds(pl.multiple_of(t * EXP_TM, EXP_TM), EXP_TM)],
                                     sem.at[tile_sem])

    def wait_block(bb, slot_):
        _wait_chunks(totc_s[bb], lambda rows: pltpu.make_async_copy(
            buf.at[slot_, pl.ds(0, rows)], xs_hbm.at[pl.ds(0, rows)], sem.at[slot_]))

    @pl.when(b == 0)
    def _():
        zbuf[...] = jnp.zeros_like(zbuf)

    @pl.when(b >= 2)
    def _():
        wait_block(b - 2, slot)

    rt = route_ref[...].T
    rank_t = jnp.dot(rt[0:N_EXPERTS].astype(BF16), tri_ref[...], preferred_element_type=F32)
    lpos_t = loffv_ref[...] * float(ROW_CH) + rank_t
    erow = lax.broadcasted_iota(jnp.int32, (N_EXPERTS, TBK), 0).astype(F32)
    lposk = [jnp.sum(jnp.where(rt[ROUTE_IDX + k:ROUTE_IDX + k + 1] == erow, lpos_t, 0.0), axis=0, keepdims=True)
             for k in range(TOP_K)]
    h2 = h2_ref[...]
    for r0 in range(0, RB, PERM_TM):
        rrow = lax.broadcasted_iota(jnp.int32, (PERM_TM, TBK), 0).astype(F32) + float(r0)
        perm = functools.reduce(jnp.add, [jnp.where(lposk[k] == rrow, 1.0, 0.0) for k in range(TOP_K)])
        buf[slot, r0:r0 + PERM_TM, :] = jnp.dot(perm.astype(BF16), h2, preferred_element_type=F32)

    _for_chunks(totc_s[b], lambda c: copy_out(_chunk_rows(buf.at[slot], c), gmap_s[b * BLK_CH + c], slot).start())

    @pl.when(b == nb - 1)
    def _():
        zchunk = _chunk_rows(zbuf, 0)

        def fill_expert(e, carry):
            def one(c, carry2):
                copy_out(zchunk, pads_s[e] + c, fill_sem).start()
                return carry2
            lax.fori_loop(0, padn_s[e], one, 0)
            return carry
        lax.fori_loop(0, N_EXPERTS, fill_expert, 0)

        def fill_tile(t, carry):
            zero_tile(t).start()
            return carry
        lax.fori_loop(nt_s[0], nt_max, fill_tile, 0)

        @pl.when(b >= 1)
        def _():
            wait_block(b - 1, 1 - slot)
        wait_block(b, slot)

        def drain_expert(e, carry):
            def one(c, carry2):
                copy_out(zchunk, 0, fill_sem).wait()
                return carry2
            lax.fori_loop(0, padn_s[e], one, 0)
            return carry
        lax.fori_loop(0, N_EXPERTS, drain_expert, 0)

        def drain_tile(t, carry):
            zero_tile(0).wait()
            return carry
        lax.fori_loop(nt_s[0], nt_max, drain_tile, 0)


def _dispatch(plan, h2, route, loffv, n_rows):
    nb = h2.shape[0] // TBK
    tri = jnp.asarray(np.triu(np.ones((TBK, TBK), np.float32), 1), BF16)
    grid_spec = pltpu.PrefetchScalarGridSpec(
        num_scalar_prefetch=5,
        grid=(nb,),
        in_specs=[
            pl.BlockSpec((TBK, D_MODEL), lambda b, *_: (b, 0)),
            pl.BlockSpec((TBK, LANES), lambda b, *_: (b, 0)),
            pl.BlockSpec((TBK, TBK), lambda b, *_: (0, 0)),
            pl.BlockSpec((None, N_EXPERTS, 1), lambda b, *_: (b, 0, 0)),
        ],
        out_specs=pl.BlockSpec(memory_space=pl.ANY),
        scratch_shapes=[pltpu.VMEM((2, RB, D_MODEL), F32), pltpu.VMEM((EXP_TM, D_MODEL), F32),
                        pltpu.SemaphoreType.DMA((4,))],
    )
    return pl.pallas_call(
        _dispatch_body,
        grid_spec=grid_spec,
        out_shape=jax.ShapeDtypeStruct((n_rows, D_MODEL), F32),
        compiler_params=_params("arbitrary"),
        name="moe_dispatch",
    )(plan["gmap"].reshape(-1), plan["totc"], plan["pad_start"], plan["pad_cnt"], plan["ntiles"],
      h2, route, tri, loffv)


def _experts_body(t0_s, n_s, nxt_s, par_s, first_s, nt_s, xs_hbm, wgu_hbm, bgu_ref, wd_hbm, bd_ref, ys_hbm,
                  wg_f, wd_f, wg_b, wd_b, xbuf, ybuf, wsem, xsem, ysem):
    e = pl.program_id(0)
    nt = nt_s[0]
    nt_max = xs_hbm.shape[0] // EXP_TM
    tile_rows = lambda t: pl.ds(pl.multiple_of(t * EXP_TM, EXP_TM), EXP_TM)

    def weight_copies(ex, s):
        return (pltpu.make_async_copy(wgu_hbm.at[ex], wg_f.at[s], wsem.at[s, 0]),
                pltpu.make_async_copy(wd_hbm.at[ex], wd_f.at[s], wsem.at[s, 1]))

    def x_copy(t, s):
        return pltpu.make_async_copy(xs_hbm.at[tile_rows(t)], xbuf.at[s], xsem.at[s])

    def y_copy(t, s):
        return pltpu.make_async_copy(ybuf.at[s], ys_hbm.at[tile_rows(t)], ysem.at[s])

    @pl.when(e == 0)
    def _():
        for cp in weight_copies(first_s[0], 0):
            cp.start()
        x_copy(0, 0).start()

    @pl.when(n_s[e] > 0)
    def _():
        s_w = par_s[e]
        for cp in weight_copies(e, s_w):
            cp.wait()
        wg_b[...] = wg_f[s_w].astype(BF16)
        wd_b[...] = wd_f[s_w].astype(BF16)

        @pl.when(nxt_s[e] >= 0)
        def _():
            for cp in weight_copies(nxt_s[e], 1 - s_w):
                cp.start()

        def tile(i, carry):
            t = t0_s[e] + i
            s = t & 1
            x_copy(t, s).wait()

            @pl.when(t + 1 < nt)
            def _():
                x_copy(t + 1, 1 - s).start()

            @pl.when(t >= 2)
            def _():
                y_copy(t - 2, s).wait()

            gu = jnp.dot(xbuf[s].astype(BF16), wg_b[...], preferred_element_type=F32) + bgu_ref[...]
            gate = jnp.minimum(gu[:, :D_FF], SWIGLU_LIMIT)
            up = jnp.clip(gu[:, D_FF:], -SWIGLU_LIMIT, SWIGLU_LIMIT)
            act = (up + 1.0) * gate * _sigmoid(SWIGLU_ALPHA * gate)
            ybuf[s] = jnp.dot(act.astype(BF16), wd_b[...], preferred_element_type=F32) + bd_ref[...]
            y_copy(t, s).start()
            return carry
        lax.fori_loop(0, n_s[e], tile, 0)

    @pl.when(e == pl.num_programs(0) - 1)
    def _():
        @pl.when(nt >= 2)
        def _():
            y_copy(nt - 2, nt & 1).wait()
        y_copy(nt - 1, (nt - 1) & 1).wait()
        ybuf[0] = jnp.zeros((EXP_TM, D_MODEL), F32)

        def fill(t, carry):
            y_copy(t, 0).start()
            return carry
        lax.fori_loop(nt, nt_max, fill, 0)

        def drain(t, carry):
            y_copy(0, 0).wait()
            return carry
        lax.fori_loop(nt, nt_max, drain, 0)


def _experts(plan, xs, w_gu, b_gu, w_down, b_down):
    n_rows = xs.shape[0]
    of_expert = lambda e, *_: (e, 0, 0)
    grid_spec = pltpu.PrefetchScalarGridSpec(
        num_scalar_prefetch=6,
        grid=(N_EXPERTS,),
        in_specs=[
            pl.BlockSpec(memory_space=pl.ANY),
            pl.BlockSpec(memory_space=pl.ANY),
            pl.BlockSpec((None, 1, 2 * D_FF), of_expert),
            pl.BlockSpec(memory_space=pl.ANY),
            pl.BlockSpec((None, 1, D_MODEL), of_expert),
        ],
        out_specs=pl.BlockSpec(memory_space=pl.ANY),
        scratch_shapes=[pltpu.VMEM((2, D_MODEL, 2 * D_FF), F32), pltpu.VMEM((2, D_FF, D_MODEL), F32),
                        pltpu.VMEM((D_MODEL, 2 * D_FF), BF16), pltpu.VMEM((D_FF, D_MODEL), BF16),
                        pltpu.VMEM((2, EXP_TM, D_MODEL), F32), pltpu.VMEM((2, EXP_TM, D_MODEL), F32),
                        pltpu.SemaphoreType.DMA((2, 2)), pltpu.SemaphoreType.DMA((2,)),
                        pltpu.SemaphoreType.DMA((2,))],
    )
    return pl.pallas_call(
        _experts_body,
        grid_spec=grid_spec,
        out_shape=jax.ShapeDtypeStruct((n_rows, D_MODEL), F32),
        compiler_params=_params("arbitrary"),
        name="moe_experts",
    )(plan["tile_start"], plan["tile_count"], plan["next_expert"], plan["weight_slot"], plan["first_expert"],
      plan["ntiles"], xs, w_gu, b_gu, w_down, b_down)


def _combine_body(gmap_s, totc_s, route_ref, tril_ref, loffrow_ref, x1_ref, gfin_ref, ys_hbm, y_ref, buf, sem):
    b = pl.program_id(0)
    nb = pl.num_programs(0)
    slot = lax.rem(b, 2)

    def copy_in(slot_, lchunk, gchunk):
        return pltpu.make_async_copy(_chunk_rows(ys_hbm, gchunk), _chunk_rows(buf.at[slot_], lchunk), sem.at[slot_])

    def fetch_block(bb, slot_):
        _for_chunks(totc_s[bb], lambda c: copy_in(slot_, c, gmap_s[bb * BLK_CH + c]).start())

    def wait_block(bb, slot_):
        _wait_chunks(totc_s[bb], lambda rows: pltpu.make_async_copy(
            ys_hbm.at[pl.ds(0, rows)], buf.at[slot_, pl.ds(0, rows)], sem.at[slot_]))

    @pl.when(b == 0)
    def _():
        buf[...] = jnp.zeros_like(buf)
        fetch_block(0, 0)

    @pl.when(b + 1 < nb)
    def _():
        fetch_block(b + 1, 1 - slot)

    wait_block(b, slot)

    route = route_ref[...]
    lane = lax.broadcasted_iota(jnp.int32, (1, LANES), 1).astype(F32)
    sel = jnp.where(lane < float(N_EXPERTS), route, 0.0).astype(BF16)
    rank = jnp.dot(tril_ref[...], sel, preferred_element_type=F32)
    lpos = loffrow_ref[...] * float(ROW_CH) + rank
    lposk, pk = [], []
    for k in range(TOP_K):
        idx = route[:, ROUTE_IDX + k:ROUTE_IDX + k + 1]
        lposk.append(jnp.sum(jnp.where(lane == idx, lpos, 0.0), axis=-1, keepdims=True))
        pk.append(route[:, ROUTE_P + k:ROUTE_P + k + 1])
    acc = x1_ref[...]
    for r0 in range(0, RB, PERM_TM):
        col = lax.broadcasted_iota(jnp.int32, (TBK, PERM_TM), 1).astype(F32) + float(r0)
        w = functools.reduce(jnp.add, [jnp.where(lposk[k] == col, pk[k], 0.0) for k in range(TOP_K)])
        acc = acc + jnp.dot(w.astype(BF16), buf[slot, r0:r0 + PERM_TM, :].astype(BF16), preferred_element_type=F32)
    y_ref[...] = _rms(acc, gfin_ref[...])


def _combine(plan, blocks, route, loffrow, x1, g_final, ys):
    b0, b1 = blocks
    nb = b1 - b0
    tril = jnp.asarray(np.tril(np.ones((TBK, TBK), np.float32), -1), BF16)
    grid_spec = pltpu.PrefetchScalarGridSpec(
        num_scalar_prefetch=2,
        grid=(nb,),
        in_specs=[
            pl.BlockSpec((TBK, LANES), lambda b, *_: (b + b0, 0)),
            pl.BlockSpec((TBK, TBK), lambda b, *_: (0, 0)),
            pl.BlockSpec((None, 1, LANES), lambda b, *_: (b + b0, 0, 0)),
            pl.BlockSpec((TBK, D_MODEL), lambda b, *_: (b + b0, 0)),
            pl.BlockSpec((1, D_MODEL), lambda b, *_: (0, 0)),
            pl.BlockSpec(memory_space=pl.ANY),
        ],
        out_specs=pl.BlockSpec((TBK, D_MODEL), lambda b, *_: (b, 0)),
        scratch_shapes=[pltpu.VMEM((2, RB, D_MODEL), F32), pltpu.SemaphoreType.DMA((2,))],
    )
    return pl.pallas_call(
        _combine_body,
        grid_spec=grid_spec,
        out_shape=jax.ShapeDtypeStruct((nb * TBK, D_MODEL), F32),
        compiler_params=_params("arbitrary"),
        name="moe_combine",
    )(plan["gmap"][b0:b1].reshape(-1), plan["totc"][b0:b1], route, tril, loffrow, x1, g_final, ys)


def kernel(x_prompt, x_sample, cache_k, cache_v, state_s, g_mix, w_in, rel_bias, lb_logits, g_out_norm,
           w_pa, w_pb, w_out, g_ffn, w_router, b_router, w_gu, b_gu, w_down, b_down, g_final):
    B, T = x_prompt.shape[:2]
    DB, S = x_sample.shape[:2]
    depth = w_in.shape[0]
    assert depth == 1 and T % ATT_QBLK == 0 and S == CHUNK
    cw = cache_k.shape[2]
    assert cw == WINDOW
    l = 0

    lower = jnp.cumsum(jax.nn.softmax(lb_logits.astype(F32), axis=0), axis=0)[l].reshape(1, HG_WIDTH)
    w_in_b = w_in[l].astype(BF16)
    wpa, wpb, wout = w_pa[l].astype(BF16), w_pb[l].astype(BF16), w_out[l].astype(BF16)
    row = lambda a: a.reshape(1, -1).astype(F32)
    base = _rel_bias_base(rel_bias[l])
    b_gu3 = b_gu[l].reshape(N_EXPERTS, 1, 2 * D_FF)
    b_down3 = b_down[l].reshape(N_EXPERTS, 1, D_MODEL)
    pad_e = LANES - N_EXPERTS
    wr = jnp.pad(w_router[l].astype(F32), ((0, 0), (0, pad_e)))
    br = jnp.concatenate([b_router[l].astype(F32), jnp.full((pad_e,), NEG, F32)]).reshape(1, LANES)

    n_tok = B * T + DB * S
    nb, nbp = n_tok // TBK, (B * T) // TBK

    def front(x, batch, seq, s0, attend):
        xf = x.reshape(batch * seq, D_MODEL)
        za, zb, zg = _inproj(xf, row(g_mix[l]), w_in_b)
        att = attend(za)
        hg, s_fin = _hgrn(zb, s0, lower, row(g_out_norm[l]), batch, seq)
        za3 = za.reshape(batch, seq, 3 * ATT_WIDTH)
        heads = lambda a: a.reshape(1, batch, a.shape[1], ATT_HEADS, ATT_DIM)
        keep = min(WINDOW, seq)
        nk = heads(za3[:, seq - keep:, ATT_WIDTH:2 * ATT_WIDTH])
        nv = heads(za3[:, seq - keep:, 2 * ATT_WIDTH:])
        return dict(mix=(att, hg, zg, xf), nk=nk, nv=nv, s=s_fin[None])

    ck = cache_k[l].reshape(DB, cw, ATT_WIDTH)
    cv = cache_v[l].reshape(DB, cw, ATT_WIDTH)
    fp = front(x_prompt, B, T, jnp.zeros((B, HG_HEADS, HG_DK, HG_DK), F32), lambda za: _attn_prompt(za, base, B, T))
    fs = front(x_sample, DB, S, state_s[l].astype(F32), lambda za: _attn_sample(za, ck, cv, base, DB, S))

    x1, h2, route, cnt = _merge(fp["mix"], fs["mix"], wpa, wpb, wout, row(g_ffn[l]), wr, br)
    cnt = cnt[:, 0, :N_EXPERTS].astype(jnp.int32)
    max_rows = n_tok * TOP_K + nb * N_EXPERTS * (ROW_CH - 1) + N_EXPERTS * (EXP_TM - 1)
    nt_max = -(-max_rows // EXP_TM)
    plan = _route_plan(cnt)
    loff_f = plan["loff"].astype(F32)
    xs = _dispatch(plan, h2, route, loff_f[:, :, None], nt_max * EXP_TM)
    ysort = _experts(plan, xs, w_gu[l], b_gu3, w_down[l], b_down3)
    loffrow = jnp.pad(loff_f, ((0, 0), (0, pad_e)))[:, None, :]
    yp = _combine(plan, (0, nbp), route, loffrow, x1, row(g_final), ysort)
    ys = _combine(plan, (nbp, nb), route, loffrow, x1, row(g_final), ysort)
    return (yp.reshape(B, T, D_MODEL), ys.reshape(DB, S, D_MODEL), fp["nk"], fp["nv"], fp["s"],
            fs["nk"], fs["nv"], fs["s"])
```

```python
import functools

import numpy as np
import jax
import jax.numpy as jnp
from jax import lax
from jax.experimental import pallas as pl
from jax.experimental.pallas import tpu as pltpu

F32 = jnp.float32
BF16 = jnp.bfloat16

D_MODEL = 1024
CHUNK = 64
LEFT_CHUNKS = 8
WINDOW = LEFT_CHUNKS * CHUNK
ATT_HEADS = 8
ATT_DIM = 64
ATT_WIDTH = ATT_HEADS * ATT_DIM
MAX_REL = 256
HG_HEADS = 4
HG_DK = 128
HG_WIDTH = HG_HEADS * HG_DK
N_EXPERTS = 32
TOP_K = 4
D_FF = D_MODEL
SWIGLU_LIMIT = 7.0
SWIGLU_ALPHA = 1.702
RMS_EPS = 1e-5

LANES = 128
NEG = -1e30
LOG2E = 1.4426950408889634
ATT_QBLK = 4 * CHUNK
ATT_KBLKS = LEFT_CHUNKS * CHUNK // ATT_QBLK + 1
ATT_PAIRS_PER_STAGE = 2
HG_C = 128
VMEM_LIMIT = 56 * 1024 * 1024
BIAS_W = 1024
SUBLANES = 8
TBK = 256
ROW_CH = SUBLANES
RB = TBK * TOP_K + N_EXPERTS * ROW_CH
MERGE_TM = 2 * TBK
PERM_TM = 256
EXP_TM = 512
CH_PER_TILE = EXP_TM // ROW_CH
BLK_CH = RB // ROW_CH
ROUTE_IDX = 64
ROUTE_P = 72

NT = (((1,), (1,)), ((), ()))
TN = (((0,), (0,)), ((), ()))


def _rms(x, g):
    return x * lax.rsqrt(jnp.mean(x * x, axis=-1, keepdims=True) + RMS_EPS) * g


def _sigmoid(x):
    return 1.0 / (1.0 + jnp.exp(-x))


def _params(*sem):
    return pltpu.CompilerParams(dimension_semantics=sem, vmem_limit_bytes=VMEM_LIMIT)


def _inproj_body(x_ref, g_ref, w_ref, za_ref, zb_ref, zg_ref):
    h = _rms(x_ref[...], g_ref[...]).astype(BF16)
    a, b = 3 * ATT_WIDTH, 3 * ATT_WIDTH + 4 * HG_WIDTH
    za_ref[...] = jnp.dot(h, w_ref[:, :a], preferred_element_type=F32)
    zb_ref[...] = jnp.dot(h, w_ref[:, a:b], preferred_element_type=F32)
    zg_ref[...] = jnp.dot(h, w_ref[:, b:], preferred_element_type=F32)


def _inproj(x, g, w_bf16, tm=512):
    n = x.shape[0]
    cols = w_bf16.shape[1]
    wa, wb, wg = 3 * ATT_WIDTH, 4 * HG_WIDTH, 2 * D_MODEL
    return pl.pallas_call(
        _inproj_body,
        grid=(n // tm,),
        in_specs=[
            pl.BlockSpec((tm, D_MODEL), lambda i: (i, 0)),
            pl.BlockSpec((1, D_MODEL), lambda i: (0, 0)),
            pl.BlockSpec((D_MODEL, cols), lambda i: (0, 0)),
        ],
        out_specs=[
            pl.BlockSpec((tm, wa), lambda i: (i, 0)),
            pl.BlockSpec((tm, wb), lambda i: (i, 0)),
            pl.BlockSpec((tm, wg), lambda i: (i, 0)),
        ],
        out_shape=[
            jax.ShapeDtypeStruct((n, wa), F32),
            jax.ShapeDtypeStruct((n, wb), F32),
            jax.ShapeDtypeStruct((n, wg), F32),
        ],
        compiler_params=_params("arbitrary"),
        name="inproj",
    )(x, g, w_bf16)


def _attn_heads(q_ref, k_refs, v_refs, bias_fn, pens, o_ref):
    lane = lax.broadcasted_iota(jnp.int32, (1, LANES), 1)
    first = lane < ATT_DIM
    halves = (first, lane >= ATT_DIM)
    for hp0 in range(0, ATT_HEADS // 2, ATT_PAIRS_PER_STAGE):
        pairs = range(hp0, hp0 + ATT_PAIRS_PER_STAGE)
        sl = {hp: slice(hp * LANES, (hp + 1) * LANES) for hp in pairs}
        scores = {}
        for hp in pairs:
            q2 = q_ref[:, sl[hp]] * (ATT_DIM ** -0.5 * LOG2E)
            ks = [k[:, sl[hp]].astype(BF16) for k in k_refs]
            for half, mine in enumerate(halves):
                qm = jnp.where(mine, q2, 0.0).astype(BF16)
                ss = []
                for j, kj in enumerate(ks):
                    s = lax.dot_general(qm, kj, NT, preferred_element_type=F32) + bias_fn(2 * hp + half, j)
                    if pens[j] is not None:
                        s = s + pens[j]
                    ss.append(s)
                scores[hp, half] = ss
        for hp in pairs:
            outs = []
            for half, mine in enumerate(halves):
                ss = scores[hp, half]
                vs = [jnp.where(mine, v[:, sl[hp]], 1.0).astype(BF16) for v in v_refs]
                if all(s.shape == ss[0].shape for s in ss):
                    m = jnp.max(functools.reduce(jnp.maximum, ss), axis=-1, keepdims=True)
                else:
                    m = functools.reduce(jnp.maximum, [jnp.max(s, axis=-1, keepdims=True) for s in ss])
                outs.append(functools.reduce(jnp.add, [
                    jnp.dot(jnp.exp2(s - m).astype(BF16), vj, preferred_element_type=F32) for s, vj in zip(ss, vs)]))
            num = jnp.where(first, outs[0], outs[1])
            den = pltpu.roll(jnp.where(first, outs[1], outs[0]), ATT_DIM, 1)
            o_ref[:, sl[hp]] = num * (1.0 / den)


def _fill_bias(base_ref, bias_ref, banded):
    nq, nk = bias_ref.shape[1:]
    if banded:
        r = lax.broadcasted_iota(jnp.int32, (nq, nk), 0)
        s = lax.broadcasted_iota(jnp.int32, (nq, nk), 1)
        qc = (r + WINDOW) // CHUNK
        kc = s // CHUNK
        pen = jnp.where(kc <= qc, jnp.where(kc >= qc - LEFT_CHUNKS, 0.0, NEG), NEG)
    for h in range(ATT_HEADS):
        rows = jnp.broadcast_to(base_ref[h:h + 1, :], (nq, BIAS_W))
        t = pltpu.roll(rows, 0, 1, stride=1, stride_axis=0)[:, :nk] * LOG2E
        bias_ref[h] = t + pen if banded else t


def _attn_prompt_body(q_ref, k0, k1, k2, v0, v1, v2, base_ref, o_ref, bias_ref):
    i = pl.program_id(1)

    @pl.when((pl.program_id(0) == 0) & (i == 0))
    def _():
        _fill_bias(base_ref, bias_ref, True)

    bias_fn = lambda h, j: bias_ref[h, :, j * ATT_QBLK:(j + 1) * ATT_QBLK]
    back = ATT_KBLKS - 1

    @pl.when(i < back)
    def _():
        pens = [jnp.where(i - back + j >= 0, 0.0, NEG) for j in range(back)] + [None]
        _attn_heads(q_ref, [k0, k1, k2], [v0, v1, v2], bias_fn, pens, o_ref)

    @pl.when(i >= back)
    def _():
        _attn_heads(q_ref, [k0, k1, k2], [v0, v1, v2], bias_fn, [None] * ATT_KBLKS, o_ref)


def _attn_prompt(za, base, batch, seq):
    nq = seq // ATT_QBLK
    back = ATT_KBLKS - 1
    qspec = pl.BlockSpec((ATT_QBLK, ATT_WIDTH), lambda b, i: (b * nq + i, 0))

    def kvspec(j, col):
        return pl.BlockSpec((ATT_QBLK, ATT_WIDTH),
                            lambda b, i: (b * nq + jnp.maximum(i - back + j, 0), col))

    return pl.pallas_call(
        _attn_prompt_body,
        grid=(batch, nq),
        in_specs=[qspec] + [kvspec(j, 1) for j in range(ATT_KBLKS)] + [kvspec(j, 2) for j in range(ATT_KBLKS)]
        + [pl.BlockSpec(base.shape, lambda b, i: (0, 0))],
        out_specs=pl.BlockSpec((ATT_QBLK, ATT_WIDTH), lambda b, i: (b * nq + i, 0)),
        out_shape=jax.ShapeDtypeStruct((batch * seq, ATT_WIDTH), F32),
        scratch_shapes=[pltpu.VMEM((ATT_HEADS, ATT_QBLK, ATT_KBLKS * ATT_QBLK), F32)],
        compiler_params=_params("arbitrary", "arbitrary"),
        name="attn_prompt",
    )(za, za, za, za, za, za, za, base)


def _attn_sample_body(q_ref, kn_ref, vn_ref, ck_ref, cv_ref, base_ref, o_ref, bias_ref):
    @pl.when(pl.program_id(0) == 0)
    def _():
        _fill_bias(base_ref, bias_ref, False)

    cw = ck_ref.shape[0]
    bias_fn = lambda h, j: bias_ref[h, :, :cw] if j == 0 else bias_ref[h, :, cw:]
    _attn_heads(q_ref, [ck_ref, kn_ref], [cv_ref, vn_ref], bias_fn, [None, None], o_ref)


def _attn_sample(za, ck, cv, base, batch, seq):
    cw = ck.shape[1]
    return pl.pallas_call(
        _attn_sample_body,
        grid=(batch,),
        in_specs=[
            pl.BlockSpec((seq, ATT_WIDTH), lambda b: (b, 0)),
            pl.BlockSpec((seq, ATT_WIDTH), lambda b: (b, 1)),
            pl.BlockSpec((seq, ATT_WIDTH), lambda b: (b, 2)),
            pl.BlockSpec((None, cw, ATT_WIDTH), lambda b: (b, 0, 0)),
            pl.BlockSpec((None, cw, ATT_WIDTH), lambda b: (b, 0, 0)),
            pl.BlockSpec(base.shape, lambda b: (0, 0)),
        ],
        out_specs=pl.BlockSpec((seq, ATT_WIDTH), lambda b: (b, 0)),
        out_shape=jax.ShapeDtypeStruct((batch * seq, ATT_WIDTH), F32),
        scratch_shapes=[pltpu.VMEM((ATT_HEADS, seq, cw + seq), F32)],
        compiler_params=_params("arbitrary"),
        name="attn_sample",
    )(za, za, za, ck, cv, base)


def _rel_bias_base(table):
    top = table[:, 2 * MAX_REL:].astype(F32)
    rev = table[:, ::-1][:, :2 * MAX_REL].astype(F32)
    left = WINDOW - MAX_REL
    return jnp.concatenate([jnp.broadcast_to(top, (ATT_HEADS, left)), rev,
                            jnp.broadcast_to(top, (ATT_HEADS, BIAS_W - left - 2 * MAX_REL))], axis=1)


def _hgrn_consts(c):
    t = np.arange(c)[:, None]
    j = np.arange(c)[None, :]
    mats = [j <= t, j > t]
    masks = []
    m = c // 2
    while m >= 1:
        ref = (t // (2 * m)) * (2 * m) + m - 1
        second = (t % (2 * m)) >= m
        if m < SUBLANES:
            mats.append((second & (j > ref) & (j <= t)) | (~second & (j > t) & (j <= ref)))
        masks.append((t // (2 * m)) == (j // (2 * m)))
        m //= 2
    return (jnp.asarray(np.concatenate(mats, 0).astype(np.float32), BF16),
            jnp.asarray(np.stack(masks).astype(np.float32)))


def _hgrn_body(zb_ref, s0_ref, lower_ref, gon_ref, p_ref, mask_ref, o_ref, sfin_ref, st_ref, *, single_step):
    c = zb_ref.shape[0]
    step = pl.program_id(1)

    def load_state():
        for h in range(HG_HEADS):
            st_ref[h] = s0_ref[0, h].T

    if single_step:
        load_state()
    else:
        pl.when(step == 0)(load_state)

    pmat = p_ref[...]
    n_levels = mask_ref.shape[0]
    part = lambda i: zb_ref[:, i * HG_WIDTH:(i + 1) * HG_WIDTH]
    head = lambda a, h: a[:, h * HG_DK:(h + 1) * HG_DK]
    q = part(0)
    low = lower_ref[...]
    f = low + (1.0 - low) * _sigmoid(part(1))
    lf = jnp.log(f)
    k = 1.0 - f
    ib = part(2)
    v = ib * _sigmoid(ib)
    og = part(3)

    hi = lf.astype(BF16)
    r1 = lf - hi.astype(F32)
    mid = r1.astype(BF16)
    lo = (r1 - mid.astype(F32)).astype(BF16)
    e = (jnp.dot(pmat, hi, preferred_element_type=F32) + jnp.dot(pmat, mid, preferred_element_type=F32)
         + jnp.dot(pmat, lo, preferred_element_type=F32))
    b = e[0:c]
    decay = jnp.exp(e[c - 1:c])
    qe = (q * jnp.exp(b)).astype(BF16)
    kt = (k * jnp.exp(e[c:2 * c])).astype(BF16)
    vb = v.astype(BF16)
    qk = q * k
    gate = og * _sigmoid(og)

    row = lax.broadcasted_iota(jnp.int32, (c, HG_WIDTH), 0)
    att = [None] * HG_HEADS
    n_rows_p = 2
    for lvl in range(n_levels):
        m = c >> (lvl + 1)
        if m >= SUBLANES:
            ref = [jnp.broadcast_to(b[p * 2 * m + m - 1:p * 2 * m + m], (2 * m, HG_WIDTH)) for p in range(c // (2 * m))]
            x = jnp.exp(-jnp.abs(b - (jnp.concatenate(ref, axis=0) if len(ref) > 1 else ref[0])))
        else:
            x = jnp.exp(e[n_rows_p * c:(n_rows_p + 1) * c])
            n_rows_p += 1
        second = (row & m) != 0
        qm = jnp.where(second, q * x, 0.0).astype(BF16)
        km = jnp.where(second, 0.0, k * x).astype(BF16)
        for h in range(HG_HEADS):
            a = lax.dot_general(head(qm, h), head(km, h), NT, preferred_element_type=F32)
            if lvl > 0:
                a = a * mask_ref[lvl]
            att[h] = a if att[h] is None else att[h] + a

    for h in range(HG_HEADS):
        st = st_ref[h]
        inter = lax.dot_general(head(qe, h), st.astype(BF16), NT, preferred_element_type=F32)
        intra = jnp.dot(att[h].astype(BF16), head(vb, h), preferred_element_type=F32)
        intra = intra + jnp.sum(head(qk, h), axis=-1, keepdims=True) * head(v, h)
        st_ref[h] = st * head(decay, h) + lax.dot_general(head(vb, h), head(kt, h), TN, preferred_element_type=F32)
        o_ref[:, h * HG_DK:(h + 1) * HG_DK] = _rms(inter + intra, gon_ref[...]) * head(gate, h)

    def write_state():
        for h in range(HG_HEADS):
            sfin_ref[0, h] = st_ref[h].T

    if single_step:
        write_state()
    else:
        pl.when(step == pl.num_programs(1) - 1)(write_state)


def _hgrn(zb, s0, lower, g_on, batch, seq):
    c = min(HG_C, seq)
    assert seq % c == 0
    pmat, masks = _hgrn_consts(c)
    nc = seq // c
    return pl.pallas_call(
        functools.partial(_hgrn_body, single_step=nc == 1),
        grid=(batch, nc),
        in_specs=[
            pl.BlockSpec((c, 4 * HG_WIDTH), lambda b, i: (b * nc + i, 0)),
            pl.BlockSpec((1, HG_HEADS, HG_DK, HG_DK), lambda b, i: (b, 0, 0, 0)),
            pl.BlockSpec((1, HG_WIDTH), lambda b, i: (0, 0)),
            pl.BlockSpec((1, HG_DK), lambda b, i: (0, 0)),
            pl.BlockSpec(pmat.shape, lambda b, i: (0, 0)),
            pl.BlockSpec(masks.shape, lambda b, i: (0, 0, 0)),
        ],
        out_specs=[
            pl.BlockSpec((c, HG_WIDTH), lambda b, i: (b * nc + i, 0)),
            pl.BlockSpec((1, HG_HEADS, HG_DK, HG_DK), lambda b, i: (b, 0, 0, 0)),
        ],
        out_shape=[
            jax.ShapeDtypeStruct((batch * seq, HG_WIDTH), F32),
            jax.ShapeDtypeStruct((batch, HG_HEADS, HG_DK, HG_DK), F32),
        ],
        scratch_shapes=[pltpu.VMEM((HG_HEADS, HG_DK, HG_DK), F32)],
        compiler_params=_params("arbitrary", "arbitrary"),
        name="hgrn2",
    )(zb, s0, lower, g_on, pmat, masks)


def _split_bf16(x):
    hi = x.astype(BF16)
    return hi, (x - hi.astype(F32)).astype(BF16)


def _merge_body(att_p, hg_p, zg_p, x_p, att_s, hg_s, zg_s, x_s, wpa_ref, wpb_ref, wout_ref, gffn_ref, wr_ref, br_ref,
                x1_ref, h2_ref, route_ref, cnt_ref, *, n_first):
    weights = (wpa_ref, wpb_ref, wout_ref, gffn_ref, wr_ref, br_ref)
    outs = (x1_ref, h2_ref, route_ref, cnt_ref)
    i = pl.program_id(0)
    pl.when(i < n_first)(functools.partial(_merge_block, att_p, hg_p, zg_p, x_p, *weights, *outs))
    pl.when(i >= n_first)(functools.partial(_merge_block, att_s, hg_s, zg_s, x_s, *weights, *outs))


def _merge_block(att_ref, hg_ref, zg_ref, x_ref, wpa_ref, wpb_ref, wout_ref, gffn_ref, wr_ref, br_ref,
                 x1_ref, h2_ref, route_ref, cnt_ref):
    pa = jnp.dot(att_ref[...].astype(BF16), wpa_ref[...], preferred_element_type=F32)
    pb = jnp.dot(hg_ref[...].astype(BF16), wpb_ref[...], preferred_element_type=F32)
    y = _sigmoid(zg_ref[:, :D_MODEL]) * pa + _sigmoid(zg_ref[:, D_MODEL:]) * pb
    x1 = x_ref[...] + jnp.dot(y.astype(BF16), wout_ref[...], preferred_element_type=F32)
    x1_ref[...] = x1
    h2 = _rms(x1, gffn_ref[...])
    h2_ref[...] = h2.astype(BF16)

    h_hi, h_lo = _split_bf16(h2)
    w_hi, w_lo = _split_bf16(wr_ref[...])
    logits = (jnp.dot(h_hi, w_hi, preferred_element_type=F32) + jnp.dot(h_lo, w_hi, preferred_element_type=F32)
              + jnp.dot(h_hi, w_lo, preferred_element_type=F32)) + br_ref[...]
    lane = lax.broadcasted_iota(jnp.int32, logits.shape, 1).astype(F32)
    cur = logits
    vals, idxs = [], []
    for _ in range(TOP_K):
        m = jnp.max(cur, axis=-1, keepdims=True)
        idx = jnp.min(jnp.where(cur == m, lane, float(LANES)), axis=-1, keepdims=True)
        vals.append(m)
        idxs.append(idx)
        cur = jnp.where(lane == idx, -jnp.inf, cur)
    es = [jnp.exp(v - vals[0]) for v in vals]
    inv = 1.0 / functools.reduce(jnp.add, es)
    route = jnp.zeros_like(logits)
    for k, (ex, idx) in enumerate(zip(es, idxs)):
        route = (route + jnp.where(lane == idx, 1.0, 0.0) + jnp.where(lane == float(ROUTE_IDX + k), idx, 0.0)
                 + jnp.where(lane == float(ROUTE_P + k), ex * inv, 0.0))
    route_ref[...] = route
    sel = jnp.where(lane < float(N_EXPERTS), route, 0.0)
    for blk in range(cnt_ref.shape[0]):
        cnt_ref[blk] = jnp.sum(sel[blk * TBK:(blk + 1) * TBK], axis=0, keepdims=True)


def _merge(first, second, wpa, wpb, wout, g_ffn, w_router, b_router):
    tm = MERGE_TM
    assert first[3].shape[0] % tm == 0 and second[3].shape[0] % tm == 0
    n1, n2 = first[3].shape[0] // tm, second[3].shape[0] // tm
    widths = (ATT_WIDTH, HG_WIDTH, 2 * D_MODEL, D_MODEL)
    spec1 = [pl.BlockSpec((tm, w), lambda i: (jnp.minimum(i, n1 - 1), 0)) for w in widths]
    spec2 = [pl.BlockSpec((tm, w), lambda i: (jnp.maximum(i - n1, 0), 0)) for w in widths]
    row = lambda w: pl.BlockSpec((tm, w), lambda i: (i, 0))
    full = lambda a: pl.BlockSpec(a.shape, lambda i: (0,) * a.ndim)
    n = (n1 + n2) * tm
    return pl.pallas_call(
        functools.partial(_merge_body, n_first=n1),
        grid=(n1 + n2,),
        in_specs=spec1 + spec2 + [full(wpa), full(wpb), full(wout), full(g_ffn), full(w_router), full(b_router)],
        out_specs=[row(D_MODEL), row(D_MODEL), row(LANES),
                   pl.BlockSpec((tm // TBK, 1, LANES), lambda i: (i, 0, 0))],
        out_shape=[
            jax.ShapeDtypeStruct((n, D_MODEL), F32),
            jax.ShapeDtypeStruct((n, D_MODEL), BF16),
            jax.ShapeDtypeStruct((n, LANES), F32),
            jax.ShapeDtypeStruct((n // TBK, 1, LANES), F32),
        ],
        compiler_params=_params("arbitrary"),
        name="merge_router",
    )(*first, *second, wpa, wpb, wout, g_ffn, w_router, b_router)


def _route_plan(cnt):
    pc = (cnt + ROW_CH - 1) // ROW_CH
    loff = jnp.cumsum(pc, axis=1) - pc
    tot = jnp.sum(pc, axis=0)
    reg = (tot + CH_PER_TILE - 1) // CH_PER_TILE * CH_PER_TILE
    gstart = jnp.cumsum(reg) - reg
    goff = gstart[None, :] + jnp.cumsum(pc, axis=0) - pc
    ntiles = jnp.sum(reg) // CH_PER_TILE
    present = reg > 0
    ids = jnp.arange(N_EXPERTS, dtype=jnp.int32)
    later = (ids[None, :] > ids[:, None]) & present[None, :]
    nxt = jnp.min(jnp.where(later, ids[None, :], N_EXPERTS), axis=1)
    nxt = jnp.where(nxt == N_EXPERTS, -1, nxt)
    slot = (jnp.cumsum(present.astype(jnp.int32)) - 1) % 2
    first = jnp.min(jnp.where(present, ids, N_EXPERTS)).reshape(1)
    j = jnp.arange(BLK_CH, dtype=jnp.int32)
    run = jnp.sum(((loff + pc)[:, None, :] <= j[None, :, None]).astype(jnp.int32), axis=2)
    shift = jnp.sum(jnp.where(run[:, :, None] == ids[None, None, :], (goff - loff)[:, None, :], 0), axis=2)
    gmap = shift + j[None, :]
    i32 = lambda a: a.astype(jnp.int32)
    return dict(loff=i32(loff), gmap=i32(gmap), totc=i32(jnp.sum(pc, axis=1)),
                pad_start=i32(gstart + tot), pad_cnt=i32(reg - tot), ntiles=i32(ntiles).reshape(1),
                tile_start=i32(gstart // CH_PER_TILE), tile_count=i32(reg // CH_PER_TILE),
                next_expert=i32(nxt), weight_slot=i32(slot), first_expert=i32(first))


def _chunk_rows(ref, chunk):
    return ref.at[pl.ds(pl.multiple_of(chunk * ROW_CH, ROW_CH), ROW_CH)]


def _for_chunks(n, do):
    log_unroll = 2
    groups = lax.shift_right_logical(n, log_unroll)

    def group(i, carry):
        for u in range(1 << log_unroll):
            do(lax.shift_left(i, log_unroll) + u)
        return carry
    lax.fori_loop(0, groups, group, 0)

    def single(c, carry):
        do(c)
        return carry
    lax.fori_loop(lax.shift_left(groups, log_unroll), n, single, 0)


def _wait_chunks(n, copy_of_rows):
    for bit in range((RB // ROW_CH).bit_length()):
        @pl.when(((n >> bit) & 1) == 1)
        def _(bit=bit):
            copy_of_rows((1 << bit) * ROW_CH).wait()


def _dispatch_body(gmap_s, totc_s, pads_s, padn_s, nt_s, h2_ref, route_ref, tri_ref, loffv_ref,
                   xs_hbm, buf, zbuf, sem):
    b = pl.program_id(0)
    nb = pl.num_programs(0)
    slot = lax.rem(b, 2)
    fill_sem, tile_sem = 2, 3
    nt_max = xs_hbm.shape[0] // EXP_TM

    def copy_out(src, gchunk, sem_i):
        return pltpu.make_async_copy(src, _chunk_rows(xs_hbm, gchunk), sem.at[sem_i])

    def zero_tile(t):
        return pltpu.make_async_copy(zbuf, xs_hbm.at[pl.ds(pl.multiple_of(t * EXP_TM, EXP_TM), EXP_TM)],
                                     sem.at[tile_sem])

    def wait_block(bb, slot_):
        _wait_chunks(totc_s[bb], lambda rows: pltpu.make_async_copy(
            buf.at[slot_, pl.ds(0, rows)], xs_hbm.at[pl.ds(0, rows)], sem.at[slot_]))

    @pl.when(b == 0)
    def _():
        zbuf[...] = jnp.zeros_like(zbuf)

    @pl.when(b >= 2)
    def _():
        wait_block(b - 2, slot)

    rt = route_ref[...].T
    rank_t = jnp.dot(rt[0:N_EXPERTS].astype(BF16), tri_ref[...], preferred_element_type=F32)
    lpos_t = loffv_ref[...] * float(ROW_CH) + rank_t
    erow = lax.broadcasted_iota(jnp.int32, (N_EXPERTS, TBK), 0).astype(F32)
    lposk = [jnp.sum(jnp.where(rt[ROUTE_IDX + k:ROUTE_IDX + k + 1] == erow, lpos_t, 0.0), axis=0, keepdims=True)
             for k in range(TOP_K)]
    h2 = h2_ref[...]
    for r0 in range(0, RB, PERM_TM):
        rrow = lax.broadcasted_iota(jnp.int32, (PERM_TM, TBK), 0).astype(F32) + float(r0)
        perm = functools.reduce(jnp.add, [jnp.where(lposk[k] == rrow, 1.0, 0.0) for k in range(TOP_K)])
        buf[slot, r0:r0 + PERM_TM, :] = jnp.dot(perm.astype(BF16), h2, preferred_element_type=F32)

    _for_chunks(totc_s[b], lambda c: copy_out(_chunk_rows(buf.at[slot], c), gmap_s[b * BLK_CH + c], slot).start())

    @pl.when(b == nb - 1)
    def _():
        zchunk = _chunk_rows(zbuf, 0)

        def fill_expert(e, carry):
            def one(c, carry2):
                copy_out(zchunk, pads_s[e] + c, fill_sem).start()
                return carry2
            lax.fori_loop(0, padn_s[e], one, 0)
            return carry
        lax.fori_loop(0, N_EXPERTS, fill_expert, 0)

        def fill_tile(t, carry):
            zero_tile(t).start()
            return carry
        lax.fori_loop(nt_s[0], nt_max, fill_tile, 0)

        @pl.when(b >= 1)
        def _():
            wait_block(b - 1, 1 - slot)
        wait_block(b, slot)

        def drain_expert(e, carry):
            def one(c, carry2):
                copy_out(zchunk, 0, fill_sem).wait()
                return carry2
            lax.fori_loop(0, padn_s[e], one, 0)
            return carry
        lax.fori_loop(0, N_EXPERTS, drain_expert, 0)

        def drain_tile(t, carry):
            zero_tile(0).wait()
            return carry
        lax.fori_loop(nt_s[0], nt_max, drain_tile, 0)


def _dispatch(plan, h2, route, loffv, n_rows):
    nb = h2.shape[0] // TBK
    tri = jnp.asarray(np.triu(np.ones((TBK, TBK), np.float32), 1), BF16)
    grid_spec = pltpu.PrefetchScalarGridSpec(
        num_scalar_prefetch=5,
        grid=(nb,),
        in_specs=[
            pl.BlockSpec((TBK, D_MODEL), lambda b, *_: (b, 0)),
            pl.BlockSpec((TBK, LANES), lambda b, *_: (b, 0)),
            pl.BlockSpec((TBK, TBK), lambda b, *_: (0, 0)),
            pl.BlockSpec((None, N_EXPERTS, 1), lambda b, *_: (b, 0, 0)),
        ],
        out_specs=pl.BlockSpec(memory_space=pl.ANY),
        scratch_shapes=[pltpu.VMEM((2, RB, D_MODEL), F32), pltpu.VMEM((EXP_TM, D_MODEL), F32),
                        pltpu.SemaphoreType.DMA((4,))],
    )
    return pl.pallas_call(
        _dispatch_body,
        grid_spec=grid_spec,
        out_shape=jax.ShapeDtypeStruct((n_rows, D_MODEL), F32),
        compiler_params=_params("arbitrary"),
        name="moe_dispatch",
    )(plan["gmap"].reshape(-1), plan["totc"], plan["pad_start"], plan["pad_cnt"], plan["ntiles"],
      h2, route, tri, loffv)


def _experts_body(t0_s, n_s, nxt_s, par_s, first_s, nt_s, xs_hbm, wgu_hbm, bgu_ref, wd_hbm, bd_ref, ys_hbm,
                  wg_f, wd_f, wg_b, wd_b, xbuf, ybuf, wsem, xsem, ysem):
    e = pl.program_id(0)
    nt = nt_s[0]
    nt_max = xs_hbm.shape[0] // EXP_TM
    tile_rows = lambda t: pl.ds(pl.multiple_of(t * EXP_TM, EXP_TM), EXP_TM)

    def weight_copies(ex, s):
        return (pltpu.make_async_copy(wgu_hbm.at[ex], wg_f.at[s], wsem.at[s, 0]),
                pltpu.make_async_copy(wd_hbm.at[ex], wd_f.at[s], wsem.at[s, 1]))

    def x_copy(t, s):
        return pltpu.make_async_copy(xs_hbm.at[tile_rows(t)], xbuf.at[s], xsem.at[s])

    def y_copy(t, s):
        return pltpu.make_async_copy(ybuf.at[s], ys_hbm.at[tile_rows(t)], ysem.at[s])

    @pl.when(e == 0)
    def _():
        for cp in weight_copies(first_s[0], 0):
            cp.start()
        x_copy(0, 0).start()

    @pl.when(n_s[e] > 0)
    def _():
        s_w = par_s[e]
        for cp in weight_copies(e, s_w):
            cp.wait()
        wg_b[...] = wg_f[s_w].astype(BF16)
        wd_b[...] = wd_f[s_w].astype(BF16)

        @pl.when(nxt_s[e] >= 0)
        def _():
            for cp in weight_copies(nxt_s[e], 1 - s_w):
                cp.start()

        def tile(i, carry):
            t = t0_s[e] + i
            s = t & 1
            x_copy(t, s).wait()

            @pl.when(t + 1 < nt)
            def _():
                x_copy(t + 1, 1 - s).start()

            @pl.when(t >= 2)
            def _():
                y_copy(t - 2, s).wait()

            gu = jnp.dot(xbuf[s].astype(BF16), wg_b[...], preferred_element_type=F32) + bgu_ref[...]
            gate = jnp.minimum(gu[:, :D_FF], SWIGLU_LIMIT)
            up = jnp.clip(gu[:, D_FF:], -SWIGLU_LIMIT, SWIGLU_LIMIT)
            act = (up + 1.0) * gate * _sigmoid(SWIGLU_ALPHA * gate)
            ybuf[s] = jnp.dot(act.astype(BF16), wd_b[...], preferred_element_type=F32) + bd_ref[...]
            y_copy(t, s).start()
            return carry
        lax.fori_loop(0, n_s[e], tile, 0)

    @pl.when(e == pl.num_programs(0) - 1)
    def _():
        @pl.when(nt >= 2)
        def _():
            y_copy(nt - 2, nt & 1).wait()
        y_copy(nt - 1, (nt - 1) & 1).wait()
        ybuf[0] = jnp.zeros((EXP_TM, D_MODEL), F32)

        def fill(t, carry):
            y_copy(t, 0).start()
            return carry
        lax.fori_loop(nt, nt_max, fill, 0)

        def drain(t, carry):
            y_copy(0, 0).wait()
            return carry
        lax.fori_loop(nt, nt_max, drain, 0)


def _experts(plan, xs, w_gu, b_gu, w_down, b_down):
    n_rows = xs.shape[0]
    of_expert = lambda e, *_: (e, 0, 0)
    grid_spec = pltpu.PrefetchScalarGridSpec(
        num_scalar_prefetch=6,
        grid=(N_EXPERTS,),
        in_specs=[
            pl.BlockSpec(memory_space=pl.ANY),
            pl.BlockSpec(memory_space=pl.ANY),
            pl.BlockSpec((None, 1, 2 * D_FF), of_expert),
            pl.BlockSpec(memory_space=pl.ANY),
            pl.BlockSpec((None, 1, D_MODEL), of_expert),
        ],
        out_specs=pl.BlockSpec(memory_space=pl.ANY),
        scratch_shapes=[pltpu.VMEM((2, D_MODEL, 2 * D_FF), F32), pltpu.VMEM((2, D_FF, D_MODEL), F32),
                        pltpu.VMEM((D_MODEL, 2 * D_FF), BF16), pltpu.VMEM((D_FF, D_MODEL), BF16),
                        pltpu.VMEM((2, EXP_TM, D_MODEL), F32), pltpu.VMEM((2, EXP_TM, D_MODEL), F32),
                        pltpu.SemaphoreType.DMA((2, 2)), pltpu.SemaphoreType.DMA((2,)),
                        pltpu.SemaphoreType.DMA((2,))],
    )
    return pl.pallas_call(
        _experts_body,
        grid_spec=grid_spec,
        out_shape=jax.ShapeDtypeStruct((n_rows, D_MODEL), F32),
        compiler_params=_params("arbitrary"),
        name="moe_experts",
    )(plan["tile_start"], plan["tile_count"], plan["next_expert"], plan["weight_slot"], plan["first_expert"],
      plan["ntiles"], xs, w_gu, b_gu, w_down, b_down)


def _combine_body(gmap_s, totc_s, route_ref, tril_ref, loffrow_ref, x1_ref, gfin_ref, ys_hbm, y_ref, buf, sem):
    b = pl.program_id(0)
    nb = pl.num_programs(0)
    slot = lax.rem(b, 2)

    def copy_in(slot_, lchunk, gchunk):
        return pltpu.make_async_copy(_chunk_rows(ys_hbm, gchunk), _chunk_rows(buf.at[slot_], lchunk), sem.at[slot_])

    def fetch_block(bb, slot_):
        _for_chunks(totc_s[bb], lambda c: copy_in(slot_, c, gmap_s[bb * BLK_CH + c]).start())

    def wait_block(bb, slot_):
        _wait_chunks(totc_s[bb], lambda rows: pltpu.make_async_copy(
            ys_hbm.at[pl.ds(0, rows)], buf.at[slot_, pl.ds(0, rows)], sem.at[slot_]))

    @pl.when(b == 0)
    def _():
        buf[...] = jnp.zeros_like(buf)
        fetch_block(0, 0)

    @pl.when(b + 1 < nb)
    def _():
        fetch_block(b + 1, 1 - slot)

    wait_block(b, slot)

    route = route_ref[...]
    lane = lax.broadcasted_iota(jnp.int32, (1, LANES), 1).astype(F32)
    sel = jnp.where(lane < float(N_EXPERTS), route, 0.0).astype(BF16)
    rank = jnp.dot(tril_ref[...], sel, preferred_element_type=F32)
    lpos = loffrow_ref[...] * float(ROW_CH) + rank
    lposk, pk = [], []
    for k in range(TOP_K):
        idx = route[:, ROUTE_IDX + k:ROUTE_IDX + k + 1]
        lposk.append(jnp.sum(jnp.where(lane == idx, lpos, 0.0), axis=-1, keepdims=True))
        pk.append(route[:, ROUTE_P + k:ROUTE_P + k + 1])
    acc = x1_ref[...]
    for r0 in range(0, RB, PERM_TM):
        col = lax.broadcasted_iota(jnp.int32, (TBK, PERM_TM), 1).astype(F32) + float(r0)
        w = functools.reduce(jnp.add, [jnp.where(lposk[k] == col, pk[k], 0.0) for k in range(TOP_K)])
        acc = acc + jnp.dot(w.astype(BF16), buf[slot, r0:r0 + PERM_TM, :].astype(BF16), preferred_element_type=F32)
    y_ref[...] = _rms(acc, gfin_ref[...])


def _combine(plan, blocks, route, loffrow, x1, g_final, ys):
    b0, b1 = blocks
    nb = b1 - b0
    tril = jnp.asarray(np.tril(np.ones((TBK, TBK), np.float32), -1), BF16)
    grid_spec = pltpu.PrefetchScalarGridSpec(
        num_scalar_prefetch=2,
        grid=(nb,),
        in_specs=[
            pl.BlockSpec((TBK, LANES), lambda b, *_: (b + b0, 0)),
            pl.BlockSpec((TBK, TBK), lambda b, *_: (0, 0)),
            pl.BlockSpec((None, 1, LANES), lambda b, *_: (b + b0, 0, 0)),
            pl.BlockSpec((TBK, D_MODEL), lambda b, *_: (b + b0, 0)),
            pl.BlockSpec((1, D_MODEL), lambda b, *_: (0, 0)),
            pl.BlockSpec(memory_space=pl.ANY),
        ],
        out_specs=pl.BlockSpec((TBK, D_MODEL), lambda b, *_: (b, 0)),
        scratch_shapes=[pltpu.VMEM((2, RB, D_MODEL), F32), pltpu.SemaphoreType.DMA((2,))],
    )
    return pl.pallas_call(
        _combine_body,
        grid_spec=grid_spec,
        out_shape=jax.ShapeDtypeStruct((nb * TBK, D_MODEL), F32),
        compiler_params=_params("arbitrary"),
        name="moe_combine",
    )(plan["gmap"][b0:b1].reshape(-1), plan["totc"][b0:b1], route, tril, loffrow, x1, g_final, ys)


def kernel(x_prompt, x_sample, cache_k, cache_v, state_s, g_mix, w_in, rel_bias, lb_logits, g_out_norm,
           w_pa, w_pb, w_out, g_ffn, w_router, b_router, w_gu, b_gu, w_down, b_down, g_final):
    B, T = x_prompt.shape[:2]
    DB, S = x_sample.shape[:2]
    depth = w_in.shape[0]
    assert depth == 1 and T % ATT_QBLK == 0 and S == CHUNK
    cw = cache_k.shape[2]
    assert cw == WINDOW
    l = 0

    lower = jnp.cumsum(jax.nn.softmax(lb_logits.astype(F32), axis=0), axis=0)[l].reshape(1, HG_WIDTH)
    w_in_b = w_in[l].astype(BF16)
    wpa, wpb, wout = w_pa[l].astype(BF16), w_pb[l].astype(BF16), w_out[l].astype(BF16)
    row = lambda a: a.reshape(1, -1).astype(F32)
    base = _rel_bias_base(rel_bias[l])
    b_gu3 = b_gu[l].reshape(N_EXPERTS, 1, 2 * D_FF)
    b_down3 = b_down[l].reshape(N_EXPERTS, 1, D_MODEL)
    pad_e = LANES - N_EXPERTS
    wr = jnp.pad(w_router[l].astype(F32), ((0, 0), (0, pad_e)))
    br = jnp.concatenate([b_router[l].astype(F32), jnp.full((pad_e,), NEG, F32)]).reshape(1, LANES)

    n_tok = B * T + DB * S
    nb, nbp = n_tok // TBK, (B * T) // TBK

    def front(x, batch, seq, s0, attend):
        xf = x.reshape(batch * seq, D_MODEL)
        za, zb, zg = _inproj(xf, row(g_mix[l]), w_in_b)
        att = attend(za)
        hg, s_fin = _hgrn(zb, s0, lower, row(g_out_norm[l]), batch, seq)
        za3 = za.reshape(batch, seq, 3 * ATT_WIDTH)
        heads = lambda a: a.reshape(1, batch, a.shape[1], ATT_HEADS, ATT_DIM)
        keep = min(WINDOW, seq)
        nk = heads(za3[:, seq - keep:, ATT_WIDTH:2 * ATT_WIDTH])
        nv = heads(za3[:, seq - keep:, 2 * ATT_WIDTH:])
        return dict(mix=(att, hg, zg, xf), nk=nk, nv=nv, s=s_fin[None])

    ck = cache_k[l].reshape(DB, cw, ATT_WIDTH)
    cv = cache_v[l].reshape(DB, cw, ATT_WIDTH)
    fp = front(x_prompt, B, T, jnp.zeros((B, HG_HEADS, HG_DK, HG_DK), F32), lambda za: _attn_prompt(za, base, B, T))
    fs = front(x_sample, DB, S, state_s[l].astype(F32), lambda za: _attn_sample(za, ck, cv, base, DB, S))

    x1, h2, route, cnt = _merge(fp["mix"], fs["mix"], wpa, wpb, wout, row(g_ffn[l]), wr, br)
    cnt = cnt[:, 0, :N_EXPERTS].astype(jnp.int32)
    max_rows = n_tok * TOP_K + nb * N_EXPERTS * (ROW_CH - 1) + N_EXPERTS * (EXP_TM - 1)
    nt_max = -(-max_rows // EXP_TM)
    plan = _route_plan(cnt)
    loff_f = plan["loff"].astype(F32)
    xs = _dispatch(plan, h2, route, loff_f[:, :, None], nt_max * EXP_TM)
    ysort = _experts(plan, xs, w_gu[l], b_gu3, w_down[l], b_down3)
    loffrow = jnp.pad(loff_f, ((0, 0), (0, pad_e)))[:, None, :]
    yp = _combine(plan, (0, nbp), route, loffrow, x1, row(g_final), ysort)
    ys = _combine(plan, (nbp, nb), route, loffrow, x1, row(g_final), ysort)
    return (yp.reshape(B, T, D_MODEL), ys.reshape(DB, S, D_MODEL), fp["nk"], fp["nv"], fp["s"],
            fs["nk"], fs["nv"], fs["s"])
```

```python
import functools

import numpy as np
import jax
import jax.numpy as jnp
from jax import lax
from jax.experimental import pallas as pl
from jax.experimental.pallas import tpu as pltpu

F32 = jnp.float32
BF16 = jnp.bfloat16

D_MODEL = 1024
CHUNK = 64
LEFT_CHUNKS = 8
WINDOW = LEFT_CHUNKS * CHUNK
ATT_HEADS = 8
ATT_DIM = 64
ATT_WIDTH = ATT_HEADS * ATT_DIM
MAX_REL = 256
HG_HEADS = 4
HG_DK = 128
HG_WIDTH = HG_HEADS * HG_DK
N_EXPERTS = 32
TOP_K = 4
D_FF = D_MODEL
SWIGLU_LIMIT = 7.0
SWIGLU_ALPHA = 1.702
RMS_EPS = 1e-5

LANES = 128
NEG = -1e30
LOG2E = 1.4426950408889634
ATT_QBLK = 4 * CHUNK
ATT_KBLKS = LEFT_CHUNKS * CHUNK // ATT_QBLK + 1
ATT_PAIRS_PER_STAGE = 2
HG_C = 128
HG_CHUNKS_PER_STEP = 4
VMEM_LIMIT = 56 * 1024 * 1024
BIAS_W = 1024
SUBLANES = 8
TBK = 256
ROW_CH = SUBLANES
RB = TBK * TOP_K + N_EXPERTS * ROW_CH
MERGE_TM = 2 * TBK
PERM_TM = 256
EXP_TM = 512
CH_PER_TILE = EXP_TM // ROW_CH
BLK_CH = RB // ROW_CH
ROUTE_IDX = 64
ROUTE_P = 72

NT = (((1,), (1,)), ((), ()))
TN = (((0,), (0,)), ((), ()))


def _rms(x, g):
    return x * lax.rsqrt(jnp.mean(x * x, axis=-1, keepdims=True) + RMS_EPS) * g


def _sigmoid(x):
    return 1.0 / (1.0 + jnp.exp(-x))


def _params(*sem):
    return pltpu.CompilerParams(dimension_semantics=sem, vmem_limit_bytes=VMEM_LIMIT)


def _inproj_body(x_ref, g_ref, w_ref, za_ref, zb_ref, zg_ref):
    h = _rms(x_ref[...], g_ref[...]).astype(BF16)
    a, b = 3 * ATT_WIDTH, 3 * ATT_WIDTH + 4 * HG_WIDTH
    za_ref[...] = jnp.dot(h, w_ref[:, :a], preferred_element_type=F32)
    zb_ref[...] = jnp.dot(h, w_ref[:, a:b], preferred_element_type=F32)
    zg_ref[...] = jnp.dot(h, w_ref[:, b:], preferred_element_type=F32)


def _inproj(x, g, w_bf16, tm=512):
    n = x.shape[0]
    cols = w_bf16.shape[1]
    wa, wb, wg = 3 * ATT_WIDTH, 4 * HG_WIDTH, 2 * D_MODEL
    return pl.pallas_call(
        _inproj_body,
        grid=(n // tm,),
        in_specs=[
            pl.BlockSpec((tm, D_MODEL), lambda i: (i, 0)),
            pl.BlockSpec((1, D_MODEL), lambda i: (0, 0)),
            pl.BlockSpec((D_MODEL, cols), lambda i: (0, 0)),
        ],
        out_specs=[
            pl.BlockSpec((tm, wa), lambda i: (i, 0)),
            pl.BlockSpec((tm, wb), lambda i: (i, 0)),
            pl.BlockSpec((tm, wg), lambda i: (i, 0)),
        ],
        out_shape=[
            jax.ShapeDtypeStruct((n, wa), F32),
            jax.ShapeDtypeStruct((n, wb), F32),
            jax.ShapeDtypeStruct((n, wg), F32),
        ],
        compiler_params=_params("arbitrary"),
        name="inproj",
    )(x, g, w_bf16)


def _attn_heads(q_ref, k_refs, v_refs, bias_fn, pens, o_ref):
    lane = lax.broadcasted_iota(jnp.int32, (1, LANES), 1)
    first = lane < ATT_DIM
    halves = (first, lane >= ATT_DIM)
    for hp0 in range(0, ATT_HEADS // 2, ATT_PAIRS_PER_STAGE):
        pairs = range(hp0, hp0 + ATT_PAIRS_PER_STAGE)
        sl = {hp: slice(hp * LANES, (hp + 1) * LANES) for hp in pairs}
        scores = {}
        for hp in pairs:
            q2 = q_ref[:, sl[hp]] * (ATT_DIM ** -0.5 * LOG2E)
            ks = [k[:, sl[hp]].astype(BF16) for k in k_refs]
            for half, mine in enumerate(halves):
                qm = jnp.where(mine, q2, 0.0).astype(BF16)
                ss = []
                for j, kj in enumerate(ks):
                    s = lax.dot_general(qm, kj, NT, preferred_element_type=F32) + bias_fn(2 * hp + half, j)
                    if pens[j] is not None:
                        s = s + pens[j]
                    ss.append(s)
                scores[hp, half] = ss
        for hp in pairs:
            outs = []
            for half, mine in enumerate(halves):
                ss = scores[hp, half]
                vs = [jnp.where(mine, v[:, sl[hp]], 1.0).astype(BF16) for v in v_refs]
                if all(s.shape == ss[0].shape for s in ss):
                    m = jnp.max(functools.reduce(jnp.maximum, ss), axis=-1, keepdims=True)
                else:
                    m = functools.reduce(jnp.maximum, [jnp.max(s, axis=-1, keepdims=True) for s in ss])
                outs.append(functools.reduce(jnp.add, [
                    jnp.dot(jnp.exp2(s - m).astype(BF16), vj, preferred_element_type=F32) for s, vj in zip(ss, vs)]))
            num = jnp.where(first, outs[0], outs[1])
            den = pltpu.roll(jnp.where(first, outs[1], outs[0]), ATT_DIM, 1)
            o_ref[:, sl[hp]] = num * (1.0 / den)


def _fill_bias(base_ref, bias_ref, banded):
    nq, nk = bias_ref.shape[1:]
    if banded:
        r = lax.broadcasted_iota(jnp.int32, (nq, nk), 0)
        s = lax.broadcasted_iota(jnp.int32, (nq, nk), 1)
        qc = (r + WINDOW) // CHUNK
        kc = s // CHUNK
        pen = jnp.where(kc <= qc, jnp.where(kc >= qc - LEFT_CHUNKS, 0.0, NEG), NEG)
    for h in range(ATT_HEADS):
        rows = jnp.broadcast_to(base_ref[h:h + 1, :], (nq, BIAS_W))
        t = pltpu.roll(rows, 0, 1, stride=1, stride_axis=0)[:, :nk] * LOG2E
        bias_ref[h] = t + pen if banded else t


def _attn_prompt_body(q_ref, k0, k1, k2, v0, v1, v2, base_ref, o_ref, bias_ref):
    i = pl.program_id(1)

    @pl.when((pl.program_id(0) == 0) & (i == 0))
    def _():
        _fill_bias(base_ref, bias_ref, True)

    bias_fn = lambda h, j: bias_ref[h, :, j * ATT_QBLK:(j + 1) * ATT_QBLK]
    back = ATT_KBLKS - 1

    @pl.when(i < back)
    def _():
        pens = [jnp.where(i - back + j >= 0, 0.0, NEG) for j in range(back)] + [None]
        _attn_heads(q_ref, [k0, k1, k2], [v0, v1, v2], bias_fn, pens, o_ref)

    @pl.when(i >= back)
    def _():
        _attn_heads(q_ref, [k0, k1, k2], [v0, v1, v2], bias_fn, [None] * ATT_KBLKS, o_ref)


def _attn_prompt(za, base, batch, seq):
    nq = seq // ATT_QBLK
    back = ATT_KBLKS - 1
    qspec = pl.BlockSpec((ATT_QBLK, ATT_WIDTH), lambda b, i: (b * nq + i, 0))

    def kvspec(j, col):
        return pl.BlockSpec((ATT_QBLK, ATT_WIDTH),
                            lambda b, i: (b * nq + jnp.maximum(i - back + j, 0), col))

    return pl.pallas_call(
        _attn_prompt_body,
        grid=(batch, nq),
        in_specs=[qspec] + [kvspec(j, 1) for j in range(ATT_KBLKS)] + [kvspec(j, 2) for j in range(ATT_KBLKS)]
        + [pl.BlockSpec(base.shape, lambda b, i: (0, 0))],
        out_specs=pl.BlockSpec((ATT_QBLK, ATT_WIDTH), lambda b, i: (b * nq + i, 0)),
        out_shape=jax.ShapeDtypeStruct((batch * seq, ATT_WIDTH), F32),
        scratch_shapes=[pltpu.VMEM((ATT_HEADS, ATT_QBLK, ATT_KBLKS * ATT_QBLK), F32)],
        compiler_params=_params("arbitrary", "arbitrary"),
        name="attn_prompt",
    )(za, za, za, za, za, za, za, base)


def _attn_sample_body(q_ref, kn_ref, vn_ref, ck_ref, cv_ref, base_ref, o_ref, bias_ref):
    @pl.when(pl.program_id(0) == 0)
    def _():
        _fill_bias(base_ref, bias_ref, False)

    cw = ck_ref.shape[0]
    bias_fn = lambda h, j: bias_ref[h, :, :cw] if j == 0 else bias_ref[h, :, cw:]
    _attn_heads(q_ref, [ck_ref, kn_ref], [cv_ref, vn_ref], bias_fn, [None, None], o_ref)


def _attn_sample(za, ck, cv, base, batch, seq):
    cw = ck.shape[1]
    return pl.pallas_call(
        _attn_sample_body,
        grid=(batch,),
        in_specs=[
            pl.BlockSpec((seq, ATT_WIDTH), lambda b: (b, 0)),
            pl.BlockSpec((seq, ATT_WIDTH), lambda b: (b, 1)),
            pl.BlockSpec((seq, ATT_WIDTH), lambda b: (b, 2)),
            pl.BlockSpec((None, cw, ATT_WIDTH), lambda b: (b, 0, 0)),
            pl.BlockSpec((None, cw, ATT_WIDTH), lambda b: (b, 0, 0)),
            pl.BlockSpec(base.shape, lambda b: (0, 0)),
        ],
        out_specs=pl.BlockSpec((seq, ATT_WIDTH), lambda b: (b, 0)),
        out_shape=jax.ShapeDtypeStruct((batch * seq, ATT_WIDTH), F32),
        scratch_shapes=[pltpu.VMEM((ATT_HEADS, seq, cw + seq), F32)],
        compiler_params=_params("arbitrary"),
        name="attn_sample",
    )(za, za, za, ck, cv, base)


def _rel_bias_base(table):
    top = table[:, 2 * MAX_REL:].astype(F32)
    rev = table[:, ::-1][:, :2 * MAX_REL].astype(F32)
    left = WINDOW - MAX_REL
    return jnp.concatenate([jnp.broadcast_to(top, (ATT_HEADS, left)), rev,
                            jnp.broadcast_to(top, (ATT_HEADS, BIAS_W - left - 2 * MAX_REL))], axis=1)


def _hgrn_consts(c):
    t = np.arange(c)[:, None]
    j = np.arange(c)[None, :]
    mats = [j <= t, j > t]
    masks = []
    m = c // 2
    while m >= 1:
        ref = (t // (2 * m)) * (2 * m) + m - 1
        second = (t % (2 * m)) >= m
        if m < SUBLANES:
            mats.append((second & (j > ref) & (j <= t)) | (~second & (j > t) & (j <= ref)))
        masks.append((t // (2 * m)) == (j // (2 * m)))
        m //= 2
    return (jnp.asarray(np.concatenate(mats, 0).astype(np.float32), BF16),
            jnp.asarray(np.stack(masks).astype(np.float32)))


def _hgrn_body(zb_ref, s0_ref, lower_ref, gon_ref, p_ref, mask_ref, o_ref, sfin_ref, st_ref, *, single_step):
    c = p_ref.shape[1]
    step = pl.program_id(1)

    def load_state():
        for h in range(HG_HEADS):
            st_ref[h] = s0_ref[0, h].T

    if single_step:
        load_state()
    else:
        pl.when(step == 0)(load_state)

    pmat = p_ref[...]
    n_levels = mask_ref.shape[0]
    head = lambda a, h: a[:, h * HG_DK:(h + 1) * HG_DK]
    low = lower_ref[...]
    row = lax.broadcasted_iota(jnp.int32, (c, HG_WIDTH), 0)

    def stage1(r0):
        part = lambda i: zb_ref[r0:r0 + c, i * HG_WIDTH:(i + 1) * HG_WIDTH]
        q = part(0)
        f = low + (1.0 - low) * _sigmoid(part(1))
        lf = jnp.log(f)
        k = 1.0 - f
        ib = part(2)
        v = ib * _sigmoid(ib)
        og = part(3)
        hi = lf.astype(BF16)
        r1 = lf - hi.astype(F32)
        mid = r1.astype(BF16)
        lo = (r1 - mid.astype(F32)).astype(BF16)
        e = (jnp.dot(pmat, hi, preferred_element_type=F32) + jnp.dot(pmat, mid, preferred_element_type=F32)
             + jnp.dot(pmat, lo, preferred_element_type=F32))
        b = e[0:c]
        return dict(q=q, k=k, v=v, e=e, b=b, decay=jnp.exp(e[c - 1:c]), qe=(q * jnp.exp(b)).astype(BF16),
                    kt=(k * jnp.exp(e[c:2 * c])).astype(BF16), vb=v.astype(BF16), qk=q * k,
                    gate=og * _sigmoid(og))

    def stage2(s):
        q, k, e, b = s["q"], s["k"], s["e"], s["b"]
        att = [None] * HG_HEADS
        n_rows_p = 2
        for lvl in range(n_levels):
            m = c >> (lvl + 1)
            if m >= SUBLANES:
                ref = [jnp.broadcast_to(b[p * 2 * m + m - 1:p * 2 * m + m], (2 * m, HG_WIDTH))
                       for p in range(c // (2 * m))]
                x = jnp.exp(-jnp.abs(b - (jnp.concatenate(ref, axis=0) if len(ref) > 1 else ref[0])))
            else:
                x = jnp.exp(e[n_rows_p * c:(n_rows_p + 1) * c])
                n_rows_p += 1
            second = (row & m) != 0
            qm = jnp.where(second, q * x, 0.0).astype(BF16)
            km = jnp.where(second, 0.0, k * x).astype(BF16)
            for h in range(HG_HEADS):
                a = lax.dot_general(head(qm, h), head(km, h), NT, preferred_element_type=F32)
                if lvl > 0:
                    a = a * mask_ref[lvl]
                att[h] = a if att[h] is None else att[h] + a
        return [jnp.dot(att[h].astype(BF16), head(s["vb"], h), preferred_element_type=F32)
                + jnp.sum(head(s["qk"], h), axis=-1, keepdims=True) * head(s["v"], h) for h in range(HG_HEADS)]

    def stage3(r0, s, intra):
        for h in range(HG_HEADS):
            st = st_ref[h]
            inter = lax.dot_general(head(s["qe"], h), st.astype(BF16), NT, preferred_element_type=F32)
            st_ref[h] = st * head(s["decay"], h) + lax.dot_general(head(s["vb"], h), head(s["kt"], h), TN,
                                                                  preferred_element_type=F32)
            o_ref[r0:r0 + c, h * HG_DK:(h + 1) * HG_DK] = _rms(inter + intra[h], gon_ref[...]) * head(s["gate"], h)

    starts = range(0, zb_ref.shape[0], c)
    firsts = [stage1(r0) for r0 in starts]
    intras = [stage2(s) for s in firsts]
    for r0, s, intra in zip(starts, firsts, intras):
        stage3(r0, s, intra)

    def write_state():
        for h in range(HG_HEADS):
            sfin_ref[0, h] = st_ref[h].T

    if single_step:
        write_state()
    else:
        pl.when(step == pl.num_programs(1) - 1)(write_state)


def _hgrn(zb, s0, lower, g_on, batch, seq):
    c = min(HG_C, seq)
    rows = min(HG_CHUNKS_PER_STEP * c, seq)
    assert seq % rows == 0 and rows % c == 0
    pmat, masks = _hgrn_consts(c)
    nc = seq // rows
    return pl.pallas_call(
        functools.partial(_hgrn_body, single_step=nc == 1),
        grid=(batch, nc),
        in_specs=[
            pl.BlockSpec((rows, 4 * HG_WIDTH), lambda b, i: (b * nc + i, 0)),
            pl.BlockSpec((1, HG_HEADS, HG_DK, HG_DK), lambda b, i: (b, 0, 0, 0)),
            pl.BlockSpec((1, HG_WIDTH), lambda b, i: (0, 0)),
            pl.BlockSpec((1, HG_DK), lambda b, i: (0, 0)),
            pl.BlockSpec(pmat.shape, lambda b, i: (0, 0)),
            pl.BlockSpec(masks.shape, lambda b, i: (0, 0, 0)),
        ],
        out_specs=[
            pl.BlockSpec((rows, HG_WIDTH), lambda b, i: (b * nc + i, 0)),
            pl.BlockSpec((1, HG_HEADS, HG_DK, HG_DK), lambda b, i: (b, 0, 0, 0)),
        ],
        out_shape=[
            jax.ShapeDtypeStruct((batch * seq, HG_WIDTH), F32),
            jax.ShapeDtypeStruct((batch, HG_HEADS, HG_DK, HG_DK), F32),
        ],
        scratch_shapes=[pltpu.VMEM((HG_HEADS, HG_DK, HG_DK), F32)],
        compiler_params=_params("arbitrary", "arbitrary"),
        name="hgrn2",
    )(zb, s0, lower, g_on, pmat, masks)


def _split_bf16(x):
    hi = x.astype(BF16)
    return hi, (x - hi.astype(F32)).astype(BF16)


def _merge_body(att_p, hg_p, zg_p, x_p, att_s, hg_s, zg_s, x_s, wpa_ref, wpb_ref, wout_ref, gffn_ref, wr_ref, br_ref,
                x1_ref, h2_ref, route_ref, cnt_ref, *, n_first):
    weights = (wpa_ref, wpb_ref, wout_ref, gffn_ref, wr_ref, br_ref)
    outs = (x1_ref, h2_ref, route_ref, cnt_ref)
    i = pl.program_id(0)
    pl.when(i < n_first)(functools.partial(_merge_block, att_p, hg_p, zg_p, x_p, *weights, *outs))
    pl.when(i >= n_first)(functools.partial(_merge_block, att_s, hg_s, zg_s, x_s, *weights, *outs))


def _merge_block(att_ref, hg_ref, zg_ref, x_ref, wpa_ref, wpb_ref, wout_ref, gffn_ref, wr_ref, br_ref,
                 x1_ref, h2_ref, route_ref, cnt_ref):
    pa = jnp.dot(att_ref[...].astype(BF16), wpa_ref[...], preferred_element_type=F32)
    pb = jnp.dot(hg_ref[...].astype(BF16), wpb_ref[...], preferred_element_type=F32)
    y = _sigmoid(zg_ref[:, :D_MODEL]) * pa + _sigmoid(zg_ref[:, D_MODEL:]) * pb
    x1 = x_ref[...] + jnp.dot(y.astype(BF16), wout_ref[...], preferred_element_type=F32)
    x1_ref[...] = x1
    h2 = _rms(x1, gffn_ref[...])
    h2_ref[...] = h2.astype(BF16)

    h_hi, h_lo = _split_bf16(h2)
    w_hi, w_lo = _split_bf16(wr_ref[...])
    logits = (jnp.dot(h_hi, w_hi, preferred_element_type=F32) + jnp.dot(h_lo, w_hi, preferred_element_type=F32)
              + jnp.dot(h_hi, w_lo, preferred_element_type=F32)) + br_ref[...]
    lane = lax.broadcasted_iota(jnp.int32, logits.shape, 1).astype(F32)
    cur = logits
    vals, idxs = [], []
    for _ in range(TOP_K):
        m = jnp.max(cur, axis=-1, keepdims=True)
        idx = jnp.min(jnp.where(cur == m, lane, float(LANES)), axis=-1, keepdims=True)
        vals.append(m)
        idxs.append(idx)
        cur = jnp.where(lane == idx, -jnp.inf, cur)
    es = [jnp.exp(v - vals[0]) for v in vals]
    inv = 1.0 / functools.reduce(jnp.add, es)
    route = jnp.zeros_like(logits)
    for k, (ex, idx) in enumerate(zip(es, idxs)):
        route = (route + jnp.where(lane == idx, 1.0, 0.0) + jnp.where(lane == float(ROUTE_IDX + k), idx, 0.0)
                 + jnp.where(lane == float(ROUTE_P + k), ex * inv, 0.0))
    route_ref[...] = route
    sel = jnp.where(lane < float(N_EXPERTS), route, 0.0)
    for blk in range(cnt_ref.shape[0]):
        cnt_ref[blk] = jnp.sum(sel[blk * TBK:(blk + 1) * TBK], axis=0, keepdims=True)


def _merge(first, second, wpa, wpb, wout, g_ffn, w_router, b_router):
    tm = MERGE_TM
    assert first[3].shape[0] % tm == 0 and second[3].shape[0] % tm == 0
    n1, n2 = first[3].shape[0] // tm, second[3].shape[0] // tm
    widths = (ATT_WIDTH, HG_WIDTH, 2 * D_MODEL, D_MODEL)
    spec1 = [pl.BlockSpec((tm, w), lambda i: (jnp.minimum(i, n1 - 1), 0)) for w in widths]
    spec2 = [pl.BlockSpec((tm, w), lambda i: (jnp.maximum(i - n1, 0), 0)) for w in widths]
    row = lambda w: pl.BlockSpec((tm, w), lambda i: (i, 0))
    full = lambda a: pl.BlockSpec(a.shape, lambda i: (0,) * a.ndim)
    n = (n1 + n2) * tm
    return pl.pallas_call(
        functools.partial(_merge_body, n_first=n1),
        grid=(n1 + n2,),
        in_specs=spec1 + spec2 + [full(wpa), full(wpb), full(wout), full(g_ffn), full(w_router), full(b_router)],
        out_specs=[row(D_MODEL), row(D_MODEL), row(LANES),
                   pl.BlockSpec((tm // TBK, 1, LANES), lambda i: (i, 0, 0))],
        out_shape=[
            jax.ShapeDtypeStruct((n, D_MODEL), F32),
            jax.ShapeDtypeStruct((n, D_MODEL), BF16),
            jax.ShapeDtypeStruct((n, LANES), F32),
            jax.ShapeDtypeStruct((n // TBK, 1, LANES), F32),
        ],
        compiler_params=_params("arbitrary"),
        name="merge_router",
    )(*first, *second, wpa, wpb, wout, g_ffn, w_router, b_router)


def _route_plan(cnt):
    pc = (cnt + ROW_CH - 1) // ROW_CH
    loff = jnp.cumsum(pc, axis=1) - pc
    tot = jnp.sum(pc, axis=0)
    reg = (tot + CH_PER_TILE - 1) // CH_PER_TILE * CH_PER_TILE
    gstart = jnp.cumsum(reg) - reg
    goff = gstart[None, :] + jnp.cumsum(pc, axis=0) - pc
    ntiles = jnp.sum(reg) // CH_PER_TILE
    present = reg > 0
    ids = jnp.arange(N_EXPERTS, dtype=jnp.int32)
    later = (ids[None, :] > ids[:, None]) & present[None, :]
    nxt = jnp.min(jnp.where(later, ids[None, :], N_EXPERTS), axis=1)
    nxt = jnp.where(nxt == N_EXPERTS, -1, nxt)
    slot = (jnp.cumsum(present.astype(jnp.int32)) - 1) % 2
    first = jnp.min(jnp.where(present, ids, N_EXPERTS)).reshape(1)
    j = jnp.arange(BLK_CH, dtype=jnp.int32)
    run = jnp.sum(((loff + pc)[:, None, :] <= j[None, :, None]).astype(jnp.int32), axis=2)
    shift = jnp.sum(jnp.where(run[:, :, None] == ids[None, None, :], (goff - loff)[:, None, :], 0), axis=2)
    gmap = shift + j[None, :]
    i32 = lambda a: a.astype(jnp.int32)
    return dict(loff=i32(loff), gmap=i32(gmap), totc=i32(jnp.sum(pc, axis=1)),
                pad_start=i32(gstart + tot), pad_cnt=i32(reg - tot), ntiles=i32(ntiles).reshape(1),
                tile_start=i32(gstart // CH_PER_TILE), tile_count=i32(reg // CH_PER_TILE),
                next_expert=i32(nxt), weight_slot=i32(slot), first_expert=i32(first))


def _chunk_rows(ref, chunk):
    return ref.at[pl.ds(pl.multiple_of(chunk * ROW_CH, ROW_CH), ROW_CH)]


def _for_chunks(n, do):
    log_unroll = 2
    groups = lax.shift_right_logical(n, log_unroll)

    def group(i, carry):
        for u in range(1 << log_unroll):
            do(lax.shift_left(i, log_unroll) + u)
        return carry
    lax.fori_loop(0, groups, group, 0)

    def single(c, carry):
        do(c)
        return carry
    lax.fori_loop(lax.shift_left(groups, log_unroll), n, single, 0)


def _wait_chunks(n, copy_of_rows):
    for bit in range((RB // ROW_CH).bit_length()):
        @pl.when(((n >> bit) & 1) == 1)
        def _(bit=bit):
            copy_of_rows((1 << bit) * ROW_CH).wait()


def _dispatch_body(gmap_s, totc_s, pads_s, padn_s, nt_s, h2_ref, route_ref, tri_ref, loffv_ref,
                   xs_hbm, buf, zbuf, sem):
    b = pl.program_id(0)
    nb = pl.num_programs(0)
    slot = lax.rem(b, 2)
    fill_sem, tile_sem = 2, 3
    nt_max = xs_hbm.shape[0] // EXP_TM

    def copy_out(src, gchunk, sem_i):
        return pltpu.make_async_copy(src, _chunk_rows(xs_hbm, gchunk), sem.at[sem_i])

    def zero_tile(t):
        return pltpu.make_async_copy(zbuf, xs_hbm.at[pl.ds(pl.multiple_of(t * EXP_TM, EXP_TM), EXP_TM)],
                                     sem.at[tile_sem])

    def wait_block(bb, slot_):
        _wait_chunks(totc_s[bb], lambda rows: pltpu.make_async_copy(
            buf.at[slot_, pl.ds(0, rows)], xs_hbm.at[pl.ds(0, rows)], sem.at[slot_]))

    @pl.when(b == 0)
    def _():
        zbuf[...] = jnp.zeros_like(zbuf)

    @pl.when(b >= 2)
    def _():
        wait_block(b - 2, slot)

    rt = route_ref[...].T
    rank_t = jnp.dot(rt[0:N_EXPERTS].astype(BF16), tri_ref[...], preferred_element_type=F32)
    lpos_t = loffv_ref[...] * float(ROW_CH) + rank_t
    erow = lax.broadcasted_iota(jnp.int32, (N_EXPERTS, TBK), 0).astype(F32)
    lposk = [jnp.sum(jnp.where(rt[ROUTE_IDX + k:ROUTE_IDX + k + 1] == erow, lpos_t, 0.0), axis=0, keepdims=True)
             for k in range(TOP_K)]
    h2 = h2_ref[...]
    for r0 in range(0, RB, PERM_TM):
        rrow = lax.broadcasted_iota(jnp.int32, (PERM_TM, TBK), 0).astype(F32) + float(r0)
        perm = functools.reduce(jnp.add, [jnp.where(lposk[k] == rrow, 1.0, 0.0) for k in range(TOP_K)])
        buf[slot, r0:r0 + PERM_TM, :] = jnp.dot(perm.astype(BF16), h2, preferred_element_type=F32)

    _for_chunks(totc_s[b], lambda c: copy_out(_chunk_rows(buf.at[slot], c), gmap_s[b * BLK_CH + c], slot).start())

    @pl.when(b == nb - 1)
    def _():
        zchunk = _chunk_rows(zbuf, 0)

        def fill_expert(e, carry):
            def one(c, carry2):
                copy_out(zchunk, pads_s[e] + c, fill_sem).start()
                return carry2
            lax.fori_loop(0, padn_s[e], one, 0)
            return carry
        lax.fori_loop(0, N_EXPERTS, fill_expert, 0)

        def fill_tile(t, carry):
            zero_tile(t).start()
            return carry
        lax.fori_loop(nt_s[0], nt_max, fill_tile, 0)

        @pl.when(b >= 1)
        def _():
            wait_block(b - 1, 1 - slot)
        wait_block(b, slot)

        def drain_expert(e, carry):
            def one(c, carry2):
                copy_out(zchunk, 0, fill_sem).wait()
                return carry2
            lax.fori_loop(0, padn_s[e], one, 0)
            return carry
        lax.fori_loop(0, N_EXPERTS, drain_expert, 0)

        def drain_tile(t, carry):
            zero_tile(0).wait()
            return carry
        lax.fori_loop(nt_s[0], nt_max, drain_tile, 0)


def _dispatch(plan, h2, route, loffv, n_rows):
    nb = h2.shape[0] // TBK
    tri = jnp.asarray(np.triu(np.ones((TBK, TBK), np.float32), 1), BF16)
    grid_spec = pltpu.PrefetchScalarGridSpec(
        num_scalar_prefetch=5,
        grid=(nb,),
        in_specs=[
            pl.BlockSpec((TBK, D_MODEL), lambda b, *_: (b, 0)),
            pl.BlockSpec((TBK, LANES), lambda b, *_: (b, 0)),
            pl.BlockSpec((TBK, TBK), lambda b, *_: (0, 0)),
            pl.BlockSpec((None, N_EXPERTS, 1), lambda b, *_: (b, 0, 0)),
        ],
        out_specs=pl.BlockSpec(memory_space=pl.ANY),
        scratch_shapes=[pltpu.VMEM((2, RB, D_MODEL), F32), pltpu.VMEM((EXP_TM, D_MODEL), F32),
                        pltpu.SemaphoreType.DMA((4,))],
    )
    return pl.pallas_call(
        _dispatch_body,
        grid_spec=grid_spec,
        out_shape=jax.ShapeDtypeStruct((n_rows, D_MODEL), F32),
        compiler_params=_params("arbitrary"),
        name="moe_dispatch",
    )(plan["gmap"].reshape(-1), plan["totc"], plan["pad_start"], plan["pad_cnt"], plan["ntiles"],
      h2, route, tri, loffv)


def _experts_body(t0_s, n_s, nxt_s, par_s, first_s, nt_s, xs_hbm, wgu_hbm, bgu_ref, wd_hbm, bd_ref, ys_hbm,
                  wg_f, wd_f, wg_b, wd_b, xbuf, ybuf, wsem, xsem, ysem):
    e = pl.program_id(0)
    nt = nt_s[0]
    nt_max = xs_hbm.shape[0] // EXP_TM
    tile_rows = lambda t: pl.ds(pl.multiple_of(t * EXP_TM, EXP_TM), EXP_TM)

    def weight_copies(ex, s):
        return (pltpu.make_async_copy(wgu_hbm.at[ex], wg_f.at[s], wsem.at[s, 0]),
                pltpu.make_async_copy(wd_hbm.at[ex], wd_f.at[s], wsem.at[s, 1]))

    def x_copy(t, s):
        return pltpu.make_async_copy(xs_hbm.at[tile_rows(t)], xbuf.at[s], xsem.at[s])

    def y_copy(t, s):
        return pltpu.make_async_copy(ybuf.at[s], ys_hbm.at[tile_rows(t)], ysem.at[s])

    @pl.when(e == 0)
    def _():
        for cp in weight_copies(first_s[0], 0):
            cp.start()
        x_copy(0, 0).start()

    @pl.when(n_s[e] > 0)
    def _():
        s_w = par_s[e]
        for cp in weight_copies(e, s_w):
            cp.wait()
        wg_b[...] = wg_f[s_w].astype(BF16)
        wd_b[...] = wd_f[s_w].astype(BF16)

        @pl.when(nxt_s[e] >= 0)
        def _():
            for cp in weight_copies(nxt_s[e], 1 - s_w):
                cp.start()

        def tile(i, carry):
            t = t0_s[e] + i
            s = t & 1
            x_copy(t, s).wait()

            @pl.when(t + 1 < nt)
            def _():
                x_copy(t + 1, 1 - s).start()

            @pl.when(t >= 2)
            def _():
                y_copy(t - 2, s).wait()

            gu = jnp.dot(xbuf[s].astype(BF16), wg_b[...], preferred_element_type=F32) + bgu_ref[...]
            gate = jnp.minimum(gu[:, :D_FF], SWIGLU_LIMIT)
            up = jnp.clip(gu[:, D_FF:], -SWIGLU_LIMIT, SWIGLU_LIMIT)
            act = (up + 1.0) * gate * _sigmoid(SWIGLU_ALPHA * gate)
            ybuf[s] = jnp.dot(act.astype(BF16), wd_b[...], preferred_element_type=F32) + bd_ref[...]
            y_copy(t, s).start()
            return carry
        lax.fori_loop(0, n_s[e], tile, 0)

    @pl.when(e == pl.num_programs(0) - 1)
    def _():
        @pl.when(nt >= 2)
        def _():
            y_copy(nt - 2, nt & 1).wait()
        y_copy(nt - 1, (nt - 1) & 1).wait()
        ybuf[0] = jnp.zeros((EXP_TM, D_MODEL), F32)

        def fill(t, carry):
            y_copy(t, 0).start()
            return carry
        lax.fori_loop(nt, nt_max, fill, 0)

        def drain(t, carry):
            y_copy(0, 0).wait()
            return carry
        lax.fori_loop(nt, nt_max, drain, 0)


def _experts(plan, xs, w_gu, b_gu, w_down, b_down):
    n_rows = xs.shape[0]
    of_expert = lambda e, *_: (e, 0, 0)
    grid_spec = pltpu.PrefetchScalarGridSpec(
        num_scalar_prefetch=6,
        grid=(N_EXPERTS,),
        in_specs=[
            pl.BlockSpec(memory_space=pl.ANY),
            pl.BlockSpec(memory_space=pl.ANY),
            pl.BlockSpec((None, 1, 2 * D_FF), of_expert),
            pl.BlockSpec(memory_space=pl.ANY),
            pl.BlockSpec((None, 1, D_MODEL), of_expert),
        ],
        out_specs=pl.BlockSpec(memory_space=pl.ANY),
        scratch_shapes=[pltpu.VMEM((2, D_MODEL, 2 * D_FF), F32), pltpu.VMEM((2, D_FF, D_MODEL), F32),
                        pltpu.VMEM((D_MODEL, 2 * D_FF), BF16), pltpu.VMEM((D_FF, D_MODEL), BF16),
                        pltpu.VMEM((2, EXP_TM, D_MODEL), F32), pltpu.VMEM((2, EXP_TM, D_MODEL), F32),
                        pltpu.SemaphoreType.DMA((2, 2)), pltpu.SemaphoreType.DMA((2,)),
                        pltpu.SemaphoreType.DMA((2,))],
    )
    return pl.pallas_call(
        _experts_body,
        grid_spec=grid_spec,
        out_shape=jax.ShapeDtypeStruct((n_rows, D_MODEL), F32),
        compiler_params=_params("arbitrary"),
        name="moe_experts",
    )(plan["tile_start"], plan["tile_count"], plan["next_expert"], plan["weight_slot"], plan["first_expert"],
      plan["ntiles"], xs, w_gu, b_gu, w_down, b_down)


def _combine_body(gmap_s, totc_s, route_ref, tril_ref, loffrow_ref, x1_ref, gfin_ref, ys_hbm, y_ref, buf, sem):
    b = pl.program_id(0)
    nb = pl.num_programs(0)
    slot = lax.rem(b, 2)

    def copy_in(slot_, lchunk, gchunk):
        return pltpu.make_async_copy(_chunk_rows(ys_hbm, gchunk), _chunk_rows(buf.at[slot_], lchunk), sem.at[slot_])

    def fetch_block(bb, slot_):
        _for_chunks(totc_s[bb], lambda c: copy_in(slot_, c, gmap_s[bb * BLK_CH + c]).start())

    def wait_block(bb, slot_):
        _wait_chunks(totc_s[bb], lambda rows: pltpu.make_async_copy(
            ys_hbm.at[pl.ds(0, rows)], buf.at[slot_, pl.ds(0, rows)], sem.at[slot_]))

    @pl.when(b == 0)
    def _():
        buf[...] = jnp.zeros_like(buf)
        fetch_block(0, 0)

    @pl.when(b + 1 < nb)
    def _():
        fetch_block(b + 1, 1 - slot)

    wait_block(b, slot)

    route = route_ref[...]
    lane = lax.broadcasted_iota(jnp.int32, (1, LANES), 1).astype(F32)
    sel = jnp.where(lane < float(N_EXPERTS), route, 0.0).astype(BF16)
    rank = jnp.dot(tril_ref[...], sel, preferred_element_type=F32)
    lpos = loffrow_ref[...] * float(ROW_CH) + rank
    lposk, pk = [], []
    for k in range(TOP_K):
        idx = route[:, ROUTE_IDX + k:ROUTE_IDX + k + 1]
        lposk.append(jnp.sum(jnp.where(lane == idx, lpos, 0.0), axis=-1, keepdims=True))
        pk.append(route[:, ROUTE_P + k:ROUTE_P + k + 1])
    acc = x1_ref[...]
    for r0 in range(0, RB, PERM_TM):
        col = lax.broadcasted_iota(jnp.int32, (TBK, PERM_TM), 1).astype(F32) + float(r0)
        w = functools.reduce(jnp.add, [jnp.where(lposk[k] == col, pk[k], 0.0) for k in range(TOP_K)])
        acc = acc + jnp.dot(w.astype(BF16), buf[slot, r0:r0 + PERM_TM, :].astype(BF16), preferred_element_type=F32)
    y_ref[...] = _rms(acc, gfin_ref[...])


def _combine(plan, blocks, route, loffrow, x1, g_final, ys):
    b0, b1 = blocks
    nb = b1 - b0
    tril = jnp.asarray(np.tril(np.ones((TBK, TBK), np.float32), -1), BF16)
    grid_spec = pltpu.PrefetchScalarGridSpec(
        num_scalar_prefetch=2,
        grid=(nb,),
        in_specs=[
            pl.BlockSpec((TBK, LANES), lambda b, *_: (b + b0, 0)),
            pl.BlockSpec((TBK, TBK), lambda b, *_: (0, 0)),
            pl.BlockSpec((None, 1, LANES), lambda b, *_: (b + b0, 0, 0)),
            pl.BlockSpec((TBK, D_MODEL), lambda b, *_: (b + b0, 0)),
            pl.BlockSpec((1, D_MODEL), lambda b, *_: (0, 0)),
            pl.BlockSpec(memory_space=pl.ANY),
        ],
        out_specs=pl.BlockSpec((TBK, D_MODEL), lambda b, *_: (b, 0)),
        scratch_shapes=[pltpu.VMEM((2, RB, D_MODEL), F32), pltpu.SemaphoreType.DMA((2,))],
    )
    return pl.pallas_call(
        _combine_body,
        grid_spec=grid_spec,
        out_shape=jax.ShapeDtypeStruct((nb * TBK, D_MODEL), F32),
        compiler_params=_params("arbitrary"),
        name="moe_combine",
    )(plan["gmap"][b0:b1].reshape(-1), plan["totc"][b0:b1], route, tril, loffrow, x1, g_final, ys)


def kernel(x_prompt, x_sample, cache_k, cache_v, state_s, g_mix, w_in, rel_bias, lb_logits, g_out_norm,
           w_pa, w_pb, w_out, g_ffn, w_router, b_router, w_gu, b_gu, w_down, b_down, g_final):
    B, T = x_prompt.shape[:2]
    DB, S = x_sample.shape[:2]
    depth = w_in.shape[0]
    assert depth == 1 and T % ATT_QBLK == 0 and S == CHUNK
    cw = cache_k.shape[2]
    assert cw == WINDOW
    l = 0

    lower = jnp.cumsum(jax.nn.softmax(lb_logits.astype(F32), axis=0), axis=0)[l].reshape(1, HG_WIDTH)
    w_in_b = w_in[l].astype(BF16)
    wpa, wpb, wout = w_pa[l].astype(BF16), w_pb[l].astype(BF16), w_out[l].astype(BF16)
    row = lambda a: a.reshape(1, -1).astype(F32)
    base = _rel_bias_base(rel_bias[l])
    b_gu3 = b_gu[l].reshape(N_EXPERTS, 1, 2 * D_FF)
    b_down3 = b_down[l].reshape(N_EXPERTS, 1, D_MODEL)
    pad_e = LANES - N_EXPERTS
    wr = jnp.pad(w_router[l].astype(F32), ((0, 0), (0, pad_e)))
    br = jnp.concatenate([b_router[l].astype(F32), jnp.full((pad_e,), NEG, F32)]).reshape(1, LANES)

    n_tok = B * T + DB * S
    nb, nbp = n_tok // TBK, (B * T) // TBK

    def front(x, batch, seq, s0, attend):
        xf = x.reshape(batch * seq, D_MODEL)
        za, zb, zg = _inproj(xf, row(g_mix[l]), w_in_b)
        att = attend(za)
        hg, s_fin = _hgrn(zb, s0, lower, row(g_out_norm[l]), batch, seq)
        za3 = za.reshape(batch, seq, 3 * ATT_WIDTH)
        heads = lambda a: a.reshape(1, batch, a.shape[1], ATT_HEADS, ATT_DIM)
        keep = min(WINDOW, seq)
        nk = heads(za3[:, seq - keep:, ATT_WIDTH:2 * ATT_WIDTH])
        nv = heads(za3[:, seq - keep:, 2 * ATT_WIDTH:])
        return dict(mix=(att, hg, zg, xf), nk=nk, nv=nv, s=s_fin[None])

    ck = cache_k[l].reshape(DB, cw, ATT_WIDTH)
    cv = cache_v[l].reshape(DB, cw, ATT_WIDTH)
    fp = front(x_prompt, B, T, jnp.zeros((B, HG_HEADS, HG_DK, HG_DK), F32), lambda za: _attn_prompt(za, base, B, T))
    fs = front(x_sample, DB, S, state_s[l].astype(F32), lambda za: _attn_sample(za, ck, cv, base, DB, S))

    x1, h2, route, cnt = _merge(fp["mix"], fs["mix"], wpa, wpb, wout, row(g_ffn[l]), wr, br)
    cnt = cnt[:, 0, :N_EXPERTS].astype(jnp.int32)
    max_rows = n_tok * TOP_K + nb * N_EXPERTS * (ROW_CH - 1) + N_EXPERTS * (EXP_TM - 1)
    nt_max = -(-max_rows // EXP_TM)
    plan = _route_plan(cnt)
    loff_f = plan["loff"].astype(F32)
    xs = _dispatch(plan, h2, route, loff_f[:, :, None], nt_max * EXP_TM)
    ysort = _experts(plan, xs, w_gu[l], b_gu3, w_down[l], b_down3)
    loffrow = jnp.pad(loff_f, ((0, 0), (0, pad_e)))[:, None, :]
    yp = _combine(plan, (0, nbp), route, loffrow, x1, row(g_final), ysort)
    ys = _combine(plan, (nbp, nb), route, loffrow, x1, row(g_final), ysort)
    return (yp.reshape(B, T, D_MODEL), ys.reshape(DB, S, D_MODEL), fp["nk"], fp["nv"], fp["s"],
            fs["nk"], fs["nv"], fs["s"])
```

```python
import functools

import numpy as np
import jax
import jax.numpy as jnp
from jax import lax
from jax.experimental import pallas as pl
from jax.experimental.pallas import tpu as pltpu

F32 = jnp.float32
BF16 = jnp.bfloat16

D_MODEL = 1024
CHUNK = 64
LEFT_CHUNKS = 8
WINDOW = LEFT_CHUNKS * CHUNK
ATT_HEADS = 8
ATT_DIM = 64
ATT_WIDTH = ATT_HEADS * ATT_DIM
MAX_REL = 256
HG_HEADS = 4
HG_DK = 128
HG_WIDTH = HG_HEADS * HG_DK
N_EXPERTS = 32
TOP_K = 4
D_FF = D_MODEL
SWIGLU_LIMIT = 7.0
SWIGLU_ALPHA = 1.702
RMS_EPS = 1e-5

LANES = 128
NEG = -1e30
LOG2E = 1.4426950408889634
ATT_QBLK = 4 * CHUNK
ATT_KBLKS = LEFT_CHUNKS * CHUNK // ATT_QBLK + 1
ATT_PAIRS_PER_STAGE = 2
HG_C = 128
HG_CHUNKS_PER_STEP = 4
VMEM_LIMIT = 56 * 1024 * 1024
BIAS_W = 1024
SUBLANES = 8
TBK = 256
ROW_CH = SUBLANES
RB = TBK * TOP_K + N_EXPERTS * ROW_CH
MERGE_TM = 2 * TBK
PERM_TM = 256
EXP_TM = 512
W_PIECES = 4
W_PIECE_ROWS = D_MODEL // W_PIECES
CH_PER_TILE = EXP_TM // ROW_CH
BLK_CH = RB // ROW_CH
ROUTE_IDX = 64
ROUTE_P = 72

NT = (((1,), (1,)), ((), ()))
TN = (((0,), (0,)), ((), ()))


def _rms(x, g):
    return x * lax.rsqrt(jnp.mean(x * x, axis=-1, keepdims=True) + RMS_EPS) * g


def _sigmoid(x):
    return 1.0 / (1.0 + jnp.exp(-x))


def _params(*sem):
    return pltpu.CompilerParams(dimension_semantics=sem, vmem_limit_bytes=VMEM_LIMIT)


def _inproj_body(x_ref, g_ref, w_ref, za_ref, zb_ref, zg_ref):
    h = _rms(x_ref[...], g_ref[...]).astype(BF16)
    a, b = 3 * ATT_WIDTH, 3 * ATT_WIDTH + 4 * HG_WIDTH
    za_ref[...] = jnp.dot(h, w_ref[:, :a], preferred_element_type=F32)
    zb_ref[...] = jnp.dot(h, w_ref[:, a:b], preferred_element_type=F32)
    zg_ref[...] = jnp.dot(h, w_ref[:, b:], preferred_element_type=F32)


def _inproj(x, g, w_bf16, tm=512):
    n = x.shape[0]
    cols = w_bf16.shape[1]
    wa, wb, wg = 3 * ATT_WIDTH, 4 * HG_WIDTH, 2 * D_MODEL
    return pl.pallas_call(
        _inproj_body,
        grid=(n // tm,),
        in_specs=[
            pl.BlockSpec((tm, D_MODEL), lambda i: (i, 0)),
            pl.BlockSpec((1, D_MODEL), lambda i: (0, 0)),
            pl.BlockSpec((D_MODEL, cols), lambda i: (0, 0)),
        ],
        out_specs=[
            pl.BlockSpec((tm, wa), lambda i: (i, 0)),
            pl.BlockSpec((tm, wb), lambda i: (i, 0)),
            pl.BlockSpec((tm, wg), lambda i: (i, 0)),
        ],
        out_shape=[
            jax.ShapeDtypeStruct((n, wa), F32),
            jax.ShapeDtypeStruct((n, wb), F32),
            jax.ShapeDtypeStruct((n, wg), F32),
        ],
        compiler_params=_params("arbitrary"),
        name="inproj",
    )(x, g, w_bf16)


def _attn_heads(q_ref, k_refs, v_refs, bias_fn, pens, o_ref):
    lane = lax.broadcasted_iota(jnp.int32, (1, LANES), 1)
    first = lane < ATT_DIM
    halves = (first, lane >= ATT_DIM)
    for hp0 in range(0, ATT_HEADS // 2, ATT_PAIRS_PER_STAGE):
        pairs = range(hp0, hp0 + ATT_PAIRS_PER_STAGE)
        sl = {hp: slice(hp * LANES, (hp + 1) * LANES) for hp in pairs}
        scores = {}
        for hp in pairs:
            q2 = q_ref[:, sl[hp]] * (ATT_DIM ** -0.5 * LOG2E)
            ks = [k[:, sl[hp]].astype(BF16) for k in k_refs]
            for half, mine in enumerate(halves):
                qm = jnp.where(mine, q2, 0.0).astype(BF16)
                ss = []
                for j, kj in enumerate(ks):
                    s = lax.dot_general(qm, kj, NT, preferred_element_type=F32) + bias_fn(2 * hp + half, j)
                    if pens[j] is not None:
                        s = s + pens[j]
                    ss.append(s)
                scores[hp, half] = ss
        for hp in pairs:
            outs = []
            for half, mine in enumerate(halves):
                ss = scores[hp, half]
                vs = [jnp.where(mine, v[:, sl[hp]], 1.0).astype(BF16) for v in v_refs]
                if all(s.shape == ss[0].shape for s in ss):
                    m = jnp.max(functools.reduce(jnp.maximum, ss), axis=-1, keepdims=True)
                else:
                    m = functools.reduce(jnp.maximum, [jnp.max(s, axis=-1, keepdims=True) for s in ss])
                outs.append(functools.reduce(jnp.add, [
                    jnp.dot(jnp.exp2(s - m).astype(BF16), vj, preferred_element_type=F32) for s, vj in zip(ss, vs)]))
            num = jnp.where(first, outs[0], outs[1])
            den = pltpu.roll(jnp.where(first, outs[1], outs[0]), ATT_DIM, 1)
            o_ref[:, sl[hp]] = num * (1.0 / den)


def _fill_bias(base_ref, bias_ref, banded):
    nq, nk = bias_ref.shape[1:]
    if banded:
        r = lax.broadcasted_iota(jnp.int32, (nq, nk), 0)
        s = lax.broadcasted_iota(jnp.int32, (nq, nk), 1)
        qc = (r + WINDOW) // CHUNK
        kc = s // CHUNK
        pen = jnp.where(kc <= qc, jnp.where(kc >= qc - LEFT_CHUNKS, 0.0, NEG), NEG)
    for h in range(ATT_HEADS):
        rows = jnp.broadcast_to(base_ref[h:h + 1, :], (nq, BIAS_W))
        t = pltpu.roll(rows, 0, 1, stride=1, stride_axis=0)[:, :nk] * LOG2E
        bias_ref[h] = t + pen if banded else t


def _attn_prompt_body(q_ref, k0, k1, k2, v0, v1, v2, base_ref, o_ref, bias_ref):
    i = pl.program_id(1)

    @pl.when((pl.program_id(0) == 0) & (i == 0))
    def _():
        _fill_bias(base_ref, bias_ref, True)

    bias_fn = lambda h, j: bias_ref[h, :, j * ATT_QBLK:(j + 1) * ATT_QBLK]
    back = ATT_KBLKS - 1

    @pl.when(i < back)
    def _():
        pens = [jnp.where(i - back + j >= 0, 0.0, NEG) for j in range(back)] + [None]
        _attn_heads(q_ref, [k0, k1, k2], [v0, v1, v2], bias_fn, pens, o_ref)

    @pl.when(i >= back)
    def _():
        _attn_heads(q_ref, [k0, k1, k2], [v0, v1, v2], bias_fn, [None] * ATT_KBLKS, o_ref)


def _attn_prompt(za, base, batch, seq):
    nq = seq // ATT_QBLK
    back = ATT_KBLKS - 1
    qspec = pl.BlockSpec((ATT_QBLK, ATT_WIDTH), lambda b, i: (b * nq + i, 0))

    def kvspec(j, col):
        return pl.BlockSpec((ATT_QBLK, ATT_WIDTH),
                            lambda b, i: (b * nq + jnp.maximum(i - back + j, 0), col))

    return pl.pallas_call(
        _attn_prompt_body,
        grid=(batch, nq),
        in_specs=[qspec] + [kvspec(j, 1) for j in range(ATT_KBLKS)] + [kvspec(j, 2) for j in range(ATT_KBLKS)]
        + [pl.BlockSpec(base.shape, lambda b, i: (0, 0))],
        out_specs=pl.BlockSpec((ATT_QBLK, ATT_WIDTH), lambda b, i: (b * nq + i, 0)),
        out_shape=jax.ShapeDtypeStruct((batch * seq, ATT_WIDTH), F32),
        scratch_shapes=[pltpu.VMEM((ATT_HEADS, ATT_QBLK, ATT_KBLKS * ATT_QBLK), F32)],
        compiler_params=_params("arbitrary", "arbitrary"),
        name="attn_prompt",
    )(za, za, za, za, za, za, za, base)


def _attn_sample_body(q_ref, kn_ref, vn_ref, ck_ref, cv_ref, base_ref, o_ref, bias_ref):
    @pl.when(pl.program_id(0) == 0)
    def _():
        _fill_bias(base_ref, bias_ref, False)

    cw = ck_ref.shape[0]
    bias_fn = lambda h, j: bias_ref[h, :, :cw] if j == 0 else bias_ref[h, :, cw:]
    _attn_heads(q_ref, [ck_ref, kn_ref], [cv_ref, vn_ref], bias_fn, [None, None], o_ref)


def _attn_sample(za, ck, cv, base, batch, seq):
    cw = ck.shape[1]
    return pl.pallas_call(
        _attn_sample_body,
        grid=(batch,),
        in_specs=[
            pl.BlockSpec((seq, ATT_WIDTH), lambda b: (b, 0)),
            pl.BlockSpec((seq, ATT_WIDTH), lambda b: (b, 1)),
            pl.BlockSpec((seq, ATT_WIDTH), lambda b: (b, 2)),
            pl.BlockSpec((None, cw, ATT_WIDTH), lambda b: (b, 0, 0)),
            pl.BlockSpec((None, cw, ATT_WIDTH), lambda b: (b, 0, 0)),
            pl.BlockSpec(base.shape, lambda b: (0, 0)),
        ],
        out_specs=pl.BlockSpec((seq, ATT_WIDTH), lambda b: (b, 0)),
        out_shape=jax.ShapeDtypeStruct((batch * seq, ATT_WIDTH), F32),
        scratch_shapes=[pltpu.VMEM((ATT_HEADS, seq, cw + seq), F32)],
        compiler_params=_params("arbitrary"),
        name="attn_sample",
    )(za, za, za, ck, cv, base)


def _rel_bias_base(table):
    top = table[:, 2 * MAX_REL:].astype(F32)
    rev = table[:, ::-1][:, :2 * MAX_REL].astype(F32)
    left = WINDOW - MAX_REL
    return jnp.concatenate([jnp.broadcast_to(top, (ATT_HEADS, left)), rev,
                            jnp.broadcast_to(top, (ATT_HEADS, BIAS_W - left - 2 * MAX_REL))], axis=1)


def _hgrn_consts(c):
    t = np.arange(c)[:, None]
    j = np.arange(c)[None, :]
    mats = [j <= t, j > t]
    masks = []
    m = c // 2
    while m >= 1:
        ref = (t // (2 * m)) * (2 * m) + m - 1
        second = (t % (2 * m)) >= m
        if m < SUBLANES:
            mats.append((second & (j > ref) & (j <= t)) | (~second & (j > t) & (j <= ref)))
        masks.append((t // (2 * m)) == (j // (2 * m)))
        m //= 2
    return (jnp.asarray(np.concatenate(mats, 0).astype(np.float32), BF16),
            jnp.asarray(np.stack(masks).astype(np.float32)))


def _hgrn_body(zb_ref, s0_ref, lower_ref, gon_ref, p_ref, mask_ref, o_ref, sfin_ref, st_ref, *, single_step):
    c = p_ref.shape[1]
    step = pl.program_id(1)

    def load_state():
        for h in range(HG_HEADS):
            st_ref[h] = s0_ref[0, h].T

    if single_step:
        load_state()
    else:
        pl.when(step == 0)(load_state)

    pmat = p_ref[...]
    n_levels = mask_ref.shape[0]
    head = lambda a, h: a[:, h * HG_DK:(h + 1) * HG_DK]
    low = lower_ref[...]
    row = lax.broadcasted_iota(jnp.int32, (c, HG_WIDTH), 0)

    def stage1(r0):
        part = lambda i: zb_ref[r0:r0 + c, i * HG_WIDTH:(i + 1) * HG_WIDTH]
        q = part(0)
        f = low + (1.0 - low) * _sigmoid(part(1))
        lf = jnp.log(f)
        k = 1.0 - f
        ib = part(2)
        v = ib * _sigmoid(ib)
        og = part(3)
        hi = lf.astype(BF16)
        r1 = lf - hi.astype(F32)
        mid = r1.astype(BF16)
        lo = (r1 - mid.astype(F32)).astype(BF16)
        e = (jnp.dot(pmat, hi, preferred_element_type=F32) + jnp.dot(pmat, mid, preferred_element_type=F32)
             + jnp.dot(pmat, lo, preferred_element_type=F32))
        b = e[0:c]
        return dict(q=q, k=k, v=v, e=e, b=b, decay=jnp.exp(e[c - 1:c]), qe=(q * jnp.exp(b)).astype(BF16),
                    kt=(k * jnp.exp(e[c:2 * c])).astype(BF16), vb=v.astype(BF16), qk=q * k,
                    gate=og * _sigmoid(og))

    def stage2(s):
        q, k, e, b = s["q"], s["k"], s["e"], s["b"]
        att = [None] * HG_HEADS
        n_rows_p = 2
        for lvl in range(n_levels):
            m = c >> (lvl + 1)
            if m >= SUBLANES:
                ref = [jnp.broadcast_to(b[p * 2 * m + m - 1:p * 2 * m + m], (2 * m, HG_WIDTH))
                       for p in range(c // (2 * m))]
                x = jnp.exp(-jnp.abs(b - (jnp.concatenate(ref, axis=0) if len(ref) > 1 else ref[0])))
            else:
                x = jnp.exp(e[n_rows_p * c:(n_rows_p + 1) * c])
                n_rows_p += 1
            second = (row & m) != 0
            qm = jnp.where(second, q * x, 0.0).astype(BF16)
            km = jnp.where(second, 0.0, k * x).astype(BF16)
            for h in range(HG_HEADS):
                a = lax.dot_general(head(qm, h), head(km, h), NT, preferred_element_type=F32)
                if lvl > 0:
                    a = a * mask_ref[lvl]
                att[h] = a if att[h] is None else att[h] + a
        return [jnp.dot(att[h].astype(BF16), head(s["vb"], h), preferred_element_type=F32)
                + jnp.sum(head(s["qk"], h), axis=-1, keepdims=True) * head(s["v"], h) for h in range(HG_HEADS)]

    def stage3(r0, s, intra):
        for h in range(HG_HEADS):
            st = st_ref[h]
            inter = lax.dot_general(head(s["qe"], h), st.astype(BF16), NT, preferred_element_type=F32)
            st_ref[h] = st * head(s["decay"], h) + lax.dot_general(head(s["vb"], h), head(s["kt"], h), TN,
                                                                  preferred_element_type=F32)
            o_ref[r0:r0 + c, h * HG_DK:(h + 1) * HG_DK] = _rms(inter + intra[h], gon_ref[...]) * head(s["gate"], h)

    starts = range(0, zb_ref.shape[0], c)
    firsts = [stage1(r0) for r0 in starts]
    intras = [stage2(s) for s in firsts]
    for r0, s, intra in zip(starts, firsts, intras):
        stage3(r0, s, intra)

    def write_state():
        for h in range(HG_HEADS):
            sfin_ref[0, h] = st_ref[h].T

    if single_step:
        write_state()
    else:
        pl.when(step == pl.num_programs(1) - 1)(write_state)


def _hgrn(zb, s0, lower, g_on, batch, seq):
    c = min(HG_C, seq)
    rows = min(HG_CHUNKS_PER_STEP * c, seq)
    assert seq % rows == 0 and rows % c == 0
    pmat, masks = _hgrn_consts(c)
    nc = seq // rows
    return pl.pallas_call(
        functools.partial(_hgrn_body, single_step=nc == 1),
        grid=(batch, nc),
        in_specs=[
            pl.BlockSpec((rows, 4 * HG_WIDTH), lambda b, i: (b * nc + i, 0)),
            pl.BlockSpec((1, HG_HEADS, HG_DK, HG_DK), lambda b, i: (b, 0, 0, 0)),
            pl.BlockSpec((1, HG_WIDTH), lambda b, i: (0, 0)),
            pl.BlockSpec((1, HG_DK), lambda b, i: (0, 0)),
            pl.BlockSpec(pmat.shape, lambda b, i: (0, 0)),
            pl.BlockSpec(masks.shape, lambda b, i: (0, 0, 0)),
        ],
        out_specs=[
            pl.BlockSpec((rows, HG_WIDTH), lambda b, i: (b * nc + i, 0)),
            pl.BlockSpec((1, HG_HEADS, HG_DK, HG_DK), lambda b, i: (b, 0, 0, 0)),
        ],
        out_shape=[
            jax.ShapeDtypeStruct((batch * seq, HG_WIDTH), F32),
            jax.ShapeDtypeStruct((batch, HG_HEADS, HG_DK, HG_DK), F32),
        ],
        scratch_shapes=[pltpu.VMEM((HG_HEADS, HG_DK, HG_DK), F32)],
        compiler_params=_params("arbitrary", "arbitrary"),
        name="hgrn2",
    )(zb, s0, lower, g_on, pmat, masks)


def _split_bf16(x):
    hi = x.astype(BF16)
    return hi, (x - hi.astype(F32)).astype(BF16)


def _merge_body(att_p, hg_p, zg_p, x_p, att_s, hg_s, zg_s, x_s, wpa_ref, wpb_ref, wout_ref, gffn_ref, wr_ref, br_ref,
                x1_ref, h2_ref, route_ref, cnt_ref, *, n_first):
    weights = (wpa_ref, wpb_ref, wout_ref, gffn_ref, wr_ref, br_ref)
    outs = (x1_ref, h2_ref, route_ref, cnt_ref)
    i = pl.program_id(0)
    pl.when(i < n_first)(functools.partial(_merge_block, att_p, hg_p, zg_p, x_p, *weights, *outs))
    pl.when(i >= n_first)(functools.partial(_merge_block, att_s, hg_s, zg_s, x_s, *weights, *outs))


def _merge_block(att_ref, hg_ref, zg_ref, x_ref, wpa_ref, wpb_ref, wout_ref, gffn_ref, wr_ref, br_ref,
                 x1_ref, h2_ref, route_ref, cnt_ref):
    pa = jnp.dot(att_ref[...].astype(BF16), wpa_ref[...], preferred_element_type=F32)
    pb = jnp.dot(hg_ref[...].astype(BF16), wpb_ref[...], preferred_element_type=F32)
    y = _sigmoid(zg_ref[:, :D_MODEL]) * pa + _sigmoid(zg_ref[:, D_MODEL:]) * pb
    x1 = x_ref[...] + jnp.dot(y.astype(BF16), wout_ref[...], preferred_element_type=F32)
    x1_ref[...] = x1
    h2 = _rms(x1, gffn_ref[...])
    h2_ref[...] = h2.astype(BF16)

    h_hi, h_lo = _split_bf16(h2)
    w_hi, w_lo = _split_bf16(wr_ref[...])
    logits = (jnp.dot(h_hi, w_hi, preferred_element_type=F32) + jnp.dot(h_lo, w_hi, preferred_element_type=F32)
              + jnp.dot(h_hi, w_lo, preferred_element_type=F32)) + br_ref[...]
    lane = lax.broadcasted_iota(jnp.int32, logits.shape, 1).astype(F32)
    cur = logits
    vals, idxs = [], []
    for _ in range(TOP_K):
        m = jnp.max(cur, axis=-1, keepdims=True)
        idx = jnp.min(jnp.where(cur == m, lane, float(LANES)), axis=-1, keepdims=True)
        vals.append(m)
        idxs.append(idx)
        cur = jnp.where(lane == idx, -jnp.inf, cur)
    es = [jnp.exp(v - vals[0]) for v in vals]
    inv = 1.0 / functools.reduce(jnp.add, es)
    route = jnp.zeros_like(logits)
    for k, (ex, idx) in enumerate(zip(es, idxs)):
        route = (route + jnp.where(lane == idx, 1.0, 0.0) + jnp.where(lane == float(ROUTE_IDX + k), idx, 0.0)
                 + jnp.where(lane == float(ROUTE_P + k), ex * inv, 0.0))
    route_ref[...] = route
    sel = jnp.where(lane < float(N_EXPERTS), route, 0.0)
    for blk in range(cnt_ref.shape[0]):
        cnt_ref[blk] = jnp.sum(sel[blk * TBK:(blk + 1) * TBK], axis=0, keepdims=True)


def _merge(first, second, wpa, wpb, wout, g_ffn, w_router, b_router):
    tm = MERGE_TM
    assert first[3].shape[0] % tm == 0 and second[3].shape[0] % tm == 0
    n1, n2 = first[3].shape[0] // tm, second[3].shape[0] // tm
    widths = (ATT_WIDTH, HG_WIDTH, 2 * D_MODEL, D_MODEL)
    spec1 = [pl.BlockSpec((tm, w), lambda i: (jnp.minimum(i, n1 - 1), 0)) for w in widths]
    spec2 = [pl.BlockSpec((tm, w), lambda i: (jnp.maximum(i - n1, 0), 0)) for w in widths]
    row = lambda w: pl.BlockSpec((tm, w), lambda i: (i, 0))
    full = lambda a: pl.BlockSpec(a.shape, lambda i: (0,) * a.ndim)
    n = (n1 + n2) * tm
    return pl.pallas_call(
        functools.partial(_merge_body, n_first=n1),
        grid=(n1 + n2,),
        in_specs=spec1 + spec2 + [full(wpa), full(wpb), full(wout), full(g_ffn), full(w_router), full(b_router)],
        out_specs=[row(D_MODEL), row(D_MODEL), row(LANES),
                   pl.BlockSpec((tm // TBK, 1, LANES), lambda i: (i, 0, 0))],
        out_shape=[
            jax.ShapeDtypeStruct((n, D_MODEL), F32),
            jax.ShapeDtypeStruct((n, D_MODEL), BF16),
            jax.ShapeDtypeStruct((n, LANES), F32),
            jax.ShapeDtypeStruct((n // TBK, 1, LANES), F32),
        ],
        compiler_params=_params("arbitrary"),
        name="merge_router",
    )(*first, *second, wpa, wpb, wout, g_ffn, w_router, b_router)


def _route_plan(cnt):
    pc = (cnt + ROW_CH - 1) // ROW_CH
    loff = jnp.cumsum(pc, axis=1) - pc
    tot = jnp.sum(pc, axis=0)
    reg = (tot + CH_PER_TILE - 1) // CH_PER_TILE * CH_PER_TILE
    gstart = jnp.cumsum(reg) - reg
    goff = gstart[None, :] + jnp.cumsum(pc, axis=0) - pc
    ntiles = jnp.sum(reg) // CH_PER_TILE
    present = reg > 0
    ids = jnp.arange(N_EXPERTS, dtype=jnp.int32)
    later = (ids[None, :] > ids[:, None]) & present[None, :]
    nxt = jnp.min(jnp.where(later, ids[None, :], N_EXPERTS), axis=1)
    nxt = jnp.where(nxt == N_EXPERTS, -1, nxt)
    slot = (jnp.cumsum(present.astype(jnp.int32)) - 1) % 2
    first = jnp.min(jnp.where(present, ids, N_EXPERTS)).reshape(1)
    j = jnp.arange(BLK_CH, dtype=jnp.int32)
    run = jnp.sum(((loff + pc)[:, None, :] <= j[None, :, None]).astype(jnp.int32), axis=2)
    shift = jnp.sum(jnp.where(run[:, :, None] == ids[None, None, :], (goff - loff)[:, None, :], 0), axis=2)
    gmap = shift + j[None, :]
    i32 = lambda a: a.astype(jnp.int32)
    return dict(loff=i32(loff), gmap=i32(gmap), totc=i32(jnp.sum(pc, axis=1)),
                pad_start=i32(gstart + tot), pad_cnt=i32(reg - tot), ntiles=i32(ntiles).reshape(1),
                tile_start=i32(gstart // CH_PER_TILE), tile_count=i32(reg // CH_PER_TILE),
                next_expert=i32(nxt), weight_slot=i32(slot), first_expert=i32(first))


def _chunk_rows(ref, chunk):
    return ref.at[pl.ds(pl.multiple_of(chunk * ROW_CH, ROW_CH), ROW_CH)]


def _for_chunks(n, do):
    log_unroll = 2
    groups = lax.shift_right_logical(n, log_unroll)

    def group(i, carry):
        for u in range(1 << log_unroll):
            do(lax.shift_left(i, log_unroll) + u)
        return carry
    lax.fori_loop(0, groups, group, 0)

    def single(c, carry):
        do(c)
        return carry
    lax.fori_loop(lax.shift_left(groups, log_unroll), n, single, 0)


def _wait_chunks(n, copy_of_rows):
    for bit in range((RB // ROW_CH).bit_length()):
        @pl.when(((n >> bit) & 1) == 1)
        def _(bit=bit):
            copy_of_rows((1 << bit) * ROW_CH).wait()


def _dispatch_body(gmap_s, totc_s, pads_s, padn_s, nt_s, h2_ref, route_ref, tri_ref, loffv_ref,
                   xs_hbm, buf, zbuf, sem):
    b = pl.program_id(0)
    nb = pl.num_programs(0)
    slot = lax.rem(b, 2)
    fill_sem, tile_sem = 2, 3
    nt_max = xs_hbm.shape[0] // EXP_TM

    def copy_out(src, gchunk, sem_i):
        return pltpu.make_async_copy(src, _chunk_rows(xs_hbm, gchunk), sem.at[sem_i])

    def zero_tile(t):
        return pltpu.make_async_copy(zbuf, xs_hbm.at[pl.ds(pl.multiple_of(t * EXP_TM, EXP_TM), EXP_TM)],
                                     sem.at[tile_sem])

    def wait_block(bb, slot_):
        _wait_chunks(totc_s[bb], lambda rows: pltpu.make_async_copy(
            buf.at[slot_, pl.ds(0, rows)], xs_hbm.at[pl.ds(0, rows)], sem.at[slot_]))

    @pl.when(b == 0)
    def _():
        zbuf[...] = jnp.zeros_like(zbuf)

    @pl.when(b >= 2)
    def _():
        wait_block(b - 2, slot)

    rt = route_ref[...].T
    rank_t = jnp.dot(rt[0:N_EXPERTS].astype(BF16), tri_ref[...], preferred_element_type=F32)
    lpos_t = loffv_ref[...] * float(ROW_CH) + rank_t
    erow = lax.broadcasted_iota(jnp.int32, (N_EXPERTS, TBK), 0).astype(F32)
    lposk = [jnp.sum(jnp.where(rt[ROUTE_IDX + k:ROUTE_IDX + k + 1] == erow, lpos_t, 0.0), axis=0, keepdims=True)
             for k in range(TOP_K)]
    h2 = h2_ref[...]
    for r0 in range(0, RB, PERM_TM):
        rrow = lax.broadcasted_iota(jnp.int32, (PERM_TM, TBK), 0).astype(F32) + float(r0)
        perm = functools.reduce(jnp.add, [jnp.where(lposk[k] == rrow, 1.0, 0.0) for k in range(TOP_K)])
        buf[slot, r0:r0 + PERM_TM, :] = jnp.dot(perm.astype(BF16), h2, preferred_element_type=F32)

    _for_chunks(totc_s[b], lambda c: copy_out(_chunk_rows(buf.at[slot], c), gmap_s[b * BLK_CH + c], slot).start())

    @pl.when(b == nb - 1)
    def _():
        zchunk = _chunk_rows(zbuf, 0)

        def fill_expert(e, carry):
            def one(c, carry2):
                copy_out(zchunk, pads_s[e] + c, fill_sem).start()
                return carry2
            lax.fori_loop(0, padn_s[e], one, 0)
            return carry
        lax.fori_loop(0, N_EXPERTS, fill_expert, 0)

        def fill_tile(t, carry):
            zero_tile(t).start()
            return carry
        lax.fori_loop(nt_s[0], nt_max, fill_tile, 0)

        @pl.when(b >= 1)
        def _():
            wait_block(b - 1, 1 - slot)
        wait_block(b, slot)

        def drain_expert(e, carry):
            def one(c, carry2):
                copy_out(zchunk, 0, fill_sem).wait()
                return carry2
            lax.fori_loop(0, padn_s[e], one, 0)
            return carry
        lax.fori_loop(0, N_EXPERTS, drain_expert, 0)

        def drain_tile(t, carry):
            zero_tile(0).wait()
            return carry
        lax.fori_loop(nt_s[0], nt_max, drain_tile, 0)


def _dispatch(plan, h2, route, loffv, n_rows):
    nb = h2.shape[0] // TBK
    tri = jnp.asarray(np.triu(np.ones((TBK, TBK), np.float32), 1), BF16)
    grid_spec = pltpu.PrefetchScalarGridSpec(
        num_scalar_prefetch=5,
        grid=(nb,),
        in_specs=[
            pl.BlockSpec((TBK, D_MODEL), lambda b, *_: (b, 0)),
            pl.BlockSpec((TBK, LANES), lambda b, *_: (b, 0)),
            pl.BlockSpec((TBK, TBK), lambda b, *_: (0, 0)),
            pl.BlockSpec((None, N_EXPERTS, 1), lambda b, *_: (b, 0, 0)),
        ],
        out_specs=pl.BlockSpec(memory_space=pl.ANY),
        scratch_shapes=[pltpu.VMEM((2, RB, D_MODEL), F32), pltpu.VMEM((EXP_TM, D_MODEL), F32),
                        pltpu.SemaphoreType.DMA((4,))],
    )
    return pl.pallas_call(
        _dispatch_body,
        grid_spec=grid_spec,
        out_shape=jax.ShapeDtypeStruct((n_rows, D_MODEL), F32),
        compiler_params=_params("arbitrary"),
        name="moe_dispatch",
    )(plan["gmap"].reshape(-1), plan["totc"], plan["pad_start"], plan["pad_cnt"], plan["ntiles"],
      h2, route, tri, loffv)


def _experts_body(t0_s, n_s, nxt_s, par_s, first_s, nt_s, xs_hbm, wgu_hbm, bgu_ref, wd_hbm, bd_ref, ys_hbm,
                  wg_f, wd_f, wg_b, wd_b, xbuf, ybuf, wsem, xsem, ysem):
    e = pl.program_id(0)
    nt = nt_s[0]
    nt_max = xs_hbm.shape[0] // EXP_TM
    tile_rows = lambda t: pl.ds(pl.multiple_of(t * EXP_TM, EXP_TM), EXP_TM)

    def weight_copies(ex, s, piece):
        rows = pl.ds(pl.multiple_of(piece * W_PIECE_ROWS, W_PIECE_ROWS), W_PIECE_ROWS)
        return (pltpu.make_async_copy(wgu_hbm.at[ex, rows], wg_f.at[s, rows], wsem.at[s, 0]),
                pltpu.make_async_copy(wd_hbm.at[ex, rows], wd_f.at[s, rows], wsem.at[s, 1]))

    def x_copy(t, s):
        return pltpu.make_async_copy(xs_hbm.at[tile_rows(t)], xbuf.at[s], xsem.at[s])

    def y_copy(t, s):
        return pltpu.make_async_copy(ybuf.at[s], ys_hbm.at[tile_rows(t)], ysem.at[s])

    @pl.when(e == 0)
    def _():
        for piece in range(W_PIECES):
            for cp in weight_copies(first_s[0], 0, piece):
                cp.start()
        x_copy(0, 0).start()

    @pl.when(n_s[e] > 0)
    def _():
        s_w = par_s[e]
        for piece in range(W_PIECES):
            for cp in weight_copies(e, s_w, piece):
                cp.wait()
        wg_b[...] = wg_f[s_w].astype(BF16)
        wd_b[...] = wd_f[s_w].astype(BF16)
        has_next = nxt_s[e] >= 0

        def request(piece):
            for cp in weight_copies(nxt_s[e], 1 - s_w, piece):
                cp.start()

        def tile(i, carry):
            t = t0_s[e] + i
            s = t & 1
            x_copy(t, s).wait()

            @pl.when(has_next & (i < W_PIECES))
            def _():
                request(i)

            @pl.when(t + 1 < nt)
            def _():
                x_copy(t + 1, 1 - s).start()

            @pl.when(t >= 2)
            def _():
                y_copy(t - 2, s).wait()

            gu = jnp.dot(xbuf[s].astype(BF16), wg_b[...], preferred_element_type=F32) + bgu_ref[...]
            gate = jnp.minimum(gu[:, :D_FF], SWIGLU_LIMIT)
            up = jnp.clip(gu[:, D_FF:], -SWIGLU_LIMIT, SWIGLU_LIMIT)
            act = (up + 1.0) * gate * _sigmoid(SWIGLU_ALPHA * gate)
            ybuf[s] = jnp.dot(act.astype(BF16), wd_b[...], preferred_element_type=F32) + bd_ref[...]
            y_copy(t, s).start()
            return carry
        lax.fori_loop(0, n_s[e], tile, 0)

        @pl.when(has_next)
        def _():
            def rest(piece, carry):
                request(piece)
                return carry
            lax.fori_loop(jnp.minimum(n_s[e], W_PIECES), W_PIECES, rest, 0)

    @pl.when(e == pl.num_programs(0) - 1)
    def _():
        @pl.when(nt >= 2)
        def _():
            y_copy(nt - 2, nt & 1).wait()
        y_copy(nt - 1, (nt - 1) & 1).wait()
        ybuf[0] = jnp.zeros((EXP_TM, D_MODEL), F32)

        def fill(t, carry):
            y_copy(t, 0).start()
            return carry
        lax.fori_loop(nt, nt_max, fill, 0)

        def drain(t, carry):
            y_copy(0, 0).wait()
            return carry
        lax.fori_loop(nt, nt_max, drain, 0)


def _experts(plan, xs, w_gu, b_gu, w_down, b_down):
    n_rows = xs.shape[0]
    of_expert = lambda e, *_: (e, 0, 0)
    grid_spec = pltpu.PrefetchScalarGridSpec(
        num_scalar_prefetch=6,
        grid=(N_EXPERTS,),
        in_specs=[
            pl.BlockSpec(memory_space=pl.ANY),
            pl.BlockSpec(memory_space=pl.ANY),
            pl.BlockSpec((None, 1, 2 * D_FF), of_expert),
            pl.BlockSpec(memory_space=pl.ANY),
            pl.BlockSpec((None, 1, D_MODEL), of_expert),
        ],
        out_specs=pl.BlockSpec(memory_space=pl.ANY),
        scratch_shapes=[pltpu.VMEM((2, D_MODEL, 2 * D_FF), F32), pltpu.VMEM((2, D_FF, D_MODEL), F32),
                        pltpu.VMEM((D_MODEL, 2 * D_FF), BF16), pltpu.VMEM((D_FF, D_MODEL), BF16),
                        pltpu.VMEM((2, EXP_TM, D_MODEL), F32), pltpu.VMEM((2, EXP_TM, D_MODEL), F32),
                        pltpu.SemaphoreType.DMA((2, 2)), pltpu.SemaphoreType.DMA((2,)),
                        pltpu.SemaphoreType.DMA((2,))],
    )
    return pl.pallas_call(
        _experts_body,
        grid_spec=grid_spec,
        out_shape=jax.ShapeDtypeStruct((n_rows, D_MODEL), F32),
        compiler_params=_params("arbitrary"),
        name="moe_experts",
    )(plan["tile_start"], plan["tile_count"], plan["next_expert"], plan["weight_slot"], plan["first_expert"],
      plan["ntiles"], xs, w_gu, b_gu, w_down, b_down)


def _combine_body(gmap_s, totc_s, route_ref, tril_ref, loffrow_ref, x1_ref, gfin_ref, ys_hbm, y_ref, buf, sem):
    b = pl.program_id(0)
    nb = pl.num_programs(0)
    slot = lax.rem(b, 2)

    def copy_in(slot_, lchunk, gchunk):
        return pltpu.make_async_copy(_chunk_rows(ys_hbm, gchunk), _chunk_rows(buf.at[slot_], lchunk), sem.at[slot_])

    def fetch_block(bb, slot_):
        _for_chunks(totc_s[bb], lambda c: copy_in(slot_, c, gmap_s[bb * BLK_CH + c]).start())

    def wait_block(bb, slot_):
        _wait_chunks(totc_s[bb], lambda rows: pltpu.make_async_copy(
            ys_hbm.at[pl.ds(0, rows)], buf.at[slot_, pl.ds(0, rows)], sem.at[slot_]))

    @pl.when(b == 0)
    def _():
        buf[...] = jnp.zeros_like(buf)
        fetch_block(0, 0)

    @pl.when(b + 1 < nb)
    def _():
        fetch_block(b + 1, 1 - slot)

    wait_block(b, slot)

    route = route_ref[...]
    lane = lax.broadcasted_iota(jnp.int32, (1, LANES), 1).astype(F32)
    sel = jnp.where(lane < float(N_EXPERTS), route, 0.0).astype(BF16)
    rank = jnp.dot(tril_ref[...], sel, preferred_element_type=F32)
    lpos = loffrow_ref[...] * float(ROW_CH) + rank
    lposk, pk = [], []
    for k in range(TOP_K):
        idx = route[:, ROUTE_IDX + k:ROUTE_IDX + k + 1]
        lposk.append(jnp.sum(jnp.where(lane == idx, lpos, 0.0), axis=-1, keepdims=True))
        pk.append(route[:, ROUTE_P + k:ROUTE_P + k + 1])
    acc = x1_ref[...]
    for r0 in range(0, RB, PERM_TM):
        col = lax.broadcasted_iota(jnp.int32, (TBK, PERM_TM), 1).astype(F32) + float(r0)
        w = functools.reduce(jnp.add, [jnp.where(lposk[k] == col, pk[k], 0.0) for k in range(TOP_K)])
        acc = acc + jnp.dot(w.astype(BF16), buf[slot, r0:r0 + PERM_TM, :].astype(BF16), preferred_element_type=F32)
    y_ref[...] = _rms(acc, gfin_ref[...])


def _combine(plan, blocks, route, loffrow, x1, g_final, ys):
    b0, b1 = blocks
    nb = b1 - b0
    tril = jnp.asarray(np.tril(np.ones((TBK, TBK), np.float32), -1), BF16)
    grid_spec = pltpu.PrefetchScalarGridSpec(
        num_scalar_prefetch=2,
        grid=(nb,),
        in_specs=[
            pl.BlockSpec((TBK, LANES), lambda b, *_: (b + b0, 0)),
            pl.BlockSpec((TBK, TBK), lambda b, *_: (0, 0)),
            pl.BlockSpec((None, 1, LANES), lambda b, *_: (b + b0, 0, 0)),
            pl.BlockSpec((TBK, D_MODEL), lambda b, *_: (b + b0, 0)),
            pl.BlockSpec((1, D_MODEL), lambda b, *_: (0, 0)),
            pl.BlockSpec(memory_space=pl.ANY),
        ],
        out_specs=pl.BlockSpec((TBK, D_MODEL), lambda b, *_: (b, 0)),
        scratch_shapes=[pltpu.VMEM((2, RB, D_MODEL), F32), pltpu.SemaphoreType.DMA((2,))],
    )
    return pl.pallas_call(
        _combine_body,
        grid_spec=grid_spec,
        out_shape=jax.ShapeDtypeStruct((nb * TBK, D_MODEL), F32),
        compiler_params=_params("arbitrary"),
        name="moe_combine",
    )(plan["gmap"][b0:b1].reshape(-1), plan["totc"][b0:b1], route, tril, loffrow, x1, g_final, ys)


def kernel(x_prompt, x_sample, cache_k, cache_v, state_s, g_mix, w_in, rel_bias, lb_logits, g_out_norm,
           w_pa, w_pb, w_out, g_ffn, w_router, b_router, w_gu, b_gu, w_down, b_down, g_final):
    B, T = x_prompt.shape[:2]
    DB, S = x_sample.shape[:2]
    depth = w_in.shape[0]
    assert depth == 1 and T % ATT_QBLK == 0 and S == CHUNK
    cw = cache_k.shape[2]
    assert cw == WINDOW
    l = 0

    lower = jnp.cumsum(jax.nn.softmax(lb_logits.astype(F32), axis=0), axis=0)[l].reshape(1, HG_WIDTH)
    w_in_b = w_in[l].astype(BF16)
    wpa, wpb, wout = w_pa[l].astype(BF16), w_pb[l].astype(BF16), w_out[l].astype(BF16)
    row = lambda a: a.reshape(1, -1).astype(F32)
    base = _rel_bias_base(rel_bias[l])
    b_gu3 = b_gu[l].reshape(N_EXPERTS, 1, 2 * D_FF)
    b_down3 = b_down[l].reshape(N_EXPERTS, 1, D_MODEL)
    pad_e = LANES - N_EXPERTS
    wr = jnp.pad(w_router[l].astype(F32), ((0, 0), (0, pad_e)))
    br = jnp.concatenate([b_router[l].astype(F32), jnp.full((pad_e,), NEG, F32)]).reshape(1, LANES)

    n_tok = B * T + DB * S
    nb, nbp = n_tok // TBK, (B * T) // TBK

    def front(x, batch, seq, s0, attend):
        xf = x.reshape(batch * seq, D_MODEL)
        za, zb, zg = _inproj(xf, row(g_mix[l]), w_in_b)
        att = attend(za)
        hg, s_fin = _hgrn(zb, s0, lower, row(g_out_norm[l]), batch, seq)
        za3 = za.reshape(batch, seq, 3 * ATT_WIDTH)
        heads = lambda a: a.reshape(1, batch, a.shape[1], ATT_HEADS, ATT_DIM)
        keep = min(WINDOW, seq)
        nk = heads(za3[:, seq - keep:, ATT_WIDTH:2 * ATT_WIDTH])
        nv = heads(za3[:, seq - keep:, 2 * ATT_WIDTH:])
        return dict(mix=(att, hg, zg, xf), nk=nk, nv=nv, s=s_fin[None])

    ck = cache_k[l].reshape(DB, cw, ATT_WIDTH)
    cv = cache_v[l].reshape(DB, cw, ATT_WIDTH)
    fp = front(x_prompt, B, T, jnp.zeros((B, HG_HEADS, HG_DK, HG_DK), F32), lambda za: _attn_prompt(za, base, B, T))
    fs = front(x_sample, DB, S, state_s[l].astype(F32), lambda za: _attn_sample(za, ck, cv, base, DB, S))

    x1, h2, route, cnt = _merge(fp["mix"], fs["mix"], wpa, wpb, wout, row(g_ffn[l]), wr, br)
    cnt = cnt[:, 0, :N_EXPERTS].astype(jnp.int32)
    max_rows = n_tok * TOP_K + nb * N_EXPERTS * (ROW_CH - 1) + N_EXPERTS * (EXP_TM - 1)
    nt_max = -(-max_rows // EXP_TM)
    plan = _route_plan(cnt)
    loff_f = plan["loff"].astype(F32)
    xs = _dispatch(plan, h2, route, loff_f[:, :, None], nt_max * EXP_TM)
    ysort = _experts(plan, xs, w_gu[l], b_gu3, w_down[l], b_down3)
    loffrow = jnp.pad(loff_f, ((0, 0), (0, pad_e)))[:, None, :]
    yp = _combine(plan, (0, nbp), route, loffrow, x1, row(g_final), ysort)
    ys = _combine(plan, (nbp, nb), route, loffrow, x1, row(g_final), ysort)
    return (yp.reshape(B, T, D_MODEL), ys.reshape(DB, S, D_MODEL), fp["nk"], fp["nv"], fp["s"],
            fs["nk"], fs["nv"], fs["s"])
```

```python
import functools

import numpy as np
import jax
import jax.numpy as jnp
from jax import lax
from jax.experimental import pallas as pl
from jax.experimental.pallas import tpu as pltpu

F32 = jnp.float32
BF16 = jnp.bfloat16

D_MODEL = 1024
CHUNK = 64
LEFT_CHUNKS = 8
WINDOW = LEFT_CHUNKS * CHUNK
ATT_HEADS = 8
ATT_DIM = 64
ATT_WIDTH = ATT_HEADS * ATT_DIM
MAX_REL = 256
HG_HEADS = 4
HG_DK = 128
HG_WIDTH = HG_HEADS * HG_DK
N_EXPERTS = 32
TOP_K = 4
D_FF = D_MODEL
SWIGLU_LIMIT = 7.0
SWIGLU_ALPHA = 1.702
RMS_EPS = 1e-5

LANES = 128
NEG = -1e30
LOG2E = 1.4426950408889634
ATT_QBLK = 4 * CHUNK
ATT_KBLKS = LEFT_CHUNKS * CHUNK // ATT_QBLK + 1
ATT_PAIRS_PER_STAGE = 2
HG_C = 128
HG_CHUNKS_PER_STEP = 4
VMEM_LIMIT = 56 * 1024 * 1024
BIAS_W = 1024
SUBLANES = 8
TBK = 256
ROW_CH = SUBLANES
RB = TBK * TOP_K + N_EXPERTS * ROW_CH
MERGE_TM = 2 * TBK
PERM_TM = 256
EXP_TM = 512
W_PIECES = 4
W_PIECE_ROWS = D_MODEL // W_PIECES
CH_PER_TILE = EXP_TM // ROW_CH
BLK_CH = RB // ROW_CH
ROUTE_IDX = 64
ROUTE_P = 72

NT = (((1,), (1,)), ((), ()))
TN = (((0,), (0,)), ((), ()))


def _rms(x, g):
    return x * lax.rsqrt(jnp.mean(x * x, axis=-1, keepdims=True) + RMS_EPS) * g


def _sigmoid(x):
    return 1.0 / (1.0 + jnp.exp(-x))


def _params(*sem):
    return pltpu.CompilerParams(dimension_semantics=sem, vmem_limit_bytes=VMEM_LIMIT)


def _inproj_body(x_ref, g_ref, w_ref, za_ref, zb_ref, zg_ref):
    h = _rms(x_ref[...], g_ref[...]).astype(BF16)
    a, b = 3 * ATT_WIDTH, 3 * ATT_WIDTH + 4 * HG_WIDTH
    za_ref[...] = jnp.dot(h, w_ref[:, :a], preferred_element_type=F32)
    zb_ref[...] = jnp.dot(h, w_ref[:, a:b], preferred_element_type=F32)
    zg_ref[...] = jnp.dot(h, w_ref[:, b:], preferred_element_type=F32)


def _inproj(x, g, w_bf16, tm=512):
    n = x.shape[0]
    cols = w_bf16.shape[1]
    wa, wb, wg = 3 * ATT_WIDTH, 4 * HG_WIDTH, 2 * D_MODEL
    return pl.pallas_call(
        _inproj_body,
        grid=(n // tm,),
        in_specs=[
            pl.BlockSpec((tm, D_MODEL), lambda i: (i, 0)),
            pl.BlockSpec((1, D_MODEL), lambda i: (0, 0)),
            pl.BlockSpec((D_MODEL, cols), lambda i: (0, 0)),
        ],
        out_specs=[
            pl.BlockSpec((tm, wa), lambda i: (i, 0)),
            pl.BlockSpec((tm, wb), lambda i: (i, 0)),
            pl.BlockSpec((tm, wg), lambda i: (i, 0)),
        ],
        out_shape=[
            jax.ShapeDtypeStruct((n, wa), F32),
            jax.ShapeDtypeStruct((n, wb), F32),
            jax.ShapeDtypeStruct((n, wg), F32),
        ],
        compiler_params=_params("arbitrary"),
        name="inproj",
    )(x, g, w_bf16)


def _attn_heads(q_ref, k_refs, v_refs, bias_fn, pens, o_ref):
    lane = lax.broadcasted_iota(jnp.int32, (1, LANES), 1)
    first = lane < ATT_DIM
    halves = (first, lane >= ATT_DIM)
    for hp0 in range(0, ATT_HEADS // 2, ATT_PAIRS_PER_STAGE):
        pairs = range(hp0, hp0 + ATT_PAIRS_PER_STAGE)
        sl = {hp: slice(hp * LANES, (hp + 1) * LANES) for hp in pairs}
        scores = {}
        for hp in pairs:
            q2 = q_ref[:, sl[hp]] * (ATT_DIM ** -0.5 * LOG2E)
            ks = [k[:, sl[hp]].astype(BF16) for k in k_refs]
            for half, mine in enumerate(halves):
                qm = jnp.where(mine, q2, 0.0).astype(BF16)
                ss = []
                for j, kj in enumerate(ks):
                    s = lax.dot_general(qm, kj, NT, preferred_element_type=F32) + bias_fn(2 * hp + half, j)
                    if pens[j] is not None:
                        s = s + pens[j]
                    ss.append(s)
                scores[hp, half] = ss
        for hp in pairs:
            outs = []
            for half, mine in enumerate(halves):
                ss = scores[hp, half]
                vs = [jnp.where(mine, v[:, sl[hp]], 1.0).astype(BF16) for v in v_refs]
                if all(s.shape == ss[0].shape for s in ss):
                    m = jnp.max(functools.reduce(jnp.maximum, ss), axis=-1, keepdims=True)
                else:
                    m = functools.reduce(jnp.maximum, [jnp.max(s, axis=-1, keepdims=True) for s in ss])
                outs.append(functools.reduce(jnp.add, [
                    jnp.dot(jnp.exp2(s - m).astype(BF16), vj, preferred_element_type=F32) for s, vj in zip(ss, vs)]))
            num = jnp.where(first, outs[0], outs[1])
            den = pltpu.roll(jnp.where(first, outs[1], outs[0]), ATT_DIM, 1)
            o_ref[:, sl[hp]] = (num * (1.0 / den)).astype(o_ref.dtype)


def _fill_bias(base_ref, bias_ref, banded):
    nq, nk = bias_ref.shape[1:]
    if banded:
        r = lax.broadcasted_iota(jnp.int32, (nq, nk), 0)
        s = lax.broadcasted_iota(jnp.int32, (nq, nk), 1)
        qc = (r + WINDOW) // CHUNK
        kc = s // CHUNK
        pen = jnp.where(kc <= qc, jnp.where(kc >= qc - LEFT_CHUNKS, 0.0, NEG), NEG)
    for h in range(ATT_HEADS):
        rows = jnp.broadcast_to(base_ref[h:h + 1, :], (nq, BIAS_W))
        t = pltpu.roll(rows, 0, 1, stride=1, stride_axis=0)[:, :nk] * LOG2E
        bias_ref[h] = t + pen if banded else t


def _attn_prompt_body(q_ref, k0, k1, k2, v0, v1, v2, base_ref, o_ref, bias_ref):
    i = pl.program_id(1)

    @pl.when((pl.program_id(0) == 0) & (i == 0))
    def _():
        _fill_bias(base_ref, bias_ref, True)

    bias_fn = lambda h, j: bias_ref[h, :, j * ATT_QBLK:(j + 1) * ATT_QBLK]
    back = ATT_KBLKS - 1

    @pl.when(i < back)
    def _():
        pens = [jnp.where(i - back + j >= 0, 0.0, NEG) for j in range(back)] + [None]
        _attn_heads(q_ref, [k0, k1, k2], [v0, v1, v2], bias_fn, pens, o_ref)

    @pl.when(i >= back)
    def _():
        _attn_heads(q_ref, [k0, k1, k2], [v0, v1, v2], bias_fn, [None] * ATT_KBLKS, o_ref)


def _attn_prompt(za, base, batch, seq):
    nq = seq // ATT_QBLK
    back = ATT_KBLKS - 1
    qspec = pl.BlockSpec((ATT_QBLK, ATT_WIDTH), lambda b, i: (b * nq + i, 0))

    def kvspec(j, col):
        return pl.BlockSpec((ATT_QBLK, ATT_WIDTH),
                            lambda b, i: (b * nq + jnp.maximum(i - back + j, 0), col))

    return pl.pallas_call(
        _attn_prompt_body,
        grid=(batch, nq),
        in_specs=[qspec] + [kvspec(j, 1) for j in range(ATT_KBLKS)] + [kvspec(j, 2) for j in range(ATT_KBLKS)]
        + [pl.BlockSpec(base.shape, lambda b, i: (0, 0))],
        out_specs=pl.BlockSpec((ATT_QBLK, ATT_WIDTH), lambda b, i: (b * nq + i, 0)),
        out_shape=jax.ShapeDtypeStruct((batch * seq, ATT_WIDTH), BF16),
        scratch_shapes=[pltpu.VMEM((ATT_HEADS, ATT_QBLK, ATT_KBLKS * ATT_QBLK), F32)],
        compiler_params=_params("arbitrary", "arbitrary"),
        name="attn_prompt",
    )(za, za, za, za, za, za, za, base)


def _attn_sample_body(q_ref, kn_ref, vn_ref, ck_ref, cv_ref, base_ref, o_ref, bias_ref):
    @pl.when(pl.program_id(0) == 0)
    def _():
        _fill_bias(base_ref, bias_ref, False)

    cw = ck_ref.shape[0]
    bias_fn = lambda h, j: bias_ref[h, :, :cw] if j == 0 else bias_ref[h, :, cw:]
    _attn_heads(q_ref, [ck_ref, kn_ref], [cv_ref, vn_ref], bias_fn, [None, None], o_ref)


def _attn_sample(za, ck, cv, base, batch, seq):
    cw = ck.shape[1]
    return pl.pallas_call(
        _attn_sample_body,
        grid=(batch,),
        in_specs=[
            pl.BlockSpec((seq, ATT_WIDTH), lambda b: (b, 0)),
            pl.BlockSpec((seq, ATT_WIDTH), lambda b: (b, 1)),
            pl.BlockSpec((seq, ATT_WIDTH), lambda b: (b, 2)),
            pl.BlockSpec((None, cw, ATT_WIDTH), lambda b: (b, 0, 0)),
            pl.BlockSpec((None, cw, ATT_WIDTH), lambda b: (b, 0, 0)),
            pl.BlockSpec(base.shape, lambda b: (0, 0)),
        ],
        out_specs=pl.BlockSpec((seq, ATT_WIDTH), lambda b: (b, 0)),
        out_shape=jax.ShapeDtypeStruct((batch * seq, ATT_WIDTH), BF16),
        scratch_shapes=[pltpu.VMEM((ATT_HEADS, seq, cw + seq), F32)],
        compiler_params=_params("arbitrary"),
        name="attn_sample",
    )(za, za, za, ck, cv, base)


def _rel_bias_base(table):
    top = table[:, 2 * MAX_REL:].astype(F32)
    rev = table[:, ::-1][:, :2 * MAX_REL].astype(F32)
    left = WINDOW - MAX_REL
    return jnp.concatenate([jnp.broadcast_to(top, (ATT_HEADS, left)), rev,
                            jnp.broadcast_to(top, (ATT_HEADS, BIAS_W - left - 2 * MAX_REL))], axis=1)


def _hgrn_consts(c):
    t = np.arange(c)[:, None]
    j = np.arange(c)[None, :]
    mats = [j <= t, j > t]
    masks = []
    m = c // 2
    while m >= 1:
        ref = (t // (2 * m)) * (2 * m) + m - 1
        second = (t % (2 * m)) >= m
        if m < SUBLANES:
            mats.append((second & (j > ref) & (j <= t)) | (~second & (j > t) & (j <= ref)))
        masks.append((t // (2 * m)) == (j // (2 * m)))
        m //= 2
    return (jnp.asarray(np.concatenate(mats, 0).astype(np.float32), BF16),
            jnp.asarray(np.stack(masks).astype(np.float32)))


def _hgrn_body(zb_ref, s0_ref, lower_ref, gon_ref, p_ref, mask_ref, o_ref, sfin_ref, st_ref, *, single_step):
    c = p_ref.shape[1]
    step = pl.program_id(1)

    def load_state():
        for h in range(HG_HEADS):
            st_ref[h] = s0_ref[0, h].T

    if single_step:
        load_state()
    else:
        pl.when(step == 0)(load_state)

    pmat = p_ref[...]
    n_levels = mask_ref.shape[0]
    head = lambda a, h: a[:, h * HG_DK:(h + 1) * HG_DK]
    low = lower_ref[...]
    row = lax.broadcasted_iota(jnp.int32, (c, HG_WIDTH), 0)

    def stage1(r0):
        part = lambda i: zb_ref[r0:r0 + c, i * HG_WIDTH:(i + 1) * HG_WIDTH]
        q = part(0)
        f = low + (1.0 - low) * _sigmoid(part(1))
        lf = jnp.log(f)
        k = 1.0 - f
        ib = part(2)
        v = ib * _sigmoid(ib)
        og = part(3)
        hi = lf.astype(BF16)
        r1 = lf - hi.astype(F32)
        mid = r1.astype(BF16)
        lo = (r1 - mid.astype(F32)).astype(BF16)
        e = (jnp.dot(pmat, hi, preferred_element_type=F32) + jnp.dot(pmat, mid, preferred_element_type=F32)
             + jnp.dot(pmat, lo, preferred_element_type=F32))
        b = e[0:c]
        return dict(q=q, k=k, v=v, e=e, b=b, decay=jnp.exp(e[c - 1:c]), qe=(q * jnp.exp(b)).astype(BF16),
                    kt=(k * jnp.exp(e[c:2 * c])).astype(BF16), vb=v.astype(BF16), qk=q * k,
                    gate=og * _sigmoid(og))

    def stage2(s):
        q, k, e, b = s["q"], s["k"], s["e"], s["b"]
        att = [None] * HG_HEADS
        n_rows_p = 2
        for lvl in range(n_levels):
            m = c >> (lvl + 1)
            if m >= SUBLANES:
                ref = [jnp.broadcast_to(b[p * 2 * m + m - 1:p * 2 * m + m], (2 * m, HG_WIDTH))
                       for p in range(c // (2 * m))]
                x = jnp.exp(-jnp.abs(b - (jnp.concatenate(ref, axis=0) if len(ref) > 1 else ref[0])))
            else:
                x = jnp.exp(e[n_rows_p * c:(n_rows_p + 1) * c])
                n_rows_p += 1
            second = (row & m) != 0
            qm = jnp.where(second, q * x, 0.0).astype(BF16)
            km = jnp.where(second, 0.0, k * x).astype(BF16)
            for h in range(HG_HEADS):
                a = lax.dot_general(head(qm, h), head(km, h), NT, preferred_element_type=F32)
                if lvl > 0:
                    a = a * mask_ref[lvl]
                att[h] = a if att[h] is None else att[h] + a
        return [jnp.dot(att[h].astype(BF16), head(s["vb"], h), preferred_element_type=F32)
                + jnp.sum(head(s["qk"], h), axis=-1, keepdims=True) * head(s["v"], h) for h in range(HG_HEADS)]

    def stage3(r0, s, intra):
        for h in range(HG_HEADS):
            st = st_ref[h]
            inter = lax.dot_general(head(s["qe"], h), st.astype(BF16), NT, preferred_element_type=F32)
            st_ref[h] = st * head(s["decay"], h) + lax.dot_general(head(s["vb"], h), head(s["kt"], h), TN,
                                                                  preferred_element_type=F32)
            o = _rms(inter + intra[h], gon_ref[...]) * head(s["gate"], h)
            o_ref[r0:r0 + c, h * HG_DK:(h + 1) * HG_DK] = o.astype(o_ref.dtype)

    starts = range(0, zb_ref.shape[0], c)
    firsts = [stage1(r0) for r0 in starts]
    intras = [stage2(s) for s in firsts]
    for r0, s, intra in zip(starts, firsts, intras):
        stage3(r0, s, intra)

    def write_state():
        for h in range(HG_HEADS):
            sfin_ref[0, h] = st_ref[h].T

    if single_step:
        write_state()
    else:
        pl.when(step == pl.num_programs(1) - 1)(write_state)


def _hgrn(zb, s0, lower, g_on, batch, seq):
    c = min(HG_C, seq)
    rows = min(HG_CHUNKS_PER_STEP * c, seq)
    assert seq % rows == 0 and rows % c == 0
    pmat, masks = _hgrn_consts(c)
    nc = seq // rows
    return pl.pallas_call(
        functools.partial(_hgrn_body, single_step=nc == 1),
        grid=(batch, nc),
        in_specs=[
            pl.BlockSpec((rows, 4 * HG_WIDTH), lambda b, i: (b * nc + i, 0)),
            pl.BlockSpec((1, HG_HEADS, HG_DK, HG_DK), lambda b, i: (b, 0, 0, 0)),
            pl.BlockSpec((1, HG_WIDTH), lambda b, i: (0, 0)),
            pl.BlockSpec((1, HG_DK), lambda b, i: (0, 0)),
            pl.BlockSpec(pmat.shape, lambda b, i: (0, 0)),
            pl.BlockSpec(masks.shape, lambda b, i: (0, 0, 0)),
        ],
        out_specs=[
            pl.BlockSpec((rows, HG_WIDTH), lambda b, i: (b * nc + i, 0)),
            pl.BlockSpec((1, HG_HEADS, HG_DK, HG_DK), lambda b, i: (b, 0, 0, 0)),
        ],
        out_shape=[
            jax.ShapeDtypeStruct((batch * seq, HG_WIDTH), BF16),
            jax.ShapeDtypeStruct((batch, HG_HEADS, HG_DK, HG_DK), F32),
        ],
        scratch_shapes=[pltpu.VMEM((HG_HEADS, HG_DK, HG_DK), F32)],
        compiler_params=_params("arbitrary", "arbitrary"),
        name="hgrn2",
    )(zb, s0, lower, g_on, pmat, masks)


def _split_bf16(x):
    hi = x.astype(BF16)
    return hi, (x - hi.astype(F32)).astype(BF16)


def _merge_body(att_p, hg_p, zg_p, x_p, att_s, hg_s, zg_s, x_s, wpa_ref, wpb_ref, wout_ref, gffn_ref, wr_ref, br_ref,
                x1_ref, h2_ref, route_ref, cnt_ref, *, n_first):
    weights = (wpa_ref, wpb_ref, wout_ref, gffn_ref, wr_ref, br_ref)
    outs = (x1_ref, h2_ref, route_ref, cnt_ref)
    i = pl.program_id(0)
    pl.when(i < n_first)(functools.partial(_merge_block, att_p, hg_p, zg_p, x_p, *weights, *outs))
    pl.when(i >= n_first)(functools.partial(_merge_block, att_s, hg_s, zg_s, x_s, *weights, *outs))


def _merge_block(att_ref, hg_ref, zg_ref, x_ref, wpa_ref, wpb_ref, wout_ref, gffn_ref, wr_ref, br_ref,
                 x1_ref, h2_ref, route_ref, cnt_ref):
    pa = jnp.dot(att_ref[...].astype(BF16), wpa_ref[...], preferred_element_type=F32)
    pb = jnp.dot(hg_ref[...].astype(BF16), wpb_ref[...], preferred_element_type=F32)
    y = _sigmoid(zg_ref[:, :D_MODEL]) * pa + _sigmoid(zg_ref[:, D_MODEL:]) * pb
    x1 = x_ref[...] + jnp.dot(y.astype(BF16), wout_ref[...], preferred_element_type=F32)
    x1_ref[...] = x1
    h2 = _rms(x1, gffn_ref[...])
    h2_ref[...] = h2.astype(BF16)

    h_hi, h_lo = _split_bf16(h2)
    w_hi, w_lo = _split_bf16(wr_ref[...])
    logits = (jnp.dot(h_hi, w_hi, preferred_element_type=F32) + jnp.dot(h_lo, w_hi, preferred_element_type=F32)
              + jnp.dot(h_hi, w_lo, preferred_element_type=F32)) + br_ref[...]
    lane = lax.broadcasted_iota(jnp.int32, logits.shape, 1).astype(F32)
    cur = logits
    vals, idxs = [], []
    for _ in range(TOP_K):
        m = jnp.max(cur, axis=-1, keepdims=True)
        idx = jnp.min(jnp.where(cur == m, lane, float(LANES)), axis=-1, keepdims=True)
        vals.append(m)
        idxs.append(idx)
        cur = jnp.where(lane == idx, -jnp.inf, cur)
    es = [jnp.exp(v - vals[0]) for v in vals]
    inv = 1.0 / functools.reduce(jnp.add, es)
    route = jnp.zeros_like(logits)
    for k, (ex, idx) in enumerate(zip(es, idxs)):
        route = (route + jnp.where(lane == idx, 1.0, 0.0) + jnp.where(lane == float(ROUTE_IDX + k), idx, 0.0)
                 + jnp.where(lane == float(ROUTE_P + k), ex * inv, 0.0))
    route_ref[...] = route
    sel = jnp.where(lane < float(N_EXPERTS), route, 0.0)
    for blk in range(cnt_ref.shape[0]):
        cnt_ref[blk] = jnp.sum(sel[blk * TBK:(blk + 1) * TBK], axis=0, keepdims=True)


def _merge(first, second, wpa, wpb, wout, g_ffn, w_router, b_router):
    tm = MERGE_TM
    assert first[3].shape[0] % tm == 0 and second[3].shape[0] % tm == 0
    n1, n2 = first[3].shape[0] // tm, second[3].shape[0] // tm
    widths = (ATT_WIDTH, HG_WIDTH, 2 * D_MODEL, D_MODEL)
    spec1 = [pl.BlockSpec((tm, w), lambda i: (jnp.minimum(i, n1 - 1), 0)) for w in widths]
    spec2 = [pl.BlockSpec((tm, w), lambda i: (jnp.maximum(i - n1, 0), 0)) for w in widths]
    row = lambda w: pl.BlockSpec((tm, w), lambda i: (i, 0))
    full = lambda a: pl.BlockSpec(a.shape, lambda i: (0,) * a.ndim)
    n = (n1 + n2) * tm
    return pl.pallas_call(
        functools.partial(_merge_body, n_first=n1),
        grid=(n1 + n2,),
        in_specs=spec1 + spec2 + [full(wpa), full(wpb), full(wout), full(g_ffn), full(w_router), full(b_router)],
        out_specs=[row(D_MODEL), row(D_MODEL), row(LANES),
                   pl.BlockSpec((tm // TBK, 1, LANES), lambda i: (i, 0, 0))],
        out_shape=[
            jax.ShapeDtypeStruct((n, D_MODEL), F32),
            jax.ShapeDtypeStruct((n, D_MODEL), BF16),
            jax.ShapeDtypeStruct((n, LANES), F32),
            jax.ShapeDtypeStruct((n // TBK, 1, LANES), F32),
        ],
        compiler_params=_params("arbitrary"),
        name="merge_router",
    )(*first, *second, wpa, wpb, wout, g_ffn, w_router, b_router)


def _route_plan(cnt):
    pc = (cnt + ROW_CH - 1) // ROW_CH
    loff = jnp.cumsum(pc, axis=1) - pc
    tot = jnp.sum(pc, axis=0)
    reg = (tot + CH_PER_TILE - 1) // CH_PER_TILE * CH_PER_TILE
    gstart = jnp.cumsum(reg) - reg
    goff = gstart[None, :] + jnp.cumsum(pc, axis=0) - pc
    ntiles = jnp.sum(reg) // CH_PER_TILE
    present = reg > 0
    ids = jnp.arange(N_EXPERTS, dtype=jnp.int32)
    later = (ids[None, :] > ids[:, None]) & present[None, :]
    nxt = jnp.min(jnp.where(later, ids[None, :], N_EXPERTS), axis=1)
    nxt = jnp.where(nxt == N_EXPERTS, -1, nxt)
    slot = (jnp.cumsum(present.astype(jnp.int32)) - 1) % 2
    first = jnp.min(jnp.where(present, ids, N_EXPERTS)).reshape(1)
    j = jnp.arange(BLK_CH, dtype=jnp.int32)
    run = jnp.sum(((loff + pc)[:, None, :] <= j[None, :, None]).astype(jnp.int32), axis=2)
    shift = jnp.sum(jnp.where(run[:, :, None] == ids[None, None, :], (goff - loff)[:, None, :], 0), axis=2)
    gmap = shift + j[None, :]
    i32 = lambda a: a.astype(jnp.int32)
    return dict(loff=i32(loff), gmap=i32(gmap), totc=i32(jnp.sum(pc, axis=1)),
                pad_start=i32(gstart + tot), pad_cnt=i32(reg - tot), ntiles=i32(ntiles).reshape(1),
                tile_start=i32(gstart // CH_PER_TILE), tile_count=i32(reg // CH_PER_TILE),
                next_expert=i32(nxt), weight_slot=i32(slot), first_expert=i32(first))


def _chunk_rows(ref, chunk):
    return ref.at[pl.ds(pl.multiple_of(chunk * ROW_CH, ROW_CH), ROW_CH)]


def _for_chunks(n, do):
    log_unroll = 2
    groups = lax.shift_right_logical(n, log_unroll)

    def group(i, carry):
        for u in range(1 << log_unroll):
            do(lax.shift_left(i, log_unroll) + u)
        return carry
    lax.fori_loop(0, groups, group, 0)

    def single(c, carry):
        do(c)
        return carry
    lax.fori_loop(lax.shift_left(groups, log_unroll), n, single, 0)


def _wait_chunks(n, copy_of_rows):
    for bit in range((RB // ROW_CH).bit_length()):
        @pl.when(((n >> bit) & 1) == 1)
        def _(bit=bit):
            copy_of_rows((1 << bit) * ROW_CH).wait()


def _dispatch_body(gmap_s, totc_s, pads_s, padn_s, nt_s, h2_ref, route_ref, tri_ref, loffv_ref,
                   xs_hbm, buf, zbuf, sem):
    b = pl.program_id(0)
    nb = pl.num_programs(0)
    slot = lax.rem(b, 2)
    fill_sem, tile_sem = 2, 3
    nt_max = xs_hbm.shape[0] // EXP_TM

    def copy_out(src, gchunk, sem_i):
        return pltpu.make_async_copy(src, _chunk_rows(xs_hbm, gchunk), sem.at[sem_i])

    def zero_tile(t):
        return pltpu.make_async_copy(zbuf, xs_hbm.at[pl.ds(pl.multiple_of(t * EXP_TM, EXP_TM), EXP_TM)],
                                     sem.at[tile_sem])

    def wait_block(bb, slot_):
        _wait_chunks(totc_s[bb], lambda rows: pltpu.make_async_copy(
            buf.at[slot_, pl.ds(0, rows)], xs_hbm.at[pl.ds(0, rows)], sem.at[slot_]))

    @pl.when(b == 0)
    def _():
        zbuf[...] = jnp.zeros_like(zbuf)

    @pl.when(b >= 2)
    def _():
        wait_block(b - 2, slot)

    rt = route_ref[...].T
    rank_t = jnp.dot(rt[0:N_EXPERTS].astype(BF16), tri_ref[...], preferred_element_type=F32)
    lpos_t = loffv_ref[...] * float(ROW_CH) + rank_t
    erow = lax.broadcasted_iota(jnp.int32, (N_EXPERTS, TBK), 0).astype(F32)
    lposk = [jnp.sum(jnp.where(rt[ROUTE_IDX + k:ROUTE_IDX + k + 1] == erow, lpos_t, 0.0), axis=0, keepdims=True)
             for k in range(TOP_K)]
    h2 = h2_ref[...]
    for r0 in range(0, RB, PERM_TM):
        rrow = lax.broadcasted_iota(jnp.int32, (PERM_TM, TBK), 0).astype(F32) + float(r0)
        perm = functools.reduce(jnp.add, [jnp.where(lposk[k] == rrow, 1.0, 0.0) for k in range(TOP_K)])
        buf[slot, r0:r0 + PERM_TM, :] = jnp.dot(perm.astype(BF16), h2, preferred_element_type=F32)

    _for_chunks(totc_s[b], lambda c: copy_out(_chunk_rows(buf.at[slot], c), gmap_s[b * BLK_CH + c], slot).start())

    @pl.when(b == nb - 1)
    def _():
        zchunk = _chunk_rows(zbuf, 0)

        def fill_expert(e, carry):
            def one(c, carry2):
                copy_out(zchunk, pads_s[e] + c, fill_sem).start()
                return carry2
            lax.fori_loop(0, padn_s[e], one, 0)
            return carry
        lax.fori_loop(0, N_EXPERTS, fill_expert, 0)

        def fill_tile(t, carry):
            zero_tile(t).start()
            return carry
        lax.fori_loop(nt_s[0], nt_max, fill_tile, 0)

        @pl.when(b >= 1)
        def _():
            wait_block(b - 1, 1 - slot)
        wait_block(b, slot)

        def drain_expert(e, carry):
            def one(c, carry2):
                copy_out(zchunk, 0, fill_sem).wait()
                return carry2
            lax.fori_loop(0, padn_s[e], one, 0)
            return carry
        lax.fori_loop(0, N_EXPERTS, drain_expert, 0)

        def drain_tile(t, carry):
            zero_tile(0).wait()
            return carry
        lax.fori_loop(nt_s[0], nt_max, drain_tile, 0)


def _dispatch(plan, h2, route, loffv, n_rows):
    nb = h2.shape[0] // TBK
    tri = jnp.asarray(np.triu(np.ones((TBK, TBK), np.float32), 1), BF16)
    grid_spec = pltpu.PrefetchScalarGridSpec(
        num_scalar_prefetch=5,
        grid=(nb,),
        in_specs=[
            pl.BlockSpec((TBK, D_MODEL), lambda b, *_: (b, 0)),
            pl.BlockSpec((TBK, LANES), lambda b, *_: (b, 0)),
            pl.BlockSpec((TBK, TBK), lambda b, *_: (0, 0)),
            pl.BlockSpec((None, N_EXPERTS, 1), lambda b, *_: (b, 0, 0)),
        ],
        out_specs=pl.BlockSpec(memory_space=pl.ANY),
        scratch_shapes=[pltpu.VMEM((2, RB, D_MODEL), F32), pltpu.VMEM((EXP_TM, D_MODEL), F32),
                        pltpu.SemaphoreType.DMA((4,))],
    )
    return pl.pallas_call(
        _dispatch_body,
        grid_spec=grid_spec,
        out_shape=jax.ShapeDtypeStruct((n_rows, D_MODEL), F32),
        compiler_params=_params("arbitrary"),
        name="moe_dispatch",
    )(plan["gmap"].reshape(-1), plan["totc"], plan["pad_start"], plan["pad_cnt"], plan["ntiles"],
      h2, route, tri, loffv)


def _experts_body(t0_s, n_s, nxt_s, par_s, first_s, nt_s, xs_hbm, wgu_hbm, bgu_ref, wd_hbm, bd_ref, ys_hbm,
                  wg_f, wd_f, wg_b, wd_b, xbuf, ybuf, wsem, xsem, ysem):
    e = pl.program_id(0)
    nt = nt_s[0]
    nt_max = xs_hbm.shape[0] // EXP_TM
    tile_rows = lambda t: pl.ds(pl.multiple_of(t * EXP_TM, EXP_TM), EXP_TM)

    def weight_copies(ex, s, piece):
        rows = pl.ds(pl.multiple_of(piece * W_PIECE_ROWS, W_PIECE_ROWS), W_PIECE_ROWS)
        return (pltpu.make_async_copy(wgu_hbm.at[ex, rows], wg_f.at[s, rows], wsem.at[s, 0]),
                pltpu.make_async_copy(wd_hbm.at[ex, rows], wd_f.at[s, rows], wsem.at[s, 1]))

    def x_copy(t, s):
        return pltpu.make_async_copy(xs_hbm.at[tile_rows(t)], xbuf.at[s], xsem.at[s])

    def y_copy(t, s):
        return pltpu.make_async_copy(ybuf.at[s], ys_hbm.at[tile_rows(t)], ysem.at[s])

    @pl.when(e == 0)
    def _():
        for piece in range(W_PIECES):
            for cp in weight_copies(first_s[0], 0, piece):
                cp.start()
        x_copy(0, 0).start()

    @pl.when(n_s[e] > 0)
    def _():
        s_w = par_s[e]
        for piece in range(W_PIECES):
            for cp in weight_copies(e, s_w, piece):
                cp.wait()
        wg_b[...] = wg_f[s_w].astype(BF16)
        wd_b[...] = wd_f[s_w].astype(BF16)
        has_next = nxt_s[e] >= 0

        def request(piece):
            for cp in weight_copies(nxt_s[e], 1 - s_w, piece):
                cp.start()

        def tile(i, carry):
            t = t0_s[e] + i
            s = t & 1
            x_copy(t, s).wait()

            @pl.when(has_next & (i < W_PIECES))
            def _():
                request(i)

            @pl.when(t + 1 < nt)
            def _():
                x_copy(t + 1, 1 - s).start()

            @pl.when(t >= 2)
            def _():
                y_copy(t - 2, s).wait()

            gu = jnp.dot(xbuf[s].astype(BF16), wg_b[...], preferred_element_type=F32) + bgu_ref[...]
            gate = jnp.minimum(gu[:, :D_FF], SWIGLU_LIMIT)
            up = jnp.clip(gu[:, D_FF:], -SWIGLU_LIMIT, SWIGLU_LIMIT)
            act = (up + 1.0) * gate * _sigmoid(SWIGLU_ALPHA * gate)
            ybuf[s] = jnp.dot(act.astype(BF16), wd_b[...], preferred_element_type=F32) + bd_ref[...]
            y_copy(t, s).start()
            return carry
        lax.fori_loop(0, n_s[e], tile, 0)

        @pl.when(has_next)
        def _():
            def rest(piece, carry):
                request(piece)
                return carry
            lax.fori_loop(jnp.minimum(n_s[e], W_PIECES), W_PIECES, rest, 0)

    @pl.when(e == pl.num_programs(0) - 1)
    def _():
        @pl.when(nt >= 2)
        def _():
            y_copy(nt - 2, nt & 1).wait()
        y_copy(nt - 1, (nt - 1) & 1).wait()
        ybuf[0] = jnp.zeros((EXP_TM, D_MODEL), F32)

        def fill(t, carry):
            y_copy(t, 0).start()
            return carry
        lax.fori_loop(nt, nt_max, fill, 0)

        def drain(t, carry):
            y_copy(0, 0).wait()
            return carry
        lax.fori_loop(nt, nt_max, drain, 0)


def _experts(plan, xs, w_gu, b_gu, w_down, b_down):
    n_rows = xs.shape[0]
    of_expert = lambda e, *_: (e, 0, 0)
    grid_spec = pltpu.PrefetchScalarGridSpec(
        num_scalar_prefetch=6,
        grid=(N_EXPERTS,),
        in_specs=[
            pl.BlockSpec(memory_space=pl.ANY),
            pl.BlockSpec(memory_space=pl.ANY),
            pl.BlockSpec((None, 1, 2 * D_FF), of_expert),
            pl.BlockSpec(memory_space=pl.ANY),
            pl.BlockSpec((None, 1, D_MODEL), of_expert),
        ],
        out_specs=pl.BlockSpec(memory_space=pl.ANY),
        scratch_shapes=[pltpu.VMEM((2, D_MODEL, 2 * D_FF), F32), pltpu.VMEM((2, D_FF, D_MODEL), F32),
                        pltpu.VMEM((D_MODEL, 2 * D_FF), BF16), pltpu.VMEM((D_FF, D_MODEL), BF16),
                        pltpu.VMEM((2, EXP_TM, D_MODEL), F32), pltpu.VMEM((2, EXP_TM, D_MODEL), F32),
                        pltpu.SemaphoreType.DMA((2, 2)), pltpu.SemaphoreType.DMA((2,)),
                        pltpu.SemaphoreType.DMA((2,))],
    )
    return pl.pallas_call(
        _experts_body,
        grid_spec=grid_spec,
        out_shape=jax.ShapeDtypeStruct((n_rows, D_MODEL), F32),
        compiler_params=_params("arbitrary"),
        name="moe_experts",
    )(plan["tile_start"], plan["tile_count"], plan["next_expert"], plan["weight_slot"], plan["first_expert"],
      plan["ntiles"], xs, w_gu, b_gu, w_down, b_down)


def _combine_body(gmap_s, totc_s, route_ref, tril_ref, loffrow_ref, x1_ref, gfin_ref, ys_hbm, y_ref, buf, sem):
    b = pl.program_id(0)
    nb = pl.num_programs(0)
    slot = lax.rem(b, 2)

    def copy_in(slot_, lchunk, gchunk):
        return pltpu.make_async_copy(_chunk_rows(ys_hbm, gchunk), _chunk_rows(buf.at[slot_], lchunk), sem.at[slot_])

    def fetch_block(bb, slot_):
        _for_chunks(totc_s[bb], lambda c: copy_in(slot_, c, gmap_s[bb * BLK_CH + c]).start())

    def wait_block(bb, slot_):
        _wait_chunks(totc_s[bb], lambda rows: pltpu.make_async_copy(
            ys_hbm.at[pl.ds(0, rows)], buf.at[slot_, pl.ds(0, rows)], sem.at[slot_]))

    @pl.when(b == 0)
    def _():
        buf[...] = jnp.zeros_like(buf)
        fetch_block(0, 0)

    @pl.when(b + 1 < nb)
    def _():
        fetch_block(b + 1, 1 - slot)

    wait_block(b, slot)

    route = route_ref[...]
    lane = lax.broadcasted_iota(jnp.int32, (1, LANES), 1).astype(F32)
    sel = jnp.where(lane < float(N_EXPERTS), route, 0.0).astype(BF16)
    rank = jnp.dot(tril_ref[...], sel, preferred_element_type=F32)
    lpos = loffrow_ref[...] * float(ROW_CH) + rank
    lposk, pk = [], []
    for k in range(TOP_K):
        idx = route[:, ROUTE_IDX + k:ROUTE_IDX + k + 1]
        lposk.append(jnp.sum(jnp.where(lane == idx, lpos, 0.0), axis=-1, keepdims=True))
        pk.append(route[:, ROUTE_P + k:ROUTE_P + k + 1])
    acc = x1_ref[...]
    for r0 in range(0, RB, PERM_TM):
        col = lax.broadcasted_iota(jnp.int32, (TBK, PERM_TM), 1).astype(F32) + float(r0)
        w = functools.reduce(jnp.add, [jnp.where(lposk[k] == col, pk[k], 0.0) for k in range(TOP_K)])
        acc = acc + jnp.dot(w.astype(BF16), buf[slot, r0:r0 + PERM_TM, :].astype(BF16), preferred_element_type=F32)
    y_ref[...] = _rms(acc, gfin_ref[...])


def _combine(plan, blocks, route, loffrow, x1, g_final, ys):
    b0, b1 = blocks
    nb = b1 - b0
    tril = jnp.asarray(np.tril(np.ones((TBK, TBK), np.float32), -1), BF16)
    grid_spec = pltpu.PrefetchScalarGridSpec(
        num_scalar_prefetch=2,
        grid=(nb,),
        in_specs=[
            pl.BlockSpec((TBK, LANES), lambda b, *_: (b + b0, 0)),
            pl.BlockSpec((TBK, TBK), lambda b, *_: (0, 0)),
            pl.BlockSpec((None, 1, LANES), lambda b, *_: (b + b0, 0, 0)),
            pl.BlockSpec((TBK, D_MODEL), lambda b, *_: (b + b0, 0)),
            pl.BlockSpec((1, D_MODEL), lambda b, *_: (0, 0)),
            pl.BlockSpec(memory_space=pl.ANY),
        ],
        out_specs=pl.BlockSpec((TBK, D_MODEL), lambda b, *_: (b, 0)),
        scratch_shapes=[pltpu.VMEM((2, RB, D_MODEL), F32), pltpu.SemaphoreType.DMA((2,))],
    )
    return pl.pallas_call(
        _combine_body,
        grid_spec=grid_spec,
        out_shape=jax.ShapeDtypeStruct((nb * TBK, D_MODEL), F32),
        compiler_params=_params("arbitrary"),
        name="moe_combine",
    )(plan["gmap"][b0:b1].reshape(-1), plan["totc"][b0:b1], route, tril, loffrow, x1, g_final, ys)


def kernel(x_prompt, x_sample, cache_k, cache_v, state_s, g_mix, w_in, rel_bias, lb_logits, g_out_norm,
           w_pa, w_pb, w_out, g_ffn, w_router, b_router, w_gu, b_gu, w_down, b_down, g_final):
    B, T = x_prompt.shape[:2]
    DB, S = x_sample.shape[:2]
    depth = w_in.shape[0]
    assert depth == 1 and T % ATT_QBLK == 0 and S == CHUNK
    cw = cache_k.shape[2]
    assert cw == WINDOW
    l = 0

    lower = jnp.cumsum(jax.nn.softmax(lb_logits.astype(F32), axis=0), axis=0)[l].reshape(1, HG_WIDTH)
    w_in_b = w_in[l].astype(BF16)
    wpa, wpb, wout = w_pa[l].astype(BF16), w_pb[l].astype(BF16), w_out[l].astype(BF16)
    row = lambda a: a.reshape(1, -1).astype(F32)
    base = _rel_bias_base(rel_bias[l])
    b_gu3 = b_gu[l].reshape(N_EXPERTS, 1, 2 * D_FF)
    b_down3 = b_down[l].reshape(N_EXPERTS, 1, D_MODEL)
    pad_e = LANES - N_EXPERTS
    wr = jnp.pad(w_router[l].astype(F32), ((0, 0), (0, pad_e)))
    br = jnp.concatenate([b_router[l].astype(F32), jnp.full((pad_e,), NEG, F32)]).reshape(1, LANES)

    n_tok = B * T + DB * S
    nb, nbp = n_tok // TBK, (B * T) // TBK

    def front(x, batch, seq, s0, attend):
        xf = x.reshape(batch * seq, D_MODEL)
        za, zb, zg = _inproj(xf, row(g_mix[l]), w_in_b)
        att = attend(za)
        hg, s_fin = _hgrn(zb, s0, lower, row(g_out_norm[l]), batch, seq)
        za3 = za.reshape(batch, seq, 3 * ATT_WIDTH)
        heads = lambda a: a.reshape(1, batch, a.shape[1], ATT_HEADS, ATT_DIM)
        keep = min(WINDOW, seq)
        nk = heads(za3[:, seq - keep:, ATT_WIDTH:2 * ATT_WIDTH])
        nv = heads(za3[:, seq - keep:, 2 * ATT_WIDTH:])
        return dict(mix=(att, hg, zg, xf), nk=nk, nv=nv, s=s_fin[None])

    ck = cache_k[l].reshape(DB, cw, ATT_WIDTH)
    cv = cache_v[l].reshape(DB, cw, ATT_WIDTH)
    fp = front(x_prompt, B, T, jnp.zeros((B, HG_HEADS, HG_DK, HG_DK), F32), lambda za: _attn_prompt(za, base, B, T))
    fs = front(x_sample, DB, S, state_s[l].astype(F32), lambda za: _attn_sample(za, ck, cv, base, DB, S))

    x1, h2, route, cnt = _merge(fp["mix"], fs["mix"], wpa, wpb, wout, row(g_ffn[l]), wr, br)
    cnt = cnt[:, 0, :N_EXPERTS].astype(jnp.int32)
    max_rows = n_tok * TOP_K + nb * N_EXPERTS * (ROW_CH - 1) + N_EXPERTS * (EXP_TM - 1)
    nt_max = -(-max_rows // EXP_TM)
    plan = _route_plan(cnt)
    loff_f = plan["loff"].astype(F32)
    xs = _dispatch(plan, h2, route, loff_f[:, :, None], nt_max * EXP_TM)
    ysort = _experts(plan, xs, w_gu[l], b_gu3, w_down[l], b_down3)
    loffrow = jnp.pad(loff_f, ((0, 0), (0, pad_e)))[:, None, :]
    yp = _combine(plan, (0, nbp), route, loffrow, x1, row(g_final), ysort)
    ys = _combine(plan, (nbp, nb), route, loffrow, x1, row(g_final), ysort)
    return (yp.reshape(B, T, D_MODEL), ys.reshape(DB, S, D_MODEL), fp["nk"], fp["nv"], fp["s"],
            fs["nk"], fs["nv"], fs["s"])
```

```python
import functools

import numpy as np
import jax
import jax.numpy as jnp
from jax import lax
from jax.experimental import pallas as pl
from jax.experimental.pallas import tpu as pltpu

F32 = jnp.float32
BF16 = jnp.bfloat16

D_MODEL = 1024
CHUNK = 64
LEFT_CHUNKS = 8
WINDOW = LEFT_CHUNKS * CHUNK
ATT_HEADS = 8
ATT_DIM = 64
ATT_WIDTH = ATT_HEADS * ATT_DIM
MAX_REL = 256
HG_HEADS = 4
HG_DK = 128
HG_WIDTH = HG_HEADS * HG_DK
N_EXPERTS = 32
TOP_K = 4
D_FF = D_MODEL
SWIGLU_LIMIT = 7.0
SWIGLU_ALPHA = 1.702
RMS_EPS = 1e-5

LANES = 128
NEG = -1e30
LOG2E = 1.4426950408889634
ATT_QBLK = 4 * CHUNK
ATT_KBLKS = LEFT_CHUNKS * CHUNK // ATT_QBLK + 1
ATT_PAIRS_PER_STAGE = 2
HG_C = 128
HG_CHUNKS_PER_STEP = 4
VMEM_LIMIT = 56 * 1024 * 1024
BIAS_W = 1024
SUBLANES = 8
TBK = 256
ROW_CH = SUBLANES
RB = TBK * TOP_K + N_EXPERTS * ROW_CH
MERGE_TM = 2 * TBK
PERM_TM = 256
EXP_TM = 512
W_PIECES = 8
W_PIECE_ROWS = D_MODEL // W_PIECES
CH_PER_TILE = EXP_TM // ROW_CH
BLK_CH = RB // ROW_CH
ROUTE_IDX = 64
ROUTE_P = 72

NT = (((1,), (1,)), ((), ()))
TN = (((0,), (0,)), ((), ()))


def _rms(x, g):
    return x * lax.rsqrt(jnp.mean(x * x, axis=-1, keepdims=True) + RMS_EPS) * g


def _sigmoid(x):
    return 1.0 / (1.0 + jnp.exp(-x))


def _params(*sem):
    return pltpu.CompilerParams(dimension_semantics=sem, vmem_limit_bytes=VMEM_LIMIT)


def _inproj_body(x_ref, g_ref, w_ref, za_ref, zb_ref, zg_ref):
    h = _rms(x_ref[...], g_ref[...]).astype(BF16)
    a, b = 3 * ATT_WIDTH, 3 * ATT_WIDTH + 4 * HG_WIDTH
    za_ref[...] = jnp.dot(h, w_ref[:, :a], preferred_element_type=F32)
    zb_ref[...] = jnp.dot(h, w_ref[:, a:b], preferred_element_type=F32)
    zg_ref[...] = jnp.dot(h, w_ref[:, b:], preferred_element_type=F32)


def _inproj(x, g, w_bf16, tm=512):
    n = x.shape[0]
    cols = w_bf16.shape[1]
    wa, wb, wg = 3 * ATT_WIDTH, 4 * HG_WIDTH, 2 * D_MODEL
    return pl.pallas_call(
        _inproj_body,
        grid=(n // tm,),
        in_specs=[
            pl.BlockSpec((tm, D_MODEL), lambda i: (i, 0)),
            pl.BlockSpec((1, D_MODEL), lambda i: (0, 0)),
            pl.BlockSpec((D_MODEL, cols), lambda i: (0, 0)),
        ],
        out_specs=[
            pl.BlockSpec((tm, wa), lambda i: (i, 0)),
            pl.BlockSpec((tm, wb), lambda i: (i, 0)),
            pl.BlockSpec((tm, wg), lambda i: (i, 0)),
        ],
        out_shape=[
            jax.ShapeDtypeStruct((n, wa), F32),
            jax.ShapeDtypeStruct((n, wb), F32),
            jax.ShapeDtypeStruct((n, wg), F32),
        ],
        compiler_params=_params("arbitrary"),
        name="inproj",
    )(x, g, w_bf16)


def _attn_heads(q_ref, k_refs, v_refs, bias_fn, pens, o_ref):
    lane = lax.broadcasted_iota(jnp.int32, (1, LANES), 1)
    first = lane < ATT_DIM
    halves = (first, lane >= ATT_DIM)
    for hp0 in range(0, ATT_HEADS // 2, ATT_PAIRS_PER_STAGE):
        pairs = range(hp0, hp0 + ATT_PAIRS_PER_STAGE)
        sl = {hp: slice(hp * LANES, (hp + 1) * LANES) for hp in pairs}
        scores = {}
        for hp in pairs:
            q2 = q_ref[:, sl[hp]] * (ATT_DIM ** -0.5 * LOG2E)
            ks = [k[:, sl[hp]].astype(BF16) for k in k_refs]
            for half, mine in enumerate(halves):
                qm = jnp.where(mine, q2, 0.0).astype(BF16)
                ss = []
                for j, kj in enumerate(ks):
                    s = lax.dot_general(qm, kj, NT, preferred_element_type=F32) + bias_fn(2 * hp + half, j)
                    if pens[j] is not None:
                        s = s + pens[j]
                    ss.append(s)
                scores[hp, half] = ss
        for hp in pairs:
            outs = []
            for half, mine in enumerate(halves):
                ss = scores[hp, half]
                vs = [jnp.where(mine, v[:, sl[hp]], 1.0).astype(BF16) for v in v_refs]
                if all(s.shape == ss[0].shape for s in ss):
                    m = jnp.max(functools.reduce(jnp.maximum, ss), axis=-1, keepdims=True)
                else:
                    m = functools.reduce(jnp.maximum, [jnp.max(s, axis=-1, keepdims=True) for s in ss])
                outs.append(functools.reduce(jnp.add, [
                    jnp.dot(jnp.exp2(s - m).astype(BF16), vj, preferred_element_type=F32) for s, vj in zip(ss, vs)]))
            num = jnp.where(first, outs[0], outs[1])
            den = pltpu.roll(jnp.where(first, outs[1], outs[0]), ATT_DIM, 1)
            o_ref[:, sl[hp]] = (num * (1.0 / den)).astype(o_ref.dtype)


def _fill_bias(base_ref, bias_ref, banded):
    nq, nk = bias_ref.shape[1:]
    if banded:
        r = lax.broadcasted_iota(jnp.int32, (nq, nk), 0)
        s = lax.broadcasted_iota(jnp.int32, (nq, nk), 1)
        qc = (r + WINDOW) // CHUNK
        kc = s // CHUNK
        pen = jnp.where(kc <= qc, jnp.where(kc >= qc - LEFT_CHUNKS, 0.0, NEG), NEG)
    for h in range(ATT_HEADS):
        rows = jnp.broadcast_to(base_ref[h:h + 1, :], (nq, BIAS_W))
        t = pltpu.roll(rows, 0, 1, stride=1, stride_axis=0)[:, :nk] * LOG2E
        bias_ref[h] = t + pen if banded else t


def _attn_prompt_body(q_ref, k0, k1, k2, v0, v1, v2, base_ref, o_ref, bias_ref):
    i = pl.program_id(1)

    @pl.when((pl.program_id(0) == 0) & (i == 0))
    def _():
        _fill_bias(base_ref, bias_ref, True)

    bias_fn = lambda h, j: bias_ref[h, :, j * ATT_QBLK:(j + 1) * ATT_QBLK]
    back = ATT_KBLKS - 1

    @pl.when(i < back)
    def _():
        pens = [jnp.where(i - back + j >= 0, 0.0, NEG) for j in range(back)] + [None]
        _attn_heads(q_ref, [k0, k1, k2], [v0, v1, v2], bias_fn, pens, o_ref)

    @pl.when(i >= back)
    def _():
        _attn_heads(q_ref, [k0, k1, k2], [v0, v1, v2], bias_fn, [None] * ATT_KBLKS, o_ref)


def _attn_prompt(za, base, batch, seq):
    nq = seq // ATT_QBLK
    back = ATT_KBLKS - 1
    qspec = pl.BlockSpec((ATT_QBLK, ATT_WIDTH), lambda b, i: (b * nq + i, 0))

    def kvspec(j, col):
        return pl.BlockSpec((ATT_QBLK, ATT_WIDTH),
                            lambda b, i: (b * nq + jnp.maximum(i - back + j, 0), col))

    return pl.pallas_call(
        _attn_prompt_body,
        grid=(batch, nq),
        in_specs=[qspec] + [kvspec(j, 1) for j in range(ATT_KBLKS)] + [kvspec(j, 2) for j in range(ATT_KBLKS)]
        + [pl.BlockSpec(base.shape, lambda b, i: (0, 0))],
        out_specs=pl.BlockSpec((ATT_QBLK, ATT_WIDTH), lambda b, i: (b * nq + i, 0)),
        out_shape=jax.ShapeDtypeStruct((batch * seq, ATT_WIDTH), BF16),
        scratch_shapes=[pltpu.VMEM((ATT_HEADS, ATT_QBLK, ATT_KBLKS * ATT_QBLK), F32)],
        compiler_params=_params("arbitrary", "arbitrary"),
        name="attn_prompt",
    )(za, za, za, za, za, za, za, base)


def _attn_sample_body(q_ref, kn_ref, vn_ref, ck_ref, cv_ref, base_ref, o_ref, bias_ref):
    @pl.when(pl.program_id(0) == 0)
    def _():
        _fill_bias(base_ref, bias_ref, False)

    cw = ck_ref.shape[0]
    bias_fn = lambda h, j: bias_ref[h, :, :cw] if j == 0 else bias_ref[h, :, cw:]
    _attn_heads(q_ref, [ck_ref, kn_ref], [cv_ref, vn_ref], bias_fn, [None, None], o_ref)


def _attn_sample(za, ck, cv, base, batch, seq):
    cw = ck.shape[1]
    return pl.pallas_call(
        _attn_sample_body,
        grid=(batch,),
        in_specs=[
            pl.BlockSpec((seq, ATT_WIDTH), lambda b: (b, 0)),
            pl.BlockSpec((seq, ATT_WIDTH), lambda b: (b, 1)),
            pl.BlockSpec((seq, ATT_WIDTH), lambda b: (b, 2)),
            pl.BlockSpec((None, cw, ATT_WIDTH), lambda b: (b, 0, 0)),
            pl.BlockSpec((None, cw, ATT_WIDTH), lambda b: (b, 0, 0)),
            pl.BlockSpec(base.shape, lambda b: (0, 0)),
        ],
        out_specs=pl.BlockSpec((seq, ATT_WIDTH), lambda b: (b, 0)),
        out_shape=jax.ShapeDtypeStruct((batch * seq, ATT_WIDTH), BF16),
        scratch_shapes=[pltpu.VMEM((ATT_HEADS, seq, cw + seq), F32)],
        compiler_params=_params("arbitrary"),
        name="attn_sample",
    )(za, za, za, ck, cv, base)


def _rel_bias_base(table):
    top = table[:, 2 * MAX_REL:].astype(F32)
    rev = table[:, ::-1][:, :2 * MAX_REL].astype(F32)
    left = WINDOW - MAX_REL
    return jnp.concatenate([jnp.broadcast_to(top, (ATT_HEADS, left)), rev,
                            jnp.broadcast_to(top, (ATT_HEADS, BIAS_W - left - 2 * MAX_REL))], axis=1)


def _hgrn_consts(c):
    t = np.arange(c)[:, None]
    j = np.arange(c)[None, :]
    mats = [j <= t, j > t]
    masks = []
    m = c // 2
    while m >= 1:
        ref = (t // (2 * m)) * (2 * m) + m - 1
        second = (t % (2 * m)) >= m
        if m < SUBLANES:
            mats.append((second & (j > ref) & (j <= t)) | (~second & (j > t) & (j <= ref)))
        masks.append((t // (2 * m)) == (j // (2 * m)))
        m //= 2
    return (jnp.asarray(np.concatenate(mats, 0).astype(np.float32), BF16),
            jnp.asarray(np.stack(masks).astype(np.float32)))


def _hgrn_body(zb_ref, s0_ref, lower_ref, gon_ref, p_ref, mask_ref, o_ref, sfin_ref, st_ref, *, single_step):
    c = p_ref.shape[1]
    step = pl.program_id(1)

    def load_state():
        for h in range(HG_HEADS):
            st_ref[h] = s0_ref[0, h].T

    if single_step:
        load_state()
    else:
        pl.when(step == 0)(load_state)

    pmat = p_ref[...]
    n_levels = mask_ref.shape[0]
    head = lambda a, h: a[:, h * HG_DK:(h + 1) * HG_DK]
    low = lower_ref[...]
    row = lax.broadcasted_iota(jnp.int32, (c, HG_WIDTH), 0)

    def stage1(r0):
        part = lambda i: zb_ref[r0:r0 + c, i * HG_WIDTH:(i + 1) * HG_WIDTH]
        q = part(0)
        f = low + (1.0 - low) * _sigmoid(part(1))
        lf = jnp.log(f)
        k = 1.0 - f
        ib = part(2)
        v = ib * _sigmoid(ib)
        og = part(3)
        hi = lf.astype(BF16)
        r1 = lf - hi.astype(F32)
        mid = r1.astype(BF16)
        lo = (r1 - mid.astype(F32)).astype(BF16)
        e = (jnp.dot(pmat, hi, preferred_element_type=F32) + jnp.dot(pmat, mid, preferred_element_type=F32)
             + jnp.dot(pmat, lo, preferred_element_type=F32))
        b = e[0:c]
        return dict(q=q, k=k, v=v, e=e, b=b, decay=jnp.exp(e[c - 1:c]), qe=(q * jnp.exp(b)).astype(BF16),
                    kt=(k * jnp.exp(e[c:2 * c])).astype(BF16), vb=v.astype(BF16), qk=q * k,
                    gate=og * _sigmoid(og))

    def stage2(s):
        q, k, e, b = s["q"], s["k"], s["e"], s["b"]
        att = [None] * HG_HEADS
        n_rows_p = 2
        for lvl in range(n_levels):
            m = c >> (lvl + 1)
            if m >= SUBLANES:
                ref = [jnp.broadcast_to(b[p * 2 * m + m - 1:p * 2 * m + m], (2 * m, HG_WIDTH))
                       for p in range(c // (2 * m))]
                x = jnp.exp(-jnp.abs(b - (jnp.concatenate(ref, axis=0) if len(ref) > 1 else ref[0])))
            else:
                x = jnp.exp(e[n_rows_p * c:(n_rows_p + 1) * c])
                n_rows_p += 1
            second = (row & m) != 0
            qm = jnp.where(second, q * x, 0.0).astype(BF16)
            km = jnp.where(second, 0.0, k * x).astype(BF16)
            for h in range(HG_HEADS):
                a = lax.dot_general(head(qm, h), head(km, h), NT, preferred_element_type=F32)
                if lvl > 0:
                    a = a * mask_ref[lvl]
                att[h] = a if att[h] is None else att[h] + a
        return [jnp.dot(att[h].astype(BF16), head(s["vb"], h), preferred_element_type=F32)
                + jnp.sum(head(s["qk"], h), axis=-1, keepdims=True) * head(s["v"], h) for h in range(HG_HEADS)]

    def stage3(r0, s, intra):
        for h in range(HG_HEADS):
            st = st_ref[h]
            inter = lax.dot_general(head(s["qe"], h), st.astype(BF16), NT, preferred_element_type=F32)
            st_ref[h] = st * head(s["decay"], h) + lax.dot_general(head(s["vb"], h), head(s["kt"], h), TN,
                                                                  preferred_element_type=F32)
            o = _rms(inter + intra[h], gon_ref[...]) * head(s["gate"], h)
            o_ref[r0:r0 + c, h * HG_DK:(h + 1) * HG_DK] = o.astype(o_ref.dtype)

    starts = range(0, zb_ref.shape[0], c)
    firsts = [stage1(r0) for r0 in starts]
    intras = [stage2(s) for s in firsts]
    for r0, s, intra in zip(starts, firsts, intras):
        stage3(r0, s, intra)

    def write_state():
        for h in range(HG_HEADS):
            sfin_ref[0, h] = st_ref[h].T

    if single_step:
        write_state()
    else:
        pl.when(step == pl.num_programs(1) - 1)(write_state)


def _hgrn(zb, s0, lower, g_on, batch, seq):
    c = min(HG_C, seq)
    rows = min(HG_CHUNKS_PER_STEP * c, seq)
    assert seq % rows == 0 and rows % c == 0
    pmat, masks = _hgrn_consts(c)
    nc = seq // rows
    return pl.pallas_call(
        functools.partial(_hgrn_body, single_step=nc == 1),
        grid=(batch, nc),
        in_specs=[
            pl.BlockSpec((rows, 4 * HG_WIDTH), lambda b, i: (b * nc + i, 0)),
            pl.BlockSpec((1, HG_HEADS, HG_DK, HG_DK), lambda b, i: (b, 0, 0, 0)),
            pl.BlockSpec((1, HG_WIDTH), lambda b, i: (0, 0)),
            pl.BlockSpec((1, HG_DK), lambda b, i: (0, 0)),
            pl.BlockSpec(pmat.shape, lambda b, i: (0, 0)),
            pl.BlockSpec(masks.shape, lambda b, i: (0, 0, 0)),
        ],
        out_specs=[
            pl.BlockSpec((rows, HG_WIDTH), lambda b, i: (b * nc + i, 0)),
            pl.BlockSpec((1, HG_HEADS, HG_DK, HG_DK), lambda b, i: (b, 0, 0, 0)),
        ],
        out_shape=[
            jax.ShapeDtypeStruct((batch * seq, HG_WIDTH), BF16),
            jax.ShapeDtypeStruct((batch, HG_HEADS, HG_DK, HG_DK), F32),
        ],
        scratch_shapes=[pltpu.VMEM((HG_HEADS, HG_DK, HG_DK), F32)],
        compiler_params=_params("arbitrary", "arbitrary"),
        name="hgrn2",
    )(zb, s0, lower, g_on, pmat, masks)


def _split_bf16(x):
    hi = x.astype(BF16)
    return hi, (x - hi.astype(F32)).astype(BF16)


def _merge_body(att_p, hg_p, zg_p, x_p, att_s, hg_s, zg_s, x_s, wpa_ref, wpb_ref, wout_ref, gffn_ref, wr_ref, br_ref,
                x1_ref, h2_ref, route_ref, cnt_ref, *, n_first):
    weights = (wpa_ref, wpb_ref, wout_ref, gffn_ref, wr_ref, br_ref)
    outs = (x1_ref, h2_ref, route_ref, cnt_ref)
    i = pl.program_id(0)
    pl.when(i < n_first)(functools.partial(_merge_block, att_p, hg_p, zg_p, x_p, *weights, *outs))
    pl.when(i >= n_first)(functools.partial(_merge_block, att_s, hg_s, zg_s, x_s, *weights, *outs))


def _merge_block(att_ref, hg_ref, zg_ref, x_ref, wpa_ref, wpb_ref, wout_ref, gffn_ref, wr_ref, br_ref,
                 x1_ref, h2_ref, route_ref, cnt_ref):
    pa = jnp.dot(att_ref[...].astype(BF16), wpa_ref[...], preferred_element_type=F32)
    pb = jnp.dot(hg_ref[...].astype(BF16), wpb_ref[...], preferred_element_type=F32)
    y = _sigmoid(zg_ref[:, :D_MODEL]) * pa + _sigmoid(zg_ref[:, D_MODEL:]) * pb
    x1 = x_ref[...] + jnp.dot(y.astype(BF16), wout_ref[...], preferred_element_type=F32)
    x1_ref[...] = x1
    h2 = _rms(x1, gffn_ref[...])
    h2_ref[...] = h2.astype(BF16)

    h_hi, h_lo = _split_bf16(h2)
    w_hi, w_lo = _split_bf16(wr_ref[...])
    logits = (jnp.dot(h_hi, w_hi, preferred_element_type=F32) + jnp.dot(h_lo, w_hi, preferred_element_type=F32)
              + jnp.dot(h_hi, w_lo, preferred_element_type=F32)) + br_ref[...]
    lane = lax.broadcasted_iota(jnp.int32, logits.shape, 1).astype(F32)
    cur = logits
    vals, idxs = [], []
    for _ in range(TOP_K):
        m = jnp.max(cur, axis=-1, keepdims=True)
        idx = jnp.min(jnp.where(cur == m, lane, float(LANES)), axis=-1, keepdims=True)
        vals.append(m)
        idxs.append(idx)
        cur = jnp.where(lane == idx, -jnp.inf, cur)
    es = [jnp.exp(v - vals[0]) for v in vals]
    inv = 1.0 / functools.reduce(jnp.add, es)
    route = jnp.zeros_like(logits)
    for k, (ex, idx) in enumerate(zip(es, idxs)):
        route = (route + jnp.where(lane == idx, 1.0, 0.0) + jnp.where(lane == float(ROUTE_IDX + k), idx, 0.0)
                 + jnp.where(lane == float(ROUTE_P + k), ex * inv, 0.0))
    route_ref[...] = route
    sel = jnp.where(lane < float(N_EXPERTS), route, 0.0)
    for blk in range(cnt_ref.shape[0]):
        cnt_ref[blk] = jnp.sum(sel[blk * TBK:(blk + 1) * TBK], axis=0, keepdims=True)


def _merge(first, second, wpa, wpb, wout, g_ffn, w_router, b_router):
    tm = MERGE_TM
    assert first[3].shape[0] % tm == 0 and second[3].shape[0] % tm == 0
    n1, n2 = first[3].shape[0] // tm, second[3].shape[0] // tm
    widths = (ATT_WIDTH, HG_WIDTH, 2 * D_MODEL, D_MODEL)
    spec1 = [pl.BlockSpec((tm, w), lambda i: (jnp.minimum(i, n1 - 1), 0)) for w in widths]
    spec2 = [pl.BlockSpec((tm, w), lambda i: (jnp.maximum(i - n1, 0), 0)) for w in widths]
    row = lambda w: pl.BlockSpec((tm, w), lambda i: (i, 0))
    full = lambda a: pl.BlockSpec(a.shape, lambda i: (0,) * a.ndim)
    n = (n1 + n2) * tm
    return pl.pallas_call(
        functools.partial(_merge_body, n_first=n1),
        grid=(n1 + n2,),
        in_specs=spec1 + spec2 + [full(wpa), full(wpb), full(wout), full(g_ffn), full(w_router), full(b_router)],
        out_specs=[row(D_MODEL), row(D_MODEL), row(LANES),
                   pl.BlockSpec((tm // TBK, 1, LANES), lambda i: (i, 0, 0))],
        out_shape=[
            jax.ShapeDtypeStruct((n, D_MODEL), F32),
            jax.ShapeDtypeStruct((n, D_MODEL), BF16),
            jax.ShapeDtypeStruct((n, LANES), F32),
            jax.ShapeDtypeStruct((n // TBK, 1, LANES), F32),
        ],
        compiler_params=_params("arbitrary"),
        name="merge_router",
    )(*first, *second, wpa, wpb, wout, g_ffn, w_router, b_router)


def _route_plan(cnt):
    pc = (cnt + ROW_CH - 1) // ROW_CH
    loff = jnp.cumsum(pc, axis=1) - pc
    tot = jnp.sum(pc, axis=0)
    reg = (tot + CH_PER_TILE - 1) // CH_PER_TILE * CH_PER_TILE
    gstart = jnp.cumsum(reg) - reg
    goff = gstart[None, :] + jnp.cumsum(pc, axis=0) - pc
    ntiles = jnp.sum(reg) // CH_PER_TILE
    present = reg > 0
    ids = jnp.arange(N_EXPERTS, dtype=jnp.int32)
    later = (ids[None, :] > ids[:, None]) & present[None, :]
    nxt = jnp.min(jnp.where(later, ids[None, :], N_EXPERTS), axis=1)
    nxt = jnp.where(nxt == N_EXPERTS, -1, nxt)
    slot = (jnp.cumsum(present.astype(jnp.int32)) - 1) % 2
    first = jnp.min(jnp.where(present, ids, N_EXPERTS)).reshape(1)
    j = jnp.arange(BLK_CH, dtype=jnp.int32)
    run = jnp.sum(((loff + pc)[:, None, :] <= j[None, :, None]).astype(jnp.int32), axis=2)
    shift = jnp.sum(jnp.where(run[:, :, None] == ids[None, None, :], (goff - loff)[:, None, :], 0), axis=2)
    gmap = shift + j[None, :]
    i32 = lambda a: a.astype(jnp.int32)
    return dict(loff=i32(loff), gmap=i32(gmap), totc=i32(jnp.sum(pc, axis=1)),
                pad_start=i32(gstart + tot), pad_cnt=i32(reg - tot), ntiles=i32(ntiles).reshape(1),
                tile_start=i32(gstart // CH_PER_TILE), tile_count=i32(reg // CH_PER_TILE),
                next_expert=i32(nxt), weight_slot=i32(slot), first_expert=i32(first))


def _chunk_rows(ref, chunk):
    return ref.at[pl.ds(pl.multiple_of(chunk * ROW_CH, ROW_CH), ROW_CH)]


def _for_chunks(n, do):
    log_unroll = 2
    groups = lax.shift_right_logical(n, log_unroll)

    def group(i, carry):
        for u in range(1 << log_unroll):
            do(lax.shift_left(i, log_unroll) + u)
        return carry
    lax.fori_loop(0, groups, group, 0)

    def single(c, carry):
        do(c)
        return carry
    lax.fori_loop(lax.shift_left(groups, log_unroll), n, single, 0)


def _wait_chunks(n, copy_of_rows):
    for bit in range((RB // ROW_CH).bit_length()):
        @pl.when(((n >> bit) & 1) == 1)
        def _(bit=bit):
            copy_of_rows((1 << bit) * ROW_CH).wait()


def _dispatch_body(gmap_s, totc_s, pads_s, padn_s, nt_s, h2_ref, route_ref, tri_ref, loffv_ref,
                   xs_hbm, buf, zbuf, sem):
    b = pl.program_id(0)
    nb = pl.num_programs(0)
    slot = lax.rem(b, 2)
    fill_sem, tile_sem = 2, 3
    nt_max = xs_hbm.shape[0] // EXP_TM

    def copy_out(src, gchunk, sem_i):
        return pltpu.make_async_copy(src, _chunk_rows(xs_hbm, gchunk), sem.at[sem_i])

    def zero_tile(t):
        return pltpu.make_async_copy(zbuf, xs_hbm.at[pl.ds(pl.multiple_of(t * EXP_TM, EXP_TM), EXP_TM)],
                                     sem.at[tile_sem])

    def wait_block(bb, slot_):
        _wait_chunks(totc_s[bb], lambda rows: pltpu.make_async_copy(
            buf.at[slot_, pl.ds(0, rows)], xs_hbm.at[pl.ds(0, rows)], sem.at[slot_]))

    @pl.when(b == 0)
    def _():
        zbuf[...] = jnp.zeros_like(zbuf)

    @pl.when(b >= 2)
    def _():
        wait_block(b - 2, slot)

    rt = route_ref[...].T
    rank_t = jnp.dot(rt[0:N_EXPERTS].astype(BF16), tri_ref[...], preferred_element_type=F32)
    lpos_t = loffv_ref[...] * float(ROW_CH) + rank_t
    erow = lax.broadcasted_iota(jnp.int32, (N_EXPERTS, TBK), 0).astype(F32)
    lposk = [jnp.sum(jnp.where(rt[ROUTE_IDX + k:ROUTE_IDX + k + 1] == erow, lpos_t, 0.0), axis=0, keepdims=True)
             for k in range(TOP_K)]
    h2 = h2_ref[...]
    for r0 in range(0, RB, PERM_TM):
        rrow = lax.broadcasted_iota(jnp.int32, (PERM_TM, TBK), 0).astype(F32) + float(r0)
        perm = functools.reduce(jnp.add, [jnp.where(lposk[k] == rrow, 1.0, 0.0) for k in range(TOP_K)])
        buf[slot, r0:r0 + PERM_TM, :] = jnp.dot(perm.astype(BF16), h2, preferred_element_type=F32)

    _for_chunks(totc_s[b], lambda c: copy_out(_chunk_rows(buf.at[slot], c), gmap_s[b * BLK_CH + c], slot).start())

    @pl.when(b == nb - 1)
    def _():
        zchunk = _chunk_rows(zbuf, 0)

        def fill_expert(e, carry):
            def one(c, carry2):
                copy_out(zchunk, pads_s[e] + c, fill_sem).start()
                return carry2
            lax.fori_loop(0, padn_s[e], one, 0)
            return carry
        lax.fori_loop(0, N_EXPERTS, fill_expert, 0)

        def fill_tile(t, carry):
            zero_tile(t).start()
            return carry
        lax.fori_loop(nt_s[0], nt_max, fill_tile, 0)

        @pl.when(b >= 1)
        def _():
            wait_block(b - 1, 1 - slot)
        wait_block(b, slot)

        def drain_expert(e, carry):
            def one(c, carry2):
                copy_out(zchunk, 0, fill_sem).wait()
                return carry2
            lax.fori_loop(0, padn_s[e], one, 0)
            return carry
        lax.fori_loop(0, N_EXPERTS, drain_expert, 0)

        def drain_tile(t, carry):
            zero_tile(0).wait()
            return carry
        lax.fori_loop(nt_s[0], nt_max, drain_tile, 0)


def _dispatch(plan, h2, route, loffv, n_rows):
    nb = h2.shape[0] // TBK
    tri = jnp.asarray(np.triu(np.ones((TBK, TBK), np.float32), 1), BF16)
    grid_spec = pltpu.PrefetchScalarGridSpec(
        num_scalar_prefetch=5,
        grid=(nb,),
        in_specs=[
            pl.BlockSpec((TBK, D_MODEL), lambda b, *_: (b, 0)),
            pl.BlockSpec((TBK, LANES), lambda b, *_: (b, 0)),
            pl.BlockSpec((TBK, TBK), lambda b, *_: (0, 0)),
            pl.BlockSpec((None, N_EXPERTS, 1), lambda b, *_: (b, 0, 0)),
        ],
        out_specs=pl.BlockSpec(memory_space=pl.ANY),
        scratch_shapes=[pltpu.VMEM((2, RB, D_MODEL), F32), pltpu.VMEM((EXP_TM, D_MODEL), F32),
                        pltpu.SemaphoreType.DMA((4,))],
    )
    return pl.pallas_call(
        _dispatch_body,
        grid_spec=grid_spec,
        out_shape=jax.ShapeDtypeStruct((n_rows, D_MODEL), F32),
        compiler_params=_params("arbitrary"),
        name="moe_dispatch",
    )(plan["gmap"].reshape(-1), plan["totc"], plan["pad_start"], plan["pad_cnt"], plan["ntiles"],
      h2, route, tri, loffv)


def _experts_body(t0_s, n_s, nxt_s, par_s, first_s, nt_s, xs_hbm, wgu_hbm, bgu_ref, wd_hbm, bd_ref, ys_hbm,
                  wg_f, wd_f, wg_b, wd_b, xbuf, ybuf, wsem, xsem, ysem):
    e = pl.program_id(0)
    nt = nt_s[0]
    nt_max = xs_hbm.shape[0] // EXP_TM
    tile_rows = lambda t: pl.ds(pl.multiple_of(t * EXP_TM, EXP_TM), EXP_TM)

    def weight_copies(ex, s, piece):
        rows = pl.ds(pl.multiple_of(piece * W_PIECE_ROWS, W_PIECE_ROWS), W_PIECE_ROWS)
        return (pltpu.make_async_copy(wgu_hbm.at[ex, rows], wg_f.at[s, rows], wsem.at[s, 0]),
                pltpu.make_async_copy(wd_hbm.at[ex, rows], wd_f.at[s, rows], wsem.at[s, 1]))

    def x_copy(t, s):
        return pltpu.make_async_copy(xs_hbm.at[tile_rows(t)], xbuf.at[s], xsem.at[s])

    def y_copy(t, s):
        return pltpu.make_async_copy(ybuf.at[s], ys_hbm.at[tile_rows(t)], ysem.at[s])

    @pl.when(e == 0)
    def _():
        for piece in range(W_PIECES):
            for cp in weight_copies(first_s[0], 0, piece):
                cp.start()
        x_copy(0, 0).start()

    @pl.when(n_s[e] > 0)
    def _():
        s_w = par_s[e]
        for piece in range(W_PIECES):
            for cp in weight_copies(e, s_w, piece):
                cp.wait()
        wg_b[...] = wg_f[s_w].astype(BF16)
        wd_b[...] = wd_f[s_w].astype(BF16)
        has_next = nxt_s[e] >= 0

        def request(piece):
            for cp in weight_copies(nxt_s[e], 1 - s_w, piece):
                cp.start()

        def tile(i, carry):
            t = t0_s[e] + i
            s = t & 1
            x_copy(t, s).wait()

            @pl.when(has_next & (2 * i < W_PIECES))
            def _():
                request(2 * i)

            @pl.when(t >= 2)
            def _():
                y_copy(t - 2, s).wait()

            gu = jnp.dot(xbuf[s].astype(BF16), wg_b[...], preferred_element_type=F32) + bgu_ref[...]
            gate = jnp.minimum(gu[:, :D_FF], SWIGLU_LIMIT)
            up = jnp.clip(gu[:, D_FF:], -SWIGLU_LIMIT, SWIGLU_LIMIT)
            act = ((up + 1.0) * gate * _sigmoid(SWIGLU_ALPHA * gate)).astype(BF16)

            @pl.when(has_next & (2 * i + 1 < W_PIECES))
            def _():
                request(2 * i + 1)

            @pl.when(t + 1 < nt)
            def _():
                x_copy(t + 1, 1 - s).start()

            ybuf[s] = jnp.dot(act, wd_b[...], preferred_element_type=F32) + bd_ref[...]
            y_copy(t, s).start()
            return carry
        lax.fori_loop(0, n_s[e], tile, 0)

        @pl.when(has_next)
        def _():
            def rest(piece, carry):
                request(piece)
                return carry
            lax.fori_loop(jnp.minimum(2 * n_s[e], W_PIECES), W_PIECES, rest, 0)

    @pl.when(e == pl.num_programs(0) - 1)
    def _():
        @pl.when(nt >= 2)
        def _():
            y_copy(nt - 2, nt & 1).wait()
        y_copy(nt - 1, (nt - 1) & 1).wait()
        ybuf[0] = jnp.zeros((EXP_TM, D_MODEL), F32)

        def fill(t, carry):
            y_copy(t, 0).start()
            return carry
        lax.fori_loop(nt, nt_max, fill, 0)

        def drain(t, carry):
            y_copy(0, 0).wait()
            return carry
        lax.fori_loop(nt, nt_max, drain, 0)


def _experts(plan, xs, w_gu, b_gu, w_down, b_down):
    n_rows = xs.shape[0]
    of_expert = lambda e, *_: (e, 0, 0)
    grid_spec = pltpu.PrefetchScalarGridSpec(
        num_scalar_prefetch=6,
        grid=(N_EXPERTS,),
        in_specs=[
            pl.BlockSpec(memory_space=pl.ANY),
            pl.BlockSpec(memory_space=pl.ANY),
            pl.BlockSpec((None, 1, 2 * D_FF), of_expert),
            pl.BlockSpec(memory_space=pl.ANY),
            pl.BlockSpec((None, 1, D_MODEL), of_expert),
        ],
        out_specs=pl.BlockSpec(memory_space=pl.ANY),
        scratch_shapes=[pltpu.VMEM((2, D_MODEL, 2 * D_FF), F32), pltpu.VMEM((2, D_FF, D_MODEL), F32),
                        pltpu.VMEM((D_MODEL, 2 * D_FF), BF16), pltpu.VMEM((D_FF, D_MODEL), BF16),
                        pltpu.VMEM((2, EXP_TM, D_MODEL), F32), pltpu.VMEM((2, EXP_TM, D_MODEL), F32),
                        pltpu.SemaphoreType.DMA((2, 2)), pltpu.SemaphoreType.DMA((2,)),
                        pltpu.SemaphoreType.DMA((2,))],
    )
    return pl.pallas_call(
        _experts_body,
        grid_spec=grid_spec,
        out_shape=jax.ShapeDtypeStruct((n_rows, D_MODEL), F32),
        compiler_params=_params("arbitrary"),
        name="moe_experts",
    )(plan["tile_start"], plan["tile_count"], plan["next_expert"], plan["weight_slot"], plan["first_expert"],
      plan["ntiles"], xs, w_gu, b_gu, w_down, b_down)


def _combine_body(gmap_s, totc_s, route_ref, tril_ref, loffrow_ref, x1_ref, gfin_ref, ys_hbm, y_ref, buf, sem):
    b = pl.program_id(0)
    nb = pl.num_programs(0)
    slot = lax.rem(b, 2)

    def copy_in(slot_, lchunk, gchunk):
        return pltpu.make_async_copy(_chunk_rows(ys_hbm, gchunk), _chunk_rows(buf.at[slot_], lchunk), sem.at[slot_])

    def fetch_block(bb, slot_):
        _for_chunks(totc_s[bb], lambda c: copy_in(slot_, c, gmap_s[bb * BLK_CH + c]).start())

    def wait_block(bb, slot_):
        _wait_chunks(totc_s[bb], lambda rows: pltpu.make_async_copy(
            ys_hbm.at[pl.ds(0, rows)], buf.at[slot_, pl.ds(0, rows)], sem.at[slot_]))

    @pl.when(b == 0)
    def _():
        buf[...] = jnp.zeros_like(buf)
        fetch_block(0, 0)

    @pl.when(b + 1 < nb)
    def _():
        fetch_block(b + 1, 1 - slot)

    wait_block(b, slot)

    route = route_ref[...]
    lane = lax.broadcasted_iota(jnp.int32, (1, LANES), 1).astype(F32)
    sel = jnp.where(lane < float(N_EXPERTS), route, 0.0).astype(BF16)
    rank = jnp.dot(tril_ref[...], sel, preferred_element_type=F32)
    lpos = loffrow_ref[...] * float(ROW_CH) + rank
    lposk, pk = [], []
    for k in range(TOP_K):
        idx = route[:, ROUTE_IDX + k:ROUTE_IDX + k + 1]
        lposk.append(jnp.sum(jnp.where(lane == idx, lpos, 0.0), axis=-1, keepdims=True))
        pk.append(route[:, ROUTE_P + k:ROUTE_P + k + 1])
    acc = x1_ref[...]
    for r0 in range(0, RB, PERM_TM):
        col = lax.broadcasted_iota(jnp.int32, (TBK, PERM_TM), 1).astype(F32) + float(r0)
        w = functools.reduce(jnp.add, [jnp.where(lposk[k] == col, pk[k], 0.0) for k in range(TOP_K)])
        acc = acc + jnp.dot(w.astype(BF16), buf[slot, r0:r0 + PERM_TM, :].astype(BF16), preferred_element_type=F32)
    y_ref[...] = _rms(acc, gfin_ref[...])


def _combine(plan, blocks, route, loffrow, x1, g_final, ys):
    b0, b1 = blocks
    nb = b1 - b0
    tril = jnp.asarray(np.tril(np.ones((TBK, TBK), np.float32), -1), BF16)
    grid_spec = pltpu.PrefetchScalarGridSpec(
        num_scalar_prefetch=2,
        grid=(nb,),
        in_specs=[
            pl.BlockSpec((TBK, LANES), lambda b, *_: (b + b0, 0)),
            pl.BlockSpec((TBK, TBK), lambda b, *_: (0, 0)),
            pl.BlockSpec((None, 1, LANES), lambda b, *_: (b + b0, 0, 0)),
            pl.BlockSpec((TBK, D_MODEL), lambda b, *_: (b + b0, 0)),
            pl.BlockSpec((1, D_MODEL), lambda b, *_: (0, 0)),
            pl.BlockSpec(memory_space=pl.ANY),
        ],
        out_specs=pl.BlockSpec((TBK, D_MODEL), lambda b, *_: (b, 0)),
        scratch_shapes=[pltpu.VMEM((2, RB, D_MODEL), F32), pltpu.SemaphoreType.DMA((2,))],
    )
    return pl.pallas_call(
        _combine_body,
        grid_spec=grid_spec,
        out_shape=jax.ShapeDtypeStruct((nb * TBK, D_MODEL), F32),
        compiler_params=_params("arbitrary"),
        name="moe_combine",
    )(plan["gmap"][b0:b1].reshape(-1), plan["totc"][b0:b1], route, tril, loffrow, x1, g_final, ys)


def kernel(x_prompt, x_sample, cache_k, cache_v, state_s, g_mix, w_in, rel_bias, lb_logits, g_out_norm,
           w_pa, w_pb, w_out, g_ffn, w_router, b_router, w_gu, b_gu, w_down, b_down, g_final):
    B, T = x_prompt.shape[:2]
    DB, S = x_sample.shape[:2]
    depth = w_in.shape[0]
    assert depth == 1 and T % ATT_QBLK == 0 and S == CHUNK
    cw = cache_k.shape[2]
    assert cw == WINDOW
    l = 0

    lower = jnp.cumsum(jax.nn.softmax(lb_logits.astype(F32), axis=0), axis=0)[l].reshape(1, HG_WIDTH)
    w_in_b = w_in[l].astype(BF16)
    wpa, wpb, wout = w_pa[l].astype(BF16), w_pb[l].astype(BF16), w_out[l].astype(BF16)
    row = lambda a: a.reshape(1, -1).astype(F32)
    base = _rel_bias_base(rel_bias[l])
    b_gu3 = b_gu[l].reshape(N_EXPERTS, 1, 2 * D_FF)
    b_down3 = b_down[l].reshape(N_EXPERTS, 1, D_MODEL)
    pad_e = LANES - N_EXPERTS
    wr = jnp.pad(w_router[l].astype(F32), ((0, 0), (0, pad_e)))
    br = jnp.concatenate([b_router[l].astype(F32), jnp.full((pad_e,), NEG, F32)]).reshape(1, LANES)

    n_tok = B * T + DB * S
    nb, nbp = n_tok // TBK, (B * T) // TBK

    def front(x, batch, seq, s0, attend):
        xf = x.reshape(batch * seq, D_MODEL)
        za, zb, zg = _inproj(xf, row(g_mix[l]), w_in_b)
        att = attend(za)
        hg, s_fin = _hgrn(zb, s0, lower, row(g_out_norm[l]), batch, seq)
        za3 = za.reshape(batch, seq, 3 * ATT_WIDTH)
        heads = lambda a: a.reshape(1, batch, a.shape[1], ATT_HEADS, ATT_DIM)
        keep = min(WINDOW, seq)
        nk = heads(za3[:, seq - keep:, ATT_WIDTH:2 * ATT_WIDTH])
        nv = heads(za3[:, seq - keep:, 2 * ATT_WIDTH:])
        return dict(mix=(att, hg, zg, xf), nk=nk, nv=nv, s=s_fin[None])

    ck = cache_k[l].reshape(DB, cw, ATT_WIDTH)
    cv = cache_v[l].reshape(DB, cw, ATT_WIDTH)
    fp = front(x_prompt, B, T, jnp.zeros((B, HG_HEADS, HG_DK, HG_DK), F32), lambda za: _attn_prompt(za, base, B, T))
    fs = front(x_sample, DB, S, state_s[l].astype(F32), lambda za: _attn_sample(za, ck, cv, base, DB, S))

    x1, h2, route, cnt = _merge(fp["mix"], fs["mix"], wpa, wpb, wout, row(g_ffn[l]), wr, br)
    cnt = cnt[:, 0, :N_EXPERTS].astype(jnp.int32)
    max_rows = n_tok * TOP_K + nb * N_EXPERTS * (ROW_CH - 1) + N_EXPERTS * (EXP_TM - 1)
    nt_max = -(-max_rows // EXP_TM)
    plan = _route_plan(cnt)
    loff_f = plan["loff"].astype(F32)
    xs = _dispatch(plan, h2, route, loff_f[:, :, None], nt_max * EXP_TM)
    ysort = _experts(plan, xs, w_gu[l], b_gu3, w_down[l], b_down3)
    loffrow = jnp.pad(loff_f, ((0, 0), (0, pad_e)))[:, None, :]
    yp = _combine(plan, (0, nbp), route, loffrow, x1, row(g_final), ysort)
    ys = _combine(plan, (nbp, nb), route, loffrow, x1, row(g_final), ysort)
    return (yp.reshape(B, T, D_MODEL), ys.reshape(DB, S, D_MODEL), fp["nk"], fp["nv"], fp["s"],
            fs["nk"], fs["nv"], fs["s"])
```

```python
import functools

import numpy as np
import jax
import jax.numpy as jnp
from jax import lax
from jax.experimental import pallas as pl
from jax.experimental.pallas import tpu as pltpu

F32 = jnp.float32
BF16 = jnp.bfloat16

D_MODEL = 1024
CHUNK = 64
LEFT_CHUNKS = 8
WINDOW = LEFT_CHUNKS * CHUNK
ATT_HEADS = 8
ATT_DIM = 64
ATT_WIDTH = ATT_HEADS * ATT_DIM
MAX_REL = 256
HG_HEADS = 4
HG_DK = 128
HG_WIDTH = HG_HEADS * HG_DK
N_EXPERTS = 32
TOP_K = 4
D_FF = D_MODEL
SWIGLU_LIMIT = 7.0
SWIGLU_ALPHA = 1.702
RMS_EPS = 1e-5

LANES = 128
NEG = -1e30
LOG2E = 1.4426950408889634
ATT_QBLK = 4 * CHUNK
ATT_KBLKS = LEFT_CHUNKS * CHUNK // ATT_QBLK + 1
ATT_PAIRS_PER_STAGE = 2
HG_C = 128
HG_CHUNKS_PER_STEP = 4
VMEM_LIMIT = 56 * 1024 * 1024
BIAS_W = 1024
SUBLANES = 8
TBK = 256
ROW_CH = SUBLANES
RB = TBK * TOP_K + N_EXPERTS * ROW_CH
MERGE_TM = 2 * TBK
PERM_TM = 256
EXP_TM = 512
W_PIECES = 4
W_PIECE_ROWS = D_MODEL // W_PIECES
CH_PER_TILE = EXP_TM // ROW_CH
BLK_CH = RB // ROW_CH
ROUTE_IDX = 64
ROUTE_P = 72

NT = (((1,), (1,)), ((), ()))
TN = (((0,), (0,)), ((), ()))


def _rms(x, g):
    return x * lax.rsqrt(jnp.mean(x * x, axis=-1, keepdims=True) + RMS_EPS) * g


def _sigmoid(x):
    return 1.0 / (1.0 + jnp.exp(-x))


def _params(*sem):
    return pltpu.CompilerParams(dimension_semantics=sem, vmem_limit_bytes=VMEM_LIMIT)


def _inproj_body(x_ref, g_ref, w_ref, za_ref, zb_ref, zg_ref):
    h = _rms(x_ref[...], g_ref[...]).astype(BF16)
    a, b = 3 * ATT_WIDTH, 3 * ATT_WIDTH + 4 * HG_WIDTH
    za_ref[...] = jnp.dot(h, w_ref[:, :a], preferred_element_type=F32)
    zb_ref[...] = jnp.dot(h, w_ref[:, a:b], preferred_element_type=F32)
    zg_ref[...] = jnp.dot(h, w_ref[:, b:], preferred_element_type=F32)


def _inproj(x, g, w_bf16, tm=512):
    n = x.shape[0]
    cols = w_bf16.shape[1]
    wa, wb, wg = 3 * ATT_WIDTH, 4 * HG_WIDTH, 2 * D_MODEL
    return pl.pallas_call(
        _inproj_body,
        grid=(n // tm,),
        in_specs=[
            pl.BlockSpec((tm, D_MODEL), lambda i: (i, 0)),
            pl.BlockSpec((1, D_MODEL), lambda i: (0, 0)),
            pl.BlockSpec((D_MODEL, cols), lambda i: (0, 0)),
        ],
        out_specs=[
            pl.BlockSpec((tm, wa), lambda i: (i, 0)),
            pl.BlockSpec((tm, wb), lambda i: (i, 0)),
            pl.BlockSpec((tm, wg), lambda i: (i, 0)),
        ],
        out_shape=[
            jax.ShapeDtypeStruct((n, wa), F32),
            jax.ShapeDtypeStruct((n, wb), F32),
            jax.ShapeDtypeStruct((n, wg), F32),
        ],
        compiler_params=_params("arbitrary"),
        name="inproj",
    )(x, g, w_bf16)


def _attn_heads(q_ref, k_refs, v_refs, bias_fn, pens, o_ref):
    lane = lax.broadcasted_iota(jnp.int32, (1, LANES), 1)
    first = lane < ATT_DIM
    halves = (first, lane >= ATT_DIM)
    for hp0 in range(0, ATT_HEADS // 2, ATT_PAIRS_PER_STAGE):
        pairs = range(hp0, hp0 + ATT_PAIRS_PER_STAGE)
        sl = {hp: slice(hp * LANES, (hp + 1) * LANES) for hp in pairs}
        scores = {}
        for hp in pairs:
            q2 = q_ref[:, sl[hp]] * (ATT_DIM ** -0.5 * LOG2E)
            ks = [k[:, sl[hp]].astype(BF16) for k in k_refs]
            for half, mine in enumerate(halves):
                qm = jnp.where(mine, q2, 0.0).astype(BF16)
                ss = []
                for j, kj in enumerate(ks):
                    s = lax.dot_general(qm, kj, NT, preferred_element_type=F32) + bias_fn(2 * hp + half, j)
                    if pens[j] is not None:
                        s = s + pens[j]
                    ss.append(s)
                scores[hp, half] = ss
        for hp in pairs:
            outs = []
            for half, mine in enumerate(halves):
                ss = scores[hp, half]
                vs = [jnp.where(mine, v[:, sl[hp]], 1.0).astype(BF16) for v in v_refs]
                if all(s.shape == ss[0].shape for s in ss):
                    m = jnp.max(functools.reduce(jnp.maximum, ss), axis=-1, keepdims=True)
                else:
                    m = functools.reduce(jnp.maximum, [jnp.max(s, axis=-1, keepdims=True) for s in ss])
                outs.append(functools.reduce(jnp.add, [
                    jnp.dot(jnp.exp2(s - m).astype(BF16), vj, preferred_element_type=F32) for s, vj in zip(ss, vs)]))
            num = jnp.where(first, outs[0], outs[1])
            den = pltpu.roll(jnp.where(first, outs[1], outs[0]), ATT_DIM, 1)
            o_ref[:, sl[hp]] = (num * (1.0 / den)).astype(o_ref.dtype)


def _fill_bias(base_ref, bias_ref, banded):
    nq, nk = bias_ref.shape[1:]
    if banded:
        r = lax.broadcasted_iota(jnp.int32, (nq, nk), 0)
        s = lax.broadcasted_iota(jnp.int32, (nq, nk), 1)
        qc = (r + WINDOW) // CHUNK
        kc = s // CHUNK
        pen = jnp.where(kc <= qc, jnp.where(kc >= qc - LEFT_CHUNKS, 0.0, NEG), NEG)
    for h in range(ATT_HEADS):
        rows = jnp.broadcast_to(base_ref[h:h + 1, :], (nq, BIAS_W))
        t = pltpu.roll(rows, 0, 1, stride=1, stride_axis=0)[:, :nk] * LOG2E
        bias_ref[h] = t + pen if banded else t


def _attn_prompt_body(q_ref, k0, k1, k2, v0, v1, v2, base_ref, o_ref, bias_ref):
    i = pl.program_id(1)

    @pl.when((pl.program_id(0) == 0) & (i == 0))
    def _():
        _fill_bias(base_ref, bias_ref, True)

    bias_fn = lambda h, j: bias_ref[h, :, j * ATT_QBLK:(j + 1) * ATT_QBLK]
    back = ATT_KBLKS - 1

    @pl.when(i < back)
    def _():
        pens = [jnp.where(i - back + j >= 0, 0.0, NEG) for j in range(back)] + [None]
        _attn_heads(q_ref, [k0, k1, k2], [v0, v1, v2], bias_fn, pens, o_ref)

    @pl.when(i >= back)
    def _():
        _attn_heads(q_ref, [k0, k1, k2], [v0, v1, v2], bias_fn, [None] * ATT_KBLKS, o_ref)


def _attn_prompt(za, base, batch, seq):
    nq = seq // ATT_QBLK
    back = ATT_KBLKS - 1
    qspec = pl.BlockSpec((ATT_QBLK, ATT_WIDTH), lambda b, i: (b * nq + i, 0))

    def kvspec(j, col):
        return pl.BlockSpec((ATT_QBLK, ATT_WIDTH),
                            lambda b, i: (b * nq + jnp.maximum(i - back + j, 0), col))

    return pl.pallas_call(
        _attn_prompt_body,
        grid=(batch, nq),
        in_specs=[qspec] + [kvspec(j, 1) for j in range(ATT_KBLKS)] + [kvspec(j, 2) for j in range(ATT_KBLKS)]
        + [pl.BlockSpec(base.shape, lambda b, i: (0, 0))],
        out_specs=pl.BlockSpec((ATT_QBLK, ATT_WIDTH), lambda b, i: (b * nq + i, 0)),
        out_shape=jax.ShapeDtypeStruct((batch * seq, ATT_WIDTH), BF16),
        scratch_shapes=[pltpu.VMEM((ATT_HEADS, ATT_QBLK, ATT_KBLKS * ATT_QBLK), F32)],
        compiler_params=_params("arbitrary", "arbitrary"),
        name="attn_prompt",
    )(za, za, za, za, za, za, za, base)


def _attn_sample_body(q_ref, kn_ref, vn_ref, ck_ref, cv_ref, base_ref, o_ref, bias_ref):
    @pl.when(pl.program_id(0) == 0)
    def _():
        _fill_bias(base_ref, bias_ref, False)

    cw = ck_ref.shape[0]
    bias_fn = lambda h, j: bias_ref[h, :, :cw] if j == 0 else bias_ref[h, :, cw:]
    _attn_heads(q_ref, [ck_ref, kn_ref], [cv_ref, vn_ref], bias_fn, [None, None], o_ref)


def _attn_sample(za, ck, cv, base, batch, seq):
    cw = ck.shape[1]
    return pl.pallas_call(
        _attn_sample_body,
        grid=(batch,),
        in_specs=[
            pl.BlockSpec((seq, ATT_WIDTH), lambda b: (b, 0)),
            pl.BlockSpec((seq, ATT_WIDTH), lambda b: (b, 1)),
            pl.BlockSpec((seq, ATT_WIDTH), lambda b: (b, 2)),
            pl.BlockSpec((None, cw, ATT_WIDTH), lambda b: (b, 0, 0)),
            pl.BlockSpec((None, cw, ATT_WIDTH), lambda b: (b, 0, 0)),
            pl.BlockSpec(base.shape, lambda b: (0, 0)),
        ],
        out_specs=pl.BlockSpec((seq, ATT_WIDTH), lambda b: (b, 0)),
        out_shape=jax.ShapeDtypeStruct((batch * seq, ATT_WIDTH), BF16),
        scratch_shapes=[pltpu.VMEM((ATT_HEADS, seq, cw + seq), F32)],
        compiler_params=_params("arbitrary"),
        name="attn_sample",
    )(za, za, za, ck, cv, base)


def _rel_bias_base(table):
    top = table[:, 2 * MAX_REL:].astype(F32)
    rev = table[:, ::-1][:, :2 * MAX_REL].astype(F32)
    left = WINDOW - MAX_REL
    return jnp.concatenate([jnp.broadcast_to(top, (ATT_HEADS, left)), rev,
                            jnp.broadcast_to(top, (ATT_HEADS, BIAS_W - left - 2 * MAX_REL))], axis=1)


def _hgrn_consts(c):
    t = np.arange(c)[:, None]
    j = np.arange(c)[None, :]
    mats = [j <= t, j > t]
    masks = []
    m = c // 2
    while m >= 1:
        ref = (t // (2 * m)) * (2 * m) + m - 1
        second = (t % (2 * m)) >= m
        if m < SUBLANES:
            mats.append((second & (j > ref) & (j <= t)) | (~second & (j > t) & (j <= ref)))
        masks.append((t // (2 * m)) == (j // (2 * m)))
        m //= 2
    return (jnp.asarray(np.concatenate(mats, 0).astype(np.float32), BF16),
            jnp.asarray(np.stack(masks).astype(np.float32)))


def _hgrn_body(zb_ref, s0_ref, lower_ref, gon_ref, p_ref, mask_ref, o_ref, sfin_ref, st_ref, *, single_step):
    c = p_ref.shape[1]
    step = pl.program_id(1)

    def load_state():
        for h in range(HG_HEADS):
            st_ref[h] = s0_ref[0, h].T

    if single_step:
        load_state()
    else:
        pl.when(step == 0)(load_state)

    pmat = p_ref[...]
    n_levels = mask_ref.shape[0]
    head = lambda a, h: a[:, h * HG_DK:(h + 1) * HG_DK]
    low = lower_ref[...]
    row = lax.broadcasted_iota(jnp.int32, (c, HG_WIDTH), 0)

    def stage1(r0):
        part = lambda i: zb_ref[r0:r0 + c, i * HG_WIDTH:(i + 1) * HG_WIDTH]
        q = part(0)
        f = low + (1.0 - low) * _sigmoid(part(1))
        lf = jnp.log(f)
        k = 1.0 - f
        ib = part(2)
        v = ib * _sigmoid(ib)
        og = part(3)
        hi = lf.astype(BF16)
        r1 = lf - hi.astype(F32)
        mid = r1.astype(BF16)
        lo = (r1 - mid.astype(F32)).astype(BF16)
        e = (jnp.dot(pmat, hi, preferred_element_type=F32) + jnp.dot(pmat, mid, preferred_element_type=F32)
             + jnp.dot(pmat, lo, preferred_element_type=F32))
        b = e[0:c]
        return dict(q=q, k=k, v=v, e=e, b=b, decay=jnp.exp(e[c - 1:c]), qe=(q * jnp.exp(b)).astype(BF16),
                    kt=(k * jnp.exp(e[c:2 * c])).astype(BF16), vb=v.astype(BF16), qk=q * k,
                    gate=og * _sigmoid(og))

    def stage2(s):
        q, k, e, b = s["q"], s["k"], s["e"], s["b"]
        att = [None] * HG_HEADS
        n_rows_p = 2
        for lvl in range(n_levels):
            m = c >> (lvl + 1)
            if m >= SUBLANES:
                ref = [jnp.broadcast_to(b[p * 2 * m + m - 1:p * 2 * m + m], (2 * m, HG_WIDTH))
                       for p in range(c // (2 * m))]
                x = jnp.exp(-jnp.abs(b - (jnp.concatenate(ref, axis=0) if len(ref) > 1 else ref[0])))
            else:
                x = jnp.exp(e[n_rows_p * c:(n_rows_p + 1) * c])
                n_rows_p += 1
            second = (row & m) != 0
            qm = jnp.where(second, q * x, 0.0).astype(BF16)
            km = jnp.where(second, 0.0, k * x).astype(BF16)
            for h in range(HG_HEADS):
                a = lax.dot_general(head(qm, h), head(km, h), NT, preferred_element_type=F32)
                if lvl > 0:
                    a = a * mask_ref[lvl]
                att[h] = a if att[h] is None else att[h] + a
        return [jnp.dot(att[h].astype(BF16), head(s["vb"], h), preferred_element_type=F32)
                + jnp.sum(head(s["qk"], h), axis=-1, keepdims=True) * head(s["v"], h) for h in range(HG_HEADS)]

    def stage3(r0, s, intra):
        for h in range(HG_HEADS):
            st = st_ref[h]
            inter = lax.dot_general(head(s["qe"], h), st.astype(BF16), NT, preferred_element_type=F32)
            st_ref[h] = st * head(s["decay"], h) + lax.dot_general(head(s["vb"], h), head(s["kt"], h), TN,
                                                                  preferred_element_type=F32)
            o = _rms(inter + intra[h], gon_ref[...]) * head(s["gate"], h)
            o_ref[r0:r0 + c, h * HG_DK:(h + 1) * HG_DK] = o.astype(o_ref.dtype)

    starts = range(0, zb_ref.shape[0], c)
    firsts = [stage1(r0) for r0 in starts]
    intras = [stage2(s) for s in firsts]
    for r0, s, intra in zip(starts, firsts, intras):
        stage3(r0, s, intra)

    def write_state():
        for h in range(HG_HEADS):
            sfin_ref[0, h] = st_ref[h].T

    if single_step:
        write_state()
    else:
        pl.when(step == pl.num_programs(1) - 1)(write_state)


def _hgrn(zb, s0, lower, g_on, batch, seq):
    c = min(HG_C, seq)
    rows = min(HG_CHUNKS_PER_STEP * c, seq)
    assert seq % rows == 0 and rows % c == 0
    pmat, masks = _hgrn_consts(c)
    nc = seq // rows
    return pl.pallas_call(
        functools.partial(_hgrn_body, single_step=nc == 1),
        grid=(batch, nc),
        in_specs=[
            pl.BlockSpec((rows, 4 * HG_WIDTH), lambda b, i: (b * nc + i, 0)),
            pl.BlockSpec((1, HG_HEADS, HG_DK, HG_DK), lambda b, i: (b, 0, 0, 0)),
            pl.BlockSpec((1, HG_WIDTH), lambda b, i: (0, 0)),
            pl.BlockSpec((1, HG_DK), lambda b, i: (0, 0)),
            pl.BlockSpec(pmat.shape, lambda b, i: (0, 0)),
            pl.BlockSpec(masks.shape, lambda b, i: (0, 0, 0)),
        ],
        out_specs=[
            pl.BlockSpec((rows, HG_WIDTH), lambda b, i: (b * nc + i, 0)),
            pl.BlockSpec((1, HG_HEADS, HG_DK, HG_DK), lambda b, i: (b, 0, 0, 0)),
        ],
        out_shape=[
            jax.ShapeDtypeStruct((batch * seq, HG_WIDTH), BF16),
            jax.ShapeDtypeStruct((batch, HG_HEADS, HG_DK, HG_DK), F32),
        ],
        scratch_shapes=[pltpu.VMEM((HG_HEADS, HG_DK, HG_DK), F32)],
        compiler_params=_params("arbitrary", "arbitrary"),
        name="hgrn2",
    )(zb, s0, lower, g_on, pmat, masks)


def _split_bf16(x):
    hi = x.astype(BF16)
    return hi, (x - hi.astype(F32)).astype(BF16)


def _merge_body(att_p, hg_p, zg_p, x_p, att_s, hg_s, zg_s, x_s, wpa_ref, wpb_ref, wout_ref, gffn_ref, wr_ref, br_ref,
                x1_ref, h2_ref, route_ref, cnt_ref, *, n_first):
    weights = (wpa_ref, wpb_ref, wout_ref, gffn_ref, wr_ref, br_ref)
    outs = (x1_ref, h2_ref, route_ref, cnt_ref)
    i = pl.program_id(0)
    pl.when(i < n_first)(functools.partial(_merge_block, att_p, hg_p, zg_p, x_p, *weights, *outs))
    pl.when(i >= n_first)(functools.partial(_merge_block, att_s, hg_s, zg_s, x_s, *weights, *outs))


def _merge_block(att_ref, hg_ref, zg_ref, x_ref, wpa_ref, wpb_ref, wout_ref, gffn_ref, wr_ref, br_ref,
                 x1_ref, h2_ref, route_ref, cnt_ref):
    pa = jnp.dot(att_ref[...].astype(BF16), wpa_ref[...], preferred_element_type=F32)
    pb = jnp.dot(hg_ref[...].astype(BF16), wpb_ref[...], preferred_element_type=F32)
    y = _sigmoid(zg_ref[:, :D_MODEL]) * pa + _sigmoid(zg_ref[:, D_MODEL:]) * pb
    x1 = x_ref[...] + jnp.dot(y.astype(BF16), wout_ref[...], preferred_element_type=F32)
    x1_ref[...] = x1
    h2 = _rms(x1, gffn_ref[...])
    h2_ref[...] = h2.astype(BF16)

    h_hi, h_lo = _split_bf16(h2)
    w_hi, w_lo = _split_bf16(wr_ref[...])
    both = jnp.dot(h_hi, jnp.concatenate([w_hi, w_lo], axis=1), preferred_element_type=F32)
    logits = both[:, :LANES] + both[:, LANES:] + jnp.dot(h_lo, w_hi, preferred_element_type=F32) + br_ref[...]
    lane = lax.broadcasted_iota(jnp.int32, logits.shape, 1).astype(F32)
    cur = logits
    vals, idxs = [], []
    for _ in range(TOP_K):
        m = jnp.max(cur, axis=-1, keepdims=True)
        idx = jnp.min(jnp.where(cur == m, lane, float(LANES)), axis=-1, keepdims=True)
        vals.append(m)
        idxs.append(idx)
        cur = jnp.where(lane == idx, -jnp.inf, cur)
    es = [jnp.exp(v - vals[0]) for v in vals]
    inv = 1.0 / functools.reduce(jnp.add, es)
    route = jnp.zeros_like(logits)
    for k, (ex, idx) in enumerate(zip(es, idxs)):
        route = (route + jnp.where(lane == idx, 1.0, 0.0) + jnp.where(lane == float(ROUTE_IDX + k), idx, 0.0)
                 + jnp.where(lane == float(ROUTE_P + k), ex * inv, 0.0))
    route_ref[...] = route
    sel = jnp.where(lane < float(N_EXPERTS), route, 0.0)
    for blk in range(cnt_ref.shape[0]):
        cnt_ref[blk] = jnp.sum(sel[blk * TBK:(blk + 1) * TBK], axis=0, keepdims=True)


def _merge(first, second, wpa, wpb, wout, g_ffn, w_router, b_router):
    tm = MERGE_TM
    assert first[3].shape[0] % tm == 0 and second[3].shape[0] % tm == 0
    n1, n2 = first[3].shape[0] // tm, second[3].shape[0] // tm
    widths = (ATT_WIDTH, HG_WIDTH, 2 * D_MODEL, D_MODEL)
    spec1 = [pl.BlockSpec((tm, w), lambda i: (jnp.minimum(i, n1 - 1), 0)) for w in widths]
    spec2 = [pl.BlockSpec((tm, w), lambda i: (jnp.maximum(i - n1, 0), 0)) for w in widths]
    row = lambda w: pl.BlockSpec((tm, w), lambda i: (i, 0))
    full = lambda a: pl.BlockSpec(a.shape, lambda i: (0,) * a.ndim)
    n = (n1 + n2) * tm
    return pl.pallas_call(
        functools.partial(_merge_body, n_first=n1),
        grid=(n1 + n2,),
        in_specs=spec1 + spec2 + [full(wpa), full(wpb), full(wout), full(g_ffn), full(w_router), full(b_router)],
        out_specs=[row(D_MODEL), row(D_MODEL), row(LANES),
                   pl.BlockSpec((tm // TBK, 1, LANES), lambda i: (i, 0, 0))],
        out_shape=[
            jax.ShapeDtypeStruct((n, D_MODEL), F32),
            jax.ShapeDtypeStruct((n, D_MODEL), BF16),
            jax.ShapeDtypeStruct((n, LANES), F32),
            jax.ShapeDtypeStruct((n // TBK, 1, LANES), F32),
        ],
        compiler_params=_params("arbitrary"),
        name="merge_router",
    )(*first, *second, wpa, wpb, wout, g_ffn, w_router, b_router)


def _route_plan(cnt):
    pc = (cnt + ROW_CH - 1) // ROW_CH
    loff = jnp.cumsum(pc, axis=1) - pc
    tot = jnp.sum(pc, axis=0)
    reg = (tot + CH_PER_TILE - 1) // CH_PER_TILE * CH_PER_TILE
    gstart = jnp.cumsum(reg) - reg
    goff = gstart[None, :] + jnp.cumsum(pc, axis=0) - pc
    ntiles = jnp.sum(reg) // CH_PER_TILE
    present = reg > 0
    ids = jnp.arange(N_EXPERTS, dtype=jnp.int32)
    later = (ids[None, :] > ids[:, None]) & present[None, :]
    nxt = jnp.min(jnp.where(later, ids[None, :], N_EXPERTS), axis=1)
    nxt = jnp.where(nxt == N_EXPERTS, -1, nxt)
    slot = (jnp.cumsum(present.astype(jnp.int32)) - 1) % 2
    first = jnp.min(jnp.where(present, ids, N_EXPERTS)).reshape(1)
    j = jnp.arange(BLK_CH, dtype=jnp.int32)
    run = jnp.sum(((loff + pc)[:, None, :] <= j[None, :, None]).astype(jnp.int32), axis=2)
    shift = jnp.sum(jnp.where(run[:, :, None] == ids[None, None, :], (goff - loff)[:, None, :], 0), axis=2)
    gmap = shift + j[None, :]
    i32 = lambda a: a.astype(jnp.int32)
    return dict(loff=i32(loff), gmap=i32(gmap), totc=i32(jnp.sum(pc, axis=1)),
                pad_start=i32(gstart + tot), pad_cnt=i32(reg - tot), ntiles=i32(ntiles).reshape(1),
                tile_start=i32(gstart // CH_PER_TILE), tile_count=i32(reg // CH_PER_TILE),
                next_expert=i32(nxt), weight_slot=i32(slot), first_expert=i32(first))


def _chunk_rows(ref, chunk):
    return ref.at[pl.ds(pl.multiple_of(chunk * ROW_CH, ROW_CH), ROW_CH)]


def _for_chunks(n, do):
    log_unroll = 2
    groups = lax.shift_right_logical(n, log_unroll)

    def group(i, carry):
        for u in range(1 << log_unroll):
            do(lax.shift_left(i, log_unroll) + u)
        return carry
    lax.fori_loop(0, groups, group, 0)

    def single(c, carry):
        do(c)
        return carry
    lax.fori_loop(lax.shift_left(groups, log_unroll), n, single, 0)


def _wait_chunks(n, copy_of_rows):
    for bit in range((RB // ROW_CH).bit_length()):
        @pl.when(((n >> bit) & 1) == 1)
        def _(bit=bit):
            copy_of_rows((1 << bit) * ROW_CH).wait()


def _dispatch_body(gmap_s, totc_s, pads_s, padn_s, nt_s, h2_ref, route_ref, tri_ref, loffv_ref,
                   xs_hbm, buf, zbuf, sem):
    b = pl.program_id(0)
    nb = pl.num_programs(0)
    slot = lax.rem(b, 2)
    fill_sem, tile_sem = 2, 3
    nt_max = xs_hbm.shape[0] // EXP_TM

    def copy_out(src, gchunk, sem_i):
        return pltpu.make_async_copy(src, _chunk_rows(xs_hbm, gchunk), sem.at[sem_i])

    def zero_tile(t):
        return pltpu.make_async_copy(zbuf, xs_hbm.at[pl.ds(pl.multiple_of(t * EXP_TM, EXP_TM), EXP_TM)],
                                     sem.at[tile_sem])

    def wait_block(bb, slot_):
        _wait_chunks(totc_s[bb], lambda rows: pltpu.make_async_copy(
            buf.at[slot_, pl.ds(0, rows)], xs_hbm.at[pl.ds(0, rows)], sem.at[slot_]))

    @pl.when(b == 0)
    def _():
        zbuf[...] = jnp.zeros_like(zbuf)

    @pl.when(b >= 2)
    def _():
        wait_block(b - 2, slot)

    rt = route_ref[...].T
    rank_t = jnp.dot(rt[0:N_EXPERTS].astype(BF16), tri_ref[...], preferred_element_type=F32)
    lpos_t = loffv_ref[...] * float(ROW_CH) + rank_t
    erow = lax.broadcasted_iota(jnp.int32, (N_EXPERTS, TBK), 0).astype(F32)
    lposk = [jnp.sum(jnp.where(rt[ROUTE_IDX + k:ROUTE_IDX + k + 1] == erow, lpos_t, 0.0), axis=0, keepdims=True)
             for k in range(TOP_K)]
    h2 = h2_ref[...]
    for r0 in range(0, RB, PERM_TM):
        rrow = lax.broadcasted_iota(jnp.int32, (PERM_TM, TBK), 0).astype(F32) + float(r0)
        perm = functools.reduce(jnp.add, [jnp.where(lposk[k] == rrow, 1.0, 0.0) for k in range(TOP_K)])
        buf[slot, r0:r0 + PERM_TM, :] = jnp.dot(perm.astype(BF16), h2, preferred_element_type=F32)

    _for_chunks(totc_s[b], lambda c: copy_out(_chunk_rows(buf.at[slot], c), gmap_s[b * BLK_CH + c], slot).start())

    @pl.when(b == nb - 1)
    def _():
        zchunk = _chunk_rows(zbuf, 0)

        def fill_expert(e, carry):
            def one(c, carry2):
                copy_out(zchunk, pads_s[e] + c, fill_sem).start()
                return carry2
            lax.fori_loop(0, padn_s[e], one, 0)
            return carry
        lax.fori_loop(0, N_EXPERTS, fill_expert, 0)

        def fill_tile(t, carry):
            zero_tile(t).start()
            return carry
        lax.fori_loop(nt_s[0], nt_max, fill_tile, 0)

        @pl.when(b >= 1)
        def _():
            wait_block(b - 1, 1 - slot)
        wait_block(b, slot)

        def drain_expert(e, carry):
            def one(c, carry2):
                copy_out(zchunk, 0, fill_sem).wait()
                return carry2
            lax.fori_loop(0, padn_s[e], one, 0)
            return carry
        lax.fori_loop(0, N_EXPERTS, drain_expert, 0)

        def drain_tile(t, carry):
            zero_tile(0).wait()
            return carry
        lax.fori_loop(nt_s[0], nt_max, drain_tile, 0)


def _dispatch(plan, h2, route, loffv, n_rows):
    nb = h2.shape[0] // TBK
    tri = jnp.asarray(np.triu(np.ones((TBK, TBK), np.float32), 1), BF16)
    grid_spec = pltpu.PrefetchScalarGridSpec(
        num_scalar_prefetch=5,
        grid=(nb,),
        in_specs=[
            pl.BlockSpec((TBK, D_MODEL), lambda b, *_: (b, 0)),
            pl.BlockSpec((TBK, LANES), lambda b, *_: (b, 0)),
            pl.BlockSpec((TBK, TBK), lambda b, *_: (0, 0)),
            pl.BlockSpec((None, N_EXPERTS, 1), lambda b, *_: (b, 0, 0)),
        ],
        out_specs=pl.BlockSpec(memory_space=pl.ANY),
        scratch_shapes=[pltpu.VMEM((2, RB, D_MODEL), F32), pltpu.VMEM((EXP_TM, D_MODEL), F32),
                        pltpu.SemaphoreType.DMA((4,))],
    )
    return pl.pallas_call(
        _dispatch_body,
        grid_spec=grid_spec,
        out_shape=jax.ShapeDtypeStruct((n_rows, D_MODEL), F32),
        compiler_params=_params("arbitrary"),
        name="moe_dispatch",
    )(plan["gmap"].reshape(-1), plan["totc"], plan["pad_start"], plan["pad_cnt"], plan["ntiles"],
      h2, route, tri, loffv)


def _experts_body(t0_s, n_s, nxt_s, par_s, first_s, nt_s, xs_hbm, wgu_hbm, bgu_ref, wd_hbm, bd_ref, ys_hbm,
                  wg_f, wd_f, wg_b, wd_b, xbuf, ybuf, wsem, xsem, ysem):
    e = pl.program_id(0)
    nt = nt_s[0]
    nt_max = xs_hbm.shape[0] // EXP_TM
    tile_rows = lambda t: pl.ds(pl.multiple_of(t * EXP_TM, EXP_TM), EXP_TM)

    def weight_copies(ex, s, piece):
        rows = pl.ds(pl.multiple_of(piece * W_PIECE_ROWS, W_PIECE_ROWS), W_PIECE_ROWS)
        return (pltpu.make_async_copy(wgu_hbm.at[ex, rows], wg_f.at[s, rows], wsem.at[s, 0]),
                pltpu.make_async_copy(wd_hbm.at[ex, rows], wd_f.at[s, rows], wsem.at[s, 1]))

    def x_copy(t, s):
        return pltpu.make_async_copy(xs_hbm.at[tile_rows(t)], xbuf.at[s], xsem.at[s])

    def y_copy(t, s):
        return pltpu.make_async_copy(ybuf.at[s], ys_hbm.at[tile_rows(t)], ysem.at[s])

    @pl.when(e == 0)
    def _():
        for piece in range(W_PIECES):
            for cp in weight_copies(first_s[0], 0, piece):
                cp.start()
        x_copy(0, 0).start()

    @pl.when(n_s[e] > 0)
    def _():
        s_w = par_s[e]
        for piece in range(W_PIECES):
            for cp in weight_copies(e, s_w, piece):
                cp.wait()
        wg_b[...] = wg_f[s_w].astype(BF16)
        wd_b[...] = wd_f[s_w].astype(BF16)
        has_next = nxt_s[e] >= 0

        def request(piece):
            for cp in weight_copies(nxt_s[e], 1 - s_w, piece):
                cp.start()

        def tile(i, carry):
            t = t0_s[e] + i
            s = t & 1
            x_copy(t, s).wait()

            @pl.when(has_next & (i < W_PIECES))
            def _():
                request(i)

            @pl.when(t + 1 < nt)
            def _():
                x_copy(t + 1, 1 - s).start()

            @pl.when(t >= 2)
            def _():
                y_copy(t - 2, s).wait()

            gu = jnp.dot(xbuf[s].astype(BF16), wg_b[...], preferred_element_type=F32) + bgu_ref[...]
            gate = jnp.minimum(gu[:, :D_FF], SWIGLU_LIMIT)
            up = jnp.clip(gu[:, D_FF:], -SWIGLU_LIMIT, SWIGLU_LIMIT)
            act = (up + 1.0) * gate * _sigmoid(SWIGLU_ALPHA * gate)
            ybuf[s] = jnp.dot(act.astype(BF16), wd_b[...], preferred_element_type=F32) + bd_ref[...]
            y_copy(t, s).start()
            return carry
        lax.fori_loop(0, n_s[e], tile, 0)

        @pl.when(has_next)
        def _():
            def rest(piece, carry):
                request(piece)
                return carry
            lax.fori_loop(jnp.minimum(n_s[e], W_PIECES), W_PIECES, rest, 0)

    @pl.when(e == pl.num_programs(0) - 1)
    def _():
        @pl.when(nt >= 2)
        def _():
            y_copy(nt - 2, nt & 1).wait()
        y_copy(nt - 1, (nt - 1) & 1).wait()
        ybuf[0] = jnp.zeros((EXP_TM, D_MODEL), F32)

        def fill(t, carry):
            y_copy(t, 0).start()
            return carry
        lax.fori_loop(nt, nt_max, fill, 0)

        def drain(t, carry):
            y_copy(0, 0).wait()
            return carry
        lax.fori_loop(nt, nt_max, drain, 0)


def _experts(plan, xs, w_gu, b_gu, w_down, b_down):
    n_rows = xs.shape[0]
    of_expert = lambda e, *_: (e, 0, 0)
    grid_spec = pltpu.PrefetchScalarGridSpec(
        num_scalar_prefetch=6,
        grid=(N_EXPERTS,),
        in_specs=[
            pl.BlockSpec(memory_space=pl.ANY),
            pl.BlockSpec(memory_space=pl.ANY),
            pl.BlockSpec((None, 1, 2 * D_FF), of_expert),
            pl.BlockSpec(memory_space=pl.ANY),
            pl.BlockSpec((None, 1, D_MODEL), of_expert),
        ],
        out_specs=pl.BlockSpec(memory_space=pl.ANY),
        scratch_shapes=[pltpu.VMEM((2, D_MODEL, 2 * D_FF), F32), pltpu.VMEM((2, D_FF, D_MODEL), F32),
                        pltpu.VMEM((D_MODEL, 2 * D_FF), BF16), pltpu.VMEM((D_FF, D_MODEL), BF16),
                        pltpu.VMEM((2, EXP_TM, D_MODEL), F32), pltpu.VMEM((2, EXP_TM, D_MODEL), F32),
                        pltpu.SemaphoreType.DMA((2, 2)), pltpu.SemaphoreType.DMA((2,)),
                        pltpu.SemaphoreType.DMA((2,))],
    )
    return pl.pallas_call(
        _experts_body,
        grid_spec=grid_spec,
        out_shape=jax.ShapeDtypeStruct((n_rows, D_MODEL), F32),
        compiler_params=_params("arbitrary"),
        name="moe_experts",
    )(plan["tile_start"], plan["tile_count"], plan["next_expert"], plan["weight_slot"], plan["first_expert"],
      plan["ntiles"], xs, w_gu, b_gu, w_down, b_down)


def _combine_body(gmap_s, totc_s, route_ref, tril_ref, loffrow_ref, x1_ref, gfin_ref, ys_hbm, y_ref, buf, sem):
    b = pl.program_id(0)
    nb = pl.num_programs(0)
    slot = lax.rem(b, 2)

    def copy_in(slot_, lchunk, gchunk):
        return pltpu.make_async_copy(_chunk_rows(ys_hbm, gchunk), _chunk_rows(buf.at[slot_], lchunk), sem.at[slot_])

    def fetch_block(bb, slot_):
        _for_chunks(totc_s[bb], lambda c: copy_in(slot_, c, gmap_s[bb * BLK_CH + c]).start())

    def wait_block(bb, slot_):
        _wait_chunks(totc_s[bb], lambda rows: pltpu.make_async_copy(
            ys_hbm.at[pl.ds(0, rows)], buf.at[slot_, pl.ds(0, rows)], sem.at[slot_]))

    @pl.when(b == 0)
    def _():
        buf[...] = jnp.zeros_like(buf)
        fetch_block(0, 0)

    @pl.when(b + 1 < nb)
    def _():
        fetch_block(b + 1, 1 - slot)

    wait_block(b, slot)

    route = route_ref[...]
    lane = lax.broadcasted_iota(jnp.int32, (1, LANES), 1).astype(F32)
    sel = jnp.where(lane < float(N_EXPERTS), route, 0.0).astype(BF16)
    rank = jnp.dot(tril_ref[...], sel, preferred_element_type=F32)
    lpos = loffrow_ref[...] * float(ROW_CH) + rank
    lposk, pk = [], []
    for k in range(TOP_K):
        idx = route[:, ROUTE_IDX + k:ROUTE_IDX + k + 1]
        lposk.append(jnp.sum(jnp.where(lane == idx, lpos, 0.0), axis=-1, keepdims=True))
        pk.append(route[:, ROUTE_P + k:ROUTE_P + k + 1])
    acc = x1_ref[...]
    col =lax.broadcasted_iota(jnp.int32, (TBK, PERM_TM), 1).astype(F32).astype(BF16)
    pkb = [p.astype(BF16) for p in pk]
    zero = jnp.zeros((), BF16)
    for r0 in range(0, RB, PERM_TM):
        rel = [(lposk[k] - float(r0)).astype(BF16) for k in range(TOP_K)]
        w = functools.reduce(jnp.add, [jnp.where(rel[k] == col, pkb[k], zero) for k in range(TOP_K)])
        acc = acc + jnp.dot(w, buf[slot, r0:r0 + PERM_TM, :].astype(BF16), preferred_element_type=F32)
    y_ref[...] = _rms(acc, gfin_ref[...])


def _combine(plan, blocks, route, loffrow, x1, g_final, ys):
    b0, b1 = blocks
    nb = b1 - b0
    tril = jnp.asarray(np.tril(np.ones((TBK, TBK), np.float32), -1), BF16)
    grid_spec = pltpu.PrefetchScalarGridSpec(
        num_scalar_prefetch=2,
        grid=(nb,),
        in_specs=[
            pl.BlockSpec((TBK, LANES), lambda b, *_: (b + b0, 0)),
            pl.BlockSpec((TBK, TBK), lambda b, *_: (0, 0)),
            pl.BlockSpec((None, 1, LANES), lambda b, *_: (b + b0, 0, 0)),
            pl.BlockSpec((TBK, D_MODEL), lambda b, *_: (b + b0, 0)),
            pl.BlockSpec((1, D_MODEL), lambda b, *_: (0, 0)),
            pl.BlockSpec(memory_space=pl.ANY),
        ],
        out_specs=pl.BlockSpec((TBK, D_MODEL), lambda b, *_: (b, 0)),
        scratch_shapes=[pltpu.VMEM((2, RB, D_MODEL), F32), pltpu.SemaphoreType.DMA((2,))],
    )
    return pl.pallas_call(
        _combine_body,
        grid_spec=grid_spec,
        out_shape=jax.ShapeDtypeStruct((nb * TBK, D_MODEL), F32),
        compiler_params=_params("arbitrary"),
        name="moe_combine",
    )(plan["gmap"][b0:b1].reshape(-1), plan["totc"][b0:b1], route, tril, loffrow, x1, g_final, ys)


def kernel(x_prompt, x_sample, cache_k, cache_v, state_s, g_mix, w_in, rel_bias, lb_logits, g_out_norm,
           w_pa, w_pb, w_out, g_ffn, w_router, b_router, w_gu, b_gu, w_down, b_down, g_final):
    B, T = x_prompt.shape[:2]
    DB, S = x_sample.shape[:2]
    depth = w_in.shape[0]
    assert depth == 1 and T % ATT_QBLK == 0 and S == CHUNK
    cw = cache_k.shape[2]
    assert cw == WINDOW
    l = 0

    lower = jnp.cumsum(jax.nn.softmax(lb_logits.astype(F32), axis=0), axis=0)[l].reshape(1, HG_WIDTH)
    w_in_b = w_in[l].astype(BF16)
    wpa, wpb, wout = w_pa[l].astype(BF16), w_pb[l].astype(BF16), w_out[l].astype(BF16)
    row = lambda a: a.reshape(1, -1).astype(F32)
    base = _rel_bias_base(rel_bias[l])
    b_gu3 = b_gu[l].reshape(N_EXPERTS, 1, 2 * D_FF)
    b_down3 = b_down[l].reshape(N_EXPERTS, 1, D_MODEL)
    pad_e = LANES - N_EXPERTS
    wr = jnp.pad(w_router[l].astype(F32), ((0, 0), (0, pad_e)))
    br = jnp.concatenate([b_router[l].astype(F32), jnp.full((pad_e,), NEG, F32)]).reshape(1, LANES)

    n_tok = B * T + DB * S
    nb, nbp = n_tok // TBK, (B * T) // TBK

    def front(x, batch, seq, s0, attend):
        xf = x.reshape(batch * seq, D_MODEL)
        za, zb, zg = _inproj(xf, row(g_mix[l]), w_in_b)
        att = attend(za)
        hg, s_fin = _hgrn(zb, s0, lower, row(g_out_norm[l]), batch, seq)
        za3 = za.reshape(batch, seq, 3 * ATT_WIDTH)
        heads = lambda a: a.reshape(1, batch, a.shape[1], ATT_HEADS, ATT_DIM)
        keep = min(WINDOW, seq)
        nk = heads(za3[:, seq - keep:, ATT_WIDTH:2 * ATT_WIDTH])
        nv = heads(za3[:, seq - keep:, 2 * ATT_WIDTH:])
        return dict(mix=(att, hg, zg, xf), nk=nk, nv=nv, s=s_fin[None])

    ck = cache_k[l].reshape(DB, cw, ATT_WIDTH)
    cv = cache_v[l].reshape(DB, cw, ATT_WIDTH)
    fp = front(x_prompt, B, T, jnp.zeros((B, HG_HEADS, HG_DK, HG_DK), F32), lambda za: _attn_prompt(za, base, B, T))
    fs = front(x_sample, DB, S, state_s[l].astype(F32), lambda za: _attn_sample(za, ck, cv, base, DB, S))

    x1, h2, route, cnt = _merge(fp["mix"], fs["mix"], wpa, wpb, wout, row(g_ffn[l]), wr, br)
    cnt = cnt[:, 0, :N_EXPERTS].astype(jnp.int32)
    max_rows = n_tok * TOP_K + nb * N_EXPERTS * (ROW_CH - 1) + N_EXPERTS * (EXP_TM - 1)
    nt_max = -(-max_rows // EXP_TM)
    plan = _route_plan(cnt)
    loff_f = plan["loff"].astype(F32)
    xs = _dispatch(plan, h2, route, loff_f[:, :, None], nt_max * EXP_TM)
    ysort = _experts(plan, xs, w_gu[l], b_gu3, w_down[l], b_down3)
    loffrow = jnp.pad(loff_f, ((0, 0), (0, pad_e)))[:, None, :]
    yp = _combine(plan, (0, nbp), route, loffrow, x1, row(g_final), ysort)
    ys = _combine(plan, (nbp, nb), route, loffrow, x1, row(g_final), ysort)
    return (yp.reshape(B, T, D_MODEL), ys.reshape(DB, S, D_MODEL), fp["nk"], fp["nv"], fp["s"],
            fs["nk"], fs["nv"], fs["s"])
```

```python
import functools

import numpy as np
import jax
import jax.numpy as jnp
from jax import lax
from jax.experimental import pallas as pl
from jax.experimental.pallas import tpu as pltpu

F32 = jnp.float32
BF16 = jnp.bfloat16

D_MODEL = 1024
CHUNK = 64
LEFT_CHUNKS = 8
WINDOW = LEFT_CHUNKS * CHUNK
ATT_HEADS = 8
ATT_DIM = 64
ATT_WIDTH = ATT_HEADS * ATT_DIM
MAX_REL = 256
HG_HEADS = 4
HG_DK = 128
HG_WIDTH = HG_HEADS * HG_DK
N_EXPERTS = 32
TOP_K = 4
D_FF = D_MODEL
SWIGLU_LIMIT = 7.0
SWIGLU_ALPHA = 1.702
RMS_EPS = 1e-5

LANES = 128
NEG = -1e30
LOG2E = 1.4426950408889634
ATT_QBLK = 4 * CHUNK
ATT_KBLKS = LEFT_CHUNKS * CHUNK // ATT_QBLK + 1
ATT_PAIRS_PER_STAGE = 2
HG_C = 128
HG_CHUNKS_PER_STEP = 4
VMEM_LIMIT = 56 * 1024 * 1024
BIAS_W = 1024
SUBLANES = 8
TBK = 256
ROW_CH = SUBLANES
RB = TBK * TOP_K + N_EXPERTS * ROW_CH
MERGE_TM = 2 * TBK
PERM_TM = 256
EXP_TM = 512
W_PIECES = 4
W_PIECE_ROWS = D_MODEL // W_PIECES
CH_PER_TILE = EXP_TM // ROW_CH
BLK_CH = RB // ROW_CH
ROUTE_IDX = 64
ROUTE_P = 72

NT = (((1,), (1,)), ((), ()))
TN = (((0,), (0,)), ((), ()))


def _rms(x, g):
    return x * lax.rsqrt(jnp.mean(x * x, axis=-1, keepdims=True) + RMS_EPS) * g


def _sigmoid(x):
    return 1.0 / (1.0 + jnp.exp(-x))


def _params(*sem):
    return pltpu.CompilerParams(dimension_semantics=sem, vmem_limit_bytes=VMEM_LIMIT)


def _inproj_body(x_ref, g_ref, w_ref, za_ref, zb_ref, zg_ref):
    h = _rms(x_ref[...], g_ref[...]).astype(BF16)
    a, b = 3 * ATT_WIDTH, 3 * ATT_WIDTH + 4 * HG_WIDTH
    za_ref[...] = jnp.dot(h, w_ref[:, :a], preferred_element_type=F32)
    zb_ref[...] = jnp.dot(h, w_ref[:, a:b], preferred_element_type=F32)
    zg_ref[...] = jnp.dot(h, w_ref[:, b:], preferred_element_type=F32)


def _inproj(x, g, w_bf16, tm=512):
    n = x.shape[0]
    assert n % tm == 0
    cols = w_bf16.shape[1]
    wa, wb, wg = 3 * ATT_WIDTH, 4 * HG_WIDTH, 2 * D_MODEL
    return pl.pallas_call(
        _inproj_body,
        grid=(n // tm,),
        in_specs=[
            pl.BlockSpec((tm, D_MODEL), lambda i: (i, 0)),
            pl.BlockSpec((1, D_MODEL), lambda i: (0, 0)),
            pl.BlockSpec((D_MODEL, cols), lambda i: (0, 0)),
        ],
        out_specs=[
            pl.BlockSpec((tm, wa), lambda i: (i, 0)),
            pl.BlockSpec((tm, wb), lambda i: (i, 0)),
            pl.BlockSpec((tm, wg), lambda i: (i, 0)),
        ],
        out_shape=[
            jax.ShapeDtypeStruct((n, wa), F32),
            jax.ShapeDtypeStruct((n, wb), F32),
            jax.ShapeDtypeStruct((n, wg), F32),
        ],
        compiler_params=_params("arbitrary"),
        name="inproj",
    )(x, g, w_bf16)


def _attn_heads(q_ref, k_refs, v_refs, bias_fn, pens, o_ref):
    lane = lax.broadcasted_iota(jnp.int32, (1, LANES), 1)
    first = lane < ATT_DIM
    halves = (first, lane >= ATT_DIM)
    for hp0 in range(0, ATT_HEADS // 2, ATT_PAIRS_PER_STAGE):
        pairs = range(hp0, hp0 + ATT_PAIRS_PER_STAGE)
        sl = {hp: slice(hp * LANES, (hp + 1) * LANES) for hp in pairs}
        scores = {}
        for hp in pairs:
            q2 = q_ref[:, sl[hp]] * (ATT_DIM ** -0.5 * LOG2E)
            ks = [k[:, sl[hp]].astype(BF16) for k in k_refs]
            for half, mine in enumerate(halves):
                qm = jnp.where(mine, q2, 0.0).astype(BF16)
                ss = []
                for j, kj in enumerate(ks):
                    s = lax.dot_general(qm, kj, NT, preferred_element_type=F32) + bias_fn(2 * hp + half, j)
                    if pens[j] is not None:
                        s = s + pens[j]
                    ss.append(s)
                scores[hp, half] = ss
        for hp in pairs:
            outs = []
            for half, mine in enumerate(halves):
                ss = scores[hp, half]
                vs = [jnp.where(mine, v[:, sl[hp]], 1.0).astype(BF16) for v in v_refs]
                if all(s.shape == ss[0].shape for s in ss):
                    m = jnp.max(functools.reduce(jnp.maximum, ss), axis=-1, keepdims=True)
                else:
                    m = functools.reduce(jnp.maximum, [jnp.max(s, axis=-1, keepdims=True) for s in ss])
                outs.append(functools.reduce(jnp.add, [
                    jnp.dot(jnp.exp2(s - m).astype(BF16), vj, preferred_element_type=F32) for s, vj in zip(ss, vs)]))
            num = jnp.where(first, outs[0], outs[1])
            den = pltpu.roll(jnp.where(first, outs[1], outs[0]), ATT_DIM, 1)
            o_ref[:, sl[hp]] = (num * (1.0 / den)).astype(o_ref.dtype)


def _fill_bias(base_ref, bias_ref, banded):
    nq, nk = bias_ref.shape[1:]
    if banded:
        r = lax.broadcasted_iota(jnp.int32, (nq, nk), 0)
        s = lax.broadcasted_iota(jnp.int32, (nq, nk), 1)
        qc = (r + WINDOW) // CHUNK
        kc = s // CHUNK
        pen = jnp.where(kc <= qc, jnp.where(kc >= qc - LEFT_CHUNKS, 0.0, NEG), NEG)
    for h in range(ATT_HEADS):
        rows = jnp.broadcast_to(base_ref[h:h + 1, :], (nq, BIAS_W))
        t = pltpu.roll(rows, 0, 1, stride=1, stride_axis=0)[:, :nk] * LOG2E
        bias_ref[h] = t + pen if banded else t


def _attn_prompt_body(q_ref, k0, k1, k2, v0, v1, v2, base_ref, o_ref, bias_ref):
    i = pl.program_id(1)

    @pl.when((pl.program_id(0) == 0) & (i == 0))
    def _():
        _fill_bias(base_ref, bias_ref, True)

    bias_fn = lambda h, j: bias_ref[h, :, j * ATT_QBLK:(j + 1) * ATT_QBLK]
    back = ATT_KBLKS - 1

    @pl.when(i < back)
    def _():
        pens = [jnp.where(i - back + j >= 0, 0.0, NEG) for j in range(back)] + [None]
        _attn_heads(q_ref, [k0, k1, k2], [v0, v1, v2], bias_fn, pens, o_ref)

    @pl.when(i >= back)
    def _():
        _attn_heads(q_ref, [k0, k1, k2], [v0, v1, v2], bias_fn, [None] * ATT_KBLKS, o_ref)


def _attn_prompt(za, base, batch, seq):
    assert seq % ATT_QBLK == 0 and BIAS_W - ATT_KBLKS * ATT_QBLK >= ATT_QBLK - 1 and WINDOW >= MAX_REL
    nq = seq // ATT_QBLK
    back = ATT_KBLKS - 1
    qspec = pl.BlockSpec((ATT_QBLK, ATT_WIDTH), lambda b, i: (b * nq + i, 0))

    def kvspec(j, col):
        return pl.BlockSpec((ATT_QBLK, ATT_WIDTH),
                            lambda b, i: (b * nq + jnp.maximum(i - back + j, 0), col))

    return pl.pallas_call(
        _attn_prompt_body,
        grid=(batch, nq),
        in_specs=[qspec] + [kvspec(j, 1) for j in range(ATT_KBLKS)] + [kvspec(j, 2) for j in range(ATT_KBLKS)]
        + [pl.BlockSpec(base.shape, lambda b, i: (0, 0))],
        out_specs=pl.BlockSpec((ATT_QBLK, ATT_WIDTH), lambda b, i: (b * nq + i, 0)),
        out_shape=jax.ShapeDtypeStruct((batch * seq, ATT_WIDTH), BF16),
        scratch_shapes=[pltpu.VMEM((ATT_HEADS, ATT_QBLK, ATT_KBLKS * ATT_QBLK), F32)],
        compiler_params=_params("arbitrary", "arbitrary"),
        name="attn_prompt",
    )(za, za, za, za, za, za, za, base)


def _attn_sample_body(q_ref, kn_ref, vn_ref, ck_ref, cv_ref, base_ref, o_ref, bias_ref):
    @pl.when(pl.program_id(0) == 0)
    def _():
        _fill_bias(base_ref, bias_ref, False)

    cw = ck_ref.shape[0]
    bias_fn = lambda h, j: bias_ref[h, :, :cw] if j == 0 else bias_ref[h, :, cw:]
    _attn_heads(q_ref, [ck_ref, kn_ref], [cv_ref, vn_ref], bias_fn, [None, None], o_ref)


def _attn_sample(za, ck, cv, base, batch, seq):
    cw = ck.shape[1]
    return pl.pallas_call(
        _attn_sample_body,
        grid=(batch,),
        in_specs=[
            pl.BlockSpec((seq, ATT_WIDTH), lambda b: (b, 0)),
            pl.BlockSpec((seq, ATT_WIDTH), lambda b: (b, 1)),
            pl.BlockSpec((seq, ATT_WIDTH), lambda b: (b, 2)),
            pl.BlockSpec((None, cw, ATT_WIDTH), lambda b: (b, 0, 0)),
            pl.BlockSpec((None, cw, ATT_WIDTH), lambda b: (b, 0, 0)),
            pl.BlockSpec(base.shape, lambda b: (0, 0)),
        ],
        out_specs=pl.BlockSpec((seq, ATT_WIDTH), lambda b: (b, 0)),
        out_shape=jax.ShapeDtypeStruct((batch * seq, ATT_WIDTH), BF16),
        scratch_shapes=[pltpu.VMEM((ATT_HEADS, seq, cw + seq), F32)],
        compiler_params=_params("arbitrary"),
        name="attn_sample",
    )(za, za, za, ck, cv, base)


def _rel_bias_base(table):
    top = table[:, 2 * MAX_REL:].astype(F32)
    rev = table[:, ::-1][:, :2 * MAX_REL].astype(F32)
    left = WINDOW - MAX_REL
    return jnp.concatenate([jnp.broadcast_to(top, (ATT_HEADS, left)), rev,
                            jnp.broadcast_to(top, (ATT_HEADS, BIAS_W - left - 2 * MAX_REL))], axis=1)


def _hgrn_consts(c):
    t = np.arange(c)[:, None]
    j = np.arange(c)[None, :]
    mats = [j <= t, j > t]
    masks = []
    m = c // 2
    while m >= 1:
        ref = (t // (2 * m)) * (2 * m) + m - 1
        second = (t % (2 * m)) >= m
        if m < SUBLANES:
            mats.append((second & (j > ref) & (j <= t)) | (~second & (j > t) & (j <= ref)))
        masks.append((t // (2 * m)) == (j // (2 * m)))
        m //= 2
    return (jnp.asarray(np.concatenate(mats, 0).astype(np.float32), BF16),
            jnp.asarray(np.stack(masks).astype(np.float32)))


def _hgrn_body(zb_ref, s0_ref, lower_ref, gon_ref, p_ref, mask_ref, o_ref, sfin_ref, st_ref, *, single_step):
    c = p_ref.shape[1]
    step = pl.program_id(1)

    def load_state():
        for h in range(HG_HEADS):
            st_ref[h] = s0_ref[0, h].T

    if single_step:
        load_state()
    else:
        pl.when(step == 0)(load_state)

    pmat = p_ref[...]
    n_levels = mask_ref.shape[0]
    head = lambda a, h: a[:, h * HG_DK:(h + 1) * HG_DK]
    low = lower_ref[...]
    row = lax.broadcasted_iota(jnp.int32, (c, HG_WIDTH), 0)

    def stage1(r0):
        part = lambda i: zb_ref[r0:r0 + c, i * HG_WIDTH:(i + 1) * HG_WIDTH]
        q = part(0)
        f = low + (1.0 - low) * _sigmoid(part(1))
        lf = jnp.log(f)
        k = 1.0 - f
        ib = part(2)
        v = ib * _sigmoid(ib)
        og = part(3)
        hi = lf.astype(BF16)
        r1 = lf - hi.astype(F32)
        mid = r1.astype(BF16)
        lo = (r1 - mid.astype(F32)).astype(BF16)
        e = (jnp.dot(pmat, hi, preferred_element_type=F32) + jnp.dot(pmat, mid, preferred_element_type=F32)
             + jnp.dot(pmat, lo, preferred_element_type=F32))
        b = e[0:c]
        return dict(q=q, k=k, v=v, e=e, b=b, decay=jnp.exp(e[c - 1:c]), qe=(q * jnp.exp(b)).astype(BF16),
                    kt=(k * jnp.exp(e[c:2 * c])).astype(BF16), vb=v.astype(BF16), qk=q * k,
                    gate=og * _sigmoid(og))

    def stage2(s):
        q, k, e, b = s["q"], s["k"], s["e"], s["b"]
        att = [None] * HG_HEADS
        n_rows_p = 2
        for lvl in range(n_levels):
            m = c >> (lvl + 1)
            if m >= SUBLANES:
                ref = [jnp.broadcast_to(b[p * 2 * m + m - 1:p * 2 * m + m], (2 * m, HG_WIDTH))
                       for p in range(c // (2 * m))]
                x = jnp.exp(-jnp.abs(b - (jnp.concatenate(ref, axis=0) if len(ref) > 1 else ref[0])))
            else:
                x = jnp.exp(e[n_rows_p * c:(n_rows_p + 1) * c])
                n_rows_p += 1
            second = (row & m) != 0
            qm = jnp.where(second, q * x, 0.0).astype(BF16)
            km = jnp.where(second, 0.0, k * x).astype(BF16)
            for h in range(HG_HEADS):
                a = lax.dot_general(head(qm, h), head(km, h), NT, preferred_element_type=F32)
                if lvl > 0:
                    a = a * mask_ref[lvl]
                att[h] = a if att[h] is None else att[h] + a
        return [jnp.dot(att[h].astype(BF16), head(s["vb"], h), preferred_element_type=F32)
                + jnp.sum(head(s["qk"], h), axis=-1, keepdims=True) * head(s["v"], h) for h in range(HG_HEADS)]

    def stage3(r0, s, intra):
        for h in range(HG_HEADS):
            st = st_ref[h]
            inter = lax.dot_general(head(s["qe"], h), st.astype(BF16), NT, preferred_element_type=F32)
            st_ref[h] = st * head(s["decay"], h) + lax.dot_general(head(s["vb"], h), head(s["kt"], h), TN,
                                                                  preferred_element_type=F32)
            o = _rms(inter + intra[h], gon_ref[...]) * head(s["gate"], h)
            o_ref[r0:r0 + c, h * HG_DK:(h + 1) * HG_DK] = o.astype(o_ref.dtype)

    starts = range(0, zb_ref.shape[0], c)
    firsts = [stage1(r0) for r0 in starts]
    intras = [stage2(s) for s in firsts]
    for r0, s, intra in zip(starts, firsts, intras):
        stage3(r0, s, intra)

    def write_state():
        for h in range(HG_HEADS):
            sfin_ref[0, h] = st_ref[h].T

    if single_step:
        write_state()
    else:
        pl.when(step == pl.num_programs(1) - 1)(write_state)


def _hgrn(zb, s0, lower, g_on, batch, seq):
    c = min(HG_C, seq)
    rows = min(HG_CHUNKS_PER_STEP * c, seq)
    assert seq % rows == 0 and rows % c == 0
    pmat, masks = _hgrn_consts(c)
    nc = seq // rows
    return pl.pallas_call(
        functools.partial(_hgrn_body, single_step=nc == 1),
        grid=(batch, nc),
        in_specs=[
            pl.BlockSpec((rows, 4 * HG_WIDTH), lambda b, i: (b * nc + i, 0)),
            pl.BlockSpec((1, HG_HEADS, HG_DK, HG_DK), lambda b, i: (b, 0, 0, 0)),
            pl.BlockSpec((1, HG_WIDTH), lambda b, i: (0, 0)),
            pl.BlockSpec((1, HG_DK), lambda b, i: (0, 0)),
            pl.BlockSpec(pmat.shape, lambda b, i: (0, 0)),
            pl.BlockSpec(masks.shape, lambda b, i: (0, 0, 0)),
        ],
        out_specs=[
            pl.BlockSpec((rows, HG_WIDTH), lambda b, i: (b * nc + i, 0)),
            pl.BlockSpec((1, HG_HEADS, HG_DK, HG_DK), lambda b, i: (b, 0, 0, 0)),
        ],
        out_shape=[
            jax.ShapeDtypeStruct((batch * seq, HG_WIDTH), BF16),
            jax.ShapeDtypeStruct((batch, HG_HEADS, HG_DK, HG_DK), F32),
        ],
        scratch_shapes=[pltpu.VMEM((HG_HEADS, HG_DK, HG_DK), F32)],
        compiler_params=_params("arbitrary", "arbitrary"),
        name="hgrn2",
    )(zb, s0, lower, g_on, pmat, masks)


def _split_bf16(x):
    hi = x.astype(BF16)
    return hi, (x - hi.astype(F32)).astype(BF16)


def _merge_body(att_p, hg_p, zg_p, x_p, att_s, hg_s, zg_s, x_s, wpa_ref, wpb_ref, wout_ref, gffn_ref, wr_ref, br_ref,
                x1_ref, h2_ref, route_ref, cnt_ref, *, n_first):
    weights = (wpa_ref, wpb_ref, wout_ref, gffn_ref, wr_ref, br_ref)
    outs = (x1_ref, h2_ref, route_ref, cnt_ref)
    i = pl.program_id(0)
    pl.when(i < n_first)(functools.partial(_merge_block, att_p, hg_p, zg_p, x_p, *weights, *outs))
    pl.when(i >= n_first)(functools.partial(_merge_block, att_s, hg_s, zg_s, x_s, *weights, *outs))


def _merge_block(att_ref, hg_ref, zg_ref, x_ref, wpa_ref, wpb_ref, wout_ref, gffn_ref, wr_ref, br_ref,
                 x1_ref, h2_ref, route_ref, cnt_ref):
    pa = jnp.dot(att_ref[...].astype(BF16), wpa_ref[...], preferred_element_type=F32)
    pb = jnp.dot(hg_ref[...].astype(BF16), wpb_ref[...], preferred_element_type=F32)
    y = _sigmoid(zg_ref[:, :D_MODEL]) * pa + _sigmoid(zg_ref[:, D_MODEL:]) * pb
    x1 = x_ref[...] + jnp.dot(y.astype(BF16), wout_ref[...], preferred_element_type=F32)
    x1_ref[...] = x1
    h2 = _rms(x1, gffn_ref[...])
    h2_ref[...] = h2.astype(BF16)

    h_hi, h_lo = _split_bf16(h2)
    w_hi, w_lo = _split_bf16(wr_ref[...])
    both = jnp.dot(h_hi, jnp.concatenate([w_hi, w_lo], axis=1), preferred_element_type=F32)
    logits = both[:, :LANES] + both[:, LANES:] + jnp.dot(h_lo, w_hi, preferred_element_type=F32) + br_ref[...]
    lane = lax.broadcasted_iota(jnp.int32, logits.shape, 1).astype(F32)
    cur = logits
    vals, idxs = [], []
    for _ in range(TOP_K):
        m = jnp.max(cur, axis=-1, keepdims=True)
        idx = jnp.min(jnp.where(cur == m, lane, float(LANES)), axis=-1, keepdims=True)
        vals.append(m)
        idxs.append(idx)
        cur = jnp.where(lane == idx, -jnp.inf, cur)
    es = [jnp.exp(v - vals[0]) for v in vals]
    inv = 1.0 / functools.reduce(jnp.add, es)
    route = jnp.zeros_like(logits)
    for k, (ex, idx) in enumerate(zip(es, idxs)):
        route = (route + jnp.where(lane == idx, 1.0, 0.0) + jnp.where(lane == float(ROUTE_IDX + k), idx, 0.0)
                 + jnp.where(lane == float(ROUTE_P + k), ex * inv, 0.0))
    route_ref[...] = route
    sel = jnp.where(lane < float(N_EXPERTS), route, 0.0)
    for blk in range(cnt_ref.shape[0]):
        cnt_ref[blk] = jnp.sum(sel[blk * TBK:(blk + 1) * TBK], axis=0, keepdims=True)


def _merge(first, second, wpa, wpb, wout, g_ffn, w_router, b_router):
    tm = MERGE_TM
    assert first[3].shape[0] % tm == 0 and second[3].shape[0] % tm == 0
    n1, n2 = first[3].shape[0] // tm, second[3].shape[0] // tm
    widths = (ATT_WIDTH, HG_WIDTH, 2 * D_MODEL, D_MODEL)
    spec1 = [pl.BlockSpec((tm, w), lambda i: (jnp.minimum(i, n1 - 1), 0)) for w in widths]
    spec2 = [pl.BlockSpec((tm, w), lambda i: (jnp.maximum(i - n1, 0), 0)) for w in widths]
    row = lambda w: pl.BlockSpec((tm, w), lambda i: (i, 0))
    full = lambda a: pl.BlockSpec(a.shape, lambda i: (0,) * a.ndim)
    n = (n1 + n2) * tm
    return pl.pallas_call(
        functools.partial(_merge_body, n_first=n1),
        grid=(n1 + n2,),
        in_specs=spec1 + spec2 + [full(wpa), full(wpb), full(wout), full(g_ffn), full(w_router), full(b_router)],
        out_specs=[row(D_MODEL), row(D_MODEL), row(LANES),
                   pl.BlockSpec((tm // TBK, 1, LANES), lambda i: (i, 0, 0))],
        out_shape=[
            jax.ShapeDtypeStruct((n, D_MODEL), F32),
            jax.ShapeDtypeStruct((n, D_MODEL), BF16),
            jax.ShapeDtypeStruct((n, LANES), F32),
            jax.ShapeDtypeStruct((n // TBK, 1, LANES), F32),
        ],
        compiler_params=_params("arbitrary"),
        name="merge_router",
    )(*first, *second, wpa, wpb, wout, g_ffn, w_router, b_router)


def _route_plan(cnt):
    pc = (cnt + ROW_CH - 1) // ROW_CH
    loff = jnp.cumsum(pc, axis=1) - pc
    tot = jnp.sum(pc, axis=0)
    reg = (tot + CH_PER_TILE - 1) // CH_PER_TILE * CH_PER_TILE
    gstart = jnp.cumsum(reg) - reg
    goff = gstart[None, :] + jnp.cumsum(pc, axis=0) - pc
    ntiles = jnp.sum(reg) // CH_PER_TILE
    present = reg > 0
    ids = jnp.arange(N_EXPERTS, dtype=jnp.int32)
    later = (ids[None, :] > ids[:, None]) & present[None, :]
    nxt = jnp.min(jnp.where(later, ids[None, :], N_EXPERTS), axis=1)
    nxt = jnp.where(nxt == N_EXPERTS, -1, nxt)
    slot = (jnp.cumsum(present.astype(jnp.int32)) - 1) % 2
    first = jnp.min(jnp.where(present, ids, N_EXPERTS)).reshape(1)
    j = jnp.arange(BLK_CH, dtype=jnp.int32)
    run = jnp.sum(((loff + pc)[:, None, :] <= j[None, :, None]).astype(jnp.int32), axis=2)
    shift = jnp.sum(jnp.where(run[:, :, None] == ids[None, None, :], (goff - loff)[:, None, :], 0), axis=2)
    gmap = shift + j[None, :]
    i32 = lambda a: a.astype(jnp.int32)
    return dict(loff=i32(loff), gmap=i32(gmap), totc=i32(jnp.sum(pc, axis=1)),
                pad_start=i32(gstart + tot), pad_cnt=i32(reg - tot), ntiles=i32(ntiles).reshape(1),
                tile_start=i32(gstart // CH_PER_TILE), tile_count=i32(reg // CH_PER_TILE),
                next_expert=i32(nxt), weight_slot=i32(slot), first_expert=i32(first))


def _chunk_rows(ref, chunk):
    return ref.at[pl.ds(pl.multiple_of(chunk * ROW_CH, ROW_CH), ROW_CH)]


def _for_chunks(n, do):
    log_unroll = 2
    groups = lax.shift_right_logical(n, log_unroll)

    def group(i, carry):
        for u in range(1 << log_unroll):
            do(lax.shift_left(i, log_unroll) + u)
        return carry
    lax.fori_loop(0, groups, group, 0)

    def single(c, carry):
        do(c)
        return carry
    lax.fori_loop(lax.shift_left(groups, log_unroll), n, single, 0)


def _wait_chunks(n, copy_of_rows):
    for bit in range((RB // ROW_CH).bit_length()):
        @pl.when(((n >> bit) & 1) == 1)
        def _(bit=bit):
            copy_of_rows((1 << bit) * ROW_CH).wait()


def _dispatch_body(gmap_s, totc_s, pads_s, padn_s, nt_s, h2_ref, route_ref, tri_ref, loffv_ref,
                   xs_hbm, buf, zbuf, sem):
    b = pl.program_id(0)
    nb = pl.num_programs(0)
    slot = lax.rem(b, 2)
    fill_sem, tile_sem = 2, 3
    nt_max = xs_hbm.shape[0] // EXP_TM

    def copy_out(src, gchunk, sem_i):
        return pltpu.make_async_copy(src, _chunk_rows(xs_hbm, gchunk), sem.at[sem_i])

    def zero_tile(t):
        return pltpu.make_async_copy(zbuf, xs_hbm.at[pl.ds(pl.multiple_of(t * EXP_TM, EXP_TM), EXP_TM)],
                                     sem.at[tile_sem])

    def wait_block(bb, slot_):
        _wait_chunks(totc_s[bb], lambda rows: pltpu.make_async_copy(
            buf.at[slot_, pl.ds(0, rows)], xs_hbm.at[pl.ds(0, rows)], sem.at[slot_]))

    @pl.when(b == 0)
    def _():
        zbuf[...] = jnp.zeros_like(zbuf)

    @pl.when(b >= 2)
    def _():
        wait_block(b - 2, slot)

    rt = route_ref[...].T
    rank_t = jnp.dot(rt[0:N_EXPERTS].astype(BF16), tri_ref[...], preferred_element_type=F32)
    lpos_t = loffv_ref[...] * float(ROW_CH) + rank_t
    erow = lax.broadcasted_iota(jnp.int32, (N_EXPERTS, TBK), 0).astype(F32)
    lposk = [jnp.sum(jnp.where(rt[ROUTE_IDX + k:ROUTE_IDX + k + 1] == erow, lpos_t, 0.0), axis=0, keepdims=True)
             for k in range(TOP_K)]
    h2 = h2_ref[...]
    for r0 in range(0, RB, PERM_TM):
        rrow = lax.broadcasted_iota(jnp.int32, (PERM_TM, TBK), 0).astype(F32) + float(r0)
        perm = functools.reduce(jnp.add, [jnp.where(lposk[k] == rrow, 1.0, 0.0) for k in range(TOP_K)])
        buf[slot, r0:r0 + PERM_TM, :] = jnp.dot(perm.astype(BF16), h2, preferred_element_type=F32)

    _for_chunks(totc_s[b], lambda c: copy_out(_chunk_rows(buf.at[slot], c), gmap_s[b * BLK_CH + c], slot).start())

    @pl.when(b == nb - 1)
    def _():
        zchunk = _chunk_rows(zbuf, 0)

        def fill_expert(e, carry):
            def one(c, carry2):
                copy_out(zchunk, pads_s[e] + c, fill_sem).start()
                return carry2
            lax.fori_loop(0, padn_s[e], one, 0)
            return carry
        lax.fori_loop(0, N_EXPERTS, fill_expert, 0)

        def fill_tile(t, carry):
            zero_tile(t).start()
            return carry
        lax.fori_loop(nt_s[0], nt_max, fill_tile, 0)

        @pl.when(b >= 1)
        def _():
            wait_block(b - 1, 1 - slot)
        wait_block(b, slot)

        def drain_expert(e, carry):
            def one(c, carry2):
                copy_out(zchunk, 0, fill_sem).wait()
                return carry2
            lax.fori_loop(0, padn_s[e], one, 0)
            return carry
        lax.fori_loop(0, N_EXPERTS, drain_expert, 0)

        def drain_tile(t, carry):
            zero_tile(0).wait()
            return carry
        lax.fori_loop(nt_s[0], nt_max, drain_tile, 0)


def _dispatch(plan, h2, route, loffv, n_rows):
    nb = h2.shape[0] // TBK
    tri = jnp.asarray(np.triu(np.ones((TBK, TBK), np.float32), 1), BF16)
    grid_spec = pltpu.PrefetchScalarGridSpec(
        num_scalar_prefetch=5,
        grid=(nb,),
        in_specs=[
            pl.BlockSpec((TBK, D_MODEL), lambda b, *_: (b, 0)),
            pl.BlockSpec((TBK, LANES), lambda b, *_: (b, 0)),
            pl.BlockSpec((TBK, TBK), lambda b, *_: (0, 0)),
            pl.BlockSpec((None, N_EXPERTS, 1), lambda b, *_: (b, 0, 0)),
        ],
        out_specs=pl.BlockSpec(memory_space=pl.ANY),
        scratch_shapes=[pltpu.VMEM((2, RB, D_MODEL), F32), pltpu.VMEM((EXP_TM, D_MODEL), F32),
                        pltpu.SemaphoreType.DMA((4,))],
    )
    return pl.pallas_call(
        _dispatch_body,
        grid_spec=grid_spec,
        out_shape=jax.ShapeDtypeStruct((n_rows, D_MODEL), F32),
        compiler_params=_params("arbitrary"),
        name="moe_dispatch",
    )(plan["gmap"].reshape(-1), plan["totc"], plan["pad_start"], plan["pad_cnt"], plan["ntiles"],
      h2, route, tri, loffv)


def _experts_body(t0_s, n_s, nxt_s, par_s, first_s, nt_s, xs_hbm, wgu_hbm, bgu_ref, wd_hbm, bd_ref, ys_hbm,
                  wg_f, wd_f, wg_b, wd_b, xbuf, ybuf, wsem, xsem, ysem):
    e = pl.program_id(0)
    nt = nt_s[0]
    nt_max = xs_hbm.shape[0] // EXP_TM
    tile_rows = lambda t: pl.ds(pl.multiple_of(t * EXP_TM, EXP_TM), EXP_TM)

    def weight_copies(ex, s, piece):
        rows = pl.ds(pl.multiple_of(piece * W_PIECE_ROWS, W_PIECE_ROWS), W_PIECE_ROWS)
        return (pltpu.make_async_copy(wgu_hbm.at[ex, rows], wg_f.at[s, rows], wsem.at[s, 0]),
                pltpu.make_async_copy(wd_hbm.at[ex, rows], wd_f.at[s, rows], wsem.at[s, 1]))

    def x_copy(t, s):
        return pltpu.make_async_copy(xs_hbm.at[tile_rows(t)], xbuf.at[s], xsem.at[s])

    def y_copy(t, s):
        return pltpu.make_async_copy(ybuf.at[s], ys_hbm.at[tile_rows(t)], ysem.at[s])

    @pl.when(e == 0)
    def _():
        for piece in range(W_PIECES):
            for cp in weight_copies(first_s[0], 0, piece):
                cp.start()
        x_copy(0, 0).start()

    @pl.when(n_s[e] > 0)
    def _():
        s_w = par_s[e]
        for piece in range(W_PIECES):
            for cp in weight_copies(e, s_w, piece):
                cp.wait()
        wg_b[...] = wg_f[s_w].astype(BF16)
        wd_b[...] = wd_f[s_w].astype(BF16)
        has_next = nxt_s[e] >= 0

        def request(piece):
            for cp in weight_copies(nxt_s[e], 1 - s_w, piece):
                cp.start()

        def tile(i, carry):
            t = t0_s[e] + i
            s = t & 1
            x_copy(t, s).wait()

            @pl.when(has_next & (i < W_PIECES))
            def _():
                request(i)

            @pl.when(t + 1 < nt)
            def _():
                x_copy(t + 1, 1 - s).start()

            @pl.when(t >= 2)
            def _():
                y_copy(t - 2, s).wait()

            gu = jnp.dot(xbuf[s].astype(BF16), wg_b[...], preferred_element_type=F32) + bgu_ref[...]
            gate = jnp.minimum(gu[:, :D_FF], SWIGLU_LIMIT)
            up = jnp.clip(gu[:, D_FF:], -SWIGLU_LIMIT, SWIGLU_LIMIT)
            act = (up + 1.0) * gate * _sigmoid(SWIGLU_ALPHA * gate)
            ybuf[s] = jnp.dot(act.astype(BF16), wd_b[...], preferred_element_type=F32) + bd_ref[...]
            y_copy(t, s).start()
            return carry
        lax.fori_loop(0, n_s[e], tile, 0)

        @pl.when(has_next)
        def _():
            def rest(piece, carry):
                request(piece)
                return carry
            lax.fori_loop(jnp.minimum(n_s[e], W_PIECES), W_PIECES, rest, 0)

    @pl.when(e == pl.num_programs(0) - 1)
    def _():
        @pl.when(nt >= 2)
        def _():
            y_copy(nt - 2, nt & 1).wait()
        y_copy(nt - 1, (nt - 1) & 1).wait()
        ybuf[0] = jnp.zeros((EXP_TM, D_MODEL), F32)

        def fill(t, carry):
            y_copy(t, 0).start()
            return carry
        lax.fori_loop(nt, nt_max, fill, 0)

        def drain(t, carry):
            y_copy(0, 0).wait()
            return carry
        lax.fori_loop(nt, nt_max, drain, 0)


def _experts(plan, xs, w_gu, b_gu, w_down, b_down):
    n_rows = xs.shape[0]
    of_expert = lambda e, *_: (e, 0, 0)
    grid_spec = pltpu.PrefetchScalarGridSpec(
        num_scalar_prefetch=6,
        grid=(N_EXPERTS,),
        in_specs=[
            pl.BlockSpec(memory_space=pl.ANY),
            pl.BlockSpec(memory_space=pl.ANY),
            pl.BlockSpec((None, 1, 2 * D_FF), of_expert),
            pl.BlockSpec(memory_space=pl.ANY),
            pl.BlockSpec((None, 1, D_MODEL), of_expert),
        ],
        out_specs=pl.BlockSpec(memory_space=pl.ANY),
        scratch_shapes=[pltpu.VMEM((2, D_MODEL, 2 * D_FF), F32), pltpu.VMEM((2, D_FF, D_MODEL), F32),
                        pltpu.VMEM((D_MODEL, 2 * D_FF), BF16), pltpu.VMEM((D_FF, D_MODEL), BF16),
                        pltpu.VMEM((2, EXP_TM, D_MODEL), F32), pltpu.VMEM((2, EXP_TM, D_MODEL), F32),
                        pltpu.SemaphoreType.DMA((2, 2)), pltpu.SemaphoreType.DMA((2,)),
                        pltpu.SemaphoreType.DMA((2,))],
    )
    return pl.pallas_call(
        _experts_body,
        grid_spec=grid_spec,
        out_shape=jax.ShapeDtypeStruct((n_rows, D_MODEL), F32),
        compiler_params=_params("arbitrary"),
        name="moe_experts",
    )(plan["tile_start"], plan["tile_count"], plan["next_expert"], plan["weight_slot"], plan["first_expert"],
      plan["ntiles"], xs, w_gu, b_gu, w_down, b_down)


def _combine_body(gmap_s, totc_s, route_ref, tril_ref, loffrow_ref, x1_ref, gfin_ref, ys_hbm, y_ref, buf, sem):
    b = pl.program_id(0)
    nb = pl.num_programs(0)
    slot = lax.rem(b, 2)

    def copy_in(slot_, lchunk, gchunk):
        return pltpu.make_async_copy(_chunk_rows(ys_hbm, gchunk), _chunk_rows(buf.at[slot_], lchunk), sem.at[slot_])

    def fetch_block(bb, slot_):
        _for_chunks(totc_s[bb], lambda c: copy_in(slot_, c, gmap_s[bb * BLK_CH + c]).start())

    def wait_block(bb, slot_):
        _wait_chunks(totc_s[bb], lambda rows: pltpu.make_async_copy(
            ys_hbm.at[pl.ds(0, rows)], buf.at[slot_, pl.ds(0, rows)], sem.at[slot_]))

    @pl.when(b == 0)
    def _():
        buf[...] = jnp.zeros_like(buf)
        fetch_block(0, 0)

    @pl.when(b + 1 < nb)
    def _():
        fetch_block(b + 1, 1 - slot)

    wait_block(b, slot)

    route = route_ref[...]
    lane = lax.broadcasted_iota(jnp.int32, (1, LANES), 1).astype(F32)
    sel = jnp.where(lane < float(N_EXPERTS), route, 0.0).astype(BF16)
    rank = jnp.dot(tril_ref[...], sel, preferred_element_type=F32)
    lpos = loffrow_ref[...] * float(ROW_CH) + rank
    lposk, pk = [], []
    for k in range(TOP_K):
        idx = route[:, ROUTE_IDX + k:ROUTE_IDX + k + 1]
        lposk.append(jnp.sum(jnp.where(lane == idx, lpos, 0.0), axis=-1, keepdims=True))
        pk.append(route[:, ROUTE_P + k:ROUTE_P + k + 1])
    acc = x1_ref[...]
    col =lax.broadcasted_iota(jnp.int32, (TBK, PERM_TM), 1).astype(F32).astype(BF16)
    pkb = [p.astype(BF16) for p in pk]
    zero = jnp.zeros((), BF16)
    for r0 in range(0, RB, PERM_TM):
        rel = [(lposk[k] - float(r0)).astype(BF16) for k in range(TOP_K)]
        w = functools.reduce(jnp.add, [jnp.where(rel[k] == col, pkb[k], zero) for k in range(TOP_K)])
        acc = acc + jnp.dot(w, buf[slot, r0:r0 + PERM_TM, :].astype(BF16), preferred_element_type=F32)
    y_ref[...] = _rms(acc, gfin_ref[...])


def _combine(plan, blocks, route, loffrow, x1, g_final, ys):
    assert RB % PERM_TM == 0 and PERM_TM <= 256
    b0, b1 = blocks
    nb = b1 - b0
    tril = jnp.asarray(np.tril(np.ones((TBK, TBK), np.float32), -1), BF16)
    grid_spec = pltpu.PrefetchScalarGridSpec(
        num_scalar_prefetch=2,
        grid=(nb,),
        in_specs=[
            pl.BlockSpec((TBK, LANES), lambda b, *_: (b + b0, 0)),
            pl.BlockSpec((TBK, TBK), lambda b, *_: (0, 0)),
            pl.BlockSpec((None, 1, LANES), lambda b, *_: (b + b0, 0, 0)),
            pl.BlockSpec((TBK, D_MODEL), lambda b, *_: (b + b0, 0)),
            pl.BlockSpec((1, D_MODEL), lambda b, *_: (0, 0)),
            pl.BlockSpec(memory_space=pl.ANY),
        ],
        out_specs=pl.BlockSpec((TBK, D_MODEL), lambda b, *_: (b, 0)),
        scratch_shapes=[pltpu.VMEM((2, RB, D_MODEL), F32), pltpu.SemaphoreType.DMA((2,))],
    )
    return pl.pallas_call(
        _combine_body,
        grid_spec=grid_spec,
        out_shape=jax.ShapeDtypeStruct((nb * TBK, D_MODEL), F32),
        compiler_params=_params("arbitrary"),
        name="moe_combine",
    )(plan["gmap"][b0:b1].reshape(-1), plan["totc"][b0:b1], route, tril, loffrow, x1, g_final, ys)


def kernel(x_prompt, x_sample, cache_k, cache_v, state_s, g_mix, w_in, rel_bias, lb_logits, g_out_norm,
           w_pa, w_pb, w_out, g_ffn, w_router, b_router, w_gu, b_gu, w_down, b_down, g_final):
    B, T = x_prompt.shape[:2]
    DB, S = x_sample.shape[:2]
    depth = w_in.shape[0]
    assert depth == 1 and T % ATT_QBLK == 0 and S == CHUNK
    cw = cache_k.shape[2]
    assert cw == WINDOW
    l = 0

    lower = jnp.cumsum(jax.nn.softmax(lb_logits.astype(F32), axis=0), axis=0)[l].reshape(1, HG_WIDTH)
    w_in_b = w_in[l].astype(BF16)
    wpa, wpb, wout = w_pa[l].astype(BF16), w_pb[l].astype(BF16), w_out[l].astype(BF16)
    row = lambda a: a.reshape(1, -1).astype(F32)
    base = _rel_bias_base(rel_bias[l])
    b_gu3 = b_gu[l].reshape(N_EXPERTS, 1, 2 * D_FF)
    b_down3 = b_down[l].reshape(N_EXPERTS, 1, D_MODEL)
    pad_e = LANES - N_EXPERTS
    wr = jnp.pad(w_router[l].astype(F32), ((0, 0), (0, pad_e)))
    br = jnp.concatenate([b_router[l].astype(F32), jnp.full((pad_e,), NEG, F32)]).reshape(1, LANES)

    n_tok = B * T + DB * S
    nb, nbp = n_tok // TBK, (B * T) // TBK

    def front(x, batch, seq, s0, attend):
        xf = x.reshape(batch * seq, D_MODEL)
        za, zb, zg = _inproj(xf, row(g_mix[l]), w_in_b)
        att = attend(za)
        hg, s_fin = _hgrn(zb, s0, lower, row(g_out_norm[l]), batch, seq)
        za3 = za.reshape(batch, seq, 3 * ATT_WIDTH)
        heads = lambda a: a.reshape(1, batch, a.shape[1], ATT_HEADS, ATT_DIM)
        keep = min(WINDOW, seq)
        nk = heads(za3[:, seq - keep:, ATT_WIDTH:2 * ATT_WIDTH])
        nv = heads(za3[:, seq - keep:, 2 * ATT_WIDTH:])
        return dict(mix=(att, hg, zg, xf), nk=nk, nv=nv, s=s_fin[None])

    ck = cache_k[l].reshape(DB, cw, ATT_WIDTH)
    cv = cache_v[l].reshape(DB, cw, ATT_WIDTH)
    fp = front(x_prompt, B, T, jnp.zeros((B, HG_HEADS, HG_DK, HG_DK), F32), lambda za: _attn_prompt(za, base, B, T))
    fs = front(x_sample, DB, S, state_s[l].astype(F32), lambda za: _attn_sample(za, ck, cv, base, DB, S))

    x1, h2, route, cnt = _merge(fp["mix"], fs["mix"], wpa, wpb, wout, row(g_ffn[l]), wr, br)
    cnt = cnt[:, 0, :N_EXPERTS].astype(jnp.int32)
    max_rows = n_tok * TOP_K + nb * N_EXPERTS * (ROW_CH - 1) + N_EXPERTS * (EXP_TM - 1)
    nt_max = -(-max_rows // EXP_TM)
    plan = _route_plan(cnt)
    loff_f = plan["loff"].astype(F32)
    xs = _dispatch(plan, h2, route, loff_f[:, :, None], nt_max * EXP_TM)
    ysort = _experts(plan, xs, w_gu[l], b_gu3, w_down[l], b_down3)
    loffrow = jnp.pad(loff_f, ((0, 0), (0, pad_e)))[:, None, :]
    yp = _combine(plan, (0, nbp), route, loffrow, x1, row(g_final), ysort)
    ys = _combine(plan, (nbp, nb), route, loffrow, x1, row(g_final), ysort)
    return (yp.reshape(B, T, D_MODEL), ys.reshape(DB, S, D_MODEL), fp["nk"], fp["nv"], fp["s"],
            fs["nk"], fs["nv"], fs["s"])
```

```python
import functools

import numpy as np
import jax
import jax.numpy as jnp
from jax import lax
from jax.experimental import pallas as pl
from jax.experimental.pallas import tpu as pltpu

F32 = jnp.float32
BF16 = jnp.bfloat16

D_MODEL = 1024
CHUNK = 64
LEFT_CHUNKS = 8
WINDOW = LEFT_CHUNKS * CHUNK
ATT_HEADS = 8
ATT_DIM = 64
ATT_WIDTH = ATT_HEADS * ATT_DIM
MAX_REL = 256
HG_HEADS = 4
HG_DK = 128
HG_WIDTH = HG_HEADS * HG_DK
N_EXPERTS = 32
TOP_K = 4
D_FF = D_MODEL
SWIGLU_LIMIT = 7.0
SWIGLU_ALPHA = 1.702
RMS_EPS = 1e-5

LANES = 128
NEG = -1e30
LOG2E = 1.4426950408889634
ATT_QBLK = 4 * CHUNK
ATT_KBLKS = LEFT_CHUNKS * CHUNK // ATT_QBLK + 1
ATT_PAIRS_PER_STAGE = 2
HG_C = 128
HG_CHUNKS_PER_STEP = 4
VMEM_LIMIT = 56 * 1024 * 1024
BIAS_W = 1024
SUBLANES = 8
TBK = 256
ROW_CH = SUBLANES
RB = TBK * TOP_K + N_EXPERTS * ROW_CH
MERGE_TM = 2 * TBK
COMB_BLKS = 2
PERM_TM = 256
EXP_TM = 512
W_PIECES = 4
W_PIECE_ROWS = D_MODEL // W_PIECES
CH_PER_TILE = EXP_TM // ROW_CH
BLK_CH = RB // ROW_CH
ROUTE_IDX = 64
ROUTE_P = 72

NT = (((1,), (1,)), ((), ()))
TN = (((0,), (0,)), ((), ()))


def _rms(x, g):
    return x * lax.rsqrt(jnp.mean(x * x, axis=-1, keepdims=True) + RMS_EPS) * g


def _sigmoid(x):
    return 1.0 / (1.0 + jnp.exp(-x))


def _params(*sem):
    return pltpu.CompilerParams(dimension_semantics=sem, vmem_limit_bytes=VMEM_LIMIT)


def _inproj_body(x_ref, g_ref, w_ref, za_ref, zb_ref, zg_ref):
    h = _rms(x_ref[...], g_ref[...]).astype(BF16)
    a, b = 3 * ATT_WIDTH, 3 * ATT_WIDTH + 4 * HG_WIDTH
    za_ref[...] = jnp.dot(h, w_ref[:, :a], preferred_element_type=F32)
    zb_ref[...] = jnp.dot(h, w_ref[:, a:b], preferred_element_type=F32)
    zg_ref[...] = jnp.dot(h, w_ref[:, b:], preferred_element_type=F32)


def _inproj(x, g, w_bf16, tm=512):
    n = x.shape[0]
    assert n % tm == 0
    cols = w_bf16.shape[1]
    wa, wb, wg = 3 * ATT_WIDTH, 4 * HG_WIDTH, 2 * D_MODEL
    return pl.pallas_call(
        _inproj_body,
        grid=(n // tm,),
        in_specs=[
            pl.BlockSpec((tm, D_MODEL), lambda i: (i, 0)),
            pl.BlockSpec((1, D_MODEL), lambda i: (0, 0)),
            pl.BlockSpec((D_MODEL, cols), lambda i: (0, 0)),
        ],
        out_specs=[
            pl.BlockSpec((tm, wa), lambda i: (i, 0)),
            pl.BlockSpec((tm, wb), lambda i: (i, 0)),
            pl.BlockSpec((tm, wg), lambda i: (i, 0)),
        ],
        out_shape=[
            jax.ShapeDtypeStruct((n, wa), F32),
            jax.ShapeDtypeStruct((n, wb), F32),
            jax.ShapeDtypeStruct((n, wg), F32),
        ],
        compiler_params=_params("arbitrary"),
        name="inproj",
    )(x, g, w_bf16)


def _attn_heads(q_ref, k_refs, v_refs, bias_fn, pens, o_ref):
    lane = lax.broadcasted_iota(jnp.int32, (1, LANES), 1)
    first = lane < ATT_DIM
    halves = (first, lane >= ATT_DIM)
    for hp0 in range(0, ATT_HEADS // 2, ATT_PAIRS_PER_STAGE):
        pairs = range(hp0, hp0 + ATT_PAIRS_PER_STAGE)
        sl = {hp: slice(hp * LANES, (hp + 1) * LANES) for hp in pairs}
        scores = {}
        for hp in pairs:
            q2 = q_ref[:, sl[hp]] * (ATT_DIM ** -0.5 * LOG2E)
            ks = [k[:, sl[hp]].astype(BF16) for k in k_refs]
            for half, mine in enumerate(halves):
                qm = jnp.where(mine, q2, 0.0).astype(BF16)
                ss = []
                for j, kj in enumerate(ks):
                    s = lax.dot_general(qm, kj, NT, preferred_element_type=F32) + bias_fn(2 * hp + half, j)
                    if pens[j] is not None:
                        s = s + pens[j]
                    ss.append(s)
                scores[hp, half] = ss
        for hp in pairs:
            outs = []
            for half, mine in enumerate(halves):
                ss = scores[hp, half]
                vs = [jnp.where(mine, v[:, sl[hp]], 1.0).astype(BF16) for v in v_refs]
                if all(s.shape == ss[0].shape for s in ss):
                    m = jnp.max(functools.reduce(jnp.maximum, ss), axis=-1, keepdims=True)
                else:
                    m = functools.reduce(jnp.maximum, [jnp.max(s, axis=-1, keepdims=True) for s in ss])
                outs.append(functools.reduce(jnp.add, [
                    jnp.dot(jnp.exp2(s - m).astype(BF16), vj, preferred_element_type=F32) for s, vj in zip(ss, vs)]))
            num = jnp.where(first, outs[0], outs[1])
            den = pltpu.roll(jnp.where(first, outs[1], outs[0]), ATT_DIM, 1)
            o_ref[:, sl[hp]] = (num * (1.0 / den)).astype(o_ref.dtype)


def _fill_bias(base_ref, bias_ref, banded):
    nq, nk = bias_ref.shape[1:]
    if banded:
        r = lax.broadcasted_iota(jnp.int32, (nq, nk), 0)
        s = lax.broadcasted_iota(jnp.int32, (nq, nk), 1)
        qc = (r + WINDOW) // CHUNK
        kc = s // CHUNK
        pen = jnp.where(kc <= qc, jnp.where(kc >= qc - LEFT_CHUNKS, 0.0, NEG), NEG)
    for h in range(ATT_HEADS):
        rows = jnp.broadcast_to(base_ref[h:h + 1, :], (nq, BIAS_W))
        t = pltpu.roll(rows, 0, 1, stride=1, stride_axis=0)[:, :nk] * LOG2E
        bias_ref[h] = t + pen if banded else t


def _attn_prompt_body(q_ref, k0, k1, k2, v0, v1, v2, base_ref, o_ref, bias_ref):
    i = pl.program_id(1)

    @pl.when((pl.program_id(0) == 0) & (i == 0))
    def _():
        _fill_bias(base_ref, bias_ref, True)

    bias_fn = lambda h, j: bias_ref[h, :, j * ATT_QBLK:(j + 1) * ATT_QBLK]
    back = ATT_KBLKS - 1

    @pl.when(i < back)
    def _():
        pens = [jnp.where(i - back + j >= 0, 0.0, NEG) for j in range(back)] + [None]
        _attn_heads(q_ref, [k0, k1, k2], [v0, v1, v2], bias_fn, pens, o_ref)

    @pl.when(i >= back)
    def _():
        _attn_heads(q_ref, [k0, k1, k2], [v0, v1, v2], bias_fn, [None] * ATT_KBLKS, o_ref)


def _attn_prompt(za, base, batch, seq):
    assert seq % ATT_QBLK == 0 and BIAS_W - ATT_KBLKS * ATT_QBLK >= ATT_QBLK - 1 and WINDOW >= MAX_REL
    nq = seq // ATT_QBLK
    back = ATT_KBLKS - 1
    qspec = pl.BlockSpec((ATT_QBLK, ATT_WIDTH), lambda b, i: (b * nq + i, 0))

    def kvspec(j, col):
        return pl.BlockSpec((ATT_QBLK, ATT_WIDTH),
                            lambda b, i: (b * nq + jnp.maximum(i - back + j, 0), col))

    return pl.pallas_call(
        _attn_prompt_body,
        grid=(batch, nq),
        in_specs=[qspec] + [kvspec(j, 1) for j in range(ATT_KBLKS)] + [kvspec(j, 2) for j in range(ATT_KBLKS)]
        + [pl.BlockSpec(base.shape, lambda b, i: (0, 0))],
        out_specs=pl.BlockSpec((ATT_QBLK, ATT_WIDTH), lambda b, i: (b * nq + i, 0)),
        out_shape=jax.ShapeDtypeStruct((batch * seq, ATT_WIDTH), BF16),
        scratch_shapes=[pltpu.VMEM((ATT_HEADS, ATT_QBLK, ATT_KBLKS * ATT_QBLK), F32)],
        compiler_params=_params("arbitrary", "arbitrary"),
        name="attn_prompt",
    )(za, za, za, za, za, za, za, base)


def _attn_sample_body(q_ref, kn_ref, vn_ref, ck_ref, cv_ref, base_ref, o_ref, bias_ref):
    @pl.when(pl.program_id(0) == 0)
    def _():
        _fill_bias(base_ref, bias_ref, False)

    cw = ck_ref.shape[0]
    bias_fn = lambda h, j: bias_ref[h, :, :cw] if j == 0 else bias_ref[h, :, cw:]
    _attn_heads(q_ref, [ck_ref, kn_ref], [cv_ref, vn_ref], bias_fn, [None, None], o_ref)


def _attn_sample(za, ck, cv, base, batch, seq):
    cw = ck.shape[1]
    return pl.pallas_call(
        _attn_sample_body,
        grid=(batch,),
        in_specs=[
            pl.BlockSpec((seq, ATT_WIDTH), lambda b: (b, 0)),
            pl.BlockSpec((seq, ATT_WIDTH), lambda b: (b, 1)),
            pl.BlockSpec((seq, ATT_WIDTH), lambda b: (b, 2)),
            pl.BlockSpec((None, cw, ATT_WIDTH), lambda b: (b, 0, 0)),
            pl.BlockSpec((None, cw, ATT_WIDTH), lambda b: (b, 0, 0)),
            pl.BlockSpec(base.shape, lambda b: (0, 0)),
        ],
        out_specs=pl.BlockSpec((seq, ATT_WIDTH), lambda b: (b, 0)),
        out_shape=jax.ShapeDtypeStruct((batch * seq, ATT_WIDTH), BF16),
        scratch_shapes=[pltpu.VMEM((ATT_HEADS, seq, cw + seq), F32)],
        compiler_params=_params("arbitrary"),
        name="attn_sample",
    )(za, za, za, ck, cv, base)


def _rel_bias_base(table):
    top = table[:, 2 * MAX_REL:].astype(F32)
    rev = table[:, ::-1][:, :2 * MAX_REL].astype(F32)
    left = WINDOW - MAX_REL
    return jnp.concatenate([jnp.broadcast_to(top, (ATT_HEADS, left)), rev,
                            jnp.broadcast_to(top, (ATT_HEADS, BIAS_W - left - 2 * MAX_REL))], axis=1)


def _hgrn_consts(c):
    t = np.arange(c)[:, None]
    j = np.arange(c)[None, :]
    mats = [j <= t, j > t]
    masks = []
    m = c // 2
    while m >= 1:
        ref = (t // (2 * m)) * (2 * m) + m - 1
        second = (t % (2 * m)) >= m
        if m < SUBLANES:
            mats.append((second & (j > ref) & (j <= t)) | (~second & (j > t) & (j <= ref)))
        masks.append((t // (2 * m)) == (j // (2 * m)))
        m //= 2
    return (jnp.asarray(np.concatenate(mats, 0).astype(np.float32), BF16),
            jnp.asarray(np.stack(masks).astype(np.float32)))


def _hgrn_body(zb_ref, s0_ref, lower_ref, gon_ref, p_ref, mask_ref, o_ref, sfin_ref, st_ref, *, single_step):
    c = p_ref.shape[1]
    step = pl.program_id(1)

    def load_state():
        for h in range(HG_HEADS):
            st_ref[h] = s0_ref[0, h].T

    if single_step:
        load_state()
    else:
        pl.when(step == 0)(load_state)

    pmat = p_ref[...]
    n_levels = mask_ref.shape[0]
    head = lambda a, h: a[:, h * HG_DK:(h + 1) * HG_DK]
    low = lower_ref[...]
    row = lax.broadcasted_iota(jnp.int32, (c, HG_WIDTH), 0)

    def stage1(r0):
        part = lambda i: zb_ref[r0:r0 + c, i * HG_WIDTH:(i + 1) * HG_WIDTH]
        q = part(0)
        f = low + (1.0 - low) * _sigmoid(part(1))
        lf = jnp.log(f)
        k = 1.0 - f
        ib = part(2)
        v = ib * _sigmoid(ib)
        og = part(3)
        hi = lf.astype(BF16)
        r1 = lf - hi.astype(F32)
        mid = r1.astype(BF16)
        lo = (r1 - mid.astype(F32)).astype(BF16)
        e = (jnp.dot(pmat, hi, preferred_element_type=F32) + jnp.dot(pmat, mid, preferred_element_type=F32)
             + jnp.dot(pmat, lo, preferred_element_type=F32))
        b = e[0:c]
        return dict(q=q, k=k, v=v, e=e, b=b, decay=jnp.exp(e[c - 1:c]), qe=(q * jnp.exp(b)).astype(BF16),
                    kt=(k * jnp.exp(e[c:2 * c])).astype(BF16), vb=v.astype(BF16), qk=q * k,
                    gate=og * _sigmoid(og))

    def stage2(s):
        q, k, e, b = s["q"], s["k"], s["e"], s["b"]
        att = [None] * HG_HEADS
        n_rows_p = 2
        for lvl in range(n_levels):
            m = c >> (lvl + 1)
            if m >= SUBLANES:
                ref = [jnp.broadcast_to(b[p * 2 * m + m - 1:p * 2 * m + m], (2 * m, HG_WIDTH))
                       for p in range(c // (2 * m))]
                x = jnp.exp(-jnp.abs(b - (jnp.concatenate(ref, axis=0) if len(ref) > 1 else ref[0])))
            else:
                x = jnp.exp(e[n_rows_p * c:(n_rows_p + 1) * c])
                n_rows_p += 1
            second = (row & m) != 0
            qm = jnp.where(second, q * x, 0.0).astype(BF16)
            km = jnp.where(second, 0.0, k * x).astype(BF16)
            for h in range(HG_HEADS):
                a = lax.dot_general(head(qm, h), head(km, h), NT, preferred_element_type=F32)
                if lvl > 0:
                    a = a * mask_ref[lvl]
                att[h] = a if att[h] is None else att[h] + a
        return [jnp.dot(att[h].astype(BF16), head(s["vb"], h), preferred_element_type=F32)
                + jnp.sum(head(s["qk"], h), axis=-1, keepdims=True) * head(s["v"], h) for h in range(HG_HEADS)]

    def stage3(r0, s, intra):
        for h in range(HG_HEADS):
            st = st_ref[h]
            inter = lax.dot_general(head(s["qe"], h), st.astype(BF16), NT, preferred_element_type=F32)
            st_ref[h] = st * head(s["decay"], h) + lax.dot_general(head(s["vb"], h), head(s["kt"], h), TN,
                                                                  preferred_element_type=F32)
            o = _rms(inter + intra[h], gon_ref[...]) * head(s["gate"], h)
            o_ref[r0:r0 + c, h * HG_DK:(h + 1) * HG_DK] = o.astype(o_ref.dtype)

    starts = range(0, zb_ref.shape[0], c)
    firsts = [stage1(r0) for r0 in starts]
    intras = [stage2(s) for s in firsts]
    for r0, s, intra in zip(starts, firsts, intras):
        stage3(r0, s, intra)

    def write_state():
        for h in range(HG_HEADS):
            sfin_ref[0, h] = st_ref[h].T

    if single_step:
        write_state()
    else:
        pl.when(step == pl.num_programs(1) - 1)(write_state)


def _hgrn(zb, s0, lower, g_on, batch, seq):
    c = min(HG_C, seq)
    rows = min(HG_CHUNKS_PER_STEP * c, seq)
    assert seq % rows == 0 and rows % c == 0
    pmat, masks = _hgrn_consts(c)
    nc = seq // rows
    return pl.pallas_call(
        functools.partial(_hgrn_body, single_step=nc == 1),
        grid=(batch, nc),
        in_specs=[
            pl.BlockSpec((rows, 4 * HG_WIDTH), lambda b, i: (b * nc + i, 0)),
            pl.BlockSpec((1, HG_HEADS, HG_DK, HG_DK), lambda b, i: (b, 0, 0, 0)),
            pl.BlockSpec((1, HG_WIDTH), lambda b, i: (0, 0)),
            pl.BlockSpec((1, HG_DK), lambda b, i: (0, 0)),
            pl.BlockSpec(pmat.shape, lambda b, i: (0, 0)),
            pl.BlockSpec(masks.shape, lambda b, i: (0, 0, 0)),
        ],
        out_specs=[
            pl.BlockSpec((rows, HG_WIDTH), lambda b, i: (b * nc + i, 0)),
            pl.BlockSpec((1, HG_HEADS, HG_DK, HG_DK), lambda b, i: (b, 0, 0, 0)),
        ],
        out_shape=[
            jax.ShapeDtypeStruct((batch * seq, HG_WIDTH), BF16),
            jax.ShapeDtypeStruct((batch, HG_HEADS, HG_DK, HG_DK), F32),
        ],
        scratch_shapes=[pltpu.VMEM((HG_HEADS, HG_DK, HG_DK), F32)],
        compiler_params=_params("arbitrary", "arbitrary"),
        name="hgrn2",
    )(zb, s0, lower, g_on, pmat, masks)


def _split_bf16(x):
    hi = x.astype(BF16)
    return hi, (x - hi.astype(F32)).astype(BF16)


def _merge_body(att_p, hg_p, zg_p, x_p, att_s, hg_s, zg_s, x_s, wpa_ref, wpb_ref, wout_ref, gffn_ref, wr_ref, br_ref,
                x1_ref, h2_ref, route_ref, cnt_ref, *, n_first):
    weights = (wpa_ref, wpb_ref, wout_ref, gffn_ref, wr_ref, br_ref)
    outs = (x1_ref, h2_ref, route_ref, cnt_ref)
    i = pl.program_id(0)
    pl.when(i < n_first)(functools.partial(_merge_block, att_p, hg_p, zg_p, x_p, *weights, *outs))
    pl.when(i >= n_first)(functools.partial(_merge_block, att_s, hg_s, zg_s, x_s, *weights, *outs))


def _merge_block(att_ref, hg_ref, zg_ref, x_ref, wpa_ref, wpb_ref, wout_ref, gffn_ref, wr_ref, br_ref,
                 x1_ref, h2_ref, route_ref, cnt_ref):
    pa = jnp.dot(att_ref[...].astype(BF16), wpa_ref[...], preferred_element_type=F32)
    pb = jnp.dot(hg_ref[...].astype(BF16), wpb_ref[...], preferred_element_type=F32)
    y = _sigmoid(zg_ref[:, :D_MODEL]) * pa + _sigmoid(zg_ref[:, D_MODEL:]) * pb
    x1 = x_ref[...] + jnp.dot(y.astype(BF16), wout_ref[...], preferred_element_type=F32)
    x1_ref[...] = x1
    h2 = _rms(x1, gffn_ref[...])
    h2_ref[...] = h2.astype(BF16)

    h_hi, h_lo = _split_bf16(h2)
    w_hi, w_lo = _split_bf16(wr_ref[...])
    both = jnp.dot(h_hi, jnp.concatenate([w_hi, w_lo], axis=1), preferred_element_type=F32)
    logits = both[:, :LANES] + both[:, LANES:] + jnp.dot(h_lo, w_hi, preferred_element_type=F32) + br_ref[...]
    lane = lax.broadcasted_iota(jnp.int32, logits.shape, 1).astype(F32)
    cur = logits
    vals, idxs = [], []
    for _ in range(TOP_K):
        m = jnp.max(cur, axis=-1, keepdims=True)
        idx = jnp.min(jnp.where(cur == m, lane, float(LANES)), axis=-1, keepdims=True)
        vals.append(m)
        idxs.append(idx)
        cur = jnp.where(lane == idx, -jnp.inf, cur)
    es = [jnp.exp(v - vals[0]) for v in vals]
    inv = 1.0 / functools.reduce(jnp.add, es)
    route = jnp.zeros_like(logits)
    for k, (ex, idx) in enumerate(zip(es, idxs)):
        route = (route + jnp.where(lane == idx, 1.0, 0.0) + jnp.where(lane == float(ROUTE_IDX + k), idx, 0.0)
                 + jnp.where(lane == float(ROUTE_P + k), ex * inv, 0.0))
    route_ref[...] = route
    sel = jnp.where(lane < float(N_EXPERTS), route, 0.0)
    for blk in range(cnt_ref.shape[0]):
        cnt_ref[blk] = jnp.sum(sel[blk * TBK:(blk + 1) * TBK], axis=0, keepdims=True)


def _merge(first, second, wpa, wpb, wout, g_ffn, w_router, b_router):
    tm = MERGE_TM
    assert first[3].shape[0] % tm == 0 and second[3].shape[0] % tm == 0
    n1, n2 = first[3].shape[0] // tm, second[3].shape[0] // tm
    widths = (ATT_WIDTH, HG_WIDTH, 2 * D_MODEL, D_MODEL)
    spec1 = [pl.BlockSpec((tm, w), lambda i: (jnp.minimum(i, n1 - 1), 0)) for w in widths]
    spec2 = [pl.BlockSpec((tm, w), lambda i: (jnp.maximum(i - n1, 0), 0)) for w in widths]
    row = lambda w: pl.BlockSpec((tm, w), lambda i: (i, 0))
    full = lambda a: pl.BlockSpec(a.shape, lambda i: (0,) * a.ndim)
    n = (n1 + n2) * tm
    return pl.pallas_call(
        functools.partial(_merge_body, n_first=n1),
        grid=(n1 + n2,),
        in_specs=spec1 + spec2 + [full(wpa), full(wpb), full(wout), full(g_ffn), full(w_router), full(b_router)],
        out_specs=[row(D_MODEL), row(D_MODEL), row(LANES),
                   pl.BlockSpec((tm // TBK, 1, LANES), lambda i: (i, 0, 0))],
        out_shape=[
            jax.ShapeDtypeStruct((n, D_MODEL), F32),
            jax.ShapeDtypeStruct((n, D_MODEL), BF16),
            jax.ShapeDtypeStruct((n, LANES), F32),
            jax.ShapeDtypeStruct((n // TBK, 1, LANES), F32),
        ],
        compiler_params=_params("arbitrary"),
        name="merge_router",
    )(*first, *second, wpa, wpb, wout, g_ffn, w_router, b_router)


def _route_plan(cnt):
    pc = (cnt + ROW_CH - 1) // ROW_CH
    loff = jnp.cumsum(pc, axis=1) - pc
    tot = jnp.sum(pc, axis=0)
    reg = (tot + CH_PER_TILE - 1) // CH_PER_TILE * CH_PER_TILE
    gstart = jnp.cumsum(reg) - reg
    goff = gstart[None, :] + jnp.cumsum(pc, axis=0) - pc
    ntiles = jnp.sum(reg) // CH_PER_TILE
    present = reg > 0
    ids = jnp.arange(N_EXPERTS, dtype=jnp.int32)
    later = (ids[None, :] > ids[:, None]) & present[None, :]
    nxt = jnp.min(jnp.where(later, ids[None, :], N_EXPERTS), axis=1)
    nxt = jnp.where(nxt == N_EXPERTS, -1, nxt)
    slot = (jnp.cumsum(present.astype(jnp.int32)) - 1) % 2
    first = jnp.min(jnp.where(present, ids, N_EXPERTS)).reshape(1)
    j = jnp.arange(BLK_CH, dtype=jnp.int32)
    run = jnp.sum(((loff + pc)[:, None, :] <= j[None, :, None]).astype(jnp.int32), axis=2)
    shift = jnp.sum(jnp.where(run[:, :, None] == ids[None, None, :], (goff - loff)[:, None, :], 0), axis=2)
    gmap = shift + j[None, :]
    i32 = lambda a: a.astype(jnp.int32)
    return dict(loff=i32(loff), gmap=i32(gmap), totc=i32(jnp.sum(pc, axis=1)),
                pad_start=i32(gstart + tot), pad_cnt=i32(reg - tot), ntiles=i32(ntiles).reshape(1),
                tile_start=i32(gstart // CH_PER_TILE), tile_count=i32(reg // CH_PER_TILE),
                next_expert=i32(nxt), weight_slot=i32(slot), first_expert=i32(first))


def _chunk_rows(ref, chunk):
    return ref.at[pl.ds(pl.multiple_of(chunk * ROW_CH, ROW_CH), ROW_CH)]


def _for_chunks(n, do):
    log_unroll = 2
    groups = lax.shift_right_logical(n, log_unroll)

    def group(i, carry):
        for u in range(1 << log_unroll):
            do(lax.shift_left(i, log_unroll) + u)
        return carry
    lax.fori_loop(0, groups, group, 0)

    def single(c, carry):
        do(c)
        return carry
    lax.fori_loop(lax.shift_left(groups, log_unroll), n, single, 0)


def _wait_chunks(n, copy_of_rows):
    for bit in range((RB // ROW_CH).bit_length()):
        @pl.when(((n >> bit) & 1) == 1)
        def _(bit=bit):
            copy_of_rows((1 << bit) * ROW_CH).wait()


def _dispatch_body(gmap_s, totc_s, pads_s, padn_s, nt_s, h2_ref, route_ref, tri_ref, loffv_ref,
                   xs_hbm, buf, zbuf, sem):
    b = pl.program_id(0)
    nb = pl.num_programs(0)
    slot = lax.rem(b, 2)
    fill_sem, tile_sem = 2, 3
    nt_max = xs_hbm.shape[0] // EXP_TM

    def copy_out(src, gchunk, sem_i):
        return pltpu.make_async_copy(src, _chunk_rows(xs_hbm, gchunk), sem.at[sem_i])

    def zero_tile(t):
        return pltpu.make_async_copy(zbuf, xs_hbm.at[pl.ds(pl.multiple_of(t * EXP_TM, EXP_TM), EXP_TM)],
                                     sem.at[tile_sem])

    def wait_block(step, slot_):
        for u in range(COMB_BLKS):
            _wait_chunks(totc_s[step * COMB_BLKS + u], lambda rows: pltpu.make_async_copy(
                buf.at[slot_, pl.ds(0, rows)], xs_hbm.at[pl.ds(0, rows)], sem.at[slot_]))

    @pl.when(b == 0)
    def _():
        zbuf[...] = jnp.zeros_like(zbuf)

    @pl.when(b >= 2)
    def _():
        wait_block(b - 2, slot)

    erow = lax.broadcasted_iota(jnp.int32, (N_EXPERTS, TBK), 0).astype(F32)

    def positions(u):
        rt = route_ref[u * TBK:(u + 1) * TBK, :].T
        rank_t = jnp.dot(rt[0:N_EXPERTS].astype(BF16), tri_ref[...], preferred_element_type=F32)
        lpos_t = loffv_ref[u] * float(ROW_CH) + rank_t
        return [jnp.sum(jnp.where(rt[ROUTE_IDX + k:ROUTE_IDX + k + 1] == erow, lpos_t, 0.0), axis=0, keepdims=True)
                for k in range(TOP_K)]

    def permute(u, lposk):
        h2 = h2_ref[u * TBK:(u + 1) * TBK, :]
        for r0 in range(0, RB, PERM_TM):
            rrow = lax.broadcasted_iota(jnp.int32, (PERM_TM, TBK), 0).astype(F32) + float(r0)
            perm = functools.reduce(jnp.add, [jnp.where(lposk[k] == rrow, 1.0, 0.0) for k in range(TOP_K)])
            buf[slot, u * RB + r0:u * RB + r0 + PERM_TM, :] = jnp.dot(perm.astype(BF16), h2,
                                                                     preferred_element_type=F32)

    pos = [positions(u) for u in range(COMB_BLKS)]
    for u in range(COMB_BLKS):
        permute(u, pos[u])
    for u in range(COMB_BLKS):
        bb = b * COMB_BLKS + u
        _for_chunks(totc_s[bb], lambda c: copy_out(_chunk_rows(buf.at[slot], u * BLK_CH + c),
                                                   gmap_s[bb * BLK_CH + c], slot).start())

    @pl.when(b == nb - 1)
    def _():
        zchunk = _chunk_rows(zbuf, 0)

        def fill_expert(e, carry):
            def one(c, carry2):
                copy_out(zchunk, pads_s[e] + c, fill_sem).start()
                return carry2
            lax.fori_loop(0, padn_s[e], one, 0)
            return carry
        lax.fori_loop(0, N_EXPERTS, fill_expert, 0)

        def fill_tile(t, carry):
            zero_tile(t).start()
            return carry
        lax.fori_loop(nt_s[0], nt_max, fill_tile, 0)

        @pl.when(b >= 1)
        def _():
            wait_block(b - 1, 1 - slot)
        wait_block(b, slot)

        def drain_expert(e, carry):
            def one(c, carry2):
                copy_out(zchunk, 0, fill_sem).wait()
                return carry2
            lax.fori_loop(0, padn_s[e], one, 0)
            return carry
        lax.fori_loop(0, N_EXPERTS, drain_expert, 0)

        def drain_tile(t, carry):
            zero_tile(0).wait()
            return carry
        lax.fori_loop(nt_s[0], nt_max, drain_tile, 0)


def _dispatch(plan, h2, route, loffv, n_rows):
    tm = COMB_BLKS * TBK
    assert h2.shape[0] % tm == 0
    tri = jnp.asarray(np.triu(np.ones((TBK, TBK), np.float32), 1), BF16)
    grid_spec = pltpu.PrefetchScalarGridSpec(
        num_scalar_prefetch=5,
        grid=(h2.shape[0] // tm,),
        in_specs=[
            pl.BlockSpec((tm, D_MODEL), lambda b, *_: (b, 0)),
            pl.BlockSpec((tm, LANES), lambda b, *_: (b, 0)),
            pl.BlockSpec((TBK, TBK), lambda b, *_: (0, 0)),
            pl.BlockSpec((COMB_BLKS, N_EXPERTS, 1), lambda b, *_: (b, 0, 0)),
        ],
        out_specs=pl.BlockSpec(memory_space=pl.ANY),
        scratch_shapes=[pltpu.VMEM((2, COMB_BLKS * RB, D_MODEL), F32), pltpu.VMEM((EXP_TM, D_MODEL), F32),
                        pltpu.SemaphoreType.DMA((4,))],
    )
    return pl.pallas_call(
        _dispatch_body,
        grid_spec=grid_spec,
        out_shape=jax.ShapeDtypeStruct((n_rows, D_MODEL), F32),
        compiler_params=_params("arbitrary"),
        name="moe_dispatch",
    )(plan["gmap"].reshape(-1), plan["totc"], plan["pad_start"], plan["pad_cnt"], plan["ntiles"],
      h2, route, tri, loffv)


def _experts_body(t0_s, n_s, nxt_s, par_s, first_s, nt_s, xs_hbm, wgu_hbm, bgu_ref, wd_hbm, bd_ref, ys_hbm,
                  wg_f, wd_f, wg_b, wd_b, xbuf, ybuf, wsem, xsem, ysem):
    e = pl.program_id(0)
    nt = nt_s[0]
    nt_max = xs_hbm.shape[0] // EXP_TM
    tile_rows = lambda t: pl.ds(pl.multiple_of(t * EXP_TM, EXP_TM), EXP_TM)

    def weight_copies(ex, s, piece):
        rows = pl.ds(pl.multiple_of(piece * W_PIECE_ROWS, W_PIECE_ROWS), W_PIECE_ROWS)
        return (pltpu.make_async_copy(wgu_hbm.at[ex, rows], wg_f.at[s, rows], wsem.at[s, 0]),
                pltpu.make_async_copy(wd_hbm.at[ex, rows], wd_f.at[s, rows], wsem.at[s, 1]))

    def x_copy(t, s):
        return pltpu.make_async_copy(xs_hbm.at[tile_rows(t)], xbuf.at[s], xsem.at[s])

    def y_copy(t, s):
        return pltpu.make_async_copy(ybuf.at[s], ys_hbm.at[tile_rows(t)], ysem.at[s])

    @pl.when(e == 0)
    def _():
        for piece in range(W_PIECES):
            for cp in weight_copies(first_s[0], 0, piece):
                cp.start()
        x_copy(0, 0).start()

    @pl.when(n_s[e] > 0)
    def _():
        s_w = par_s[e]
        for piece in range(W_PIECES):
            for cp in weight_copies(e, s_w, piece):
                cp.wait()
        wg_b[...] = wg_f[s_w].astype(BF16)
        wd_b[...] = wd_f[s_w].astype(BF16)
        has_next = nxt_s[e] >= 0

        def request(piece):
            for cp in weight_copies(nxt_s[e], 1 - s_w, piece):
                cp.start()

        def tile(i, carry):
            t = t0_s[e] + i
            s = t & 1
            x_copy(t, s).wait()

            @pl.when(has_next & (i < W_PIECES))
            def _():
                request(i)

            @pl.when(t + 1 < nt)
            def _():
                x_copy(t + 1, 1 - s).start()

            @pl.when(t >= 2)
            def _():
                y_copy(t - 2, s).wait()

            gu = jnp.dot(xbuf[s].astype(BF16), wg_b[...], preferred_element_type=F32) + bgu_ref[...]
            gate = jnp.minimum(gu[:, :D_FF], SWIGLU_LIMIT)
            up = jnp.clip(gu[:, D_FF:], -SWIGLU_LIMIT, SWIGLU_LIMIT)
            act = (up + 1.0) * gate * _sigmoid(SWIGLU_ALPHA * gate)
            ybuf[s] = jnp.dot(act.astype(BF16), wd_b[...], preferred_element_type=F32) + bd_ref[...]
            y_copy(t, s).start()
            return carry
        lax.fori_loop(0, n_s[e], tile, 0)

        @pl.when(has_next)
        def _():
            def rest(piece, carry):
                request(piece)
                return carry
            lax.fori_loop(jnp.minimum(n_s[e], W_PIECES), W_PIECES, rest, 0)

    @pl.when(e == pl.num_programs(0) - 1)
    def _():
        @pl.when(nt >= 2)
        def _():
            y_copy(nt - 2, nt & 1).wait()
        y_copy(nt - 1, (nt - 1) & 1).wait()
        ybuf[0] = jnp.zeros((EXP_TM, D_MODEL), F32)

        def fill(t, carry):
            y_copy(t, 0).start()
            return carry
        lax.fori_loop(nt, nt_max, fill, 0)

        def drain(t, carry):
            y_copy(0, 0).wait()
            return carry
        lax.fori_loop(nt, nt_max, drain, 0)


def _experts(plan, xs, w_gu, b_gu, w_down, b_down):
    n_rows = xs.shape[0]
    of_expert = lambda e, *_: (e, 0, 0)
    grid_spec = pltpu.PrefetchScalarGridSpec(
        num_scalar_prefetch=6,
        grid=(N_EXPERTS,),
        in_specs=[
            pl.BlockSpec(memory_space=pl.ANY),
            pl.BlockSpec(memory_space=pl.ANY),
            pl.BlockSpec((None, 1, 2 * D_FF), of_expert),
            pl.BlockSpec(memory_space=pl.ANY),
            pl.BlockSpec((None, 1, D_MODEL), of_expert),
        ],
        out_specs=pl.BlockSpec(memory_space=pl.ANY),
        scratch_shapes=[pltpu.VMEM((2, D_MODEL, 2 * D_FF), F32), pltpu.VMEM((2, D_FF, D_MODEL), F32),
                        pltpu.VMEM((D_MODEL, 2 * D_FF), BF16), pltpu.VMEM((D_FF, D_MODEL), BF16),
                        pltpu.VMEM((2, EXP_TM, D_MODEL), F32), pltpu.VMEM((2, EXP_TM, D_MODEL), F32),
                        pltpu.SemaphoreType.DMA((2, 2)), pltpu.SemaphoreType.DMA((2,)),
                        pltpu.SemaphoreType.DMA((2,))],
    )
    return pl.pallas_call(
        _experts_body,
        grid_spec=grid_spec,
        out_shape=jax.ShapeDtypeStruct((n_rows, D_MODEL), F32),
        compiler_params=_params("arbitrary"),
        name="moe_experts",
    )(plan["tile_start"], plan["tile_count"], plan["next_expert"], plan["weight_slot"], plan["first_expert"],
      plan["ntiles"], xs, w_gu, b_gu, w_down, b_down)


def _combine_body(gmap_s, totc_s, route_ref, tril_ref, loffrow_ref, x1_ref, gfin_ref, ys_hbm, y_ref, buf, sem):
    i = pl.program_id(0)
    n_steps = pl.num_programs(0)
    slot = lax.rem(i, 2)

    def copy_in(slot_, lchunk, gchunk):
        return pltpu.make_async_copy(_chunk_rows(ys_hbm, gchunk), _chunk_rows(buf.at[slot_], lchunk), sem.at[slot_])

    def fetch_step(step, slot_):
        for u in range(COMB_BLKS):
            bb = step * COMB_BLKS + u
            _for_chunks(totc_s[bb], lambda c: copy_in(slot_, u * BLK_CH + c, gmap_s[bb * BLK_CH + c]).start())

    def wait_step(step, slot_):
        for u in range(COMB_BLKS):
            _wait_chunks(totc_s[step * COMB_BLKS + u], lambda rows: pltpu.make_async_copy(
                ys_hbm.at[pl.ds(0, rows)], buf.at[slot_, pl.ds(0, rows)], sem.at[slot_]))

    @pl.when(i == 0)
    def _():
        buf[...] = jnp.zeros_like(buf)
        fetch_step(0, 0)

    @pl.when(i + 1 < n_steps)
    def _():
        fetch_step(i + 1, 1 - slot)

    wait_step(i, slot)

    lane = lax.broadcasted_iota(jnp.int32, (1, LANES), 1).astype(F32)
    col = lax.broadcasted_iota(jnp.int32, (TBK, PERM_TM), 1).astype(F32).astype(BF16)
    zero = jnp.zeros((), BF16)

    def positions(u):
        route = route_ref[u * TBK:(u + 1) * TBK, :]
        sel = jnp.where(lane < float(N_EXPERTS), route, 0.0).astype(BF16)
        rank = jnp.dot(tril_ref[...], sel, preferred_element_type=F32)
        lpos = loffrow_ref[u] * float(ROW_CH) + rank
        lposk, pkb = [], []
        for k in range(TOP_K):
            idx = route[:, ROUTE_IDX + k:ROUTE_IDX + k + 1]
            lposk.append(jnp.sum(jnp.where(lane == idx, lpos, 0.0), axis=-1, keepdims=True))
            pkb.append(route[:, ROUTE_P + k:ROUTE_P + k + 1].astype(BF16))
        return lposk, pkb

    def gather(u, lposk, pkb):
        acc = x1_ref[u * TBK:(u + 1) * TBK, :]
        for r0 in range(0, RB, PERM_TM):
            rel = [(lposk[k] - float(r0)).astype(BF16) for k in range(TOP_K)]
            w = functools.reduce(jnp.add, [jnp.where(rel[k] == col, pkb[k], zero) for k in range(TOP_K)])
            rows = buf[slot, u * RB + r0:u * RB + r0 + PERM_TM, :].astype(BF16)
            acc = acc + jnp.dot(w, rows, preferred_element_type=F32)
        y_ref[u * TBK:(u + 1) * TBK, :] = _rms(acc, gfin_ref[...])

    pos = [positions(u) for u in range(COMB_BLKS)]
    for u in range(COMB_BLKS):
        gather(u, *pos[u])


def _combine(plan, blocks, route, loffrow, x1, g_final, ys):
    assert RB % PERM_TM == 0 and PERM_TM <= 256
    b0, b1 = blocks
    assert b0 % COMB_BLKS == 0 and b1 % COMB_BLKS == 0
    n_steps, s0 = (b1 - b0) // COMB_BLKS, b0 // COMB_BLKS
    tm = COMB_BLKS * TBK
    tril = jnp.asarray(np.tril(np.ones((TBK, TBK), np.float32), -1), BF16)
    grid_spec = pltpu.PrefetchScalarGridSpec(
        num_scalar_prefetch=2,
        grid=(n_steps,),
        in_specs=[
            pl.BlockSpec((tm, LANES), lambda i, *_: (i + s0, 0)),
            pl.BlockSpec((TBK, TBK), lambda i, *_: (0, 0)),
            pl.BlockSpec((COMB_BLKS, 1, LANES), lambda i, *_: (i + s0, 0, 0)),
            pl.BlockSpec((tm, D_MODEL), lambda i, *_: (i + s0, 0)),
            pl.BlockSpec((1, D_MODEL), lambda i, *_: (0, 0)),
            pl.BlockSpec(memory_space=pl.ANY),
        ],
        out_specs=pl.BlockSpec((tm, D_MODEL), lambda i, *_: (i, 0)),
        scratch_shapes=[pltpu.VMEM((2, COMB_BLKS * RB, D_MODEL), F32), pltpu.SemaphoreType.DMA((2,))],
    )
    return pl.pallas_call(
        _combine_body,
        grid_spec=grid_spec,
        out_shape=jax.ShapeDtypeStruct(((b1 - b0) * TBK, D_MODEL), F32),
        compiler_params=_params("arbitrary"),
        name="moe_combine",
    )(plan["gmap"][b0:b1].reshape(-1), plan["totc"][b0:b1], route, tril, loffrow, x1, g_final, ys)


def kernel(x_prompt, x_sample, cache_k, cache_v, state_s, g_mix, w_in, rel_bias, lb_logits, g_out_norm,
           w_pa, w_pb, w_out, g_ffn, w_router, b_router, w_gu, b_gu, w_down, b_down, g_final):
    B, T = x_prompt.shape[:2]
    DB, S = x_sample.shape[:2]
    depth = w_in.shape[0]
    assert depth == 1 and T % ATT_QBLK == 0 and S == CHUNK
    cw = cache_k.shape[2]
    assert cw == WINDOW
    l = 0

    lower = jnp.cumsum(jax.nn.softmax(lb_logits.astype(F32), axis=0), axis=0)[l].reshape(1, HG_WIDTH)
    w_in_b = w_in[l].astype(BF16)
    wpa, wpb, wout = w_pa[l].astype(BF16), w_pb[l].astype(BF16), w_out[l].astype(BF16)
    row = lambda a: a.reshape(1, -1).astype(F32)
    base = _rel_bias_base(rel_bias[l])
    b_gu3 = b_gu[l].reshape(N_EXPERTS, 1, 2 * D_FF)
    b_down3 = b_down[l].reshape(N_EXPERTS, 1, D_MODEL)
    pad_e = LANES - N_EXPERTS
    wr = jnp.pad(w_router[l].astype(F32), ((0, 0), (0, pad_e)))
    br = jnp.concatenate([b_router[l].astype(F32), jnp.full((pad_e,), NEG, F32)]).reshape(1, LANES)

    n_tok = B * T + DB * S
    nb, nbp = n_tok // TBK, (B * T) // TBK

    def front(x, batch, seq, s0, attend):
        xf = x.reshape(batch * seq, D_MODEL)
        za, zb, zg = _inproj(xf, row(g_mix[l]), w_in_b)
        att = attend(za)
        hg, s_fin = _hgrn(zb, s0, lower, row(g_out_norm[l]), batch, seq)
        za3 = za.reshape(batch, seq, 3 * ATT_WIDTH)
        heads = lambda a: a.reshape(1, batch, a.shape[1], ATT_HEADS, ATT_DIM)
        keep = min(WINDOW, seq)
        nk = heads(za3[:, seq - keep:, ATT_WIDTH:2 * ATT_WIDTH])
        nv = heads(za3[:, seq - keep:, 2 * ATT_WIDTH:])
        return dict(mix=(att, hg, zg, xf), nk=nk, nv=nv, s=s_fin[None])

    ck = cache_k[l].reshape(DB, cw, ATT_WIDTH)
    cv = cache_v[l].reshape(DB, cw, ATT_WIDTH)
    fp = front(x_prompt, B, T, jnp.zeros((B, HG_HEADS, HG_DK, HG_DK), F32), lambda za: _attn_prompt(za, base, B, T))
    fs = front(x_sample, DB, S, state_s[l].astype(F32), lambda za: _attn_sample(za, ck, cv, base, DB, S))

    x1, h2, route, cnt = _merge(fp["mix"], fs["mix"], wpa, wpb, wout, row(g_ffn[l]), wr, br)
    cnt = cnt[:, 0, :N_EXPERTS].astype(jnp.int32)
    max_rows = n_tok * TOP_K + nb * N_EXPERTS * (ROW_CH - 1) + N_EXPERTS * (EXP_TM - 1)
    nt_max = -(-max_rows // EXP_TM)
    plan = _route_plan(cnt)
    loff_f = plan["loff"].astype(F32)
    xs = _dispatch(plan, h2, route, loff_f[:, :, None], nt_max * EXP_TM)
    ysort = _experts(plan, xs, w_gu[l], b_gu3, w_down[l], b_down3)
    loffrow = jnp.pad(loff_f, ((0, 0), (0, pad_e)))[:, None, :]
    yp = _combine(plan, (0, nbp), route, loffrow, x1, row(g_final), ysort)
    ys = _combine(plan, (nbp, nb), route, loffrow, x1, row(g_final), ysort)
    return (yp.reshape(B, T, D_MODEL), ys.reshape(DB, S, D_MODEL), fp["nk"], fp["nv"], fp["s"],
            fs["nk"], fs["nv"], fs["s"])
```

```python
import functools

import numpy as np
import jax
import jax.numpy as jnp
from jax import lax
from jax.experimental import pallas as pl
from jax.experimental.pallas import tpu as pltpu

F32 = jnp.float32
BF16 = jnp.bfloat16

D_MODEL = 1024
CHUNK = 64
LEFT_CHUNKS = 8
WINDOW = LEFT_CHUNKS * CHUNK
ATT_HEADS = 8
ATT_DIM = 64
ATT_WIDTH = ATT_HEADS * ATT_DIM
MAX_REL = 256
HG_HEADS = 4
HG_DK = 128
HG_WIDTH = HG_HEADS * HG_DK
N_EXPERTS = 32
TOP_K = 4
D_FF = D_MODEL
SWIGLU_LIMIT = 7.0
SWIGLU_ALPHA = 1.702
RMS_EPS = 1e-5

LANES = 128
NEG = -1e30
LOG2E = 1.4426950408889634
ATT_QBLK = 4 * CHUNK
ATT_KBLKS = LEFT_CHUNKS * CHUNK // ATT_QBLK + 1
ATT_PAIRS_PER_STAGE = 2
HG_C = 128
HG_CHUNKS_PER_STEP = 8
VMEM_LIMIT = 56 * 1024 * 1024
BIAS_W = 1024
SUBLANES = 8
TBK = 256
ROW_CH = SUBLANES
RB = TBK * TOP_K + N_EXPERTS * ROW_CH
MERGE_TM = 2 * TBK
COMB_BLKS = 2
PERM_TM = 256
EXP_TM = 512
W_PIECES = 4
W_PIECE_ROWS = D_MODEL // W_PIECES
CH_PER_TILE = EXP_TM // ROW_CH
BLK_CH = RB // ROW_CH
ROUTE_IDX = 64
ROUTE_P = 72

NT = (((1,), (1,)), ((), ()))
TN = (((0,), (0,)), ((), ()))


def _rms(x, g):
    return x * lax.rsqrt(jnp.mean(x * x, axis=-1, keepdims=True) + RMS_EPS) * g


def _sigmoid(x):
    return 1.0 / (1.0 + jnp.exp(-x))


def _params(*sem):
    return pltpu.CompilerParams(dimension_semantics=sem, vmem_limit_bytes=VMEM_LIMIT)


def _inproj_body(x_ref, g_ref, w_ref, za_ref, zb_ref, zg_ref):
    h = _rms(x_ref[...], g_ref[...]).astype(BF16)
    a, b = 3 * ATT_WIDTH, 3 * ATT_WIDTH + 4 * HG_WIDTH
    za_ref[...] = jnp.dot(h, w_ref[:, :a], preferred_element_type=F32)
    zb_ref[...] = jnp.dot(h, w_ref[:, a:b], preferred_element_type=F32)
    zg_ref[...] = jnp.dot(h, w_ref[:, b:], preferred_element_type=F32)


def _inproj(x, g, w_bf16, tm=512):
    n = x.shape[0]
    assert n % tm == 0
    cols = w_bf16.shape[1]
    wa, wb, wg = 3 * ATT_WIDTH, 4 * HG_WIDTH, 2 * D_MODEL
    return pl.pallas_call(
        _inproj_body,
        grid=(n // tm,),
        in_specs=[
            pl.BlockSpec((tm, D_MODEL), lambda i: (i, 0)),
            pl.BlockSpec((1, D_MODEL), lambda i: (0, 0)),
            pl.BlockSpec((D_MODEL, cols), lambda i: (0, 0)),
        ],
        out_specs=[
            pl.BlockSpec((tm, wa), lambda i: (i, 0)),
            pl.BlockSpec((tm, wb), lambda i: (i, 0)),
            pl.BlockSpec((tm, wg), lambda i: (i, 0)),
        ],
        out_shape=[
            jax.ShapeDtypeStruct((n, wa), F32),
            jax.ShapeDtypeStruct((n, wb), F32),
            jax.ShapeDtypeStruct((n, wg), F32),
        ],
        compiler_params=_params("arbitrary"),
        name="inproj",
    )(x, g, w_bf16)


def _attn_heads(q_ref, k_refs, v_refs, bias_fn, pens, o_ref):
    lane = lax.broadcasted_iota(jnp.int32, (1, LANES), 1)
    first = lane < ATT_DIM
    halves = (first, lane >= ATT_DIM)
    for hp0 in range(0, ATT_HEADS // 2, ATT_PAIRS_PER_STAGE):
        pairs = range(hp0, hp0 + ATT_PAIRS_PER_STAGE)
        sl = {hp: slice(hp * LANES, (hp + 1) * LANES) for hp in pairs}
        scores = {}
        for hp in pairs:
            q2 = q_ref[:, sl[hp]] * (ATT_DIM ** -0.5 * LOG2E)
            ks = [k[:, sl[hp]].astype(BF16) for k in k_refs]
            for half, mine in enumerate(halves):
                qm = jnp.where(mine, q2, 0.0).astype(BF16)
                ss = []
                for j, kj in enumerate(ks):
                    s = lax.dot_general(qm, kj, NT, preferred_element_type=F32) + bias_fn(2 * hp + half, j)
                    if pens[j] is not None:
                        s = s + pens[j]
                    ss.append(s)
                scores[hp, half] = ss
        for hp in pairs:
            outs = []
            for half, mine in enumerate(halves):
                ss = scores[hp, half]
                vs = [jnp.where(mine, v[:, sl[hp]], 1.0).astype(BF16) for v in v_refs]
                if all(s.shape == ss[0].shape for s in ss):
                    m = jnp.max(functools.reduce(jnp.maximum, ss), axis=-1, keepdims=True)
                else:
                    m = functools.reduce(jnp.maximum, [jnp.max(s, axis=-1, keepdims=True) for s in ss])
                outs.append(functools.reduce(jnp.add, [
                    jnp.dot(jnp.exp2(s - m).astype(BF16), vj, preferred_element_type=F32) for s, vj in zip(ss, vs)]))
            num = jnp.where(first, outs[0], outs[1])
            den = pltpu.roll(jnp.where(first, outs[1], outs[0]), ATT_DIM, 1)
            o_ref[:, sl[hp]] = (num * (1.0 / den)).astype(o_ref.dtype)


def _fill_bias(base_ref, bias_ref, banded):
    nq, nk = bias_ref.shape[1:]
    if banded:
        r = lax.broadcasted_iota(jnp.int32, (nq, nk), 0)
        s = lax.broadcasted_iota(jnp.int32, (nq, nk), 1)
        qc = (r + WINDOW) // CHUNK
        kc = s // CHUNK
        pen = jnp.where(kc <= qc, jnp.where(kc >= qc - LEFT_CHUNKS, 0.0, NEG), NEG)
    for h in range(ATT_HEADS):
        rows = jnp.broadcast_to(base_ref[h:h + 1, :], (nq, BIAS_W))
        t = pltpu.roll(rows, 0, 1, stride=1, stride_axis=0)[:, :nk] * LOG2E
        bias_ref[h] = t + pen if banded else t


def _attn_prompt_body(q_ref, k0, k1, k2, v0, v1, v2, base_ref, o_ref, bias_ref):
    i = pl.program_id(1)

    @pl.when((pl.program_id(0) == 0) & (i == 0))
    def _():
        _fill_bias(base_ref, bias_ref, True)

    bias_fn = lambda h, j: bias_ref[h, :, j * ATT_QBLK:(j + 1) * ATT_QBLK]
    back = ATT_KBLKS - 1

    @pl.when(i < back)
    def _():
        pens = [jnp.where(i - back + j >= 0, 0.0, NEG) for j in range(back)] + [None]
        _attn_heads(q_ref, [k0, k1, k2], [v0, v1, v2], bias_fn, pens, o_ref)

    @pl.when(i >= back)
    def _():
        _attn_heads(q_ref, [k0, k1, k2], [v0, v1, v2], bias_fn, [None] * ATT_KBLKS, o_ref)


def _attn_prompt(za, base, batch, seq):
    assert seq % ATT_QBLK == 0 and BIAS_W - ATT_KBLKS * ATT_QBLK >= ATT_QBLK - 1 and WINDOW >= MAX_REL
    nq = seq // ATT_QBLK
    back = ATT_KBLKS - 1
    qspec = pl.BlockSpec((ATT_QBLK, ATT_WIDTH), lambda b, i: (b * nq + i, 0))

    def kvspec(j, col):
        return pl.BlockSpec((ATT_QBLK, ATT_WIDTH),
                            lambda b, i: (b * nq + jnp.maximum(i - back + j, 0), col))

    return pl.pallas_call(
        _attn_prompt_body,
        grid=(batch, nq),
        in_specs=[qspec] + [kvspec(j, 1) for j in range(ATT_KBLKS)] + [kvspec(j, 2) for j in range(ATT_KBLKS)]
        + [pl.BlockSpec(base.shape, lambda b, i: (0, 0))],
        out_specs=pl.BlockSpec((ATT_QBLK, ATT_WIDTH), lambda b, i: (b * nq + i, 0)),
        out_shape=jax.ShapeDtypeStruct((batch * seq, ATT_WIDTH), BF16),
        scratch_shapes=[pltpu.VMEM((ATT_HEADS, ATT_QBLK, ATT_KBLKS * ATT_QBLK), F32)],
        compiler_params=_params("arbitrary", "arbitrary"),
        name="attn_prompt",
    )(za, za, za, za, za, za, za, base)


def _attn_sample_body(q_ref, kn_ref, vn_ref, ck_ref, cv_ref, base_ref, o_ref, bias_ref):
    @pl.when(pl.program_id(0) == 0)
    def _():
        _fill_bias(base_ref, bias_ref, False)

    cw = ck_ref.shape[0]
    bias_fn = lambda h, j: bias_ref[h, :, :cw] if j == 0 else bias_ref[h, :, cw:]
    _attn_heads(q_ref, [ck_ref, kn_ref], [cv_ref, vn_ref], bias_fn, [None, None], o_ref)


def _attn_sample(za, ck, cv, base, batch, seq):
    cw = ck.shape[1]
    return pl.pallas_call(
        _attn_sample_body,
        grid=(batch,),
        in_specs=[
            pl.BlockSpec((seq, ATT_WIDTH), lambda b: (b, 0)),
            pl.BlockSpec((seq, ATT_WIDTH), lambda b: (b, 1)),
            pl.BlockSpec((seq, ATT_WIDTH), lambda b: (b, 2)),
            pl.BlockSpec((None, cw, ATT_WIDTH), lambda b: (b, 0, 0)),
            pl.BlockSpec((None, cw, ATT_WIDTH), lambda b: (b, 0, 0)),
            pl.BlockSpec(base.shape, lambda b: (0, 0)),
        ],
        out_specs=pl.BlockSpec((seq, ATT_WIDTH), lambda b: (b, 0)),
        out_shape=jax.ShapeDtypeStruct((batch * seq, ATT_WIDTH), BF16),
        scratch_shapes=[pltpu.VMEM((ATT_HEADS, seq, cw + seq), F32)],
        compiler_params=_params("arbitrary"),
        name="attn_sample",
    )(za, za, za, ck, cv, base)


def _rel_bias_base(table):
    top = table[:, 2 * MAX_REL:].astype(F32)
    rev = table[:, ::-1][:, :2 * MAX_REL].astype(F32)
    left = WINDOW - MAX_REL
    return jnp.concatenate([jnp.broadcast_to(top, (ATT_HEADS, left)), rev,
                            jnp.broadcast_to(top, (ATT_HEADS, BIAS_W - left - 2 * MAX_REL))], axis=1)


def _hgrn_consts(c):
    t = np.arange(c)[:, None]
    j = np.arange(c)[None, :]
    mats = [j <= t, j > t]
    masks = []
    m = c // 2
    while m >= 1:
        ref = (t // (2 * m)) * (2 * m) + m - 1
        second = (t % (2 * m)) >= m
        if m < SUBLANES:
            mats.append((second & (j > ref) & (j <= t)) | (~second & (j > t) & (j <= ref)))
        masks.append((t // (2 * m)) == (j // (2 * m)))
        m //= 2
    return (jnp.asarray(np.concatenate(mats, 0).astype(np.float32), BF16),
            jnp.asarray(np.stack(masks).astype(np.float32)))


def _hgrn_body(zb_ref, s0_ref, lower_ref, gon_ref, p_ref, mask_ref, o_ref, sfin_ref, st_ref, *, single_step):
    c = p_ref.shape[1]
    step = pl.program_id(1)

    def load_state():
        for h in range(HG_HEADS):
            st_ref[h] = s0_ref[0, h].T

    if single_step:
        load_state()
    else:
        pl.when(step == 0)(load_state)

    pmat = p_ref[...]
    n_levels = mask_ref.shape[0]
    head = lambda a, h: a[:, h * HG_DK:(h + 1) * HG_DK]
    low = lower_ref[...]
    row = lax.broadcasted_iota(jnp.int32, (c, HG_WIDTH), 0)

    def stage1(r0):
        part = lambda i: zb_ref[r0:r0 + c, i * HG_WIDTH:(i + 1) * HG_WIDTH]
        q = part(0)
        f = low + (1.0 - low) * _sigmoid(part(1))
        lf = jnp.log(f)
        k = 1.0 - f
        ib = part(2)
        v = ib * _sigmoid(ib)
        og = part(3)
        hi = lf.astype(BF16)
        r1 = lf - hi.astype(F32)
        mid = r1.astype(BF16)
        lo = (r1 - mid.astype(F32)).astype(BF16)
        e = (jnp.dot(pmat, hi, preferred_element_type=F32) + jnp.dot(pmat, mid, preferred_element_type=F32)
             + jnp.dot(pmat, lo, preferred_element_type=F32))
        b = e[0:c]
        return dict(q=q, k=k, v=v, e=e, b=b, decay=jnp.exp(e[c - 1:c]), qe=(q * jnp.exp(b)).astype(BF16),
                    kt=(k * jnp.exp(e[c:2 * c])).astype(BF16), vb=v.astype(BF16), qk=q * k,
                    gate=og * _sigmoid(og))

    def stage2(s):
        q, k, e, b = s["q"], s["k"], s["e"], s["b"]
        att = [None] * HG_HEADS
        n_rows_p = 2
        for lvl in range(n_levels):
            m = c >> (lvl + 1)
            if m >= SUBLANES:
                ref = [jnp.broadcast_to(b[p * 2 * m + m - 1:p * 2 * m + m], (2 * m, HG_WIDTH))
                       for p in range(c // (2 * m))]
                x = jnp.exp(-jnp.abs(b - (jnp.concatenate(ref, axis=0) if len(ref) > 1 else ref[0])))
            else:
                x = jnp.exp(e[n_rows_p * c:(n_rows_p + 1) * c])
                n_rows_p += 1
            second = (row & m) != 0
            qm = jnp.where(second, q * x, 0.0).astype(BF16)
            km = jnp.where(second, 0.0, k * x).astype(BF16)
            for h in range(HG_HEADS):
                a = lax.dot_general(head(qm, h), head(km, h), NT, preferred_element_type=F32)
                if lvl > 0:
                    a = a * mask_ref[lvl]
                att[h] = a if att[h] is None else att[h] + a
        return [jnp.dot(att[h].astype(BF16), head(s["vb"], h), preferred_element_type=F32)
                + jnp.sum(head(s["qk"], h), axis=-1, keepdims=True) * head(s["v"], h) for h in range(HG_HEADS)]

    def stage3(r0, s, intra):
        for h in range(HG_HEADS):
            st = st_ref[h]
            inter = lax.dot_general(head(s["qe"], h), st.astype(BF16), NT, preferred_element_type=F32)
            st_ref[h] = st * head(s["decay"], h) + lax.dot_general(head(s["vb"], h), head(s["kt"], h), TN,
                                                                  preferred_element_type=F32)
            o = _rms(inter + intra[h], gon_ref[...]) * head(s["gate"], h)
            o_ref[r0:r0 + c, h * HG_DK:(h + 1) * HG_DK] = o.astype(o_ref.dtype)

    starts = range(0, zb_ref.shape[0], c)
    firsts = [stage1(r0) for r0 in starts]
    intras = [stage2(s) for s in firsts]
    for r0, s, intra in zip(starts, firsts, intras):
        stage3(r0, s, intra)

    def write_state():
        for h in range(HG_HEADS):
            sfin_ref[0, h] = st_ref[h].T

    if single_step:
        write_state()
    else:
        pl.when(step == pl.num_programs(1) - 1)(write_state)


def _hgrn(zb, s0, lower, g_on, batch, seq):
    c = min(HG_C, seq)
    rows = min(HG_CHUNKS_PER_STEP * c, seq)
    assert seq % rows == 0 and rows % c == 0
    pmat, masks = _hgrn_consts(c)
    nc = seq // rows
    return pl.pallas_call(
        functools.partial(_hgrn_body, single_step=nc == 1),
        grid=(batch, nc),
        in_specs=[
            pl.BlockSpec((rows, 4 * HG_WIDTH), lambda b, i: (b * nc + i, 0)),
            pl.BlockSpec((1, HG_HEADS, HG_DK, HG_DK), lambda b, i: (b, 0, 0, 0)),
            pl.BlockSpec((1, HG_WIDTH), lambda b, i: (0, 0)),
            pl.BlockSpec((1, HG_DK), lambda b, i: (0, 0)),
            pl.BlockSpec(pmat.shape, lambda b, i: (0, 0)),
            pl.BlockSpec(masks.shape, lambda b, i: (0, 0, 0)),
        ],
        out_specs=[
            pl.BlockSpec((rows, HG_WIDTH), lambda b, i: (b * nc + i, 0)),
            pl.BlockSpec((1, HG_HEADS, HG_DK, HG_DK), lambda b, i: (b, 0, 0, 0)),
        ],
        out_shape=[
            jax.ShapeDtypeStruct((batch * seq, HG_WIDTH), BF16),
            jax.ShapeDtypeStruct((batch, HG_HEADS, HG_DK, HG_DK), F32),
        ],
        scratch_shapes=[pltpu.VMEM((HG_HEADS, HG_DK, HG_DK), F32)],
        compiler_params=_params("arbitrary", "arbitrary"),
        name="hgrn2",
    )(zb, s0, lower, g_on, pmat, masks)


def _split_bf16(x):
    hi = x.astype(BF16)
    return hi, (x - hi.astype(F32)).astype(BF16)


def _merge_body(att_p, hg_p, zg_p, x_p, att_s, hg_s, zg_s, x_s, wpa_ref, wpb_ref, wout_ref, gffn_ref, wr_ref, br_ref,
                x1_ref, h2_ref, route_ref, cnt_ref, *, n_first):
    weights = (wpa_ref, wpb_ref, wout_ref, gffn_ref, wr_ref, br_ref)
    outs = (x1_ref, h2_ref, route_ref, cnt_ref)
    i = pl.program_id(0)
    pl.when(i < n_first)(functools.partial(_merge_block, att_p, hg_p, zg_p, x_p, *weights, *outs))
    pl.when(i >= n_first)(functools.partial(_merge_block, att_s, hg_s, zg_s, x_s, *weights, *outs))


def _merge_block(att_ref, hg_ref, zg_ref, x_ref, wpa_ref, wpb_ref, wout_ref, gffn_ref, wr_ref, br_ref,
                 x1_ref, h2_ref, route_ref, cnt_ref):
    pa = jnp.dot(att_ref[...].astype(BF16), wpa_ref[...], preferred_element_type=F32)
    pb = jnp.dot(hg_ref[...].astype(BF16), wpb_ref[...], preferred_element_type=F32)
    y = _sigmoid(zg_ref[:, :D_MODEL]) * pa + _sigmoid(zg_ref[:, D_MODEL:]) * pb
    x1 = x_ref[...] + jnp.dot(y.astype(BF16), wout_ref[...], preferred_element_type=F32)
    x1_ref[...] = x1
    h2 = _rms(x1, gffn_ref[...])
    h2_ref[...] = h2.astype(BF16)

    h_hi, h_lo = _split_bf16(h2)
    w_hi, w_lo = _split_bf16(wr_ref[...])
    both = jnp.dot(h_hi, jnp.concatenate([w_hi, w_lo], axis=1), preferred_element_type=F32)
    logits = both[:, :LANES] + both[:, LANES:] + jnp.dot(h_lo, w_hi, preferred_element_type=F32) + br_ref[...]
    lane = lax.broadcasted_iota(jnp.int32, logits.shape, 1).astype(F32)
    cur = logits
    vals, idxs = [], []
    for _ in range(TOP_K):
        m = jnp.max(cur, axis=-1, keepdims=True)
        idx = jnp.min(jnp.where(cur == m, lane, float(LANES)), axis=-1, keepdims=True)
        vals.append(m)
        idxs.append(idx)
        cur = jnp.where(lane == idx, -jnp.inf, cur)
    es = [jnp.exp(v - vals[0]) for v in vals]
    inv = 1.0 / functools.reduce(jnp.add, es)
    route = jnp.zeros_like(logits)
    for k, (ex, idx) in enumerate(zip(es, idxs)):
        route = (route + jnp.where(lane == idx, 1.0, 0.0) + jnp.where(lane == float(ROUTE_IDX + k), idx, 0.0)
                 + jnp.where(lane == float(ROUTE_P + k), ex * inv, 0.0))
    route_ref[...] = route
    sel = jnp.where(lane < float(N_EXPERTS), route, 0.0)
    for blk in range(cnt_ref.shape[0]):
        cnt_ref[blk] = jnp.sum(sel[blk * TBK:(blk + 1) * TBK], axis=0, keepdims=True)


def _merge(first, second, wpa, wpb, wout, g_ffn, w_router, b_router):
    tm = MERGE_TM
    assert first[3].shape[0] % tm == 0 and second[3].shape[0] % tm == 0
    n1, n2 = first[3].shape[0] // tm, second[3].shape[0] // tm
    widths = (ATT_WIDTH, HG_WIDTH, 2 * D_MODEL, D_MODEL)
    spec1 = [pl.BlockSpec((tm, w), lambda i: (jnp.minimum(i, n1 - 1), 0)) for w in widths]
    spec2 = [pl.BlockSpec((tm, w), lambda i: (jnp.maximum(i - n1, 0), 0)) for w in widths]
    row = lambda w: pl.BlockSpec((tm, w), lambda i: (i, 0))
    full = lambda a: pl.BlockSpec(a.shape, lambda i: (0,) * a.ndim)
    n = (n1 + n2) * tm
    return pl.pallas_call(
        functools.partial(_merge_body, n_first=n1),
        grid=(n1 + n2,),
        in_specs=spec1 + spec2 + [full(wpa), full(wpb), full(wout), full(g_ffn), full(w_router), full(b_router)],
        out_specs=[row(D_MODEL), row(D_MODEL), row(LANES),
                   pl.BlockSpec((tm // TBK, 1, LANES), lambda i: (i, 0, 0))],
        out_shape=[
            jax.ShapeDtypeStruct((n, D_MODEL), F32),
            jax.ShapeDtypeStruct((n, D_MODEL), BF16),
            jax.ShapeDtypeStruct((n, LANES), F32),
            jax.ShapeDtypeStruct((n // TBK, 1, LANES), F32),
        ],
        compiler_params=_params("arbitrary"),
        name="merge_router",
    )(*first, *second, wpa, wpb, wout, g_ffn, w_router, b_router)


def _route_plan(cnt):
    pc = (cnt + ROW_CH - 1) // ROW_CH
    loff = jnp.cumsum(pc, axis=1) - pc
    tot = jnp.sum(pc, axis=0)
    reg = (tot + CH_PER_TILE - 1) // CH_PER_TILE * CH_PER_TILE
    gstart = jnp.cumsum(reg) - reg
    goff = gstart[None, :] + jnp.cumsum(pc, axis=0) - pc
    ntiles = jnp.sum(reg) // CH_PER_TILE
    present = reg > 0
    ids = jnp.arange(N_EXPERTS, dtype=jnp.int32)
    later = (ids[None, :] > ids[:, None]) & present[None, :]
    nxt = jnp.min(jnp.where(later, ids[None, :], N_EXPERTS), axis=1)
    nxt = jnp.where(nxt == N_EXPERTS, -1, nxt)
    slot = (jnp.cumsum(present.astype(jnp.int32)) - 1) % 2
    first = jnp.min(jnp.where(present, ids, N_EXPERTS)).reshape(1)
    j = jnp.arange(BLK_CH, dtype=jnp.int32)
    run = jnp.sum(((loff + pc)[:, None, :] <= j[None, :, None]).astype(jnp.int32), axis=2)
    shift = jnp.sum(jnp.where(run[:, :, None] == ids[None, None, :], (goff - loff)[:, None, :], 0), axis=2)
    gmap = shift + j[None, :]
    i32 = lambda a: a.astype(jnp.int32)
    return dict(loff=i32(loff), gmap=i32(gmap), totc=i32(jnp.sum(pc, axis=1)),
                pad_start=i32(gstart + tot), pad_cnt=i32(reg - tot), ntiles=i32(ntiles).reshape(1),
                tile_start=i32(gstart // CH_PER_TILE), tile_count=i32(reg // CH_PER_TILE),
                next_expert=i32(nxt), weight_slot=i32(slot), first_expert=i32(first))


def _chunk_rows(ref, chunk):
    return ref.at[pl.ds(pl.multiple_of(chunk * ROW_CH, ROW_CH), ROW_CH)]


def _for_chunks(n, do):
    log_unroll = 3
    groups = lax.shift_right_logical(n, log_unroll)

    def group(i, carry):
        for u in range(1 << log_unroll):
            do(lax.shift_left(i, log_unroll) + u)
        return carry
    lax.fori_loop(0, groups, group, 0)

    def single(c, carry):
        do(c)
        return carry
    lax.fori_loop(lax.shift_left(groups, log_unroll), n, single, 0)


def _wait_chunks(n, copy_of_rows):
    for bit in range((RB // ROW_CH).bit_length()):
        @pl.when(((n >> bit) & 1) == 1)
        def _(bit=bit):
            copy_of_rows((1 << bit) * ROW_CH).wait()


def _dispatch_body(gmap_s, totc_s, pads_s, padn_s, nt_s, h2_ref, route_ref, tri_ref, loffv_ref,
                   xs_hbm, buf, zbuf, sem):
    b = pl.program_id(0)
    nb = pl.num_programs(0)
    slot = lax.rem(b, 2)
    fill_sem, tile_sem = 2, 3
    nt_max = xs_hbm.shape[0] // EXP_TM

    def copy_out(src, gchunk, sem_i):
        return pltpu.make_async_copy(src, _chunk_rows(xs_hbm, gchunk), sem.at[sem_i])

    def zero_tile(t):
        return pltpu.make_async_copy(zbuf, xs_hbm.at[pl.ds(pl.multiple_of(t * EXP_TM, EXP_TM), EXP_TM)],
                                     sem.at[tile_sem])

    def wait_block(step, slot_):
        for u in range(COMB_BLKS):
            _wait_chunks(totc_s[step * COMB_BLKS + u], lambda rows: pltpu.make_async_copy(
                buf.at[slot_, pl.ds(0, rows)], xs_hbm.at[pl.ds(0, rows)], sem.at[slot_]))

    @pl.when(b == 0)
    def _():
        zbuf[...] = jnp.zeros_like(zbuf)

    @pl.when(b >= 2)
    def _():
        wait_block(b - 2, slot)

    erow = lax.broadcasted_iota(jnp.int32, (N_EXPERTS, TBK), 0).astype(F32)

    def positions(u):
        rt = route_ref[u * TBK:(u + 1) * TBK, :].T
        rank_t = jnp.dot(rt[0:N_EXPERTS].astype(BF16), tri_ref[...], preferred_element_type=F32)
        lpos_t = loffv_ref[u] * float(ROW_CH) + rank_t
        return [jnp.sum(jnp.where(rt[ROUTE_IDX + k:ROUTE_IDX + k + 1] == erow, lpos_t, 0.0), axis=0, keepdims=True)
                for k in range(TOP_K)]

    def permute(u, lposk):
        h2 = h2_ref[u * TBK:(u + 1) * TBK, :]
        for r0 in range(0, RB, PERM_TM):
            rrow = lax.broadcasted_iota(jnp.int32, (PERM_TM, TBK), 0).astype(F32) + float(r0)
            perm = functools.reduce(jnp.add, [jnp.where(lposk[k] == rrow, 1.0, 0.0) for k in range(TOP_K)])
            buf[slot, u * RB + r0:u * RB + r0 + PERM_TM, :] = jnp.dot(perm.astype(BF16), h2,
                                                                     preferred_element_type=F32)

    pos = [positions(u) for u in range(COMB_BLKS)]
    for u in range(COMB_BLKS):
        permute(u, pos[u])
    for u in range(COMB_BLKS):
        bb = b * COMB_BLKS + u
        _for_chunks(totc_s[bb], lambda c: copy_out(_chunk_rows(buf.at[slot], u * BLK_CH + c),
                                                   gmap_s[bb * BLK_CH + c], slot).start())

    @pl.when(b == nb - 1)
    def _():
        zchunk = _chunk_rows(zbuf, 0)

        def fill_expert(e, carry):
            def one(c, carry2):
                copy_out(zchunk, pads_s[e] + c, fill_sem).start()
                return carry2
            lax.fori_loop(0, padn_s[e], one, 0)
            return carry
        lax.fori_loop(0, N_EXPERTS, fill_expert, 0)

        def fill_tile(t, carry):
            zero_tile(t).start()
            return carry
        lax.fori_loop(nt_s[0], nt_max, fill_tile, 0)

        @pl.when(b >= 1)
        def _():
            wait_block(b - 1, 1 - slot)
        wait_block(b, slot)

        def drain_expert(e, carry):
            def one(c, carry2):
                copy_out(zchunk, 0, fill_sem).wait()
                return carry2
            lax.fori_loop(0, padn_s[e], one, 0)
            return carry
        lax.fori_loop(0, N_EXPERTS, drain_expert, 0)

        def drain_tile(t, carry):
            zero_tile(0).wait()
            return carry
        lax.fori_loop(nt_s[0], nt_max, drain_tile, 0)


def _dispatch(plan, h2, route, loffv, n_rows):
    tm = COMB_BLKS * TBK
    assert h2.shape[0] % tm == 0
    tri = jnp.asarray(np.triu(np.ones((TBK, TBK), np.float32), 1), BF16)
    grid_spec = pltpu.PrefetchScalarGridSpec(
        num_scalar_prefetch=5,
        grid=(h2.shape[0] // tm,),
        in_specs=[
            pl.BlockSpec((tm, D_MODEL), lambda b, *_: (b, 0)),
            pl.BlockSpec((tm, LANES), lambda b, *_: (b, 0)),
            pl.BlockSpec((TBK, TBK), lambda b, *_: (0, 0)),
            pl.BlockSpec((COMB_BLKS, N_EXPERTS, 1), lambda b, *_: (b, 0, 0)),
        ],
        out_specs=pl.BlockSpec(memory_space=pl.ANY),
        scratch_shapes=[pltpu.VMEM((2, COMB_BLKS * RB, D_MODEL), F32), pltpu.VMEM((EXP_TM, D_MODEL), F32),
                        pltpu.SemaphoreType.DMA((4,))],
    )
    return pl.pallas_call(
        _dispatch_body,
        grid_spec=grid_spec,
        out_shape=jax.ShapeDtypeStruct((n_rows, D_MODEL), F32),
        compiler_params=_params("arbitrary"),
        name="moe_dispatch",
    )(plan["gmap"].reshape(-1), plan["totc"], plan["pad_start"], plan["pad_cnt"], plan["ntiles"],
      h2, route, tri, loffv)


def _experts_body(t0_s, n_s, nxt_s, par_s, first_s, nt_s, xs_hbm, wgu_hbm, bgu_ref, wd_hbm, bd_ref, ys_hbm,
                  wg_f, wd_f, wg_b, wd_b, xbuf, ybuf, wsem, xsem, ysem):
    e = pl.program_id(0)
    nt = nt_s[0]
    nt_max = xs_hbm.shape[0] // EXP_TM
    tile_rows = lambda t: pl.ds(pl.multiple_of(t * EXP_TM, EXP_TM), EXP_TM)

    def weight_copies(ex, s, piece):
        rows = pl.ds(pl.multiple_of(piece * W_PIECE_ROWS, W_PIECE_ROWS), W_PIECE_ROWS)
        return (pltpu.make_async_copy(wgu_hbm.at[ex, rows], wg_f.at[s, rows], wsem.at[s, 0]),
                pltpu.make_async_copy(wd_hbm.at[ex, rows], wd_f.at[s, rows], wsem.at[s, 1]))

    def x_copy(t, s):
        return pltpu.make_async_copy(xs_hbm.at[tile_rows(t)], xbuf.at[s], xsem.at[s])

    def y_copy(t, s):
        return pltpu.make_async_copy(ybuf.at[s], ys_hbm.at[tile_rows(t)], ysem.at[s])

    @pl.when(e == 0)
    def _():
        for piece in range(W_PIECES):
            for cp in weight_copies(first_s[0], 0, piece):
                cp.start()
        x_copy(0, 0).start()

    @pl.when(n_s[e] > 0)
    def _():
        s_w = par_s[e]
        for piece in range(W_PIECES):
            for cp in weight_copies(e, s_w, piece):
                cp.wait()
        wg_b[...] = wg_f[s_w].astype(BF16)
        wd_b[...] = wd_f[s_w].astype(BF16)
        has_next = nxt_s[e] >= 0

        def request(piece):
            for cp in weight_copies(nxt_s[e], 1 - s_w, piece):
                cp.start()

        def tile(i, carry):
            t = t0_s[e] + i
            s = t & 1
            x_copy(t, s).wait()

            @pl.when(has_next & (i < W_PIECES))
            def _():
                request(i)

            @pl.when(t + 1 < nt)
            def _():
                x_copy(t + 1, 1 - s).start()

            @pl.when(t >= 2)
            def _():
                y_copy(t - 2, s).wait()

            gu = jnp.dot(xbuf[s].astype(BF16), wg_b[...], preferred_element_type=F32) + bgu_ref[...]
            gate = jnp.minimum(gu[:, :D_FF], SWIGLU_LIMIT)
            up = jnp.clip(gu[:, D_FF:], -SWIGLU_LIMIT, SWIGLU_LIMIT)
            act = (up + 1.0) * gate * _sigmoid(SWIGLU_ALPHA * gate)
            ybuf[s] = jnp.dot(act.astype(BF16), wd_b[...], preferred_element_type=F32) + bd_ref[...]
            y_copy(t, s).start()
            return carry
        lax.fori_loop(0, n_s[e], tile, 0)

        @pl.when(has_next)
        def _():
            def rest(piece, carry):
                request(piece)
                return carry
            lax.fori_loop(jnp.minimum(n_s[e], W_PIECES), W_PIECES, rest, 0)

    @pl.when(e == pl.num_programs(0) - 1)
    def _():
        @pl.when(nt >= 2)
        def _():
            y_copy(nt - 2, nt & 1).wait()
        y_copy(nt - 1, (nt - 1) & 1).wait()
        ybuf[0] = jnp.zeros((EXP_TM, D_MODEL), F32)

        def fill(t, carry):
            y_copy(t, 0).start()
            return carry
        lax.fori_loop(nt, nt_max, fill, 0)

        def drain(t, carry):
            y_copy(0, 0).wait()
            return carry
        lax.fori_loop(nt, nt_max, drain, 0)


def _experts(plan, xs, w_gu, b_gu, w_down, b_down):
    n_rows = xs.shape[0]
    of_expert = lambda e, *_: (e, 0, 0)
    grid_spec = pltpu.PrefetchScalarGridSpec(
        num_scalar_prefetch=6,
        grid=(N_EXPERTS,),
        in_specs=[
            pl.BlockSpec(memory_space=pl.ANY),
            pl.BlockSpec(memory_space=pl.ANY),
            pl.BlockSpec((None, 1, 2 * D_FF), of_expert),
            pl.BlockSpec(memory_space=pl.ANY),
            pl.BlockSpec((None, 1, D_MODEL), of_expert),
        ],
        out_specs=pl.BlockSpec(memory_space=pl.ANY),
        scratch_shapes=[pltpu.VMEM((2, D_MODEL, 2 * D_FF), F32), pltpu.VMEM((2, D_FF, D_MODEL), F32),
                        pltpu.VMEM((D_MODEL, 2 * D_FF), BF16), pltpu.VMEM((D_FF, D_MODEL), BF16),
                        pltpu.VMEM((2, EXP_TM, D_MODEL), F32), pltpu.VMEM((2, EXP_TM, D_MODEL), F32),
                        pltpu.SemaphoreType.DMA((2, 2)), pltpu.SemaphoreType.DMA((2,)),
                        pltpu.SemaphoreType.DMA((2,))],
    )
    return pl.pallas_call(
        _experts_body,
        grid_spec=grid_spec,
        out_shape=jax.ShapeDtypeStruct((n_rows, D_MODEL), F32),
        compiler_params=_params("arbitrary"),
        name="moe_experts",
    )(plan["tile_start"], plan["tile_count"], plan["next_expert"], plan["weight_slot"], plan["first_expert"],
      plan["ntiles"], xs, w_gu, b_gu, w_down, b_down)


def _combine_body(gmap_s, totc_s, route_ref, tril_ref, loffrow_ref, x1_ref, gfin_ref, ys_hbm, y_ref, buf, sem):
    i = pl.program_id(0)
    n_steps = pl.num_programs(0)
    slot = lax.rem(i, 2)

    def copy_in(slot_, lchunk, gchunk):
        return pltpu.make_async_copy(_chunk_rows(ys_hbm, gchunk), _chunk_rows(buf.at[slot_], lchunk), sem.at[slot_])

    def fetch_step(step, slot_):
        for u in range(COMB_BLKS):
            bb = step * COMB_BLKS + u
            _for_chunks(totc_s[bb], lambda c: copy_in(slot_, u * BLK_CH + c, gmap_s[bb * BLK_CH + c]).start())

    def wait_step(step, slot_):
        for u in range(COMB_BLKS):
            _wait_chunks(totc_s[step * COMB_BLKS + u], lambda rows: pltpu.make_async_copy(
                ys_hbm.at[pl.ds(0, rows)], buf.at[slot_, pl.ds(0, rows)], sem.at[slot_]))

    @pl.when(i == 0)
    def _():
        buf[...] = jnp.zeros_like(buf)
        fetch_step(0, 0)

    @pl.when(i + 1 < n_steps)
    def _():
        fetch_step(i + 1, 1 - slot)

    wait_step(i, slot)

    lane = lax.broadcasted_iota(jnp.int32, (1, LANES), 1).astype(F32)
    col = lax.broadcasted_iota(jnp.int32, (TBK, PERM_TM), 1).astype(F32).astype(BF16)
    zero = jnp.zeros((), BF16)

    def positions(u):
        route = route_ref[u * TBK:(u + 1) * TBK, :]
        sel = jnp.where(lane < float(N_EXPERTS), route, 0.0).astype(BF16)
        rank = jnp.dot(tril_ref[...], sel, preferred_element_type=F32)
        lpos = loffrow_ref[u] * float(ROW_CH) + rank
        lposk, pkb = [], []
        for k in range(TOP_K):
            idx = route[:, ROUTE_IDX + k:ROUTE_IDX + k + 1]
            lposk.append(jnp.sum(jnp.where(lane == idx, lpos, 0.0), axis=-1, keepdims=True))
            pkb.append(route[:, ROUTE_P + k:ROUTE_P + k + 1].astype(BF16))
        return lposk, pkb

    def gather(u, lposk, pkb):
        acc = x1_ref[u * TBK:(u + 1) * TBK, :]
        for r0 in range(0, RB, PERM_TM):
            rel = [(lposk[k] - float(r0)).astype(BF16) for k in range(TOP_K)]
            w = functools.reduce(jnp.add, [jnp.where(rel[k] == col, pkb[k], zero) for k in range(TOP_K)])
            rows = buf[slot, u * RB + r0:u * RB + r0 + PERM_TM, :].astype(BF16)
            acc = acc + jnp.dot(w, rows, preferred_element_type=F32)
        y_ref[u * TBK:(u + 1) * TBK, :] = _rms(acc, gfin_ref[...])

    pos = [positions(u) for u in range(COMB_BLKS)]
    for u in range(COMB_BLKS):
        gather(u, *pos[u])


def _combine(plan, blocks, route, loffrow, x1, g_final, ys):
    assert RB % PERM_TM == 0 and PERM_TM <= 256
    b0, b1 = blocks
    assert b0 % COMB_BLKS == 0 and b1 % COMB_BLKS == 0
    n_steps, s0 = (b1 - b0) // COMB_BLKS, b0 // COMB_BLKS
    tm = COMB_BLKS * TBK
    tril = jnp.asarray(np.tril(np.ones((TBK, TBK), np.float32), -1), BF16)
    grid_spec = pltpu.PrefetchScalarGridSpec(
        num_scalar_prefetch=2,
        grid=(n_steps,),
        in_specs=[
            pl.BlockSpec((tm, LANES), lambda i, *_: (i + s0, 0)),
            pl.BlockSpec((TBK, TBK), lambda i, *_: (0, 0)),
            pl.BlockSpec((COMB_BLKS, 1, LANES), lambda i, *_: (i + s0, 0, 0)),
            pl.BlockSpec((tm, D_MODEL), lambda i, *_: (i + s0, 0)),
            pl.BlockSpec((1, D_MODEL), lambda i, *_: (0, 0)),
            pl.BlockSpec(memory_space=pl.ANY),
        ],
        out_specs=pl.BlockSpec((tm, D_MODEL), lambda i, *_: (i, 0)),
        scratch_shapes=[pltpu.VMEM((2, COMB_BLKS * RB, D_MODEL), F32), pltpu.SemaphoreType.DMA((2,))],
    )
    return pl.pallas_call(
        _combine_body,
        grid_spec=grid_spec,
        out_shape=jax.ShapeDtypeStruct(((b1 - b0) * TBK, D_MODEL), F32),
        compiler_params=_params("arbitrary"),
        name="moe_combine",
    )(plan["gmap"][b0:b1].reshape(-1), plan["totc"][b0:b1], route, tril, loffrow, x1, g_final, ys)


def kernel(x_prompt, x_sample, cache_k, cache_v, state_s, g_mix, w_in, rel_bias, lb_logits, g_out_norm,
           w_pa, w_pb, w_out, g_ffn, w_router, b_router, w_gu, b_gu, w_down, b_down, g_final):
    B, T = x_prompt.shape[:2]
    DB, S = x_sample.shape[:2]
    depth = w_in.shape[0]
    assert depth == 1 and T % ATT_QBLK == 0 and S == CHUNK
    cw = cache_k.shape[2]
    assert cw == WINDOW
    l = 0

    lower = jnp.cumsum(jax.nn.softmax(lb_logits.astype(F32), axis=0), axis=0)[l].reshape(1, HG_WIDTH)
    w_in_b = w_in[l].astype(BF16)
    wpa, wpb, wout = w_pa[l].astype(BF16), w_pb[l].astype(BF16), w_out[l].astype(BF16)
    row = lambda a: a.reshape(1, -1).astype(F32)
    base = _rel_bias_base(rel_bias[l])
    b_gu3 = b_gu[l].reshape(N_EXPERTS, 1, 2 * D_FF)
    b_down3 = b_down[l].reshape(N_EXPERTS, 1, D_MODEL)
    pad_e = LANES - N_EXPERTS
    wr = jnp.pad(w_router[l].astype(F32), ((0, 0), (0, pad_e)))
    br = jnp.concatenate([b_router[l].astype(F32), jnp.full((pad_e,), NEG, F32)]).reshape(1, LANES)

    n_tok = B * T + DB * S
    nb, nbp = n_tok // TBK, (B * T) // TBK

    def front(x, batch, seq, s0, attend):
        xf = x.reshape(batch * seq, D_MODEL)
        za, zb, zg = _inproj(xf, row(g_mix[l]), w_in_b)
        att = attend(za)
        hg, s_fin = _hgrn(zb, s0, lower, row(g_out_norm[l]), batch, seq)
        za3 = za.reshape(batch, seq, 3 * ATT_WIDTH)
        heads = lambda a: a.reshape(1, batch, a.shape[1], ATT_HEADS, ATT_DIM)
        keep = min(WINDOW, seq)
        nk = heads(za3[:, seq - keep:, ATT_WIDTH:2 * ATT_WIDTH])
        nv = heads(za3[:, seq - keep:, 2 * ATT_WIDTH:])
        return dict(mix=(att, hg, zg, xf), nk=nk, nv=nv, s=s_fin[None])

    ck = cache_k[l].reshape(DB, cw, ATT_WIDTH)
    cv = cache_v[l].reshape(DB, cw, ATT_WIDTH)
    fp = front(x_prompt, B, T, jnp.zeros((B, HG_HEADS, HG_DK, HG_DK), F32), lambda za: _attn_prompt(za, base, B, T))
    fs = front(x_sample, DB, S, state_s[l].astype(F32), lambda za: _attn_sample(za, ck, cv, base, DB, S))

    x1, h2, route, cnt = _merge(fp["mix"], fs["mix"], wpa, wpb, wout, row(g_ffn[l]), wr, br)
    cnt = cnt[:, 0, :N_EXPERTS].astype(jnp.int32)
    max_rows = n_tok * TOP_K + nb * N_EXPERTS * (ROW_CH - 1) + N_EXPERTS * (EXP_TM - 1)
    nt_max = -(-max_rows // EXP_TM)
    plan = _route_plan(cnt)
    loff_f = plan["loff"].astype(F32)
    xs = _dispatch(plan, h2, route, loff_f[:, :, None], nt_max * EXP_TM)
    ysort = _experts(plan, xs, w_gu[l], b_gu3, w_down[l], b_down3)
    loffrow = jnp.pad(loff_f, ((0, 0), (0, pad_e)))[:, None, :]
    yp = _combine(plan, (0, nbp), route, loffrow, x1, row(g_final), ysort)
    ys = _combine(plan, (nbp, nb), route, loffrow, x1, row(g_final), ysort)
    return (yp.reshape(B, T, D_MODEL), ys.reshape(DB, S, D_MODEL), fp["nk"], fp["nv"], fp["s"],
            fs["nk"], fs["nv"], fs["s"])
```

```python
import functools

import numpy as np
import jax
import jax.numpy as jnp
from jax import lax
from jax.experimental import pallas as pl
from jax.experimental.pallas import tpu as pltpu

F32 = jnp.float32
BF16 = jnp.bfloat16

D_MODEL = 1024
CHUNK = 64
LEFT_CHUNKS = 8
WINDOW = LEFT_CHUNKS * CHUNK
ATT_HEADS = 8
ATT_DIM = 64
ATT_WIDTH = ATT_HEADS * ATT_DIM
MAX_REL = 256
HG_HEADS = 4
HG_DK = 128
HG_WIDTH = HG_HEADS * HG_DK
N_EXPERTS = 32
TOP_K = 4
D_FF = D_MODEL
SWIGLU_LIMIT = 7.0
SWIGLU_ALPHA = 1.702
RMS_EPS = 1e-5

LANES = 128
NEG = -1e30
LOG2E = 1.4426950408889634
ATT_QBLK = 4 * CHUNK
ATT_KBLKS = LEFT_CHUNKS * CHUNK // ATT_QBLK + 1
ATT_PAIRS_PER_STAGE = 2
HG_C = 128
HG_CHUNKS_PER_STEP = 8
VMEM_LIMIT = 56 * 1024 * 1024
BIAS_W = 1024
SUBLANES = 8
TBK = 256
ROW_CH = SUBLANES
RB = TBK * TOP_K + N_EXPERTS * ROW_CH
MERGE_TM = 2 * TBK
COMB_BLKS = 2
DISP_BLKS = 4
PERM_TM = 256
EXP_TM = 512
W_PIECES = 4
W_PIECE_ROWS = D_MODEL // W_PIECES
CH_PER_TILE = EXP_TM // ROW_CH
BLK_CH = RB // ROW_CH
ROUTE_IDX = 64
ROUTE_P = 72

NT = (((1,), (1,)), ((), ()))
TN = (((0,), (0,)), ((), ()))


def _rms(x, g):
    return x * lax.rsqrt(jnp.mean(x * x, axis=-1, keepdims=True) + RMS_EPS) * g


def _sigmoid(x):
    return 1.0 / (1.0 + jnp.exp(-x))


def _params(*sem):
    return pltpu.CompilerParams(dimension_semantics=sem, vmem_limit_bytes=VMEM_LIMIT)


def _inproj_body(x_ref, g_ref, w_ref, za_ref, zb_ref, zg_ref):
    h = _rms(x_ref[...], g_ref[...]).astype(BF16)
    a, b = 3 * ATT_WIDTH, 3 * ATT_WIDTH + 4 * HG_WIDTH
    za_ref[...] = jnp.dot(h, w_ref[:, :a], preferred_element_type=F32)
    zb_ref[...] = jnp.dot(h, w_ref[:, a:b], preferred_element_type=F32)
    zg_ref[...] = jnp.dot(h, w_ref[:, b:], preferred_element_type=F32)


def _inproj(x, g, w_bf16, tm=512):
    n = x.shape[0]
    assert n % tm == 0
    cols = w_bf16.shape[1]
    wa, wb, wg = 3 * ATT_WIDTH, 4 * HG_WIDTH, 2 * D_MODEL
    return pl.pallas_call(
        _inproj_body,
        grid=(n // tm,),
        in_specs=[
            pl.BlockSpec((tm, D_MODEL), lambda i: (i, 0)),
            pl.BlockSpec((1, D_MODEL), lambda i: (0, 0)),
            pl.BlockSpec((D_MODEL, cols), lambda i: (0, 0)),
        ],
        out_specs=[
            pl.BlockSpec((tm, wa), lambda i: (i, 0)),
            pl.BlockSpec((tm, wb), lambda i: (i, 0)),
            pl.BlockSpec((tm, wg), lambda i: (i, 0)),
        ],
        out_shape=[
            jax.ShapeDtypeStruct((n, wa), F32),
            jax.ShapeDtypeStruct((n, wb), F32),
            jax.ShapeDtypeStruct((n, wg), F32),
        ],
        compiler_params=_params("arbitrary"),
        name="inproj",
    )(x, g, w_bf16)


def _attn_heads(q_ref, k_refs, v_refs, bias_fn, pens, o_ref):
    lane = lax.broadcasted_iota(jnp.int32, (1, LANES), 1)
    first = lane < ATT_DIM
    halves = (first, lane >= ATT_DIM)
    for hp0 in range(0, ATT_HEADS // 2, ATT_PAIRS_PER_STAGE):
        pairs = range(hp0, hp0 + ATT_PAIRS_PER_STAGE)
        sl = {hp: slice(hp * LANES, (hp + 1) * LANES) for hp in pairs}
        scores = {}
        for hp in pairs:
            q2 = q_ref[:, sl[hp]] * (ATT_DIM ** -0.5 * LOG2E)
            ks = [k[:, sl[hp]].astype(BF16) for k in k_refs]
            for half, mine in enumerate(halves):
                qm = jnp.where(mine, q2, 0.0).astype(BF16)
                ss = []
                for j, kj in enumerate(ks):
                    s = lax.dot_general(qm, kj, NT, preferred_element_type=F32) + bias_fn(2 * hp + half, j)
                    if pens[j] is not None:
                        s = s + pens[j]
                    ss.append(s)
                scores[hp, half] = ss
        for hp in pairs:
            outs = []
            for half, mine in enumerate(halves):
                ss = scores[hp, half]
                vs = [jnp.where(mine, v[:, sl[hp]], 1.0).astype(BF16) for v in v_refs]
                if all(s.shape == ss[0].shape for s in ss):
                    m = jnp.max(functools.reduce(jnp.maximum, ss), axis=-1, keepdims=True)
                else:
                    m = functools.reduce(jnp.maximum, [jnp.max(s, axis=-1, keepdims=True) for s in ss])
                outs.append(functools.reduce(jnp.add, [
                    jnp.dot(jnp.exp2(s - m).astype(BF16), vj, preferred_element_type=F32) for s, vj in zip(ss, vs)]))
            num = jnp.where(first, outs[0], outs[1])
            den = pltpu.roll(jnp.where(first, outs[1], outs[0]), ATT_DIM, 1)
            o_ref[:, sl[hp]] = (num * (1.0 / den)).astype(o_ref.dtype)


def _fill_bias(base_ref, bias_ref, banded):
    nq, nk = bias_ref.shape[1:]
    if banded:
        r = lax.broadcasted_iota(jnp.int32, (nq, nk), 0)
        s = lax.broadcasted_iota(jnp.int32, (nq, nk), 1)
        qc = (r + WINDOW) // CHUNK
        kc = s // CHUNK
        pen = jnp.where(kc <= qc, jnp.where(kc >= qc - LEFT_CHUNKS, 0.0, NEG), NEG)
    for h in range(ATT_HEADS):
        rows = jnp.broadcast_to(base_ref[h:h + 1, :], (nq, BIAS_W))
        t = pltpu.roll(rows, 0, 1, stride=1, stride_axis=0)[:, :nk] * LOG2E
        bias_ref[h] = t + pen if banded else t


def _attn_prompt_body(q_ref, k0, k1, k2, v0, v1, v2, base_ref, o_ref, bias_ref):
    i = pl.program_id(1)

    @pl.when((pl.program_id(0) == 0) & (i == 0))
    def _():
        _fill_bias(base_ref, bias_ref, True)

    bias_fn = lambda h, j: bias_ref[h, :, j * ATT_QBLK:(j + 1) * ATT_QBLK]
    back = ATT_KBLKS - 1

    @pl.when(i < back)
    def _():
        pens = [jnp.where(i - back + j >= 0, 0.0, NEG) for j in range(back)] + [None]
        _attn_heads(q_ref, [k0, k1, k2], [v0, v1, v2], bias_fn, pens, o_ref)

    @pl.when(i >= back)
    def _():
        _attn_heads(q_ref, [k0, k1, k2], [v0, v1, v2], bias_fn, [None] * ATT_KBLKS, o_ref)


def _attn_prompt(za, base, batch, seq):
    assert seq % ATT_QBLK == 0 and BIAS_W - ATT_KBLKS * ATT_QBLK >= ATT_QBLK - 1 and WINDOW >= MAX_REL
    nq = seq // ATT_QBLK
    back = ATT_KBLKS - 1
    qspec = pl.BlockSpec((ATT_QBLK, ATT_WIDTH), lambda b, i: (b * nq + i, 0))

    def kvspec(j, col):
        return pl.BlockSpec((ATT_QBLK, ATT_WIDTH),
                            lambda b, i: (b * nq + jnp.maximum(i - back + j, 0), col))

    return pl.pallas_call(
        _attn_prompt_body,
        grid=(batch, nq),
        in_specs=[qspec] + [kvspec(j, 1) for j in range(ATT_KBLKS)] + [kvspec(j, 2) for j in range(ATT_KBLKS)]
        + [pl.BlockSpec(base.shape, lambda b, i: (0, 0))],
        out_specs=pl.BlockSpec((ATT_QBLK, ATT_WIDTH), lambda b, i: (b * nq + i, 0)),
        out_shape=jax.ShapeDtypeStruct((batch * seq, ATT_WIDTH), BF16),
        scratch_shapes=[pltpu.VMEM((ATT_HEADS, ATT_QBLK, ATT_KBLKS * ATT_QBLK), F32)],
        compiler_params=_params("arbitrary", "arbitrary"),
        name="attn_prompt",
    )(za, za, za, za, za, za, za, base)


def _attn_sample_body(q_ref, kn_ref, vn_ref, ck_ref, cv_ref, base_ref, o_ref, bias_ref):
    @pl.when(pl.program_id(0) == 0)
    def _():
        _fill_bias(base_ref, bias_ref, False)

    cw = ck_ref.shape[0]
    bias_fn = lambda h, j: bias_ref[h, :, :cw] if j == 0 else bias_ref[h, :, cw:]
    _attn_heads(q_ref, [ck_ref, kn_ref], [cv_ref, vn_ref], bias_fn, [None, None], o_ref)


def _attn_sample(za, ck, cv, base, batch, seq):
    cw = ck.shape[1]
    return pl.pallas_call(
        _attn_sample_body,
        grid=(batch,),
        in_specs=[
            pl.BlockSpec((seq, ATT_WIDTH), lambda b: (b, 0)),
            pl.BlockSpec((seq, ATT_WIDTH), lambda b: (b, 1)),
            pl.BlockSpec((seq, ATT_WIDTH), lambda b: (b, 2)),
            pl.BlockSpec((None, cw, ATT_WIDTH), lambda b: (b, 0, 0)),
            pl.BlockSpec((None, cw, ATT_WIDTH), lambda b: (b, 0, 0)),
            pl.BlockSpec(base.shape, lambda b: (0, 0)),
        ],
        out_specs=pl.BlockSpec((seq, ATT_WIDTH), lambda b: (b, 0)),
        out_shape=jax.ShapeDtypeStruct((batch * seq, ATT_WIDTH), BF16),
        scratch_shapes=[pltpu.VMEM((ATT_HEADS, seq, cw + seq), F32)],
        compiler_params=_params("arbitrary"),
        name="attn_sample",
    )(za, za, za, ck, cv, base)


def _rel_bias_base(table):
    top = table[:, 2 * MAX_REL:].astype(F32)
    rev = table[:, ::-1][:, :2 * MAX_REL].astype(F32)
    left = WINDOW - MAX_REL
    return jnp.concatenate([jnp.broadcast_to(top, (ATT_HEADS, left)), rev,
                            jnp.broadcast_to(top, (ATT_HEADS, BIAS_W - left - 2 * MAX_REL))], axis=1)


def _hgrn_consts(c):
    t = np.arange(c)[:, None]
    j = np.arange(c)[None, :]
    mats = [j <= t, j > t]
    masks = []
    m = c // 2
    while m >= 1:
        ref = (t // (2 * m)) * (2 * m) + m - 1
        second = (t % (2 * m)) >= m
        if m < SUBLANES:
            mats.append((second & (j > ref) & (j <= t)) | (~second & (j > t) & (j <= ref)))
        masks.append((t // (2 * m)) == (j // (2 * m)))
        m //= 2
    return (jnp.asarray(np.concatenate(mats, 0).astype(np.float32), BF16),
            jnp.asarray(np.stack(masks).astype(np.float32)))


def _hgrn_body(zb_ref, s0_ref, lower_ref, gon_ref, p_ref, mask_ref, o_ref, sfin_ref, st_ref, *, single_step):
    c = p_ref.shape[1]
    step = pl.program_id(1)

    def load_state():
        for h in range(HG_HEADS):
            st_ref[h] = s0_ref[0, h].T

    if single_step:
        load_state()
    else:
        pl.when(step == 0)(load_state)

    pmat = p_ref[...]
    n_levels = mask_ref.shape[0]
    head = lambda a, h: a[:, h * HG_DK:(h + 1) * HG_DK]
    low = lower_ref[...]
    row = lax.broadcasted_iota(jnp.int32, (c, HG_WIDTH), 0)

    def stage1(r0):
        part = lambda i: zb_ref[r0:r0 + c, i * HG_WIDTH:(i + 1) * HG_WIDTH]
        q = part(0)
        f = low + (1.0 - low) * _sigmoid(part(1))
        lf = jnp.log(f)
        k = 1.0 - f
        ib = part(2)
        v = ib * _sigmoid(ib)
        og = part(3)
        hi = lf.astype(BF16)
        r1 = lf - hi.astype(F32)
        mid = r1.astype(BF16)
        lo = (r1 - mid.astype(F32)).astype(BF16)
        e = (jnp.dot(pmat, hi, preferred_element_type=F32) + jnp.dot(pmat, mid, preferred_element_type=F32)
             + jnp.dot(pmat, lo, preferred_element_type=F32))
        b = e[0:c]
        return dict(q=q, k=k, v=v, e=e, b=b, decay=jnp.exp(e[c - 1:c]), qe=(q * jnp.exp(b)).astype(BF16),
                    kt=(k * jnp.exp(e[c:2 * c])).astype(BF16), vb=v.astype(BF16), qk=q * k,
                    gate=og * _sigmoid(og))

    def stage2(s):
        q, k, e, b = s["q"], s["k"], s["e"], s["b"]
        att = [None] * HG_HEADS
        n_rows_p = 2
        for lvl in range(n_levels):
            m = c >> (lvl + 1)
            if m >= SUBLANES:
                ref = [jnp.broadcast_to(b[p * 2 * m + m - 1:p * 2 * m + m], (2 * m, HG_WIDTH))
                       for p in range(c // (2 * m))]
                x = jnp.exp(-jnp.abs(b - (jnp.concatenate(ref, axis=0) if len(ref) > 1 else ref[0])))
            else:
                x = jnp.exp(e[n_rows_p * c:(n_rows_p + 1) * c])
                n_rows_p += 1
            second = (row & m) != 0
            qm = jnp.where(second, q * x, 0.0).astype(BF16)
            km = jnp.where(second, 0.0, k * x).astype(BF16)
            for h in range(HG_HEADS):
                a = lax.dot_general(head(qm, h), head(km, h), NT, preferred_element_type=F32)
                if lvl > 0:
                    a = a * mask_ref[lvl]
                att[h] = a if att[h] is None else att[h] + a
        return [jnp.dot(att[h].astype(BF16), head(s["vb"], h), preferred_element_type=F32)
                + jnp.sum(head(s["qk"], h), axis=-1, keepdims=True) * head(s["v"], h) for h in range(HG_HEADS)]

    def stage3(r0, s, intra):
        for h in range(HG_HEADS):
            st = st_ref[h]
            inter = lax.dot_general(head(s["qe"], h), st.astype(BF16), NT, preferred_element_type=F32)
            st_ref[h] = st * head(s["decay"], h) + lax.dot_general(head(s["vb"], h), head(s["kt"], h), TN,
                                                                  preferred_element_type=F32)
            o = _rms(inter + intra[h], gon_ref[...]) * head(s["gate"], h)
            o_ref[r0:r0 + c, h * HG_DK:(h + 1) * HG_DK] = o.astype(o_ref.dtype)

    starts = range(0, zb_ref.shape[0], c)
    firsts = [stage1(r0) for r0 in starts]
    intras = [stage2(s) for s in firsts]
    for r0, s, intra in zip(starts, firsts, intras):
        stage3(r0, s, intra)

    def write_state():
        for h in range(HG_HEADS):
            sfin_ref[0, h] = st_ref[h].T

    if single_step:
        write_state()
    else:
        pl.when(step == pl.num_programs(1) - 1)(write_state)


def _hgrn(zb, s0, lower, g_on, batch, seq):
    c = min(HG_C, seq)
    rows = min(HG_CHUNKS_PER_STEP * c, seq)
    assert seq % rows == 0 and rows % c == 0
    pmat, masks = _hgrn_consts(c)
    nc = seq // rows
    return pl.pallas_call(
        functools.partial(_hgrn_body, single_step=nc == 1),
        grid=(batch, nc),
        in_specs=[
            pl.BlockSpec((rows, 4 * HG_WIDTH), lambda b, i: (b * nc + i, 0)),
            pl.BlockSpec((1, HG_HEADS, HG_DK, HG_DK), lambda b, i: (b, 0, 0, 0)),
            pl.BlockSpec((1, HG_WIDTH), lambda b, i: (0, 0)),
            pl.BlockSpec((1, HG_DK), lambda b, i: (0, 0)),
            pl.BlockSpec(pmat.shape, lambda b, i: (0, 0)),
            pl.BlockSpec(masks.shape, lambda b, i: (0, 0, 0)),
        ],
        out_specs=[
            pl.BlockSpec((rows, HG_WIDTH), lambda b, i: (b * nc + i, 0)),
            pl.BlockSpec((1, HG_HEADS, HG_DK, HG_DK), lambda b, i: (b, 0, 0, 0)),
        ],
        out_shape=[
            jax.ShapeDtypeStruct((batch * seq, HG_WIDTH), BF16),
            jax.ShapeDtypeStruct((batch, HG_HEADS, HG_DK, HG_DK), F32),
        ],
        scratch_shapes=[pltpu.VMEM((HG_HEADS, HG_DK, HG_DK), F32)],
        compiler_params=_params("arbitrary", "arbitrary"),
        name="hgrn2",
    )(zb, s0, lower, g_on, pmat, masks)


def _split_bf16(x):
    hi = x.astype(BF16)
    return hi, (x - hi.astype(F32)).astype(BF16)


def _merge_body(att_p, hg_p, zg_p, x_p, att_s, hg_s, zg_s, x_s, wpa_ref, wpb_ref, wout_ref, gffn_ref, wr_ref, br_ref,
                x1_ref, h2_ref, route_ref, cnt_ref, *, n_first):
    weights = (wpa_ref, wpb_ref, wout_ref, gffn_ref, wr_ref, br_ref)
    outs = (x1_ref, h2_ref, route_ref, cnt_ref)
    i = pl.program_id(0)
    pl.when(i < n_first)(functools.partial(_merge_block, att_p, hg_p, zg_p, x_p, *weights, *outs))
    pl.when(i >= n_first)(functools.partial(_merge_block, att_s, hg_s, zg_s, x_s, *weights, *outs))


def _merge_block(att_ref, hg_ref, zg_ref, x_ref, wpa_ref, wpb_ref, wout_ref, gffn_ref, wr_ref, br_ref,
                 x1_ref, h2_ref, route_ref, cnt_ref):
    pa = jnp.dot(att_ref[...].astype(BF16), wpa_ref[...], preferred_element_type=F32)
    pb = jnp.dot(hg_ref[...].astype(BF16), wpb_ref[...], preferred_element_type=F32)
    y = _sigmoid(zg_ref[:, :D_MODEL]) * pa + _sigmoid(zg_ref[:, D_MODEL:]) * pb
    x1 = x_ref[...] + jnp.dot(y.astype(BF16), wout_ref[...], preferred_element_type=F32)
    x1_ref[...] = x1
    h2 = _rms(x1, gffn_ref[...])
    h2_ref[...] = h2.astype(BF16)

    h_hi, h_lo = _split_bf16(h2)
    w_hi, w_lo = _split_bf16(wr_ref[...])
    both = jnp.dot(h_hi, jnp.concatenate([w_hi, w_lo], axis=1), preferred_element_type=F32)
    logits = both[:, :LANES] + both[:, LANES:] + jnp.dot(h_lo, w_hi, preferred_element_type=F32) + br_ref[...]
    lane = lax.broadcasted_iota(jnp.int32, logits.shape, 1).astype(F32)
    cur = logits
    vals, idxs = [], []
    for _ in range(TOP_K):
        m = jnp.max(cur, axis=-1, keepdims=True)
        idx = jnp.min(jnp.where(cur == m, lane, float(LANES)), axis=-1, keepdims=True)
        vals.append(m)
        idxs.append(idx)
        cur = jnp.where(lane == idx, -jnp.inf, cur)
    es = [jnp.exp(v - vals[0]) for v in vals]
    inv = 1.0 / functools.reduce(jnp.add, es)
    route = jnp.zeros_like(logits)
    for k, (ex, idx) in enumerate(zip(es, idxs)):
        route = (route + jnp.where(lane == idx, 1.0, 0.0) + jnp.where(lane == float(ROUTE_IDX + k), idx, 0.0)
                 + jnp.where(lane == float(ROUTE_P + k), ex * inv, 0.0))
    route_ref[...] = route
    sel = jnp.where(lane < float(N_EXPERTS), route, 0.0)
    for blk in range(cnt_ref.shape[0]):
        cnt_ref[blk] = jnp.sum(sel[blk * TBK:(blk + 1) * TBK], axis=0, keepdims=True)


def _merge(first, second, wpa, wpb, wout, g_ffn, w_router, b_router):
    tm = MERGE_TM
    assert first[3].shape[0] % tm == 0 and second[3].shape[0] % tm == 0
    n1, n2 = first[3].shape[0] // tm, second[3].shape[0] // tm
    widths = (ATT_WIDTH, HG_WIDTH, 2 * D_MODEL, D_MODEL)
    spec1 = [pl.BlockSpec((tm, w), lambda i: (jnp.minimum(i, n1 - 1), 0)) for w in widths]
    spec2 = [pl.BlockSpec((tm, w), lambda i: (jnp.maximum(i - n1, 0), 0)) for w in widths]
    row = lambda w: pl.BlockSpec((tm, w), lambda i: (i, 0))
    full = lambda a: pl.BlockSpec(a.shape, lambda i: (0,) * a.ndim)
    n = (n1 + n2) * tm
    return pl.pallas_call(
        functools.partial(_merge_body, n_first=n1),
        grid=(n1 + n2,),
        in_specs=spec1 + spec2 + [full(wpa), full(wpb), full(wout), full(g_ffn), full(w_router), full(b_router)],
        out_specs=[row(D_MODEL), row(D_MODEL), row(LANES),
                   pl.BlockSpec((tm // TBK, 1, LANES), lambda i: (i, 0, 0))],
        out_shape=[
            jax.ShapeDtypeStruct((n, D_MODEL), F32),
            jax.ShapeDtypeStruct((n, D_MODEL), BF16),
            jax.ShapeDtypeStruct((n, LANES), F32),
            jax.ShapeDtypeStruct((n // TBK, 1, LANES), F32),
        ],
        compiler_params=_params("arbitrary"),
        name="merge_router",
    )(*first, *second, wpa, wpb, wout, g_ffn, w_router, b_router)


def _route_plan(cnt):
    pc = (cnt + ROW_CH - 1) // ROW_CH
    loff = jnp.cumsum(pc, axis=1) - pc
    tot = jnp.sum(pc, axis=0)
    reg = (tot + CH_PER_TILE - 1) // CH_PER_TILE * CH_PER_TILE
    gstart = jnp.cumsum(reg) - reg
    goff = gstart[None, :] + jnp.cumsum(pc, axis=0) - pc
    ntiles = jnp.sum(reg) // CH_PER_TILE
    present = reg > 0
    ids = jnp.arange(N_EXPERTS, dtype=jnp.int32)
    later = (ids[None, :] > ids[:, None]) & present[None, :]
    nxt = jnp.min(jnp.where(later, ids[None, :], N_EXPERTS), axis=1)
    nxt = jnp.where(nxt == N_EXPERTS, -1, nxt)
    slot = (jnp.cumsum(present.astype(jnp.int32)) - 1) % 2
    first = jnp.min(jnp.where(present, ids, N_EXPERTS)).reshape(1)
    j = jnp.arange(BLK_CH, dtype=jnp.int32)
    run = jnp.sum(((loff + pc)[:, None, :] <= j[None, :, None]).astype(jnp.int32), axis=2)
    shift = jnp.sum(jnp.where(run[:, :, None] == ids[None, None, :], (goff - loff)[:, None, :], 0), axis=2)
    gmap = shift + j[None, :]
    i32 = lambda a: a.astype(jnp.int32)
    return dict(loff=i32(loff), gmap=i32(gmap), totc=i32(jnp.sum(pc, axis=1)),
                pad_start=i32(gstart + tot), pad_cnt=i32(reg - tot), ntiles=i32(ntiles).reshape(1),
                tile_start=i32(gstart // CH_PER_TILE), tile_count=i32(reg // CH_PER_TILE),
                next_expert=i32(nxt), weight_slot=i32(slot), first_expert=i32(first))


def _chunk_rows(ref, chunk):
    return ref.at[pl.ds(pl.multiple_of(chunk * ROW_CH, ROW_CH), ROW_CH)]


def _for_chunks(n, do):
    log_unroll = 3
    groups = lax.shift_right_logical(n, log_unroll)

    def group(i, carry):
        for u in range(1 << log_unroll):
            do(lax.shift_left(i, log_unroll) + u)
        return carry
    lax.fori_loop(0, groups, group, 0)

    def single(c, carry):
        do(c)
        return carry
    lax.fori_loop(lax.shift_left(groups, log_unroll), n, single, 0)


def _wait_chunks(n, copy_of_rows):
    for bit in range((RB // ROW_CH).bit_length()):
        @pl.when(((n >> bit) & 1) == 1)
        def _(bit=bit):
            copy_of_rows((1 << bit) * ROW_CH).wait()


def _dispatch_body(gmap_s, totc_s, pads_s, padn_s, nt_s, h2_ref, route_ref, tri_ref, loffv_ref,
                   xs_hbm, buf, zbuf, sem):
    b = pl.program_id(0)
    nb = pl.num_programs(0)
    slot = lax.rem(b, 2)
    fill_sem, tile_sem = 2, 3
    nt_max = xs_hbm.shape[0] // EXP_TM

    def copy_out(src, gchunk, sem_i):
        return pltpu.make_async_copy(src, _chunk_rows(xs_hbm, gchunk), sem.at[sem_i])

    def zero_tile(t):
        return pltpu.make_async_copy(zbuf, xs_hbm.at[pl.ds(pl.multiple_of(t * EXP_TM, EXP_TM), EXP_TM)],
                                     sem.at[tile_sem])

    def wait_block(step, slot_):
        for u in range(DISP_BLKS):
            _wait_chunks(totc_s[step * DISP_BLKS + u], lambda rows: pltpu.make_async_copy(
                buf.at[slot_, pl.ds(0, rows)], xs_hbm.at[pl.ds(0, rows)], sem.at[slot_]))

    @pl.when(b == 0)
    def _():
        zbuf[...] = jnp.zeros_like(zbuf)

    @pl.when(b >= 2)
    def _():
        wait_block(b - 2, slot)

    erow = lax.broadcasted_iota(jnp.int32, (N_EXPERTS, TBK), 0).astype(F32)

    def positions(u):
        rt = route_ref[u * TBK:(u + 1) * TBK, :].T
        rank_t = jnp.dot(rt[0:N_EXPERTS].astype(BF16), tri_ref[...], preferred_element_type=F32)
        lpos_t = loffv_ref[u] * float(ROW_CH) + rank_t
        return [jnp.sum(jnp.where(rt[ROUTE_IDX + k:ROUTE_IDX + k + 1] == erow, lpos_t, 0.0), axis=0, keepdims=True)
                for k in range(TOP_K)]

    def permute(u, lposk):
        h2 = h2_ref[u * TBK:(u + 1) * TBK, :]
        for r0 in range(0, RB, PERM_TM):
            rrow = lax.broadcasted_iota(jnp.int32, (PERM_TM, TBK), 0).astype(F32) + float(r0)
            perm = functools.reduce(jnp.add, [jnp.where(lposk[k] == rrow, 1.0, 0.0) for k in range(TOP_K)])
            buf[slot, u * RB + r0:u * RB + r0 + PERM_TM, :] = jnp.dot(perm.astype(BF16), h2,
                                                                     preferred_element_type=F32)

    pos = [positions(u) for u in range(DISP_BLKS)]
    for u in range(DISP_BLKS):
        permute(u, pos[u])
    for u in range(DISP_BLKS):
        bb = b * DISP_BLKS + u
        _for_chunks(totc_s[bb], lambda c: copy_out(_chunk_rows(buf.at[slot], u * BLK_CH + c),
                                                   gmap_s[bb * BLK_CH + c], slot).start())

    @pl.when(b == nb - 1)
    def _():
        zchunk = _chunk_rows(zbuf, 0)

        def fill_expert(e, carry):
            def one(c, carry2):
                copy_out(zchunk, pads_s[e] + c, fill_sem).start()
                return carry2
            lax.fori_loop(0, padn_s[e], one, 0)
            return carry
        lax.fori_loop(0, N_EXPERTS, fill_expert, 0)

        def fill_tile(t, carry):
            zero_tile(t).start()
            return carry
        lax.fori_loop(nt_s[0], nt_max, fill_tile, 0)

        @pl.when(b >= 1)
        def _():
            wait_block(b - 1, 1 - slot)
        wait_block(b, slot)

        def drain_expert(e, carry):
            def one(c, carry2):
                copy_out(zchunk, 0, fill_sem).wait()
                return carry2
            lax.fori_loop(0, padn_s[e], one, 0)
            return carry
        lax.fori_loop(0, N_EXPERTS, drain_expert, 0)

        def drain_tile(t, carry):
            zero_tile(0).wait()
            return carry
        lax.fori_loop(nt_s[0], nt_max, drain_tile, 0)


def _dispatch(plan, h2, route, loffv, n_rows):
    tm = DISP_BLKS * TBK
    assert h2.shape[0] % tm == 0
    tri = jnp.asarray(np.triu(np.ones((TBK, TBK), np.float32), 1), BF16)
    grid_spec = pltpu.PrefetchScalarGridSpec(
        num_scalar_prefetch=5,
        grid=(h2.shape[0] // tm,),
        in_specs=[
            pl.BlockSpec((tm, D_MODEL), lambda b, *_: (b, 0)),
            pl.BlockSpec((tm, LANES), lambda b, *_: (b, 0)),
            pl.BlockSpec((TBK, TBK), lambda b, *_: (0, 0)),
            pl.BlockSpec((DISP_BLKS, N_EXPERTS, 1), lambda b, *_: (b, 0, 0)),
        ],
        out_specs=pl.BlockSpec(memory_space=pl.ANY),
        scratch_shapes=[pltpu.VMEM((2, DISP_BLKS * RB, D_MODEL), F32), pltpu.VMEM((EXP_TM, D_MODEL), F32),
                        pltpu.SemaphoreType.DMA((4,))],
    )
    return pl.pallas_call(
        _dispatch_body,
        grid_spec=grid_spec,
        out_shape=jax.ShapeDtypeStruct((n_rows, D_MODEL), F32),
        compiler_params=_params("arbitrary"),
        name="moe_dispatch",
    )(plan["gmap"].reshape(-1), plan["totc"], plan["pad_start"], plan["pad_cnt"], plan["ntiles"],
      h2, route, tri, loffv)


def _experts_body(t0_s, n_s, nxt_s, par_s, first_s, nt_s, xs_hbm, wgu_hbm, bgu_ref, wd_hbm, bd_ref, ys_hbm,
                  wg_f, wd_f, wg_b, wd_b, xbuf, ybuf, wsem, xsem, ysem):
    e = pl.program_id(0)
    nt = nt_s[0]
    nt_max = xs_hbm.shape[0] // EXP_TM
    tile_rows = lambda t: pl.ds(pl.multiple_of(t * EXP_TM, EXP_TM), EXP_TM)

    def weight_copies(ex, s, piece):
        rows = pl.ds(pl.multiple_of(piece * W_PIECE_ROWS, W_PIECE_ROWS), W_PIECE_ROWS)
        return (pltpu.make_async_copy(wgu_hbm.at[ex, rows], wg_f.at[s, rows], wsem.at[s, 0]),
                pltpu.make_async_copy(wd_hbm.at[ex, rows], wd_f.at[s, rows], wsem.at[s, 1]))

    def x_copy(t, s):
        return pltpu.make_async_copy(xs_hbm.at[tile_rows(t)], xbuf.at[s], xsem.at[s])

    def y_copy(t, s):
        return pltpu.make_async_copy(ybuf.at[s], ys_hbm.at[tile_rows(t)], ysem.at[s])

    @pl.when(e == 0)
    def _():
        for piece in range(W_PIECES):
            for cp in weight_copies(first_s[0], 0, piece):
                cp.start()
        x_copy(0, 0).start()

    @pl.when(n_s[e] > 0)
    def _():
        s_w = par_s[e]
        for piece in range(W_PIECES):
            for cp in weight_copies(e, s_w, piece):
                cp.wait()
        wg_b[...] = wg_f[s_w].astype(BF16)
        wd_b[...] = wd_f[s_w].astype(BF16)
        has_next = nxt_s[e] >= 0

        def request(piece):
            for cp in weight_copies(nxt_s[e], 1 - s_w, piece):
                cp.start()

        def tile(i, carry):
            t = t0_s[e] + i
            s = t & 1
            x_copy(t, s).wait()

            @pl.when(has_next & (i < W_PIECES))
            def _():
                request(i)

            @pl.when(t + 1 < nt)
            def _():
                x_copy(t + 1, 1 - s).start()

            @pl.when(t >= 2)
            def _():
                y_copy(t - 2, s).wait()

            gu = jnp.dot(xbuf[s].astype(BF16), wg_b[...], preferred_element_type=F32) + bgu_ref[...]
            gate = jnp.minimum(gu[:, :D_FF], SWIGLU_LIMIT)
            up = jnp.clip(gu[:, D_FF:], -SWIGLU_LIMIT, SWIGLU_LIMIT)
            act = (up + 1.0) * gate * _sigmoid(SWIGLU_ALPHA * gate)
            ybuf[s] = jnp.dot(act.astype(BF16), wd_b[...], preferred_element_type=F32) + bd_ref[...]
            y_copy(t, s).start()
            return carry
        lax.fori_loop(0, n_s[e], tile, 0)

        @pl.when(has_next)
        def _():
            def rest(piece, carry):
                request(piece)
                return carry
            lax.fori_loop(jnp.minimum(n_s[e], W_PIECES), W_PIECES, rest, 0)

    @pl.when(e == pl.num_programs(0) - 1)
    def _():
        @pl.when(nt >= 2)
        def _():
            y_copy(nt - 2, nt & 1).wait()
        y_copy(nt - 1, (nt - 1) & 1).wait()
        ybuf[0] = jnp.zeros((EXP_TM, D_MODEL), F32)

        def fill(t, carry):
            y_copy(t, 0).start()
            return carry
        lax.fori_loop(nt, nt_max, fill, 0)

        def drain(t, carry):
            y_copy(0, 0).wait()
            return carry
        lax.fori_loop(nt, nt_max, drain, 0)


def _experts(plan, xs, w_gu, b_gu, w_down, b_down):
    n_rows = xs.shape[0]
    of_expert = lambda e, *_: (e, 0, 0)
    grid_spec = pltpu.PrefetchScalarGridSpec(
        num_scalar_prefetch=6,
        grid=(N_EXPERTS,),
        in_specs=[
            pl.BlockSpec(memory_space=pl.ANY),
            pl.BlockSpec(memory_space=pl.ANY),
            pl.BlockSpec((None, 1, 2 * D_FF), of_expert),
            pl.BlockSpec(memory_space=pl.ANY),
            pl.BlockSpec((None, 1, D_MODEL), of_expert),
        ],
        out_specs=pl.BlockSpec(memory_space=pl.ANY),
        scratch_shapes=[pltpu.VMEM((2, D_MODEL, 2 * D_FF), F32), pltpu.VMEM((2, D_FF, D_MODEL), F32),
                        pltpu.VMEM((D_MODEL, 2 * D_FF), BF16), pltpu.VMEM((D_FF, D_MODEL), BF16),
                        pltpu.VMEM((2, EXP_TM, D_MODEL), F32), pltpu.VMEM((2, EXP_TM, D_MODEL), F32),
                        pltpu.SemaphoreType.DMA((2, 2)), pltpu.SemaphoreType.DMA((2,)),
                        pltpu.SemaphoreType.DMA((2,))],
    )
    return pl.pallas_call(
        _experts_body,
        grid_spec=grid_spec,
        out_shape=jax.ShapeDtypeStruct((n_rows, D_MODEL), F32),
        compiler_params=_params("arbitrary"),
        name="moe_experts",
    )(plan["tile_start"], plan["tile_count"], plan["next_expert"], plan["weight_slot"], plan["first_expert"],
      plan["ntiles"], xs, w_gu, b_gu, w_down, b_down)


def _combine_body(gmap_s, totc_s, route_ref, tril_ref, loffrow_ref, x1_ref, gfin_ref, ys_hbm, y_ref, buf, sem):
    i = pl.program_id(0)
    n_steps = pl.num_programs(0)
    slot = lax.rem(i, 2)

    def copy_in(slot_, lchunk, gchunk):
        return pltpu.make_async_copy(_chunk_rows(ys_hbm, gchunk), _chunk_rows(buf.at[slot_], lchunk), sem.at[slot_])

    def fetch_step(step, slot_):
        for u in range(COMB_BLKS):
            bb = step * COMB_BLKS + u
            _for_chunks(totc_s[bb], lambda c: copy_in(slot_, u * BLK_CH + c, gmap_s[bb * BLK_CH + c]).start())

    def wait_step(step, slot_):
        for u in range(COMB_BLKS):
            _wait_chunks(totc_s[step * COMB_BLKS + u], lambda rows: pltpu.make_async_copy(
                ys_hbm.at[pl.ds(0, rows)], buf.at[slot_, pl.ds(0, rows)], sem.at[slot_]))

    @pl.when(i == 0)
    def _():
        buf[...] = jnp.zeros_like(buf)
        fetch_step(0, 0)

    @pl.when(i + 1 < n_steps)
    def _():
        fetch_step(i + 1, 1 - slot)

    wait_step(i, slot)

    lane = lax.broadcasted_iota(jnp.int32, (1, LANES), 1).astype(F32)
    col = lax.broadcasted_iota(jnp.int32, (TBK, PERM_TM), 1).astype(F32).astype(BF16)
    zero = jnp.zeros((), BF16)

    def positions(u):
        route = route_ref[u * TBK:(u + 1) * TBK, :]
        sel = jnp.where(lane < float(N_EXPERTS), route, 0.0).astype(BF16)
        rank = jnp.dot(tril_ref[...], sel, preferred_element_type=F32)
        lpos = loffrow_ref[u] * float(ROW_CH) + rank
        lposk, pkb = [], []
        for k in range(TOP_K):
            idx = route[:, ROUTE_IDX + k:ROUTE_IDX + k + 1]
            lposk.append(jnp.sum(jnp.where(lane == idx, lpos, 0.0), axis=-1, keepdims=True))
            pkb.append(route[:, ROUTE_P + k:ROUTE_P + k + 1].astype(BF16))
        return lposk, pkb

    def gather(u, lposk, pkb):
        acc = x1_ref[u * TBK:(u + 1) * TBK, :]
        for r0 in range(0, RB, PERM_TM):
            rel = [(lposk[k] - float(r0)).astype(BF16) for k in range(TOP_K)]
            w = functools.reduce(jnp.add, [jnp.where(rel[k] == col, pkb[k], zero) for k in range(TOP_K)])
            rows = buf[slot, u * RB + r0:u * RB + r0 + PERM_TM, :].astype(BF16)
            acc = acc + jnp.dot(w, rows, preferred_element_type=F32)
        y_ref[u * TBK:(u + 1) * TBK, :] = _rms(acc, gfin_ref[...])

    pos = [positions(u) for u in range(COMB_BLKS)]
    for u in range(COMB_BLKS):
        gather(u, *pos[u])


def _combine(plan, blocks, route, loffrow, x1, g_final, ys):
    assert RB % PERM_TM == 0 and PERM_TM <= 256
    b0, b1 = blocks
    assert b0 % COMB_BLKS == 0 and b1 % COMB_BLKS == 0
    n_steps, s0 = (b1 - b0) // COMB_BLKS, b0 // COMB_BLKS
    tm = COMB_BLKS * TBK
    tril = jnp.asarray(np.tril(np.ones((TBK, TBK), np.float32), -1), BF16)
    grid_spec = pltpu.PrefetchScalarGridSpec(
        num_scalar_prefetch=2,
        grid=(n_steps,),
        in_specs=[
            pl.BlockSpec((tm, LANES), lambda i, *_: (i + s0, 0)),
            pl.BlockSpec((TBK, TBK), lambda i, *_: (0, 0)),
            pl.BlockSpec((COMB_BLKS, 1, LANES), lambda i, *_: (i + s0, 0, 0)),
            pl.BlockSpec((tm, D_MODEL), lambda i, *_: (i + s0, 0)),
            pl.BlockSpec((1, D_MODEL), lambda i, *_: (0, 0)),
            pl.BlockSpec(memory_space=pl.ANY),
        ],
        out_specs=pl.BlockSpec((tm, D_MODEL), lambda i, *_: (i, 0)),
        scratch_shapes=[pltpu.VMEM((2, COMB_BLKS * RB, D_MODEL), F32), pltpu.SemaphoreType.DMA((2,))],
    )
    return pl.pallas_call(
        _combine_body,
        grid_spec=grid_spec,
        out_shape=jax.ShapeDtypeStruct(((b1 - b0) * TBK, D_MODEL), F32),
        compiler_params=_params("arbitrary"),
        name="moe_combine",
    )(plan["gmap"][b0:b1].reshape(-1), plan["totc"][b0:b1], route, tril, loffrow, x1, g_final, ys)


def kernel(x_prompt, x_sample, cache_k, cache_v, state_s, g_mix, w_in, rel_bias, lb_logits, g_out_norm,
           w_pa, w_pb, w_out, g_ffn, w_router, b_router, w_gu, b_gu, w_down, b_down, g_final):
    B, T = x_prompt.shape[:2]
    DB, S = x_sample.shape[:2]
    depth = w_in.shape[0]
    assert depth == 1 and T % ATT_QBLK == 0 and S == CHUNK
    cw = cache_k.shape[2]
    assert cw == WINDOW
    l = 0

    lower = jnp.cumsum(jax.nn.softmax(lb_logits.astype(F32), axis=0), axis=0)[l].reshape(1, HG_WIDTH)
    w_in_b = w_in[l].astype(BF16)
    wpa, wpb, wout = w_pa[l].astype(BF16), w_pb[l].astype(BF16), w_out[l].astype(BF16)
    row = lambda a: a.reshape(1, -1).astype(F32)
    base = _rel_bias_base(rel_bias[l])
    b_gu3 = b_gu[l].reshape(N_EXPERTS, 1, 2 * D_FF)
    b_down3 = b_down[l].reshape(N_EXPERTS, 1, D_MODEL)
    pad_e = LANES - N_EXPERTS
    wr = jnp.pad(w_router[l].astype(F32), ((0, 0), (0, pad_e)))
    br = jnp.concatenate([b_router[l].astype(F32), jnp.full((pad_e,), NEG, F32)]).reshape(1, LANES)

    n_tok = B * T + DB * S
    nb, nbp = n_tok // TBK, (B * T) // TBK

    def front(x, batch, seq, s0, attend):
        xf = x.reshape(batch * seq, D_MODEL)
        za, zb, zg = _inproj(xf, row(g_mix[l]), w_in_b)
        att = attend(za)
        hg, s_fin = _hgrn(zb, s0, lower, row(g_out_norm[l]), batch, seq)
        za3 = za.reshape(batch, seq, 3 * ATT_WIDTH)
        heads = lambda a: a.reshape(1, batch, a.shape[1], ATT_HEADS, ATT_DIM)
        keep = min(WINDOW, seq)
        nk = heads(za3[:, seq - keep:, ATT_WIDTH:2 * ATT_WIDTH])
        nv = heads(za3[:, seq - keep:, 2 * ATT_WIDTH:])
        return dict(mix=(att, hg, zg, xf), nk=nk, nv=nv, s=s_fin[None])

    ck = cache_k[l].reshape(DB, cw, ATT_WIDTH)
    cv = cache_v[l].reshape(DB, cw, ATT_WIDTH)
    fp = front(x_prompt, B, T, jnp.zeros((B, HG_HEADS, HG_DK, HG_DK), F32), lambda za: _attn_prompt(za, base, B, T))
    fs = front(x_sample, DB, S, state_s[l].astype(F32), lambda za: _attn_sample(za, ck, cv, base, DB, S))

    x1, h2, route, cnt = _merge(fp["mix"], fs["mix"], wpa, wpb, wout, row(g_ffn[l]), wr, br)
    cnt = cnt[:, 0, :N_EXPERTS].astype(jnp.int32)
    max_rows = n_tok * TOP_K + nb * N_EXPERTS * (ROW_CH - 1) + N_EXPERTS * (EXP_TM - 1)
    nt_max = -(-max_rows // EXP_TM)
    plan = _route_plan(cnt)
    loff_f = plan["loff"].astype(F32)
    xs = _dispatch(plan, h2, route, loff_f[:, :, None], nt_max * EXP_TM)
    ysort = _experts(plan, xs, w_gu[l], b_gu3, w_down[l], b_down3)
    loffrow = jnp.pad(loff_f, ((0, 0), (0, pad_e)))[:, None, :]
    yp = _combine(plan, (0, nbp), route, loffrow, x1, row(g_final), ysort)
    ys = _combine(plan, (nbp, nb), route, loffrow, x1, row(g_final), ysort)
    return (yp.reshape(B, T, D_MODEL), ys.reshape(DB, S, D_MODEL), fp["nk"], fp["nv"], fp["s"],
            fs["nk"], fs["nv"], fs["s"])
```

```python
import functools

import numpy as np
import jax
import jax.numpy as jnp
from jax import lax
from jax.experimental import pallas as pl
from jax.experimental.pallas import tpu as pltpu

F32 = jnp.float32
BF16 = jnp.bfloat16

D_MODEL = 1024
CHUNK = 64
LEFT_CHUNKS = 8
WINDOW = LEFT_CHUNKS * CHUNK
ATT_HEADS = 8
ATT_DIM = 64
ATT_WIDTH = ATT_HEADS * ATT_DIM
MAX_REL = 256
HG_HEADS = 4
HG_DK = 128
HG_WIDTH = HG_HEADS * HG_DK
N_EXPERTS = 32
TOP_K = 4
D_FF = D_MODEL
SWIGLU_LIMIT = 7.0
SWIGLU_ALPHA = 1.702
RMS_EPS = 1e-5

LANES = 128
NEG = -1e30
LOG2E = 1.4426950408889634
ATT_QBLK = 4 * CHUNK
ATT_KBLKS = LEFT_CHUNKS * CHUNK // ATT_QBLK + 1
ATT_PAIRS_PER_STAGE = 2
HG_C = 128
HG_CHUNKS_PER_STEP = 8
VMEM_LIMIT = 56 * 1024 * 1024
BIAS_W = 1024
SUBLANES = 8
TBK = 256
ROW_CH = SUBLANES
RB = TBK * TOP_K + N_EXPERTS * ROW_CH
MERGE_TM = 2 * TBK
COMB_BLKS = 2
DISP_BLKS = 4
PERM_TM = 256
EXP_TM = 512
W_PIECES = 4
W_PIECE_ROWS = D_MODEL // W_PIECES
CH_PER_TILE = EXP_TM // ROW_CH
BLK_CH = RB // ROW_CH
ROUTE_IDX = 64
ROUTE_P = 72

NT = (((1,), (1,)), ((), ()))
TN = (((0,), (0,)), ((), ()))


def _rms(x, g):
    return x * lax.rsqrt(jnp.mean(x * x, axis=-1, keepdims=True) + RMS_EPS) * g


def _sigmoid(x):
    return 1.0 / (1.0 + jnp.exp(-x))


def _params(*sem):
    return pltpu.CompilerParams(dimension_semantics=sem, vmem_limit_bytes=VMEM_LIMIT)


def _inproj_body(x_ref, g_ref, w_ref, za_ref, zb_ref, zg_ref):
    h = _rms(x_ref[...], g_ref[...]).astype(BF16)
    a, b = 3 * ATT_WIDTH, 3 * ATT_WIDTH + 4 * HG_WIDTH
    za_ref[...] = jnp.dot(h, w_ref[:, :a], preferred_element_type=F32)
    zb_ref[...] = jnp.dot(h, w_ref[:, a:b], preferred_element_type=F32)
    zg_ref[...] = jnp.dot(h, w_ref[:, b:], preferred_element_type=F32)


def _inproj(x, g, w_bf16, tm=512):
    n = x.shape[0]
    assert n % tm == 0
    cols = w_bf16.shape[1]
    wa, wb, wg = 3 * ATT_WIDTH, 4 * HG_WIDTH, 2 * D_MODEL
    return pl.pallas_call(
        _inproj_body,
        grid=(n // tm,),
        in_specs=[
            pl.BlockSpec((tm, D_MODEL), lambda i: (i, 0)),
            pl.BlockSpec((1, D_MODEL), lambda i: (0, 0)),
            pl.BlockSpec((D_MODEL, cols), lambda i: (0, 0)),
        ],
        out_specs=[
            pl.BlockSpec((tm, wa), lambda i: (i, 0)),
            pl.BlockSpec((tm, wb), lambda i: (i, 0)),
            pl.BlockSpec((tm, wg), lambda i: (i, 0)),
        ],
        out_shape=[
            jax.ShapeDtypeStruct((n, wa), F32),
            jax.ShapeDtypeStruct((n, wb), F32),
            jax.ShapeDtypeStruct((n, wg), F32),
        ],
        compiler_params=_params("arbitrary"),
        name="inproj",
    )(x, g, w_bf16)


def _attn_heads(q_ref, k_refs, v_refs, bias_fn, pens, o_ref):
    lane = lax.broadcasted_iota(jnp.int32, (1, LANES), 1)
    first = lane < ATT_DIM
    halves = (first, lane >= ATT_DIM)
    for hp0 in range(0, ATT_HEADS // 2, ATT_PAIRS_PER_STAGE):
        pairs = range(hp0, hp0 + ATT_PAIRS_PER_STAGE)
        sl = {hp: slice(hp * LANES, (hp + 1) * LANES) for hp in pairs}
        scores = {}
        for hp in pairs:
            q2 = q_ref[:, sl[hp]] * (ATT_DIM ** -0.5 * LOG2E)
            ks = [k[:, sl[hp]].astype(BF16) for k in k_refs]
            for half, mine in enumerate(halves):
                qm = jnp.where(mine, q2, 0.0).astype(BF16)
                ss = []
                for j, kj in enumerate(ks):
                    s = lax.dot_general(qm, kj, NT, preferred_element_type=F32) + bias_fn(2 * hp + half, j)
                    if pens[j] is not None:
                        s = s + pens[j]
                    ss.append(s)
                scores[hp, half] = ss
        for hp in pairs:
            outs = []
            for half, mine in enumerate(halves):
                ss = scores[hp, half]
                vs = [jnp.where(mine, v[:, sl[hp]], 1.0).astype(BF16) for v in v_refs]
                if all(s.shape == ss[0].shape for s in ss):
                    m = jnp.max(functools.reduce(jnp.maximum, ss), axis=-1, keepdims=True)
                else:
                    m = functools.reduce(jnp.maximum, [jnp.max(s, axis=-1, keepdims=True) for s in ss])
                outs.append(functools.reduce(jnp.add, [
                    jnp.dot(jnp.exp2(s - m).astype(BF16), vj, preferred_element_type=F32) for s, vj in zip(ss, vs)]))
            num = jnp.where(first, outs[0], outs[1])
            den = pltpu.roll(jnp.where(first, outs[1], outs[0]), ATT_DIM, 1)
            o_ref[:, sl[hp]] = (num * (1.0 / den)).astype(o_ref.dtype)


def _fill_bias(base_ref, bias_ref, banded):
    nq, nk = bias_ref.shape[1:]
    if banded:
        r = lax.broadcasted_iota(jnp.int32, (nq, nk), 0)
        s = lax.broadcasted_iota(jnp.int32, (nq, nk), 1)
        qc = (r + WINDOW) // CHUNK
        kc = s // CHUNK
        pen = jnp.where(kc <= qc, jnp.where(kc >= qc - LEFT_CHUNKS, 0.0, NEG), NEG)
    for h in range(ATT_HEADS):
        rows = jnp.broadcast_to(base_ref[h:h + 1, :], (nq, BIAS_W))
        t = pltpu.roll(rows, 0, 1, stride=1, stride_axis=0)[:, :nk] * LOG2E
        bias_ref[h] = t + pen if banded else t


def _attn_prompt_body(q_ref, k0, k1, k2, v0, v1, v2, base_ref, o_ref, bias_ref):
    i = pl.program_id(1)

    @pl.when((pl.program_id(0) == 0) & (i == 0))
    def _():
        _fill_bias(base_ref, bias_ref, True)

    bias_fn = lambda h, j: bias_ref[h, :, j * ATT_QBLK:(j + 1) * ATT_QBLK]
    back = ATT_KBLKS - 1

    @pl.when(i < back)
    def _():
        pens = [jnp.where(i - back + j >= 0, 0.0, NEG) for j in range(back)] + [None]
        _attn_heads(q_ref, [k0, k1, k2], [v0, v1, v2], bias_fn, pens, o_ref)

    @pl.when(i >= back)
    def _():
        _attn_heads(q_ref, [k0, k1, k2], [v0, v1, v2], bias_fn, [None] * ATT_KBLKS, o_ref)


def _attn_prompt(za, base, batch, seq):
    assert seq % ATT_QBLK == 0 and BIAS_W - ATT_KBLKS * ATT_QBLK >= ATT_QBLK - 1 and WINDOW >= MAX_REL
    nq = seq // ATT_QBLK
    back = ATT_KBLKS - 1
    qspec = pl.BlockSpec((ATT_QBLK, ATT_WIDTH), lambda b, i: (b * nq + i, 0))

    def kvspec(j, col):
        return pl.BlockSpec((ATT_QBLK, ATT_WIDTH),
                            lambda b, i: (b * nq + jnp.maximum(i - back + j, 0), col))

    return pl.pallas_call(
        _attn_prompt_body,
        grid=(batch, nq),
        in_specs=[qspec] + [kvspec(j, 1) for j in range(ATT_KBLKS)] + [kvspec(j, 2) for j in range(ATT_KBLKS)]
        + [pl.BlockSpec(base.shape, lambda b, i: (0, 0))],
        out_specs=pl.BlockSpec((ATT_QBLK, ATT_WIDTH), lambda b, i: (b * nq + i, 0)),
        out_shape=jax.ShapeDtypeStruct((batch * seq, ATT_WIDTH), BF16),
        scratch_shapes=[pltpu.VMEM((ATT_HEADS, ATT_QBLK, ATT_KBLKS * ATT_QBLK), F32)],
        compiler_params=_params("arbitrary", "arbitrary"),
        name="attn_prompt",
    )(za, za, za, za, za, za, za, base)


def _attn_sample_body(q_ref, kn_ref, vn_ref, ck_ref, cv_ref, base_ref, o_ref, bias_ref):
    @pl.when(pl.program_id(0) == 0)
    def _():
        _fill_bias(base_ref, bias_ref, False)

    cw = ck_ref.shape[0]
    bias_fn = lambda h, j: bias_ref[h, :, :cw] if j == 0 else bias_ref[h, :, cw:]
    _attn_heads(q_ref, [ck_ref, kn_ref], [cv_ref, vn_ref], bias_fn, [None, None], o_ref)


def _attn_sample(za, ck, cv, base, batch, seq):
    cw = ck.shape[1]
    return pl.pallas_call(
        _attn_sample_body,
        grid=(batch,),
        in_specs=[
            pl.BlockSpec((seq, ATT_WIDTH), lambda b: (b, 0)),
            pl.BlockSpec((seq, ATT_WIDTH), lambda b: (b, 1)),
            pl.BlockSpec((seq, ATT_WIDTH), lambda b: (b, 2)),
            pl.BlockSpec((None, cw, ATT_WIDTH), lambda b: (b, 0, 0)),
            pl.BlockSpec((None, cw, ATT_WIDTH), lambda b: (b, 0, 0)),
            pl.BlockSpec(base.shape, lambda b: (0, 0)),
        ],
        out_specs=pl.BlockSpec((seq, ATT_WIDTH), lambda b: (b, 0)),
        out_shape=jax.ShapeDtypeStruct((batch * seq, ATT_WIDTH), BF16),
        scratch_shapes=[pltpu.VMEM((ATT_HEADS, seq, cw + seq), F32)],
        compiler_params=_params("arbitrary"),
        name="attn_sample",
    )(za, za, za, ck, cv, base)


def _rel_bias_base(table):
    top = table[:, 2 * MAX_REL:].astype(F32)
    rev = table[:, ::-1][:, :2 * MAX_REL].astype(F32)
    left = WINDOW - MAX_REL
    return jnp.concatenate([jnp.broadcast_to(top, (ATT_HEADS, left)), rev,
                            jnp.broadcast_to(top, (ATT_HEADS, BIAS_W - left - 2 * MAX_REL))], axis=1)


def _hgrn_consts(c):
    t = np.arange(c)[:, None]
    j = np.arange(c)[None, :]
    mats = [j <= t, j > t]
    masks = []
    m = c // 2
    while m >= 1:
        ref = (t // (2 * m)) * (2 * m) + m - 1
        second = (t % (2 * m)) >= m
        if m < SUBLANES:
            mats.append((second & (j > ref) & (j <= t)) | (~second & (j > t) & (j <= ref)))
        masks.append((t // (2 * m)) == (j // (2 * m)))
        m //= 2
    return (jnp.asarray(np.concatenate(mats, 0).astype(np.float32), BF16),
            jnp.asarray(np.stack(masks).astype(np.float32)))


def _hgrn_body(zb_ref, s0_ref, lower_ref, gon_ref, p_ref, mask_ref, o_ref, sfin_ref, st_ref, *, single_step):
    c = p_ref.shape[1]
    step = pl.program_id(1)

    def load_state():
        for h in range(HG_HEADS):
            st_ref[h] = s0_ref[0, h].T

    if single_step:
        load_state()
    else:
        pl.when(step == 0)(load_state)

    pmat = p_ref[...]
    n_levels = mask_ref.shape[0]
    head = lambda a, h: a[:, h * HG_DK:(h + 1) * HG_DK]
    low = lower_ref[...]
    row = lax.broadcasted_iota(jnp.int32, (c, HG_WIDTH), 0)

    def stage1(r0):
        part = lambda i: zb_ref[r0:r0 + c, i * HG_WIDTH:(i + 1) * HG_WIDTH]
        q = part(0)
        f = low + (1.0 - low) * _sigmoid(part(1))
        lf = jnp.log(f)
        k = 1.0 - f
        ib = part(2)
        v = ib * _sigmoid(ib)
        og = part(3)
        hi = lf.astype(BF16)
        r1 = lf - hi.astype(F32)
        mid = r1.astype(BF16)
        lo = (r1 - mid.astype(F32)).astype(BF16)
        e = (jnp.dot(pmat, hi, preferred_element_type=F32) + jnp.dot(pmat, mid, preferred_element_type=F32)
             + jnp.dot(pmat, lo, preferred_element_type=F32))
        b = e[0:c]
        return dict(q=q, k=k, v=v, e=e, b=b, decay=jnp.exp(e[c - 1:c]), qe=(q * jnp.exp(b)).astype(BF16),
                    kt=(k * jnp.exp(e[c:2 * c])).astype(BF16), vb=v.astype(BF16), qk=q * k,
                    gate=og * _sigmoid(og))

    def stage2(s):
        q, k, e, b = s["q"], s["k"], s["e"], s["b"]
        att = [None] * HG_HEADS
        n_rows_p = 2
        for lvl in range(n_levels):
            m = c >> (lvl + 1)
            if m >= SUBLANES:
                ref = [jnp.broadcast_to(b[p * 2 * m + m - 1:p * 2 * m + m], (2 * m, HG_WIDTH))
                       for p in range(c // (2 * m))]
                x = jnp.exp(-jnp.abs(b - (jnp.concatenate(ref, axis=0) if len(ref) > 1 else ref[0])))
            else:
                x = jnp.exp(e[n_rows_p * c:(n_rows_p + 1) * c])
                n_rows_p += 1
            second = (row & m) != 0
            qm = jnp.where(second, q * x, 0.0).astype(BF16)
            km = jnp.where(second, 0.0, k * x).astype(BF16)
            for h in range(HG_HEADS):
                a = lax.dot_general(head(qm, h), head(km, h), NT, preferred_element_type=F32)
                if lvl > 0:
                    a = a * mask_ref[lvl]
                att[h] = a if att[h] is None else att[h] + a
        return [jnp.dot(att[h].astype(BF16), head(s["vb"], h), preferred_element_type=F32)
                + jnp.sum(head(s["qk"], h), axis=-1, keepdims=True) * head(s["v"], h) for h in range(HG_HEADS)]

    def stage3(r0, s, intra):
        for h in range(HG_HEADS):
            st = st_ref[h]
            inter = lax.dot_general(head(s["qe"], h), st.astype(BF16), NT, preferred_element_type=F32)
            st_ref[h] = st * head(s["decay"], h) + lax.dot_general(head(s["vb"], h), head(s["kt"], h), TN,
                                                                  preferred_element_type=F32)
            o = _rms(inter + intra[h], gon_ref[...]) * head(s["gate"], h)
            o_ref[r0:r0 + c, h * HG_DK:(h + 1) * HG_DK] = o.astype(o_ref.dtype)

    starts = range(0, zb_ref.shape[0], c)
    firsts = [stage1(r0) for r0 in starts]
    intras = [stage2(s) for s in firsts]
    for r0, s, intra in zip(starts, firsts, intras):
        stage3(r0, s, intra)

    def write_state():
        for h in range(HG_HEADS):
            sfin_ref[0, h] = st_ref[h].T

    if single_step:
        write_state()
    else:
        pl.when(step == pl.num_programs(1) - 1)(write_state)


def _hgrn(zb, s0, lower, g_on, batch, seq):
    c = min(HG_C, seq)
    rows = min(HG_CHUNKS_PER_STEP * c, seq)
    assert seq % rows == 0 and rows % c == 0
    pmat, masks = _hgrn_consts(c)
    nc = seq // rows
    return pl.pallas_call(
        functools.partial(_hgrn_body, single_step=nc == 1),
        grid=(batch, nc),
        in_specs=[
            pl.BlockSpec((rows, 4 * HG_WIDTH), lambda b, i: (b * nc + i, 0)),
            pl.BlockSpec((1, HG_HEADS, HG_DK, HG_DK), lambda b, i: (b, 0, 0, 0)),
            pl.BlockSpec((1, HG_WIDTH), lambda b, i: (0, 0)),
            pl.BlockSpec((1, HG_DK), lambda b, i: (0, 0)),
            pl.BlockSpec(pmat.shape, lambda b, i: (0, 0)),
            pl.BlockSpec(masks.shape, lambda b, i: (0, 0, 0)),
        ],
        out_specs=[
            pl.BlockSpec((rows, HG_WIDTH), lambda b, i: (b * nc + i, 0)),
            pl.BlockSpec((1, HG_HEADS, HG_DK, HG_DK), lambda b, i: (b, 0, 0, 0)),
        ],
        out_shape=[
            jax.ShapeDtypeStruct((batch * seq, HG_WIDTH), BF16),
            jax.ShapeDtypeStruct((batch, HG_HEADS, HG_DK, HG_DK), F32),
        ],
        scratch_shapes=[pltpu.VMEM((HG_HEADS, HG_DK, HG_DK), F32)],
        compiler_params=_params("arbitrary", "arbitrary"),
        name="hgrn2",
    )(zb, s0, lower, g_on, pmat, masks)


def _split_bf16(x):
    hi = x.astype(BF16)
    return hi, (x - hi.astype(F32)).astype(BF16)


def _merge_body(att_p, hg_p, zg_p, x_p, att_s, hg_s, zg_s, x_s, wpa_ref, wpb_ref, wout_ref, gffn_ref, wr_ref, br_ref,
                x1_ref, h2_ref, route_ref, cnt_ref, *, n_first):
    weights = (wpa_ref, wpb_ref, wout_ref, gffn_ref, wr_ref, br_ref)
    outs = (x1_ref, h2_ref, route_ref, cnt_ref)
    i = pl.program_id(0)
    pl.when(i < n_first)(functools.partial(_merge_block, att_p, hg_p, zg_p, x_p, *weights, *outs))
    pl.when(i >= n_first)(functools.partial(_merge_block, att_s, hg_s, zg_s, x_s, *weights, *outs))


def _merge_block(att_ref, hg_ref, zg_ref, x_ref, wpa_ref, wpb_ref, wout_ref, gffn_ref, wr_ref, br_ref,
                 x1_ref, h2_ref, route_ref, cnt_ref):
    pa = jnp.dot(att_ref[...].astype(BF16), wpa_ref[...], preferred_element_type=F32)
    pb = jnp.dot(hg_ref[...].astype(BF16), wpb_ref[...], preferred_element_type=F32)
    y = _sigmoid(zg_ref[:, :D_MODEL]) * pa + _sigmoid(zg_ref[:, D_MODEL:]) * pb
    x1 = x_ref[...] + jnp.dot(y.astype(BF16), wout_ref[...], preferred_element_type=F32)
    x1_ref[...] = x1
    h2 = _rms(x1, gffn_ref[...])
    h2_ref[...] = h2.astype(BF16)

    h_hi, h_lo = _split_bf16(h2)
    w_hi, w_lo = _split_bf16(wr_ref[...])
    both = jnp.dot(h_hi, jnp.concatenate([w_hi, w_lo], axis=1), preferred_element_type=F32)
    logits = both[:, :LANES] + both[:, LANES:] + jnp.dot(h_lo, w_hi, preferred_element_type=F32) + br_ref[...]
    lane = lax.broadcasted_iota(jnp.int32, logits.shape, 1).astype(F32)
    cur = logits
    vals, idxs = [], []
    for _ in range(TOP_K):
        m = jnp.max(cur, axis=-1, keepdims=True)
        idx = jnp.min(jnp.where(cur == m, lane, float(LANES)), axis=-1, keepdims=True)
        vals.append(m)
        idxs.append(idx)
        cur = jnp.where(lane == idx, -jnp.inf, cur)
    es = [jnp.exp(v - vals[0]) for v in vals]
    inv = 1.0 / functools.reduce(jnp.add, es)
    route = jnp.zeros_like(logits)
    for k, (ex, idx) in enumerate(zip(es, idxs)):
        route = (route + jnp.where(lane == idx, 1.0, 0.0) + jnp.where(lane == float(ROUTE_IDX + k), idx, 0.0)
                 + jnp.where(lane == float(ROUTE_P + k), ex * inv, 0.0))
    route_ref[...] = route
    sel = jnp.where(lane < float(N_EXPERTS), route, 0.0)
    for blk in range(cnt_ref.shape[0]):
        cnt_ref[blk] = jnp.sum(sel[blk * TBK:(blk + 1) * TBK], axis=0, keepdims=True)


def _merge(first, second, wpa, wpb, wout, g_ffn, w_router, b_router):
    tm = MERGE_TM
    assert first[3].shape[0] % tm == 0 and second[3].shape[0] % tm == 0
    n1, n2 = first[3].shape[0] // tm, second[3].shape[0] // tm
    widths = (ATT_WIDTH, HG_WIDTH, 2 * D_MODEL, D_MODEL)
    spec1 = [pl.BlockSpec((tm, w), lambda i: (jnp.minimum(i, n1 - 1), 0)) for w in widths]
    spec2 = [pl.BlockSpec((tm, w), lambda i: (jnp.maximum(i - n1, 0), 0)) for w in widths]
    row = lambda w: pl.BlockSpec((tm, w), lambda i: (i, 0))
    full = lambda a: pl.BlockSpec(a.shape, lambda i: (0,) * a.ndim)
    n = (n1 + n2) * tm
    return pl.pallas_call(
        functools.partial(_merge_body, n_first=n1),
        grid=(n1 + n2,),
        in_specs=spec1 + spec2 + [full(wpa), full(wpb), full(wout), full(g_ffn), full(w_router), full(b_router)],
        out_specs=[row(D_MODEL), row(D_MODEL), row(LANES),
                   pl.BlockSpec((tm // TBK, 1, LANES), lambda i: (i, 0, 0))],
        out_shape=[
            jax.ShapeDtypeStruct((n, D_MODEL), F32),
            jax.ShapeDtypeStruct((n, D_MODEL), BF16),
            jax.ShapeDtypeStruct((n, LANES), F32),
            jax.ShapeDtypeStruct((n // TBK, 1, LANES), F32),
        ],
        compiler_params=_params("arbitrary"),
        name="merge_router",
    )(*first, *second, wpa, wpb, wout, g_ffn, w_router, b_router)


def _route_plan(cnt):
    pc = (cnt + ROW_CH - 1) // ROW_CH
    loff = jnp.cumsum(pc, axis=1) - pc
    tot = jnp.sum(pc, axis=0)
    reg = (tot + CH_PER_TILE - 1) // CH_PER_TILE * CH_PER_TILE
    gstart = jnp.cumsum(reg) - reg
    goff = gstart[None, :] + jnp.cumsum(pc, axis=0) - pc
    ntiles = jnp.sum(reg) // CH_PER_TILE
    present = reg > 0
    ids = jnp.arange(N_EXPERTS, dtype=jnp.int32)
    later = (ids[None, :] > ids[:, None]) & present[None, :]
    nxt = jnp.min(jnp.where(later, ids[None, :], N_EXPERTS), axis=1)
    nxt = jnp.where(nxt == N_EXPERTS, -1, nxt)
    slot = (jnp.cumsum(present.astype(jnp.int32)) - 1) % 2
    first = jnp.min(jnp.where(present, ids, N_EXPERTS)).reshape(1)
    j = jnp.arange(BLK_CH, dtype=jnp.int32)
    run = jnp.sum(((loff + pc)[:, None, :] <= j[None, :, None]).astype(jnp.int32), axis=2)
    shift = jnp.sum(jnp.where(run[:, :, None] == ids[None, None, :], (goff - loff)[:, None, :], 0), axis=2)
    gmap = shift + j[None, :]
    i32 = lambda a: a.astype(jnp.int32)
    return dict(loff=i32(loff), gmap=i32(gmap), totc=i32(jnp.sum(pc, axis=1)),
                pad_start=i32(gstart + tot), pad_cnt=i32(reg - tot), ntiles=i32(ntiles).reshape(1),
                tile_start=i32(gstart // CH_PER_TILE), tile_count=i32(reg // CH_PER_TILE),
                next_expert=i32(nxt), weight_slot=i32(slot), first_expert=i32(first))


def _chunk_rows(ref, chunk):
    return ref.at[pl.ds(pl.multiple_of(chunk * ROW_CH, ROW_CH), ROW_CH)]


def _for_chunks(n, do):
    log_unroll = 3
    groups = lax.shift_right_logical(n, log_unroll)

    def group(i, carry):
        for u in range(1 << log_unroll):
            do(lax.shift_left(i, log_unroll) + u, u % 2)
        return carry
    lax.fori_loop(0, groups, group, 0)

    def single(c, carry):
        do(c, 0)
        return carry
    lax.fori_loop(lax.shift_left(groups, log_unroll), n, single, 0)


def _wait_chunks(n, copy_of_rows):
    for bit in range((RB // ROW_CH).bit_length()):
        @pl.when(((n >> bit) & 1) == 1)
        def _(bit=bit):
            copy_of_rows((1 << bit) * ROW_CH).wait()


def _dispatch_body(gmap_s, totc_s, pads_s, padn_s, nt_s, h2_ref, route_ref, tri_ref, loffv_ref,
                   xs_hbm, buf, zbuf, sem):
    b = pl.program_id(0)
    nb = pl.num_programs(0)
    slot = lax.rem(b, 2)
    fill_sem, tile_sem = 2, 3
    nt_max = xs_hbm.shape[0] // EXP_TM

    def copy_out(src, gchunk, sem_i):
        return pltpu.make_async_copy(src, _chunk_rows(xs_hbm, gchunk), sem.at[sem_i])

    def zero_tile(t):
        return pltpu.make_async_copy(zbuf, xs_hbm.at[pl.ds(pl.multiple_of(t * EXP_TM, EXP_TM), EXP_TM)],
                                     sem.at[tile_sem])

    def wait_block(step, slot_):
        for u in range(DISP_BLKS):
            _wait_chunks(totc_s[step * DISP_BLKS + u], lambda rows: pltpu.make_async_copy(
                buf.at[slot_, pl.ds(0, rows)], xs_hbm.at[pl.ds(0, rows)], sem.at[slot_]))

    @pl.when(b == 0)
    def _():
        zbuf[...] = jnp.zeros_like(zbuf)

    @pl.when(b >= 2)
    def _():
        wait_block(b - 2, slot)

    erow = lax.broadcasted_iota(jnp.int32, (N_EXPERTS, TBK), 0).astype(F32)

    def positions(u):
        rt = route_ref[u * TBK:(u + 1) * TBK, :].T
        rank_t = jnp.dot(rt[0:N_EXPERTS].astype(BF16), tri_ref[...], preferred_element_type=F32)
        lpos_t = loffv_ref[u] * float(ROW_CH) + rank_t
        return [jnp.sum(jnp.where(rt[ROUTE_IDX + k:ROUTE_IDX + k + 1] == erow, lpos_t, 0.0), axis=0, keepdims=True)
                for k in range(TOP_K)]

    def permute(u, lposk):
        h2 = h2_ref[u * TBK:(u + 1) * TBK, :]
        for r0 in range(0, RB, PERM_TM):
            rrow = lax.broadcasted_iota(jnp.int32, (PERM_TM, TBK), 0).astype(F32) + float(r0)
            perm = functools.reduce(jnp.add, [jnp.where(lposk[k] == rrow, 1.0, 0.0) for k in range(TOP_K)])
            buf[slot, u * RB + r0:u * RB + r0 + PERM_TM, :] = jnp.dot(perm.astype(BF16), h2,
                                                                     preferred_element_type=F32)

    pos = [positions(u) for u in range(DISP_BLKS)]
    for u in range(DISP_BLKS):
        permute(u, pos[u])
    for u in range(DISP_BLKS):
        bb = b * DISP_BLKS + u
        _for_chunks(totc_s[bb], lambda c, prio: copy_out(_chunk_rows(buf.at[slot], u * BLK_CH + c),
                                                         gmap_s[bb * BLK_CH + c], slot).start(priority=prio))

    @pl.when(b == nb - 1)
    def _():
        zchunk = _chunk_rows(zbuf, 0)

        def fill_expert(e, carry):
            def one(c, carry2):
                copy_out(zchunk, pads_s[e] + c, fill_sem).start()
                return carry2
            lax.fori_loop(0, padn_s[e], one, 0)
            return carry
        lax.fori_loop(0, N_EXPERTS, fill_expert, 0)

        def fill_tile(t, carry):
            zero_tile(t).start()
            return carry
        lax.fori_loop(nt_s[0], nt_max, fill_tile, 0)

        @pl.when(b >= 1)
        def _():
            wait_block(b - 1, 1 - slot)
        wait_block(b, slot)

        def drain_expert(e, carry):
            def one(c, carry2):
                copy_out(zchunk, 0, fill_sem).wait()
                return carry2
            lax.fori_loop(0, padn_s[e], one, 0)
            return carry
        lax.fori_loop(0, N_EXPERTS, drain_expert, 0)

        def drain_tile(t, carry):
            zero_tile(0).wait()
            return carry
        lax.fori_loop(nt_s[0], nt_max, drain_tile, 0)


def _dispatch(plan, h2, route, loffv, n_rows):
    tm = DISP_BLKS * TBK
    assert h2.shape[0] % tm == 0
    tri = jnp.asarray(np.triu(np.ones((TBK, TBK), np.float32), 1), BF16)
    grid_spec = pltpu.PrefetchScalarGridSpec(
        num_scalar_prefetch=5,
        grid=(h2.shape[0] // tm,),
        in_specs=[
            pl.BlockSpec((tm, D_MODEL), lambda b, *_: (b, 0)),
            pl.BlockSpec((tm, LANES), lambda b, *_: (b, 0)),
            pl.BlockSpec((TBK, TBK), lambda b, *_: (0, 0)),
            pl.BlockSpec((DISP_BLKS, N_EXPERTS, 1), lambda b, *_: (b, 0, 0)),
        ],
        out_specs=pl.BlockSpec(memory_space=pl.ANY),
        scratch_shapes=[pltpu.VMEM((2, DISP_BLKS * RB, D_MODEL), F32), pltpu.VMEM((EXP_TM, D_MODEL), F32),
                        pltpu.SemaphoreType.DMA((4,))],
    )
    return pl.pallas_call(
        _dispatch_body,
        grid_spec=grid_spec,
        out_shape=jax.ShapeDtypeStruct((n_rows, D_MODEL), F32),
        compiler_params=_params("arbitrary"),
        name="moe_dispatch",
    )(plan["gmap"].reshape(-1), plan["totc"], plan["pad_start"], plan["pad_cnt"], plan["ntiles"],
      h2, route, tri, loffv)


def _experts_body(t0_s, n_s, nxt_s, par_s, first_s, nt_s, xs_hbm, wgu_hbm, bgu_ref, wd_hbm, bd_ref, ys_hbm,
                  wg_f, wd_f, wg_b, wd_b, xbuf, ybuf, wsem, xsem, ysem):
    e = pl.program_id(0)
    nt = nt_s[0]
    nt_max = xs_hbm.shape[0] // EXP_TM
    tile_rows = lambda t: pl.ds(pl.multiple_of(t * EXP_TM, EXP_TM), EXP_TM)

    def weight_copies(ex, s, piece):
        rows = pl.ds(pl.multiple_of(piece * W_PIECE_ROWS, W_PIECE_ROWS), W_PIECE_ROWS)
        return (pltpu.make_async_copy(wgu_hbm.at[ex, rows], wg_f.at[s, rows], wsem.at[s, 0]),
                pltpu.make_async_copy(wd_hbm.at[ex, rows], wd_f.at[s, rows], wsem.at[s, 1]))

    def x_copy(t, s):
        return pltpu.make_async_copy(xs_hbm.at[tile_rows(t)], xbuf.at[s], xsem.at[s])

    def y_copy(t, s):
        return pltpu.make_async_copy(ybuf.at[s], ys_hbm.at[tile_rows(t)], ysem.at[s])

    @pl.when(e == 0)
    def _():
        for piece in range(W_PIECES):
            for cp in weight_copies(first_s[0], 0, piece):
                cp.start()
        x_copy(0, 0).start()

    @pl.when(n_s[e] > 0)
    def _():
        s_w = par_s[e]
        for piece in range(W_PIECES):
            for cp in weight_copies(e, s_w, piece):
                cp.wait()
        wg_b[...] = wg_f[s_w].astype(BF16)
        wd_b[...] = wd_f[s_w].astype(BF16)
        has_next = nxt_s[e] >= 0

        def request(piece):
            for cp in weight_copies(nxt_s[e], 1 - s_w, piece):
                cp.start()

        def tile(i, carry):
            t = t0_s[e] + i
            s = t & 1
            x_copy(t, s).wait()

            @pl.when(has_next & (i < W_PIECES))
            def _():
                request(i)

            @pl.when(t + 1 < nt)
            def _():
                x_copy(t + 1, 1 - s).start()

            @pl.when(t >= 2)
            def _():
                y_copy(t - 2, s).wait()

            gu = jnp.dot(xbuf[s].astype(BF16), wg_b[...], preferred_element_type=F32) + bgu_ref[...]
            gate = jnp.minimum(gu[:, :D_FF], SWIGLU_LIMIT)
            up = jnp.clip(gu[:, D_FF:], -SWIGLU_LIMIT, SWIGLU_LIMIT)
            act = (up + 1.0) * gate * _sigmoid(SWIGLU_ALPHA * gate)
            ybuf[s] = jnp.dot(act.astype(BF16), wd_b[...], preferred_element_type=F32) + bd_ref[...]
            y_copy(t, s).start()
            return carry
        lax.fori_loop(0, n_s[e], tile, 0)

        @pl.when(has_next)
        def _():
            def rest(piece, carry):
                request(piece)
                return carry
            lax.fori_loop(jnp.minimum(n_s[e], W_PIECES), W_PIECES, rest, 0)

    @pl.when(e == pl.num_programs(0) - 1)
    def _():
        @pl.when(nt >= 2)
        def _():
            y_copy(nt - 2, nt & 1).wait()
        y_copy(nt - 1, (nt - 1) & 1).wait()
        ybuf[0] = jnp.zeros((EXP_TM, D_MODEL), F32)

        def fill(t, carry):
            y_copy(t, 0).start()
            return carry
        lax.fori_loop(nt, nt_max, fill, 0)

        def drain(t, carry):
            y_copy(0, 0).wait()
            return carry
        lax.fori_loop(nt, nt_max, drain, 0)


def _experts(plan, xs, w_gu, b_gu, w_down, b_down):
    n_rows = xs.shape[0]
    of_expert = lambda e, *_: (e, 0, 0)
    grid_spec = pltpu.PrefetchScalarGridSpec(
        num_scalar_prefetch=6,
        grid=(N_EXPERTS,),
        in_specs=[
            pl.BlockSpec(memory_space=pl.ANY),
            pl.BlockSpec(memory_space=pl.ANY),
            pl.BlockSpec((None, 1, 2 * D_FF), of_expert),
            pl.BlockSpec(memory_space=pl.ANY),
            pl.BlockSpec((None, 1, D_MODEL), of_expert),
        ],
        out_specs=pl.BlockSpec(memory_space=pl.ANY),
        scratch_shapes=[pltpu.VMEM((2, D_MODEL, 2 * D_FF), F32), pltpu.VMEM((2, D_FF, D_MODEL), F32),
                        pltpu.VMEM((D_MODEL, 2 * D_FF), BF16), pltpu.VMEM((D_FF, D_MODEL), BF16),
                        pltpu.VMEM((2, EXP_TM, D_MODEL), F32), pltpu.VMEM((2, EXP_TM, D_MODEL), F32),
                        pltpu.SemaphoreType.DMA((2, 2)), pltpu.SemaphoreType.DMA((2,)),
                        pltpu.SemaphoreType.DMA((2,))],
    )
    return pl.pallas_call(
        _experts_body,
        grid_spec=grid_spec,
        out_shape=jax.ShapeDtypeStruct((n_rows, D_MODEL), F32),
        compiler_params=_params("arbitrary"),
        name="moe_experts",
    )(plan["tile_start"], plan["tile_count"], plan["next_expert"], plan["weight_slot"], plan["first_expert"],
      plan["ntiles"], xs, w_gu, b_gu, w_down, b_down)


def _combine_body(gmap_s, totc_s, route_ref, tril_ref, loffrow_ref, x1_ref, gfin_ref, ys_hbm, y_ref, buf, sem):
    i = pl.program_id(0)
    n_steps = pl.num_programs(0)
    slot = lax.rem(i, 2)

    def copy_in(slot_, lchunk, gchunk):
        return pltpu.make_async_copy(_chunk_rows(ys_hbm, gchunk), _chunk_rows(buf.at[slot_], lchunk), sem.at[slot_])

    def fetch_step(step, slot_):
        for u in range(COMB_BLKS):
            bb = step * COMB_BLKS + u
            _for_chunks(totc_s[bb], lambda c, prio: copy_in(slot_, u * BLK_CH + c,
                                                            gmap_s[bb * BLK_CH + c]).start(priority=prio))

    def wait_step(step, slot_):
        for u in range(COMB_BLKS):
            _wait_chunks(totc_s[step * COMB_BLKS + u], lambda rows: pltpu.make_async_copy(
                ys_hbm.at[pl.ds(0, rows)], buf.at[slot_, pl.ds(0, rows)], sem.at[slot_]))

    @pl.when(i == 0)
    def _():
        buf[...] = jnp.zeros_like(buf)
        fetch_step(0, 0)

    @pl.when(i + 1 < n_steps)
    def _():
        fetch_step(i + 1, 1 - slot)

    wait_step(i, slot)

    lane = lax.broadcasted_iota(jnp.int32, (1, LANES), 1).astype(F32)
    col = lax.broadcasted_iota(jnp.int32, (TBK, PERM_TM), 1).astype(F32).astype(BF16)
    zero = jnp.zeros((), BF16)

    def positions(u):
        route = route_ref[u * TBK:(u + 1) * TBK, :]
        sel = jnp.where(lane < float(N_EXPERTS), route, 0.0).astype(BF16)
        rank = jnp.dot(tril_ref[...], sel, preferred_element_type=F32)
        lpos = loffrow_ref[u] * float(ROW_CH) + rank
        lposk, pkb = [], []
        for k in range(TOP_K):
            idx = route[:, ROUTE_IDX + k:ROUTE_IDX + k + 1]
            lposk.append(jnp.sum(jnp.where(lane == idx, lpos, 0.0), axis=-1, keepdims=True))
            pkb.append(route[:, ROUTE_P + k:ROUTE_P + k + 1].astype(BF16))
        return lposk, pkb

    def gather(u, lposk, pkb):
        acc = x1_ref[u * TBK:(u + 1) * TBK, :]
        for r0 in range(0, RB, PERM_TM):
            rel = [(lposk[k] - float(r0)).astype(BF16) for k in range(TOP_K)]
            w = functools.reduce(jnp.add, [jnp.where(rel[k] == col, pkb[k], zero) for k in range(TOP_K)])
            rows = buf[slot, u * RB + r0:u * RB + r0 + PERM_TM, :].astype(BF16)
            acc = acc + jnp.dot(w, rows, preferred_element_type=F32)
        y_ref[u * TBK:(u + 1) * TBK, :] = _rms(acc, gfin_ref[...])

    pos = [positions(u) for u in range(COMB_BLKS)]
    for u in range(COMB_BLKS):
        gather(u, *pos[u])


def _combine(plan, blocks, route, loffrow, x1, g_final, ys):
    assert RB % PERM_TM == 0 and PERM_TM <= 256
    b0, b1 = blocks
    assert b0 % COMB_BLKS == 0 and b1 % COMB_BLKS == 0
    n_steps, s0 = (b1 - b0) // COMB_BLKS, b0 // COMB_BLKS
    tm = COMB_BLKS * TBK
    tril = jnp.asarray(np.tril(np.ones((TBK, TBK), np.float32), -1), BF16)
    grid_spec = pltpu.PrefetchScalarGridSpec(
        num_scalar_prefetch=2,
        grid=(n_steps,),
        in_specs=[
            pl.BlockSpec((tm, LANES), lambda i, *_: (i + s0, 0)),
            pl.BlockSpec((TBK, TBK), lambda i, *_: (0, 0)),
            pl.BlockSpec((COMB_BLKS, 1, LANES), lambda i, *_: (i + s0, 0, 0)),
            pl.BlockSpec((tm, D_MODEL), lambda i, *_: (i + s0, 0)),
            pl.BlockSpec((1, D_MODEL), lambda i, *_: (0, 0)),
            pl.BlockSpec(memory_space=pl.ANY),
        ],
        out_specs=pl.BlockSpec((tm, D_MODEL), lambda i, *_: (i, 0)),
        scratch_shapes=[pltpu.VMEM((2, COMB_BLKS * RB, D_MODEL), F32), pltpu.SemaphoreType.DMA((2,))],
    )
    return pl.pallas_call(
        _combine_body,
        grid_spec=grid_spec,
        out_shape=jax.ShapeDtypeStruct(((b1 - b0) * TBK, D_MODEL), F32),
        compiler_params=_params("arbitrary"),
        name="moe_combine",
    )(plan["gmap"][b0:b1].reshape(-1), plan["totc"][b0:b1], route, tril, loffrow, x1, g_final, ys)


def kernel(x_prompt, x_sample, cache_k, cache_v, state_s, g_mix, w_in, rel_bias, lb_logits, g_out_norm,
           w_pa, w_pb, w_out, g_ffn, w_router, b_router, w_gu, b_gu, w_down, b_down, g_final):
    B, T = x_prompt.shape[:2]
    DB, S = x_sample.shape[:2]
    depth = w_in.shape[0]
    assert depth == 1 and T % ATT_QBLK == 0 and S == CHUNK
    cw = cache_k.shape[2]
    assert cw == WINDOW
    l = 0

    lower = jnp.cumsum(jax.nn.softmax(lb_logits.astype(F32), axis=0), axis=0)[l].reshape(1, HG_WIDTH)
    w_in_b = w_in[l].astype(BF16)
    wpa, wpb, wout = w_pa[l].astype(BF16), w_pb[l].astype(BF16), w_out[l].astype(BF16)
    row = lambda a: a.reshape(1, -1).astype(F32)
    base = _rel_bias_base(rel_bias[l])
    b_gu3 = b_gu[l].reshape(N_EXPERTS, 1, 2 * D_FF)
    b_down3 = b_down[l].reshape(N_EXPERTS, 1, D_MODEL)
    pad_e = LANES - N_EXPERTS
    wr = jnp.pad(w_router[l].astype(F32), ((0, 0), (0, pad_e)))
    br = jnp.concatenate([b_router[l].astype(F32), jnp.full((pad_e,), NEG, F32)]).reshape(1, LANES)

    n_tok = B * T + DB * S
    nb, nbp = n_tok // TBK, (B * T) // TBK

    def front(x, batch, seq, s0, attend):
        xf = x.reshape(batch * seq, D_MODEL)
        za, zb, zg = _inproj(xf, row(g_mix[l]), w_in_b)
        att = attend(za)
        hg, s_fin = _hgrn(zb, s0, lower, row(g_out_norm[l]), batch, seq)
        za3 = za.reshape(batch, seq, 3 * ATT_WIDTH)
        heads = lambda a: a.reshape(1, batch, a.shape[1], ATT_HEADS, ATT_DIM)
        keep = min(WINDOW, seq)
        nk = heads(za3[:, seq - keep:, ATT_WIDTH:2 * ATT_WIDTH])
        nv = heads(za3[:, seq - keep:, 2 * ATT_WIDTH:])
        return dict(mix=(att, hg, zg, xf), nk=nk, nv=nv, s=s_fin[None])

    ck = cache_k[l].reshape(DB, cw, ATT_WIDTH)
    cv = cache_v[l].reshape(DB, cw, ATT_WIDTH)
    fp = front(x_prompt, B, T, jnp.zeros((B, HG_HEADS, HG_DK, HG_DK), F32), lambda za: _attn_prompt(za, base, B, T))
    fs = front(x_sample, DB, S, state_s[l].astype(F32), lambda za: _attn_sample(za, ck, cv, base, DB, S))

    x1, h2, route, cnt = _merge(fp["mix"], fs["mix"], wpa, wpb, wout, row(g_ffn[l]), wr, br)
    cnt = cnt[:, 0, :N_EXPERTS].astype(jnp.int32)
    max_rows = n_tok * TOP_K + nb * N_EXPERTS * (ROW_CH - 1) + N_EXPERTS * (EXP_TM - 1)
    nt_max = -(-max_rows // EXP_TM)
    plan = _route_plan(cnt)
    loff_f = plan["loff"].astype(F32)
    xs = _dispatch(plan, h2, route, loff_f[:, :, None], nt_max * EXP_TM)
    ysort = _experts(plan, xs, w_gu[l], b_gu3, w_down[l], b_down3)
    loffrow = jnp.pad(loff_f, ((0, 0), (0, pad_e)))[:, None, :]
    yp = _combine(plan, (0, nbp), route, loffrow, x1, row(g_final), ysort)
    ys = _combine(plan, (nbp, nb), route, loffrow, x1, row(g_final), ysort)
    return (yp.reshape(B, T, D_MODEL), ys.reshape(DB, S, D_MODEL), fp["nk"], fp["nv"], fp["s"],
            fs["nk"], fs["nv"], fs["s"])
```
